```python
import jax, jax.numpy as jnp
from jax import lax
import numpy as np

D_MODEL = 1024
BATCH = 8
SEQ = 2048
DEPTH = 1
DEC_BATCH = 32
DEC_SEQ = 4
PAST_LEN = 16384
PAGE_SIZE = 128

D_MLA = D_MODEL // 2
D_CONV = D_MODEL - D_MLA
N_HEADS = 8
QK_NOPE = 64
QK_ROPE = 32
V_HEAD = D_MLA // N_HEADS
Q_LORA = 384
KV_LORA = 256
CONV_W = 3
ROPE_THETA = 10000.0
EPS = 1e-6
Q_BLOCK = 128
ATTN_SCALE = (QK_NOPE + QK_ROPE) ** -0.5
IN_SIZES = (Q_LORA, KV_LORA, QK_ROPE, D_MLA, D_CONV, D_CONV, D_CONV, D_CONV)
D_IN = Q_LORA + KV_LORA + QK_ROPE + D_MLA + 4 * D_CONV

kernel_name = "hymba_mla_shortconv_decode_step"


def rmsnorm(x, g):
    xf = x.astype(jnp.float32)
    y = xf * lax.rsqrt(jnp.mean(xf * xf, axis=-1, keepdims=True) + EPS) * g.astype(jnp.float32)
    return y.astype(x.dtype)


def rope(x, pos):
    r = x.shape[-1]
    inv_freq = ROPE_THETA ** (-jnp.arange(0, r, 2, dtype=jnp.float32) / r)
    ang = pos.astype(jnp.float32)[:, None] * inv_freq[None, :]
    ang = jnp.concatenate([ang, ang], axis=-1)
    shape = (1, x.shape[1]) + (1,) * (x.ndim - 3) + (r,)
    cos = jnp.cos(ang).reshape(shape)
    sin = jnp.sin(ang).reshape(shape)
    xf = x.astype(jnp.float32)
    x1, x2 = xf[..., : r // 2], xf[..., r // 2:]
    rot = jnp.concatenate([-x2, x1], axis=-1)
    return (xf * cos + rot * sin).astype(x.dtype)


def split_points():
    pts, acc = [], 0
    for s in IN_SIZES[:-1]:
        acc += s
        pts.append(acc)
    return pts


def branch_inputs(xn, pos, w_in, g_qnorm, w_uq, g_kvnorm, w_ukv):
    b, t = xn.shape[0], xn.shape[1]
    z = xn @ w_in
    c_q, c_kv, k_pe, g_mla, b_gate, c_gate, h, g_conv = jnp.split(z, split_points(), axis=-1)
    q = (rmsnorm(c_q, g_qnorm) @ w_uq).reshape(b, t, N_HEADS, QK_NOPE + QK_ROPE)
    q_nope, q_pe = q[..., :QK_NOPE], rope(q[..., QK_NOPE:], pos)
    ckv = rmsnorm(c_kv, g_kvnorm)
    kpe = rope(k_pe, pos)
    w_uk = w_ukv.reshape(KV_LORA, N_HEADS, QK_NOPE + V_HEAD)[..., :QK_NOPE]
    q_lat = jnp.einsum('bthn,chn->bthc', q_nope, w_uk)
    u = c_gate * h
    return q_lat, q_pe, ckv, kpe, g_mla, b_gate, u, g_conv


def mla_attend(q_lat, q_pe, ckv, kpe, q_pos, k_pos):
    s = jnp.einsum('bqhc,bkc->bhqk', q_lat, ckv) + jnp.einsum('bqhr,bkr->bhqk', q_pe, kpe)
    s = s.astype(jnp.float32) * ATTN_SCALE
    mask = k_pos[None, :] <= q_pos[:, None]
    s = jnp.where(mask[None, None], s, -jnp.inf)
    p = jax.nn.softmax(s, axis=-1).astype(ckv.dtype)
    return jnp.einsum('bhqk,bkc->bqhc', p, ckv)


def prompt_attention(q_lat, q_pe, ckv, kpe, pos):
    b, s = q_lat.shape[0], q_lat.shape[1]
    nb = s // Q_BLOCK
    qb = q_lat.reshape(b, nb, Q_BLOCK, N_HEADS, KV_LORA).transpose(1, 0, 2, 3, 4)
    pb = q_pe.reshape(b, nb, Q_BLOCK, N_HEADS, QK_ROPE).transpose(1, 0, 2, 3, 4)
    posb = pos.reshape(nb, Q_BLOCK)
    o = lax.map(lambda a: mla_attend(a[0], a[1], ckv, kpe, a[2], pos), (qb, pb, posb))
    return o.transpose(1, 0, 2, 3, 4).reshape(b, s, N_HEADS, KV_LORA)


def short_conv(u_pad, w_conv):
    t = u_pad.shape[1] - (CONV_W - 1)
    y = w_conv[0] * u_pad[:, 0:t]
    for k in range(1, CONV_W):
        y = y + w_conv[k] * u_pad[:, k:k + t]
    return y


def branch_output(o_lat, g_mla, y_conv, b_gate, g_conv, w_ukv, w_out, g_post):
    b, t = o_lat.shape[0], o_lat.shape[1]
    w_uv = w_ukv.reshape(KV_LORA, N_HEADS, QK_NOPE + V_HEAD)[..., QK_NOPE:]
    o = jnp.einsum('bthc,chv->bthv', o_lat, w_uv).reshape(b, t, D_MLA)
    mix = jnp.concatenate([o * jax.nn.silu(g_mla), b_gate * y_conv * jax.nn.silu(g_conv)], axis=-1)
    return rmsnorm(mix @ w_out, g_post)


def setup_inputs(seed: int = 0) -> dict:
    key = jax.random.key(seed)
    ks = jax.random.split(key, 20)
    n_pages = PAST_LEN // PAGE_SIZE
    n_used = DEC_BATCH * n_pages
    n_phys = n_used + n_used // 4
    f32 = jnp.float32

    def w(k, shape, fan_in):
        return jax.random.normal(k, shape, f32) * fan_in ** -0.5

    def gain(k, d):
        return 1.0 + 0.02 * jax.random.normal(k, (DEPTH, d), f32)

    page_table = jax.random.permutation(ks[0], n_phys)[:n_used].reshape(DEC_BATCH, n_pages).astype(jnp.int32)
    return {
        "x_prompt": jax.random.normal(ks[1], (BATCH, SEQ, D_MODEL), f32),
        "x_sample": jax.random.normal(ks[2], (DEC_BATCH, DEC_SEQ, D_MODEL), f32),
        "cache_ckv": jax.random.normal(ks[3], (DEPTH, n_phys, PAGE_SIZE, KV_LORA), f32),
        "cache_kpe": jax.random.normal(ks[4], (DEPTH, n_phys, PAGE_SIZE, QK_ROPE), f32),
        "state_conv": jax.random.normal(ks[5], (DEPTH, DEC_BATCH, CONV_W - 1, D_CONV), f32),
        "page_table": page_table,
        "g_pre": gain(ks[6], D_MODEL),
        "w_in": w(ks[7], (DEPTH, D_MODEL, D_IN), D_MODEL),
        "g_qnorm": gain(ks[8], Q_LORA),
        "w_uq": w(ks[9], (DEPTH, Q_LORA, N_HEADS * (QK_NOPE + QK_ROPE)), Q_LORA),
        "g_kvnorm": gain(ks[10], KV_LORA),
        "w_ukv": w(ks[11], (DEPTH, KV_LORA, N_HEADS * (QK_NOPE + V_HEAD)), KV_LORA),
        "w_conv": w(ks[12], (DEPTH, CONV_W, D_CONV), CONV_W),
        "w_out": w(ks[13], (DEPTH, D_MODEL, D_MODEL), D_MODEL),
        "g_post": gain(ks[14], D_MODEL),
    }


def reference(x_prompt, x_sample, cache_ckv, cache_kpe, state_conv, page_table,
              g_pre, w_in, g_qnorm, w_uq, g_kvnorm, w_ukv, w_conv, w_out, g_post):
    pos_p = jnp.arange(SEQ, dtype=jnp.int32)
    pos_s = PAST_LEN + jnp.arange(DEC_SEQ, dtype=jnp.int32)
    k_pos_s = jnp.arange(PAST_LEN + DEC_SEQ, dtype=jnp.int32)
    xp, xs = x_prompt, x_sample
    ckv_p_l, kpe_p_l, conv_p_l, ckv_s_l, kpe_s_l, conv_s_l = [], [], [], [], [], []
    for l in range(DEPTH):
        q_lat, q_pe, ckv, kpe, g_mla, b_gate, u, g_conv = branch_inputs(
            rmsnorm(xp, g_pre[l]), pos_p, w_in[l], g_qnorm[l], w_uq[l], g_kvnorm[l], w_ukv[l])
        o_lat = prompt_attention(q_lat, q_pe, ckv, kpe, pos_p)
        u_pad = jnp.concatenate([jnp.zeros((u.shape[0], CONV_W - 1, D_CONV), u.dtype), u], axis=1)
        y_conv = short_conv(u_pad, w_conv[l])
        xp = xp + branch_output(o_lat, g_mla, y_conv, b_gate, g_conv, w_ukv[l], w_out[l], g_post[l])
        ckv_p_l.append(ckv)
        kpe_p_l.append(kpe)
        conv_p_l.append(u_pad[:, -(CONV_W - 1):])
        q_lat, q_pe, ckv, kpe, g_mla, b_gate, u, g_conv = branch_inputs(
            rmsnorm(xs, g_pre[l]), pos_s, w_in[l], g_qnorm[l], w_uq[l], g_kvnorm[l], w_ukv[l])
        past_ckv = cache_ckv[l][page_table].reshape(DEC_BATCH, PAST_LEN, KV_LORA)
        past_kpe = cache_kpe[l][page_table].reshape(DEC_BATCH, PAST_LEN, QK_ROPE)
        keys_ckv = jnp.concatenate([past_ckv, ckv], axis=1)
        keys_kpe = jnp.concatenate([past_kpe, kpe], axis=1)
        o_lat = mla_attend(q_lat, q_pe, keys_ckv, keys_kpe, pos_s, k_pos_s)
        u_pad = jnp.concatenate([state_conv[l].astype(u.dtype), u], axis=1)
        y_conv = short_conv(u_pad, w_conv[l])
        xs = xs + branch_output(o_lat, g_mla, y_conv, b_gate, g_conv, w_ukv[l], w_out[l], g_post[l])
        ckv_s_l.append(ckv)
        kpe_s_l.append(kpe)
        conv_s_l.append(u_pad[:, -(CONV_W - 1):])
    new_ckv_prompt = jnp.stack(ckv_p_l)
    new_kpe_prompt = jnp.stack(kpe_p_l)
    new_conv_prompt = jnp.stack(conv_p_l)
    new_ckv_sample = jnp.stack(ckv_s_l)
    new_kpe_sample = jnp.stack(kpe_s_l)
    new_conv_sample = jnp.stack(conv_s_l)
    return (xp, xs, new_ckv_prompt, new_kpe_prompt, new_conv_prompt, new_ckv_sample, new_kpe_sample, new_conv_sample)
```

```python
import functools

import jax
import jax.numpy as jnp
from jax import lax
from jax.experimental import pallas as pl
from jax.experimental.pallas import tpu as pltpu

N_HEADS = 8
QK_NOPE = 64
QK_ROPE = 32
V_HEAD = 64
CONV_W = 3
ROPE_THETA = 10000.0
EPS = 1e-6
ATTN_SCALE = (QK_NOPE + QK_ROPE) ** -0.5

LANES = 128
HEAD_SLAB = LANES
VMEM_LIMIT_BYTES = 56 * 1024 * 1024

F32 = jnp.float32
BF16 = jnp.bfloat16


def _silu(x):
    return x * (1.0 / (1.0 + jnp.exp(-x)))


def _rms(x, g):
    return x * lax.rsqrt(jnp.mean(x * x, axis=-1, keepdims=True) + EPS) * g


def _rope_table_kernel(pos_ref, invf_ref, cq1_ref, cq2_ref, ck1_ref, ck2_ref):
    ang = pos_ref[...] * invf_ref[...]
    lane = lax.broadcasted_iota(jnp.int32, ang.shape, 1)
    rope = (lane >= QK_NOPE) & (lane < QK_NOPE + QK_ROPE)
    c = jnp.where(rope, jnp.cos(ang), 0.0)
    s = jnp.where(rope, jnp.sin(ang), 0.0)
    cq1_ref[...] = jnp.where(lane < QK_NOPE, ATTN_SCALE, c * ATTN_SCALE)
    cq2_ref[...] = s * ATTN_SCALE
    ck1_ref[...] = c
    ck2_ref[...] = s


def _rope_tables(pos):
    t = pos.shape[0]
    r = QK_ROPE
    inv_freq = ROPE_THETA ** (-jnp.arange(0, r, 2, dtype=F32) / r)
    invf = jnp.zeros((1, LANES), F32).at[0, QK_NOPE:QK_NOPE + r].set(jnp.concatenate([inv_freq, inv_freq]))
    out = jax.ShapeDtypeStruct((t, LANES), F32)
    return pl.pallas_call(
        _rope_table_kernel,
        out_shape=(out, out, out, out),
        name="rope_tables",
    )(pos.astype(F32).reshape(t, 1), invf)


C_Q0, C_Q1 = 0, 384
C_KV0, C_KV1 = 384, 640
C_KP0, C_KP1 = 640, 768
C_GM0 = 768
GROUP = 512
D_IN_PACKED = C_GM0 + 5 * GROUP


def _rope_slab(blk, c1, c2):
    return blk * c1 + pltpu.roll(blk, LANES - QK_ROPE, 1) * c2


def _project_common(x_ref, gpre_ref, w1_ref, gq_ref, wq_ref, gkv_ref, cq1_ref, cq2_ref, ck1_ref, ck2_ref):
    x = x_ref[...]
    xn = _rms(x, gpre_ref[...]).astype(BF16)

    def proj(c0, c1):
        return jnp.dot(xn, w1_ref[:, c0:c1], preferred_element_type=F32)

    cqn = _rms(proj(C_Q0, C_Q1), gq_ref[...]).astype(BF16)
    qraw = jnp.dot(cqn, wq_ref[...], preferred_element_type=F32)
    cq1, cq2 = cq1_ref[...], cq2_ref[...]
    q_heads = [_rope_slab(qraw[:, h * HEAD_SLAB:(h + 1) * HEAD_SLAB], cq1, cq2) for h in range(N_HEADS)]
    ckv = _rms(proj(C_KV0, C_KV1), gkv_ref[...])
    kpe_blk = _rope_slab(proj(C_KP0, C_KP1), ck1_ref[...], ck2_ref[...])
    return proj, q_heads, ckv, kpe_blk


def _conv_gate(proj, um1, um2, u, wconv_ref):
    w = wconv_ref[...]
    y = w[0:1, :] * um2 + w[1:2, :] * um1 + w[2:3, :] * u
    bg = proj(C_GM0 + GROUP, C_GM0 + 2 * GROUP)
    gc = proj(C_GM0 + 4 * GROUP, C_GM0 + 5 * GROUP)
    return bg * y * _silu(gc)


def _proj_prompt_kernel(x_ref, gpre_ref, w1_ref, gq_ref, wq_ref, gkv_ref, wk_ref, wv_ref, wconv_ref,
                        cq1_ref, cq2_ref, ck1_ref, ck2_ref,
                        q_out, k_out, v_out, ckv_out, kpe_out, gm_out, cv_out, conv_out,
                        carry_ref, *, tiles_per_seq):
    tm = x_ref.shape[0]
    step = pl.program_id(0)
    proj, q_heads, ckv, kpe_blk = _project_common(
        x_ref, gpre_ref, w1_ref, gq_ref, wq_ref, gkv_ref, cq1_ref, cq2_ref, ck1_ref, ck2_ref)
    for h in range(N_HEADS):
        q_out[:, h * HEAD_SLAB:(h + 1) * HEAD_SLAB] = q_heads[h].astype(BF16)
    ckv_out[...] = ckv
    kpe_out[...] = kpe_blk[:, QK_NOPE:QK_NOPE + QK_ROPE]
    ckvb = ckv.astype(BF16)
    kfull = jnp.dot(ckvb, wk_ref[...], preferred_element_type=F32)
    for h in range(N_HEADS):
        k_out[:, h * HEAD_SLAB:(h + 1) * HEAD_SLAB] = (kfull[:, h * HEAD_SLAB:(h + 1) * HEAD_SLAB] + kpe_blk).astype(BF16)
    v_out[...] = jnp.dot(ckvb, wv_ref[...], preferred_element_type=F32).astype(BF16)
    gm_out[...] = _silu(proj(C_GM0, C_GM0 + GROUP)).astype(BF16)

    u = proj(C_GM0 + 2 * GROUP, C_GM0 + 3 * GROUP) * proj(C_GM0 + 3 * GROUP, C_GM0 + 4 * GROUP)

    @pl.when(step % tiles_per_seq == 0)
    def _():
        carry_ref[...] = jnp.zeros_like(carry_ref)

    prev = carry_ref[...]
    p1, p2 = prev[7:8, :], prev[6:7, :]
    row = lax.broadcasted_iota(jnp.int32, u.shape, 0)
    um1 = jnp.where(row == 0, p1, pltpu.roll(u, 1, 0))
    um2 = jnp.where(row == 0, p2, jnp.where(row == 1, p1, pltpu.roll(u, 2, 0)))
    cv_out[...] = _conv_gate(proj, um1, um2, u, wconv_ref).astype(BF16)
    carry_ref[...] = u[tm - 8:tm, :]
    conv_out[0] = u[tm - (CONV_W - 1):tm, :]


def _const_spec(shape):
    nd = len(shape)
    return pl.BlockSpec(shape, lambda *_: (0,) * nd, pipeline_mode=pl.Buffered(1))


def _proj_prompt(x2d, seq, gpre, w1, gq, wq, gkv, wk, wv, wconv, tabs, tm):
    t, d = x2d.shape
    nb = t // seq
    hs = N_HEADS * HEAD_SLAB
    row = lambda w: pl.BlockSpec((tm, w), lambda i: (i, 0))
    tab = pl.BlockSpec((tm, LANES), lambda i: (i % (seq // tm), 0))
    out_shape = (
        jax.ShapeDtypeStruct((t, hs), BF16),
        jax.ShapeDtypeStruct((t, hs), BF16),
        jax.ShapeDtypeStruct((t, hs), BF16),
        jax.ShapeDtypeStruct((t, gkv.shape[1]), F32),
        jax.ShapeDtypeStruct((t, QK_ROPE), F32),
        jax.ShapeDtypeStruct((t, GROUP), BF16),
        jax.ShapeDtypeStruct((t, GROUP), BF16),
        jax.ShapeDtypeStruct((nb, CONV_W - 1, GROUP), F32),
    )
    return pl.pallas_call(
        functools.partial(_proj_prompt_kernel, tiles_per_seq=seq // tm),
        grid=(t // tm,),
        in_specs=[row(d), _const_spec(gpre.shape), _const_spec(w1.shape), _const_spec(gq.shape),
                  _const_spec(wq.shape), _const_spec(gkv.shape), _const_spec(wk.shape), _const_spec(wv.shape),
                  _const_spec(wconv.shape), tab, tab, tab, tab],
        out_specs=(row(hs), row(hs), row(hs), row(gkv.shape[1]), row(QK_ROPE), row(GROUP), row(GROUP),
                   pl.BlockSpec((1, CONV_W - 1, GROUP), lambda i: (i // (seq // tm), 0, 0))),
        out_shape=out_shape,
        scratch_shapes=[pltpu.VMEM((8, GROUP), F32)],
        compiler_params=pltpu.CompilerParams(dimension_semantics=("arbitrary",),
                                             vmem_limit_bytes=VMEM_LIMIT_BYTES),
        name="proj_prompt",
    )(x2d, gpre, w1, gq, wq, gkv, wk, wv, wconv, *tabs)


def _attn_kernel(q_ref, k_ref, v_ref, o_ref, *, tq, tk):
    i = pl.program_id(1)
    dn = (((1,), (1,)), ((), ()))
    n_diag = tq // tk
    qpos = i * tq + lax.broadcasted_iota(jnp.int32, (tq, tk), 0)
    kcol = lax.broadcasted_iota(jnp.int32, (tq, tk), 1)
    lane = lax.broadcasted_iota(jnp.int32, (tq, LANES), 1)

    def head(h):
        sl = slice(h * HEAD_SLAB, (h + 1) * HEAD_SLAB)
        q = q_ref[:, sl]

        def tile(j, carry, masked):
            m, l, acc = carry
            rows = pl.ds(pl.multiple_of(j * tk, tk), tk)
            s = lax.dot_general(q, k_ref[rows, sl], dn, preferred_element_type=F32)
            if masked:
                s = jnp.where(j * tk + kcol <= qpos, s, -jnp.inf)
            m_new = jnp.maximum(m, jnp.max(s, axis=-1, keepdims=True))
            alpha = jnp.exp(m - m_new)
            p = jnp.exp(s - m_new)
            l = alpha * l + jnp.sum(p, axis=-1, keepdims=True)
            acc = alpha * acc + jnp.dot(p.astype(BF16), v_ref[rows, sl], preferred_element_type=F32)
            return m_new, l, acc

        carry = (jnp.full((tq, 1), -jnp.inf, F32), jnp.zeros((tq, 1), F32), jnp.zeros((tq, LANES), F32))
        n_full = i * n_diag
        for d in range(n_diag):
            carry = tile(n_full + d, carry, True)
        m, l, acc = lax.fori_loop(0, n_full, lambda j, c: tile(j, c, False), carry)
        return acc * (1.0 / l)

    for hp in range(N_HEADS // 2):
        lo, hi = head(2 * hp), head(2 * hp + 1)
        pair = jnp.where(lane < V_HEAD, lo, pltpu.roll(hi, V_HEAD, 1))
        o_ref[:, hp * LANES:(hp + 1) * LANES] = pair.astype(o_ref.dtype)


def _attention(q, k, v, nb, seq, tq, tk):
    hs = q.shape[1]
    return pl.pallas_call(
        functools.partial(_attn_kernel, tq=tq, tk=tk),
        grid=(nb, seq // tq),
        in_specs=[pl.BlockSpec((tq, hs), lambda b, i: (b * (seq // tq) + i, 0)),
                  pl.BlockSpec((seq, hs), lambda b, i: (b, 0)),
                  pl.BlockSpec((seq, hs), lambda b, i: (b, 0))],
        out_specs=pl.BlockSpec((tq, N_HEADS * V_HEAD), lambda b, i: (b * (seq // tq) + i, 0)),
        out_shape=jax.ShapeDtypeStruct((q.shape[0], N_HEADS * V_HEAD), BF16),
        compiler_params=pltpu.CompilerParams(dimension_semantics=("arbitrary", "arbitrary"),
                                             vmem_limit_bytes=VMEM_LIMIT_BYTES),
        name="prompt_attn",
    )(q, k, v)


def _out_prompt_kernel(o_ref, gm_ref, cv_ref, x_ref, wo_ref, gpost_ref, y_ref):
    half = o_ref.shape[1]
    mla = (o_ref[...].astype(F32) * gm_ref[...].astype(F32)).astype(BF16)
    y = jnp.dot(mla, wo_ref[0:half, :], preferred_element_type=F32)
    y = y + jnp.dot(cv_ref[...], wo_ref[half:, :], preferred_element_type=F32)
    y_ref[...] = x_ref[...] + _rms(y, gpost_ref[...])


def _out_prompt(o, gm, cv, x2d, wo, gpost, tm):
    t, d = x2d.shape
    row = lambda w: pl.BlockSpec((tm, w), lambda i: (i, 0))
    return pl.pallas_call(
        _out_prompt_kernel,
        grid=(t // tm,),
        in_specs=[row(o.shape[1]), row(gm.shape[1]), row(cv.shape[1]), row(d),
                  _const_spec(wo.shape), _const_spec(gpost.shape)],
        out_specs=row(d),
        out_shape=jax.ShapeDtypeStruct((t, d), F32),
        compiler_params=pltpu.CompilerParams(dimension_semantics=("arbitrary",),
                                             vmem_limit_bytes=VMEM_LIMIT_BYTES),
        name="out_prompt",
    )(o, gm, cv, x2d, wo, gpost)


def _proj_sample_kernel(x_ref, gpre_ref, w1_ref, gq_ref, wq_ref, gkv_ref, wukt_ref, wconv_ref,
                        cq1_ref, cq2_ref, ck1_ref, ck2_ref, s0_ref, s1_ref,
                        ql_out, qp_out, ckv_out, kpe_out, gm_out, cv_out, conv_out, *, dec_seq):
    proj, q_heads, ckv, kpe_blk = _project_common(
        x_ref, gpre_ref, w1_ref, gq_ref, wq_ref, gkv_ref, cq1_ref, cq2_ref, ck1_ref, ck2_ref)
    for h in range(N_HEADS):
        qh = q_heads[h]
        ql_out[h] = jnp.dot(qh.astype(BF16), wukt_ref[h], preferred_element_type=F32).astype(BF16)
        qp_out[h] = pltpu.roll(qh, LANES - QK_NOPE, 1)[:, 0:QK_ROPE].astype(BF16)
    ckv_out[...] = ckv
    kpe_out[...] = kpe_blk[:, QK_NOPE:QK_NOPE + QK_ROPE]
    gm_out[...] = _silu(proj(C_GM0, C_GM0 + GROUP))

    u = proj(C_GM0 + 2 * GROUP, C_GM0 + 3 * GROUP) * proj(C_GM0 + 3 * GROUP, C_GM0 + 4 * GROUP)
    t_in_seq = lax.broadcasted_iota(jnp.int32, u.shape, 0) % dec_seq
    s0, s1 = s0_ref[...], s1_ref[...]
    um1 = jnp.where(t_in_seq == 0, s1, pltpu.roll(u, 1, 0))
    um2 = jnp.where(t_in_seq == 0, s0, jnp.where(t_in_seq == 1, s1, pltpu.roll(u, 2, 0)))
    cv_out[...] = _conv_gate(proj, um1, um2, u, wconv_ref)
    conv_out[...] = u


def _proj_sample(x2d, dec_seq, gpre, w1, gq, wq, gkv, wukt, wconv, tabs, s0, s1):
    t, d = x2d.shape
    c = gkv.shape[1]
    out_shape = (
        jax.ShapeDtypeStruct((N_HEADS, t, c), BF16),
        jax.ShapeDtypeStruct((N_HEADS, t, QK_ROPE), BF16),
        jax.ShapeDtypeStruct((t, c), F32),
        jax.ShapeDtypeStruct((t, QK_ROPE), F32),
        jax.ShapeDtypeStruct((t, GROUP), F32),
        jax.ShapeDtypeStruct((t, GROUP), F32),
        jax.ShapeDtypeStruct((t, GROUP), F32),
    )
    return pl.pallas_call(
        functools.partial(_proj_sample_kernel, dec_seq=dec_seq),
        out_shape=out_shape,
        compiler_params=pltpu.CompilerParams(vmem_limit_bytes=VMEM_LIMIT_BYTES),
        name="proj_sample",
    )(x2d, gpre, w1, gq, wq, gkv, wukt, wconv, *tabs, s0, s1)


def _decode_attn_kernel(pt_ref, ql_ref, qp_ref, cnew_ref, knew_ref, *rest, pages, dec_seq):
    ckv_pages = rest[:pages]
    kpe_pages = rest[pages:2 * pages]
    o_ref = rest[2 * pages]
    m_ref, l_ref, acc_ref = rest[2 * pages + 1:]
    c = pl.program_id(1)
    dn = (((1,), (1,)), ((), ()))
    ql = ql_ref[0]
    qp = qp_ref[0]
    rows = ql.shape[0]

    @pl.when(c == 0)
    def _():
        qlf, qpf = ql.astype(F32), qp.astype(F32)
        tok = lax.broadcasted_iota(jnp.int32, (rows, 1), 0) // N_HEADS
        s = []
        for j in range(dec_seq):
            sj = (jnp.sum(qlf * cnew_ref[0, j:j + 1, :], axis=-1, keepdims=True)
                  + jnp.sum(qpf * knew_ref[0, j:j + 1, :], axis=-1, keepdims=True))
            s.append(jnp.where(tok >= j, sj, -jnp.inf))
        m = functools.reduce(jnp.maximum, s)
        p = [jnp.exp(sj - m) for sj in s]
        m_ref[...] = m
        l_ref[...] = functools.reduce(lambda a, b: a + b, p)
        acc_ref[...] = functools.reduce(lambda a, b: a + b,
                                        [p[j] * cnew_ref[0, j:j + 1, :] for j in range(dec_seq)])

    kc = [r[0, 0].astype(BF16) for r in ckv_pages]
    kp = [r[0, 0].astype(BF16) for r in kpe_pages]
    s = jnp.concatenate(
        [lax.dot_general(ql, kc[i], dn, preferred_element_type=F32)
         + lax.dot_general(qp, kp[i], dn, preferred_element_type=F32) for i in range(pages)], axis=1)
    m_old = m_ref[...]
    m_new = jnp.maximum(m_old, jnp.max(s, axis=-1, keepdims=True))
    alpha = jnp.exp(m_old - m_new)
    p = jnp.exp(s - m_new)
    page = kc[0].shape[0]
    pv = functools.reduce(
        lambda a, b: a + b,
        [jnp.dot(p[:, i * page:(i + 1) * page].astype(BF16), kc[i], preferred_element_type=F32)
         for i in range(pages)])
    m_ref[...] = m_new
    l_ref[...] = alpha * l_ref[...] + jnp.sum(p, axis=-1, keepdims=True)
    acc_ref[...] = alpha * acc_ref[...] + pv

    @pl.when(c == pl.num_programs(1) - 1)
    def _():
        o_ref[0] = acc_ref[...] * (1.0 / l_ref[...])


def _decode_attention(page_table, ql, qp, cnew, knew, cache_ckv, cache_kpe, layer, pages):
    nb, rows, c = ql.shape
    dec_seq = cnew.shape[1]
    n_pages = page_table.shape[1]
    page = cache_ckv.shape[2]
    r = cache_kpe.shape[3]
    per_b = lambda w: pl.BlockSpec((1, rows, w), lambda b, j, pt: (b, 0, 0))
    new = lambda w: pl.BlockSpec((1, dec_seq, w), lambda b, j, pt: (b, 0, 0))

    def page_spec(w, i):
        return pl.BlockSpec((1, 1, page, w), lambda b, j, pt: (layer, pt[b, j * pages + i], 0, 0))

    grid_spec = pltpu.PrefetchScalarGridSpec(
        num_scalar_prefetch=1,
        grid=(nb, n_pages // pages),
        in_specs=[per_b(c), per_b(r), new(c), new(r)]
                 + [page_spec(c, i) for i in range(pages)] + [page_spec(r, i) for i in range(pages)],
        out_specs=per_b(c),
        scratch_shapes=[pltpu.VMEM((rows, 1), F32), pltpu.VMEM((rows, 1), F32), pltpu.VMEM((rows, c), F32)],
    )
    return pl.pallas_call(
        functools.partial(_decode_attn_kernel, pages=pages, dec_seq=dec_seq),
        grid_spec=grid_spec,
        out_shape=jax.ShapeDtypeStruct((nb, rows, c), F32),
        compiler_params=pltpu.CompilerParams(dimension_semantics=("arbitrary", "arbitrary"),
                                             vmem_limit_bytes=VMEM_LIMIT_BYTES),
        name="decode_attn",
    )(page_table, ql, qp, cnew, knew, *([cache_ckv] * pages), *([cache_kpe] * pages))


def _out_sample_kernel(ol_ref, gm_ref, cv_ref, x_ref, wuv_ref, wo_ref, gpost_ref, y_ref):
    half = gm_ref.shape[1]
    o = jnp.concatenate(
        [jnp.dot(ol_ref[h].astype(BF16), wuv_ref[h], preferred_element_type=F32) for h in range(N_HEADS)], axis=1)
    mla = (o * gm_ref[...]).astype(BF16)
    y = jnp.dot(mla, wo_ref[0:half, :], preferred_element_type=F32)
    y = y + jnp.dot(cv_ref[...].astype(BF16), wo_ref[half:, :], preferred_element_type=F32)
    y_ref[...] = x_ref[...] + _rms(y, gpost_ref[...])


def _out_sample(ol, gm, cv, x2d, wuv, wo, gpost):
    return pl.pallas_call(
        _out_sample_kernel,
        out_shape=jax.ShapeDtypeStruct(x2d.shape, F32),
        compiler_params=pltpu.CompilerParams(vmem_limit_bytes=VMEM_LIMIT_BYTES),
        name="out_sample",
    )(ol, gm, cv, x2d, wuv, wo, gpost)


def _rot_cols(w):
    r = w.shape[-1] // 2
    return jnp.concatenate([-w[..., r:], w[..., :r]], axis=-1)


def _pack_weights(w_in, w_uq, w_ukv, w_out):
    d, q_lora, kv_lora = w_in.shape[0], w_uq.shape[0], w_ukv.shape[0]
    cq, ckv, kpe, rest = (w_in[:, :C_Q1], w_in[:, C_KV0:C_KV1], w_in[:, C_KV1:C_KV1 + QK_ROPE],
                          w_in[:, C_KV1 + QK_ROPE:])
    w1 = jnp.concatenate([cq, ckv, jnp.zeros((d, QK_NOPE), F32), kpe, _rot_cols(kpe), rest], axis=1).astype(BF16)
    uq = w_uq.reshape(q_lora, N_HEADS, QK_NOPE + QK_ROPE)
    pe = uq[:, :, QK_NOPE:]
    wq = jnp.concatenate([uq[:, :, :QK_NOPE], pe, _rot_cols(pe)], axis=-1).reshape(q_lora, N_HEADS * HEAD_SLAB)
    ukv = w_ukv.reshape(kv_lora, N_HEADS, QK_NOPE + V_HEAD)
    uk, uv = ukv[:, :, :QK_NOPE], ukv[:, :, QK_NOPE:]
    pad = jnp.zeros((kv_lora, N_HEADS, HEAD_SLAB - QK_NOPE), F32)
    wk = jnp.concatenate([uk, pad], axis=-1).reshape(kv_lora, N_HEADS * HEAD_SLAB)
    wv = jnp.concatenate([uv, pad], axis=-1).reshape(kv_lora, N_HEADS * HEAD_SLAB)
    wukt = jnp.concatenate([uk.transpose(1, 2, 0), jnp.zeros((N_HEADS, HEAD_SLAB - QK_NOPE, kv_lora), F32)], axis=1)
    wuv = uv.transpose(1, 0, 2)
    return (w1, wq.astype(BF16), wk.astype(BF16), wv.astype(BF16), wukt.astype(BF16), wuv.astype(BF16),
            w_out.astype(BF16))


PROMPT_TILE = 512
ATTN_TQ = 512
ATTN_TK = 512
DECODE_PAGES = 16


def kernel(x_prompt, x_sample, cache_ckv, cache_kpe, state_conv, page_table, g_pre, w_in, g_qnorm, w_uq,
           g_kvnorm, w_ukv, w_conv, w_out, g_post):
    depth = w_in.shape[0]
    nb, seq, d = x_prompt.shape
    db, dec_seq, _ = x_sample.shape
    past_len = page_table.shape[1] * cache_ckv.shape[2]
    c = cache_ckv.shape[3]

    tabs_p = _rope_tables(jnp.arange(seq, dtype=jnp.int32))
    tabs_s = _rope_tables(jnp.tile(past_len + jnp.arange(dec_seq, dtype=jnp.int32), db))

    xp = x_prompt.reshape(nb * seq, d)
    xs = x_sample.reshape(db * dec_seq, d)
    outs = [[] for _ in range(6)]
    for l in range(depth):
        w1, wq, wk, wv, wukt, wuv, wo = _pack_weights(w_in[l], w_uq[l], w_ukv[l], w_out[l])
        gpre, gq, gkv, gpost = g_pre[l][None], g_qnorm[l][None], g_kvnorm[l][None], g_post[l][None]

        q, k, v, ckv_p, kpe_p, gm, cv, conv_p = _proj_prompt(
            xp, seq, gpre, w1, gq, wq, gkv, wk, wv, w_conv[l], tabs_p, PROMPT_TILE)
        o = _attention(q, k, v, nb, seq, ATTN_TQ, ATTN_TK)
        xp = _out_prompt(o, gm, cv, xp, wo, gpost, PROMPT_TILE)
        outs[0].append(ckv_p.reshape(nb, seq, c))
        outs[1].append(kpe_p.reshape(nb, seq, QK_ROPE))
        outs[2].append(conv_p)

        st = state_conv[l].astype(F32)
        s0 = jnp.repeat(st[:, 0], dec_seq, axis=0)
        s1 = jnp.repeat(st[:, 1], dec_seq, axis=0)
        ql, qp, ckv_s, kpe_s, gm_s, cv_s, u_s = _proj_sample(
            xs, dec_seq, gpre, w1, gq, wq, gkv, wukt, w_conv[l], tabs_s, s0, s1)
        rows = dec_seq * N_HEADS
        to_rows = lambda a: a.reshape(N_HEADS, db, dec_seq, -1).transpose(1, 2, 0, 3).reshape(db, rows, -1)
        ol = _decode_attention(page_table, to_rows(ql), to_rows(qp), ckv_s.reshape(db, dec_seq, c),
                               kpe_s.reshape(db, dec_seq, QK_ROPE), cache_ckv, cache_kpe, l, DECODE_PAGES)
        ol = ol.reshape(db, dec_seq, N_HEADS, c).transpose(2, 0, 1, 3).reshape(N_HEADS, db * dec_seq, c)
        xs = _out_sample(ol, gm_s, cv_s, xs, wuv, wo, gpost)
        outs[3].append(ckv_s.reshape(db, dec_seq, c))
        outs[4].append(kpe_s.reshape(db, dec_seq, QK_ROPE))
        u_pad = jnp.concatenate([st, u_s.reshape(db, dec_seq, GROUP)], axis=1)
        outs[5].append(u_pad[:, -(CONV_W - 1):])

    return (xp.reshape(nb, seq, d), xs.reshape(db, dec_seq, d), *[jnp.stack(o_) for o_ in outs])
```

```python
import functools

import jax
import jax.numpy as jnp
from jax import lax
from jax.experimental import pallas as pl
from jax.experimental.pallas import tpu as pltpu

N_HEADS = 8
QK_NOPE = 64
QK_ROPE = 32
V_HEAD = 64
CONV_W = 3
ROPE_THETA = 10000.0
EPS = 1e-6
ATTN_SCALE = (QK_NOPE + QK_ROPE) ** -0.5

LANES = 128
HEAD_SLAB = LANES
VMEM_LIMIT_BYTES = 56 * 1024 * 1024

F32 = jnp.float32
BF16 = jnp.bfloat16


def _silu(x):
    return x * (1.0 / (1.0 + jnp.exp(-x)))


def _rms(x, g):
    return x * lax.rsqrt(jnp.mean(x * x, axis=-1, keepdims=True) + EPS) * g


def _rope_table_kernel(pos_ref, invf_ref, cq1_ref, cq2_ref, ck1_ref, ck2_ref):
    ang = pos_ref[...] * invf_ref[...]
    lane = lax.broadcasted_iota(jnp.int32, ang.shape, 1)
    rope = (lane >= QK_NOPE) & (lane < QK_NOPE + QK_ROPE)
    c = jnp.where(rope, jnp.cos(ang), 0.0)
    s = jnp.where(rope, jnp.sin(ang), 0.0)
    cq1_ref[...] = jnp.where(lane < QK_NOPE, ATTN_SCALE, c * ATTN_SCALE)
    cq2_ref[...] = s * ATTN_SCALE
    ck1_ref[...] = c
    ck2_ref[...] = s


def _rope_tables(pos):
    t = pos.shape[0]
    r = QK_ROPE
    inv_freq = ROPE_THETA ** (-jnp.arange(0, r, 2, dtype=F32) / r)
    invf = jnp.zeros((1, LANES), F32).at[0, QK_NOPE:QK_NOPE + r].set(jnp.concatenate([inv_freq, inv_freq]))
    out = jax.ShapeDtypeStruct((t, LANES), F32)
    return pl.pallas_call(
        _rope_table_kernel,
        out_shape=(out, out, out, out),
        name="rope_tables",
    )(pos.astype(F32).reshape(t, 1), invf)


C_Q0, C_Q1 = 0, 384
C_KV0, C_KV1 = 384, 640
C_KP0, C_KP1 = 640, 768
C_GM0 = 768
GROUP = 512
D_IN_PACKED = C_GM0 + 5 * GROUP


def _rope_slab(blk, c1, c2):
    return blk * c1 + pltpu.roll(blk, LANES - QK_ROPE, 1) * c2


def _project_common(x_ref, gpre_ref, w1_ref, gq_ref, wq_ref, gkv_ref, cq1_ref, cq2_ref, ck1_ref, ck2_ref):
    x = x_ref[...]
    xn = _rms(x, gpre_ref[...]).astype(BF16)

    def proj(c0, c1):
        return jnp.dot(xn, w1_ref[:, c0:c1], preferred_element_type=F32)

    cqn = _rms(proj(C_Q0, C_Q1), gq_ref[...]).astype(BF16)
    qraw = jnp.dot(cqn, wq_ref[...], preferred_element_type=F32)
    cq1, cq2 = cq1_ref[...], cq2_ref[...]
    q_heads = [_rope_slab(qraw[:, h * HEAD_SLAB:(h + 1) * HEAD_SLAB], cq1, cq2) for h in range(N_HEADS)]
    ckv = _rms(proj(C_KV0, C_KV1), gkv_ref[...])
    kpe_blk = _rope_slab(proj(C_KP0, C_KP1), ck1_ref[...], ck2_ref[...])
    return proj, q_heads, ckv, kpe_blk


def _conv_gate(proj, um1, um2, u, wconv_ref):
    w = wconv_ref[...]
    y = w[0:1, :] * um2 + w[1:2, :] * um1 + w[2:3, :] * u
    bg = proj(C_GM0 + GROUP, C_GM0 + 2 * GROUP)
    gc = proj(C_GM0 + 4 * GROUP, C_GM0 + 5 * GROUP)
    return bg * y * _silu(gc)


def _proj_prompt_kernel(x_ref, gpre_ref, w1_ref, gq_ref, wq_ref, gkv_ref, wk_ref, wv_ref, wconv_ref,
                        cq1_ref, cq2_ref, ck1_ref, ck2_ref,
                        q_out, k_out, v_out, ckv_out, kpe_out, gm_out, cv_out, conv_out,
                        carry_ref, *, tiles_per_seq):
    tm = x_ref.shape[0]
    step = pl.program_id(0)
    proj, q_heads, ckv, kpe_blk = _project_common(
        x_ref, gpre_ref, w1_ref, gq_ref, wq_ref, gkv_ref, cq1_ref, cq2_ref, ck1_ref, ck2_ref)
    for h in range(N_HEADS):
        q_out[:, h * HEAD_SLAB:(h + 1) * HEAD_SLAB] = q_heads[h].astype(BF16)
    ckv_out[...] = ckv
    kpe_out[...] = kpe_blk[:, QK_NOPE:QK_NOPE + QK_ROPE]
    ckvb = ckv.astype(BF16)
    kfull = jnp.dot(ckvb, wk_ref[...], preferred_element_type=F32)
    for h in range(N_HEADS):
        k_out[:, h * HEAD_SLAB:(h + 1) * HEAD_SLAB] = (kfull[:, h * HEAD_SLAB:(h + 1) * HEAD_SLAB] + kpe_blk).astype(BF16)
    v_out[...] = jnp.dot(ckvb, wv_ref[...], preferred_element_type=F32).astype(BF16)
    gm_out[...] = _silu(proj(C_GM0, C_GM0 + GROUP)).astype(BF16)

    u = proj(C_GM0 + 2 * GROUP, C_GM0 + 3 * GROUP) * proj(C_GM0 + 3 * GROUP, C_GM0 + 4 * GROUP)

    @pl.when(step % tiles_per_seq == 0)
    def _():
        carry_ref[...] = jnp.zeros_like(carry_ref)

    prev = carry_ref[...]
    p1, p2 = prev[7:8, :], prev[6:7, :]
    row = lax.broadcasted_iota(jnp.int32, u.shape, 0)
    um1 = jnp.where(row == 0, p1, pltpu.roll(u, 1, 0))
    um2 = jnp.where(row == 0, p2, jnp.where(row == 1, p1, pltpu.roll(u, 2, 0)))
    cv_out[...] = _conv_gate(proj, um1, um2, u, wconv_ref).astype(BF16)
    carry_ref[...] = u[tm - 8:tm, :]
    conv_out[0] = u[tm - (CONV_W - 1):tm, :]


def _const_spec(shape):
    nd = len(shape)
    return pl.BlockSpec(shape, lambda *_: (0,) * nd, pipeline_mode=pl.Buffered(1))


def _proj_prompt(x2d, seq, gpre, w1, gq, wq, gkv, wk, wv, wconv, tabs, tm):
    t, d = x2d.shape
    nb = t // seq
    hs = N_HEADS * HEAD_SLAB
    row = lambda w: pl.BlockSpec((tm, w), lambda i: (i, 0))
    tab = pl.BlockSpec((tm, LANES), lambda i: (i % (seq // tm), 0))
    out_shape = (
        jax.ShapeDtypeStruct((t, hs), BF16),
        jax.ShapeDtypeStruct((t, hs), BF16),
        jax.ShapeDtypeStruct((t, hs), BF16),
        jax.ShapeDtypeStruct((t, gkv.shape[1]), F32),
        jax.ShapeDtypeStruct((t, QK_ROPE), F32),
        jax.ShapeDtypeStruct((t, GROUP), BF16),
        jax.ShapeDtypeStruct((t, GROUP), BF16),
        jax.ShapeDtypeStruct((nb, CONV_W - 1, GROUP), F32),
    )
    return pl.pallas_call(
        functools.partial(_proj_prompt_kernel, tiles_per_seq=seq // tm),
        grid=(t // tm,),
        in_specs=[row(d), _const_spec(gpre.shape), _const_spec(w1.shape), _const_spec(gq.shape),
                  _const_spec(wq.shape), _const_spec(gkv.shape), _const_spec(wk.shape), _const_spec(wv.shape),
                  _const_spec(wconv.shape), tab, tab, tab, tab],
        out_specs=(row(hs), row(hs), row(hs), row(gkv.shape[1]), row(QK_ROPE), row(GROUP), row(GROUP),
                   pl.BlockSpec((1, CONV_W - 1, GROUP), lambda i: (i // (seq // tm), 0, 0))),
        out_shape=out_shape,
        scratch_shapes=[pltpu.VMEM((8, GROUP), F32)],
        compiler_params=pltpu.CompilerParams(dimension_semantics=("arbitrary",),
                                             vmem_limit_bytes=VMEM_LIMIT_BYTES),
        name="proj_prompt",
    )(x2d, gpre, w1, gq, wq, gkv, wk, wv, wconv, *tabs)


def _attn_kernel(q_ref, k_ref, v_ref, o_ref, *, tq, tk):
    i = pl.program_id(1)
    dn = (((1,), (1,)), ((), ()))
    n_diag = tq // tk
    qpos = i * tq + lax.broadcasted_iota(jnp.int32, (tq, tk), 0)
    kcol = lax.broadcasted_iota(jnp.int32, (tq, tk), 1)
    lane = lax.broadcasted_iota(jnp.int32, (tq, LANES), 1)

    def head(h):
        sl = slice(h * HEAD_SLAB, (h + 1) * HEAD_SLAB)
        q = q_ref[:, sl]

        def tile(j, carry, masked):
            m, l, acc = carry
            rows = pl.ds(pl.multiple_of(j * tk, tk), tk)
            s = lax.dot_general(q, k_ref[rows, sl], dn, preferred_element_type=F32)
            if masked:
                s = jnp.where(j * tk + kcol <= qpos, s, -jnp.inf)
            m_new = jnp.maximum(m, jnp.max(s, axis=-1, keepdims=True))
            alpha = jnp.exp(m - m_new)
            p = jnp.exp(s - m_new)
            l = alpha * l + jnp.sum(p, axis=-1, keepdims=True)
            acc = alpha * acc + jnp.dot(p.astype(BF16), v_ref[rows, sl], preferred_element_type=F32)
            return m_new, l, acc

        carry = (jnp.full((tq, 1), -jnp.inf, F32), jnp.zeros((tq, 1), F32), jnp.zeros((tq, LANES), F32))
        n_full = i * n_diag
        for d in range(n_diag):
            carry = tile(n_full + d, carry, True)
        m, l, acc = lax.fori_loop(0, n_full, lambda j, c: tile(j, c, False), carry)
        return acc * (1.0 / l)

    for hp in range(N_HEADS // 2):
        lo, hi = head(2 * hp), head(2 * hp + 1)
        pair = jnp.where(lane < V_HEAD, lo, pltpu.roll(hi, V_HEAD, 1))
        o_ref[:, hp * LANES:(hp + 1) * LANES] = pair.astype(o_ref.dtype)


def _attention(q, k, v, nb, seq, tq, tk):
    hs = q.shape[1]
    return pl.pallas_call(
        functools.partial(_attn_kernel, tq=tq, tk=tk),
        grid=(nb, seq // tq),
        in_specs=[pl.BlockSpec((tq, hs), lambda b, i: (b * (seq // tq) + i, 0)),
                  pl.BlockSpec((seq, hs), lambda b, i: (b, 0)),
                  pl.BlockSpec((seq, hs), lambda b, i: (b, 0))],
        out_specs=pl.BlockSpec((tq, N_HEADS * V_HEAD), lambda b, i: (b * (seq // tq) + i, 0)),
        out_shape=jax.ShapeDtypeStruct((q.shape[0], N_HEADS * V_HEAD), BF16),
        compiler_params=pltpu.CompilerParams(dimension_semantics=("arbitrary", "arbitrary"),
                                             vmem_limit_bytes=VMEM_LIMIT_BYTES),
        name="prompt_attn",
    )(q, k, v)


def _out_prompt_kernel(o_ref, gm_ref, cv_ref, x_ref, wo_ref, gpost_ref, y_ref):
    half = o_ref.shape[1]
    mla = (o_ref[...].astype(F32) * gm_ref[...].astype(F32)).astype(BF16)
    y = jnp.dot(mla, wo_ref[0:half, :], preferred_element_type=F32)
    y = y + jnp.dot(cv_ref[...], wo_ref[half:, :], preferred_element_type=F32)
    y_ref[...] = x_ref[...] + _rms(y, gpost_ref[...])


def _out_prompt(o, gm, cv, x2d, wo, gpost, tm):
    t, d = x2d.shape
    row = lambda w: pl.BlockSpec((tm, w), lambda i: (i, 0))
    return pl.pallas_call(
        _out_prompt_kernel,
        grid=(t // tm,),
        in_specs=[row(o.shape[1]), row(gm.shape[1]), row(cv.shape[1]), row(d),
                  _const_spec(wo.shape), _const_spec(gpost.shape)],
        out_specs=row(d),
        out_shape=jax.ShapeDtypeStruct((t, d), F32),
        compiler_params=pltpu.CompilerParams(dimension_semantics=("arbitrary",),
                                             vmem_limit_bytes=VMEM_LIMIT_BYTES),
        name="out_prompt",
    )(o, gm, cv, x2d, wo, gpost)


def _proj_sample_kernel(x_ref, gpre_ref, w1_ref, gq_ref, wq_ref, gkv_ref, wukt_ref, wconv_ref,
                        cq1_ref, cq2_ref, ck1_ref, ck2_ref, s0_ref, s1_ref,
                        ql_out, qp_out, ckv_out, kpe_out, gm_out, cv_out, conv_out, *, dec_seq):
    proj, q_heads, ckv, kpe_blk = _project_common(
        x_ref, gpre_ref, w1_ref, gq_ref, wq_ref, gkv_ref, cq1_ref, cq2_ref, ck1_ref, ck2_ref)
    for h in range(N_HEADS):
        qh = q_heads[h]
        ql_out[h] = jnp.dot(qh.astype(BF16), wukt_ref[h], preferred_element_type=F32).astype(BF16)
        qp_out[h] = pltpu.roll(qh, LANES - QK_NOPE, 1)[:, 0:QK_ROPE].astype(BF16)
    ckv_out[...] = ckv
    kpe_out[...] = kpe_blk[:, QK_NOPE:QK_NOPE + QK_ROPE]
    gm_out[...] = _silu(proj(C_GM0, C_GM0 + GROUP))

    u = proj(C_GM0 + 2 * GROUP, C_GM0 + 3 * GROUP) * proj(C_GM0 + 3 * GROUP, C_GM0 + 4 * GROUP)
    t_in_seq = lax.broadcasted_iota(jnp.int32, u.shape, 0) % dec_seq
    s0, s1 = s0_ref[...], s1_ref[...]
    um1 = jnp.where(t_in_seq == 0, s1, pltpu.roll(u, 1, 0))
    um2 = jnp.where(t_in_seq == 0, s0, jnp.where(t_in_seq == 1, s1, pltpu.roll(u, 2, 0)))
    cv_out[...] = _conv_gate(proj, um1, um2, u, wconv_ref)
    conv_out[...] = u


def _proj_sample(x2d, dec_seq, gpre, w1, gq, wq, gkv, wukt, wconv, tabs, s0, s1):
    t, d = x2d.shape
    c = gkv.shape[1]
    out_shape = (
        jax.ShapeDtypeStruct((N_HEADS, t, c), BF16),
        jax.ShapeDtypeStruct((N_HEADS, t, QK_ROPE), BF16),
        jax.ShapeDtypeStruct((t, c), F32),
        jax.ShapeDtypeStruct((t, QK_ROPE), F32),
        jax.ShapeDtypeStruct((t, GROUP), F32),
        jax.ShapeDtypeStruct((t, GROUP), F32),
        jax.ShapeDtypeStruct((t, GROUP), F32),
    )
    return pl.pallas_call(
        functools.partial(_proj_sample_kernel, dec_seq=dec_seq),
        out_shape=out_shape,
        compiler_params=pltpu.CompilerParams(vmem_limit_bytes=VMEM_LIMIT_BYTES),
        name="proj_sample",
    )(x2d, gpre, w1, gq, wq, gkv, wukt, wconv, *tabs, s0, s1)


TOK_GROUP = LANES // QK_ROPE


def _decode_attn_kernel(pt_ref, qlt_ref, qpt_ref, ql_ref, qp_ref, cnew_ref, knew_ref, y_hbm, x_hbm, o_ref,
                        ybuf, xbuf, sem, m_ref, l_ref, acc_ref, *, pages, layer, dec_seq):
    b = pl.program_id(0)
    n_b = pl.num_programs(0)
    n_pages = pt_ref.shape[1]
    n_chunks = n_pages // pages
    prow = ybuf.shape[1] // pages
    c = ybuf.shape[2] // TOK_GROUP
    rows = ql_ref.shape[1]
    dn = (((1,), (1,)), ((), ()))

    def chunk_copies(bb, ch, slot, page_ids):
        cps = []
        for i in range(pages):
            pid = page_ids(bb, ch * pages + i)
            dst = pl.ds(i * prow, prow)
            cps.append(pltpu.make_async_copy(y_hbm.at[layer, pid], ybuf.at[slot, dst, :], sem.at[0, slot]))
            cps.append(pltpu.make_async_copy(x_hbm.at[layer, pid], xbuf.at[slot, dst, :], sem.at[1, slot]))
        return cps

    def start(bb, ch, slot):
        for cp in chunk_copies(bb, ch, slot, lambda r, j: pt_ref[r, j]):
            cp.start()

    def wait(slot):
        for cp in chunk_copies(0, 0, slot, lambda r, j: 0):
            cp.wait()

    @pl.when(b == 0)
    def _():
        start(0, 0, 0)

    m_ref[...] = jnp.full(m_ref.shape, -jnp.inf, F32)
    l_ref[...] = jnp.zeros(l_ref.shape, F32)
    acc_ref[...] = jnp.zeros(acc_ref.shape, F32)
    qlt, qpt = qlt_ref[0], qpt_ref[0]

    def compute(slot):
        yb = ybuf[slot].astype(BF16)
        xb = xbuf[slot].astype(BF16)
        st = (lax.dot_general(qlt, yb, dn, preferred_element_type=F32)
              + lax.dot_general(qpt, xb, dn, preferred_element_type=F32))
        m_old = m_ref[...]
        m_new = jnp.maximum(m_old, jnp.max(st, axis=-1, keepdims=True))
        alpha = jnp.exp(m_old - m_new)
        p = jnp.exp(st - m_new)
        l_ref[...] = alpha * l_ref[...] + jnp.sum(p, axis=-1, keepdims=True)
        m_ref[...] = m_new
        pb = p.astype(BF16)
        for g in range(TOK_GROUP):
            rs = slice(g * rows, (g + 1) * rows)
            pv = jnp.dot(pb[rs, :], yb[:, g * c:(g + 1) * c], preferred_element_type=F32)
            acc_ref[rs, :] = alpha[rs, :] * acc_ref[rs, :] + pv

    def pair(it, carry):
        ch = 2 * it
        start(b, ch + 1, 1)
        wait(0)
        compute(0)

        @pl.when(ch + 2 < n_chunks)
        def _():
            start(b, ch + 2, 0)

        @pl.when((ch + 2 >= n_chunks) & (b + 1 < n_b))
        def _():
            start(b + 1, 0, 0)

        wait(1)
        compute(1)
        return carry

    lax.fori_loop(0, n_chunks // 2, pair, 0)

    qlf, qpf = ql_ref[0].astype(F32), qp_ref[0].astype(F32)
    tok = lax.broadcasted_iota(jnp.int32, (rows, 1), 0) // N_HEADS
    s_new = []
    for j in range(dec_seq):
        sj = (jnp.sum(qlf * cnew_ref[0, j:j + 1, :], axis=-1, keepdims=True)
              + jnp.sum(qpf * knew_ref[0, j:j + 1, :], axis=-1, keepdims=True))
        s_new.append(jnp.where(tok >= j, sj, -jnp.inf))
    m_n = functools.reduce(jnp.maximum, s_new)
    p_n = [jnp.exp(sj - m_n) for sj in s_new]
    l_n = functools.reduce(lambda a, e: a + e, p_n)
    acc_n = functools.reduce(lambda a, e: a + e, [p_n[j] * cnew_ref[0, j:j + 1, :] for j in range(dec_seq)])

    ms = [m_ref[g * rows:(g + 1) * rows, :] for g in range(TOK_GROUP)] + [m_n]
    ls = [l_ref[g * rows:(g + 1) * rows, :] for g in range(TOK_GROUP)] + [l_n]
    accs = [acc_ref[g * rows:(g + 1) * rows, :] for g in range(TOK_GROUP)] + [acc_n]
    m_t = functools.reduce(jnp.maximum, ms)
    ws = [jnp.exp(m - m_t) for m in ms]
    l_t = functools.reduce(lambda a, e: a + e, [w * l for w, l in zip(ws, ls)])
    o_t = functools.reduce(lambda a, e: a + e, [w * a for w, a in zip(ws, accs)])
    o_ref[0] = o_t * (1.0 / l_t)


def _decode_attention(page_table, ql, qp, cnew, knew, cache_ckv, cache_kpe, layer, pages):
    nb, rows, c = ql.shape
    dec_seq = cnew.shape[1]
    depth, n_phys, page, r = cache_kpe.shape
    g = TOK_GROUP
    assert page % g == 0 and r * g == LANES and (page_table.shape[1] // pages) % 2 == 0
    yv = cache_ckv.reshape(depth, n_phys, page // g, g * c)
    xv = cache_kpe.reshape(depth, n_phys, page // g, g * r)
    eye = jnp.eye(g, dtype=jnp.bool_)[None, :, None, :, None]
    qlt = jnp.where(eye, ql[:, None, :, None, :], 0).reshape(nb, g * rows, g * c)
    qpt = jnp.where(eye, qp[:, None, :, None, :], 0).reshape(nb, g * rows, g * r)
    per_b = lambda a: pl.BlockSpec((1,) + a.shape[1:], lambda b, pt: (b, 0, 0))
    chunk_rows = pages * (page // g)
    grid_spec = pltpu.PrefetchScalarGridSpec(
        num_scalar_prefetch=1,
        grid=(nb,),
        in_specs=[per_b(qlt), per_b(qpt), per_b(ql), per_b(qp), per_b(cnew), per_b(knew),
                  pl.BlockSpec(memory_space=pl.ANY), pl.BlockSpec(memory_space=pl.ANY)],
        out_specs=pl.BlockSpec((1, rows, c), lambda b, pt: (b, 0, 0)),
        scratch_shapes=[pltpu.VMEM((2, chunk_rows, g * c), F32), pltpu.VMEM((2, chunk_rows, g * r), F32),
                        pltpu.SemaphoreType.DMA((2, 2)),
                        pltpu.VMEM((g * rows, 1), F32), pltpu.VMEM((g * rows, 1), F32),
                        pltpu.VMEM((g * rows, c), F32)],
    )
    return pl.pallas_call(
        functools.partial(_decode_attn_kernel, pages=pages, layer=layer, dec_seq=dec_seq),
        grid_spec=grid_spec,
        out_shape=jax.ShapeDtypeStruct((nb, rows, c), F32),
        compiler_params=pltpu.CompilerParams(dimension_semantics=("arbitrary",),
                                             vmem_limit_bytes=VMEM_LIMIT_BYTES),
        name="decode_attn",
    )(page_table, qlt, qpt, ql, qp, cnew, knew, yv, xv)


def _out_sample_kernel(ol_ref, gm_ref, cv_ref, x_ref, wuv_ref, wo_ref, gpost_ref, y_ref):
    half = gm_ref.shape[1]
    o = jnp.concatenate(
        [jnp.dot(ol_ref[h].astype(BF16), wuv_ref[h], preferred_element_type=F32) for h in range(N_HEADS)], axis=1)
    mla = (o * gm_ref[...]).astype(BF16)
    y = jnp.dot(mla, wo_ref[0:half, :], preferred_element_type=F32)
    y = y + jnp.dot(cv_ref[...].astype(BF16), wo_ref[half:, :], preferred_element_type=F32)
    y_ref[...] = x_ref[...] + _rms(y, gpost_ref[...])


def _out_sample(ol, gm, cv, x2d, wuv, wo, gpost):
    return pl.pallas_call(
        _out_sample_kernel,
        out_shape=jax.ShapeDtypeStruct(x2d.shape, F32),
        compiler_params=pltpu.CompilerParams(vmem_limit_bytes=VMEM_LIMIT_BYTES),
        name="out_sample",
    )(ol, gm, cv, x2d, wuv, wo, gpost)


def _rot_cols(w):
    r = w.shape[-1] // 2
    return jnp.concatenate([-w[..., r:], w[..., :r]], axis=-1)


def _pack_weights(w_in, w_uq, w_ukv, w_out):
    d, q_lora, kv_lora = w_in.shape[0], w_uq.shape[0], w_ukv.shape[0]
    cq, ckv, kpe, rest = (w_in[:, :C_Q1], w_in[:, C_KV0:C_KV1], w_in[:, C_KV1:C_KV1 + QK_ROPE],
                          w_in[:, C_KV1 + QK_ROPE:])
    w1 = jnp.concatenate([cq, ckv, jnp.zeros((d, QK_NOPE), F32), kpe, _rot_cols(kpe), rest], axis=1).astype(BF16)
    uq = w_uq.reshape(q_lora, N_HEADS, QK_NOPE + QK_ROPE)
    pe = uq[:, :, QK_NOPE:]
    wq = jnp.concatenate([uq[:, :, :QK_NOPE], pe, _rot_cols(pe)], axis=-1).reshape(q_lora, N_HEADS * HEAD_SLAB)
    ukv = w_ukv.reshape(kv_lora, N_HEADS, QK_NOPE + V_HEAD)
    uk, uv = ukv[:, :, :QK_NOPE], ukv[:, :, QK_NOPE:]
    pad = jnp.zeros((kv_lora, N_HEADS, HEAD_SLAB - QK_NOPE), F32)
    wk = jnp.concatenate([uk, pad], axis=-1).reshape(kv_lora, N_HEADS * HEAD_SLAB)
    wv = jnp.concatenate([uv, pad], axis=-1).reshape(kv_lora, N_HEADS * HEAD_SLAB)
    wukt = jnp.concatenate([uk.transpose(1, 2, 0), jnp.zeros((N_HEADS, HEAD_SLAB - QK_NOPE, kv_lora), F32)], axis=1)
    wuv = uv.transpose(1, 0, 2)
    return (w1, wq.astype(BF16), wk.astype(BF16), wv.astype(BF16), wukt.astype(BF16), wuv.astype(BF16),
            w_out.astype(BF16))


PROMPT_TILE = 512
ATTN_TQ = 512
ATTN_TK = 512
DECODE_PAGES = 32


def kernel(x_prompt, x_sample, cache_ckv, cache_kpe, state_conv, page_table, g_pre, w_in, g_qnorm, w_uq,
           g_kvnorm, w_ukv, w_conv, w_out, g_post):
    depth = w_in.shape[0]
    nb, seq, d = x_prompt.shape
    db, dec_seq, _ = x_sample.shape
    past_len = page_table.shape[1] * cache_ckv.shape[2]
    c = cache_ckv.shape[3]

    tabs_p = _rope_tables(jnp.arange(seq, dtype=jnp.int32))
    tabs_s = _rope_tables(jnp.tile(past_len + jnp.arange(dec_seq, dtype=jnp.int32), db))

    xp = x_prompt.reshape(nb * seq, d)
    xs = x_sample.reshape(db * dec_seq, d)
    outs = [[] for _ in range(6)]
    for l in range(depth):
        w1, wq, wk, wv, wukt, wuv, wo = _pack_weights(w_in[l], w_uq[l], w_ukv[l], w_out[l])
        gpre, gq, gkv, gpost = g_pre[l][None], g_qnorm[l][None], g_kvnorm[l][None], g_post[l][None]

        q, k, v, ckv_p, kpe_p, gm, cv, conv_p = _proj_prompt(
            xp, seq, gpre, w1, gq, wq, gkv, wk, wv, w_conv[l], tabs_p, PROMPT_TILE)
        o = _attention(q, k, v, nb, seq, ATTN_TQ, ATTN_TK)
        xp = _out_prompt(o, gm, cv, xp, wo, gpost, PROMPT_TILE)
        outs[0].append(ckv_p.reshape(nb, seq, c))
        outs[1].append(kpe_p.reshape(nb, seq, QK_ROPE))
        outs[2].append(conv_p)

        st = state_conv[l].astype(F32)
        s0 = jnp.repeat(st[:, 0], dec_seq, axis=0)
        s1 = jnp.repeat(st[:, 1], dec_seq, axis=0)
        ql, qp, ckv_s, kpe_s, gm_s, cv_s, u_s = _proj_sample(
            xs, dec_seq, gpre, w1, gq, wq, gkv, wukt, w_conv[l], tabs_s, s0, s1)
        rows = dec_seq * N_HEADS
        to_rows = lambda a: a.reshape(N_HEADS, db, dec_seq, -1).transpose(1, 2, 0, 3).reshape(db, rows, -1)
        ol = _decode_attention(page_table, to_rows(ql), to_rows(qp), ckv_s.reshape(db, dec_seq, c),
                               kpe_s.reshape(db, dec_seq, QK_ROPE), cache_ckv, cache_kpe, l, DECODE_PAGES)
        ol = ol.reshape(db, dec_seq, N_HEADS, c).transpose(2, 0, 1, 3).reshape(N_HEADS, db * dec_seq, c)
        xs = _out_sample(ol, gm_s, cv_s, xs, wuv, wo, gpost)
        outs[3].append(ckv_s.reshape(db, dec_seq, c))
        outs[4].append(kpe_s.reshape(db, dec_seq, QK_ROPE))
        u_pad = jnp.concatenate([st, u_s.reshape(db, dec_seq, GROUP)], axis=1)
        outs[5].append(u_pad[:, -(CONV_W - 1):])

    return (xp.reshape(nb, seq, d), xs.reshape(db, dec_seq, d), *[jnp.stack(o_) for o_ in outs])
```

```python
import functools

import jax
import jax.numpy as jnp
from jax import lax
from jax.experimental import pallas as pl
from jax.experimental.pallas import tpu as pltpu

N_HEADS = 8
QK_NOPE = 64
QK_ROPE = 32
V_HEAD = 64
CONV_W = 3
ROPE_THETA = 10000.0
EPS = 1e-6
ATTN_SCALE = (QK_NOPE + QK_ROPE) ** -0.5

LANES = 128
HEAD_SLAB = LANES
VMEM_LIMIT_BYTES = 56 * 1024 * 1024

F32 = jnp.float32
BF16 = jnp.bfloat16


def _silu(x):
    return x * (1.0 / (1.0 + jnp.exp(-x)))


def _rms(x, g):
    return x * lax.rsqrt(jnp.mean(x * x, axis=-1, keepdims=True) + EPS) * g


def _rope_table_kernel(pos_ref, invf_ref, cq1_ref, cq2_ref, ck1_ref, ck2_ref):
    ang = pos_ref[...] * invf_ref[...]
    lane = lax.broadcasted_iota(jnp.int32, ang.shape, 1)
    rope = (lane >= QK_NOPE) & (lane < QK_NOPE + QK_ROPE)
    c = jnp.where(rope, jnp.cos(ang), 0.0)
    s = jnp.where(rope, jnp.sin(ang), 0.0)
    cq1_ref[...] = jnp.where(lane < QK_NOPE, ATTN_SCALE, c * ATTN_SCALE)
    cq2_ref[...] = s * ATTN_SCALE
    ck1_ref[...] = c
    ck2_ref[...] = s


def _rope_tables(pos):
    t = pos.shape[0]
    r = QK_ROPE
    inv_freq = ROPE_THETA ** (-jnp.arange(0, r, 2, dtype=F32) / r)
    invf = jnp.zeros((1, LANES), F32).at[0, QK_NOPE:QK_NOPE + r].set(jnp.concatenate([inv_freq, inv_freq]))
    out = jax.ShapeDtypeStruct((t, LANES), F32)
    return pl.pallas_call(
        _rope_table_kernel,
        out_shape=(out, out, out, out),
        name="rope_tables",
    )(pos.astype(F32).reshape(t, 1), invf)


C_Q0, C_Q1 = 0, 384
C_KV0, C_KV1 = 384, 640
C_KP0, C_KP1 = 640, 768
C_GM0 = 768
GROUP = 512
D_IN_PACKED = C_GM0 + 5 * GROUP


def _rope_slab(blk, c1, c2):
    return blk * c1 + pltpu.roll(blk, LANES - QK_ROPE, 1) * c2


def _project_common(x_ref, gpre_ref, w1_ref, gq_ref, wq_ref, gkv_ref, cq1_ref, cq2_ref, ck1_ref, ck2_ref):
    x = x_ref[...]
    xn = _rms(x, gpre_ref[...]).astype(BF16)

    def proj(c0, c1):
        return jnp.dot(xn, w1_ref[:, c0:c1], preferred_element_type=F32)

    cqn = _rms(proj(C_Q0, C_Q1), gq_ref[...]).astype(BF16)
    qraw = jnp.dot(cqn, wq_ref[...], preferred_element_type=F32)
    cq1, cq2 = cq1_ref[...], cq2_ref[...]
    q_heads = [_rope_slab(qraw[:, h * HEAD_SLAB:(h + 1) * HEAD_SLAB], cq1, cq2) for h in range(N_HEADS)]
    ckv = _rms(proj(C_KV0, C_KV1), gkv_ref[...])
    kpe_blk = _rope_slab(proj(C_KP0, C_KP1), ck1_ref[...], ck2_ref[...])
    return proj, q_heads, ckv, kpe_blk


def _conv_gate(proj, um1, um2, u, wconv_ref):
    w = wconv_ref[...]
    y = w[0:1, :] * um2 + w[1:2, :] * um1 + w[2:3, :] * u
    bg = proj(C_GM0 + GROUP, C_GM0 + 2 * GROUP)
    gc = proj(C_GM0 + 4 * GROUP, C_GM0 + 5 * GROUP)
    return bg * y * _silu(gc)


def _proj_prompt_kernel(x_ref, gpre_ref, w1_ref, gq_ref, wq_ref, gkv_ref, wk_ref, wv_ref, wconv_ref,
                        cq1_ref, cq2_ref, ck1_ref, ck2_ref,
                        q_out, k_out, v_out, ckv_out, kpe_out, gm_out, cv_out, conv_out,
                        carry_ref, *, tiles_per_seq):
    tm = x_ref.shape[0]
    step = pl.program_id(0)
    proj, q_heads, ckv, kpe_blk = _project_common(
        x_ref, gpre_ref, w1_ref, gq_ref, wq_ref, gkv_ref, cq1_ref, cq2_ref, ck1_ref, ck2_ref)
    for h in range(N_HEADS):
        q_out[:, h * HEAD_SLAB:(h + 1) * HEAD_SLAB] = q_heads[h].astype(BF16)
    ckv_out[...] = ckv
    kpe_out[...] = kpe_blk[:, QK_NOPE:QK_NOPE + QK_ROPE]
    ckvb = ckv.astype(BF16)
    kfull = jnp.dot(ckvb, wk_ref[...], preferred_element_type=F32)
    for h in range(N_HEADS):
        k_out[:, h * HEAD_SLAB:(h + 1) * HEAD_SLAB] = (kfull[:, h * HEAD_SLAB:(h + 1) * HEAD_SLAB] + kpe_blk).astype(BF16)
    v_out[...] = jnp.dot(ckvb, wv_ref[...], preferred_element_type=F32).astype(BF16)
    gm_out[...] = _silu(proj(C_GM0, C_GM0 + GROUP)).astype(BF16)

    u = proj(C_GM0 + 2 * GROUP, C_GM0 + 3 * GROUP) * proj(C_GM0 + 3 * GROUP, C_GM0 + 4 * GROUP)

    @pl.when(step % tiles_per_seq == 0)
    def _():
        carry_ref[...] = jnp.zeros_like(carry_ref)

    prev = carry_ref[...]
    p1, p2 = prev[7:8, :], prev[6:7, :]
    row = lax.broadcasted_iota(jnp.int32, u.shape, 0)
    um1 = jnp.where(row == 0, p1, pltpu.roll(u, 1, 0))
    um2 = jnp.where(row == 0, p2, jnp.where(row == 1, p1, pltpu.roll(u, 2, 0)))
    cv_out[...] = _conv_gate(proj, um1, um2, u, wconv_ref).astype(BF16)
    carry_ref[...] = u[tm - 8:tm, :]
    conv_out[0] = u[tm - (CONV_W - 1):tm, :]


def _const_spec(shape):
    nd = len(shape)
    return pl.BlockSpec(shape, lambda *_: (0,) * nd, pipeline_mode=pl.Buffered(1))


def _proj_prompt(x2d, seq, gpre, w1, gq, wq, gkv, wk, wv, wconv, tabs, tm):
    t, d = x2d.shape
    nb = t // seq
    hs = N_HEADS * HEAD_SLAB
    row = lambda w: pl.BlockSpec((tm, w), lambda i: (i, 0))
    tab = pl.BlockSpec((tm, LANES), lambda i: (i % (seq // tm), 0))
    out_shape = (
        jax.ShapeDtypeStruct((t, hs), BF16),
        jax.ShapeDtypeStruct((t, hs), BF16),
        jax.ShapeDtypeStruct((t, hs), BF16),
        jax.ShapeDtypeStruct((t, gkv.shape[1]), F32),
        jax.ShapeDtypeStruct((t, QK_ROPE), F32),
        jax.ShapeDtypeStruct((t, GROUP), BF16),
        jax.ShapeDtypeStruct((t, GROUP), BF16),
        jax.ShapeDtypeStruct((nb, CONV_W - 1, GROUP), F32),
    )
    return pl.pallas_call(
        functools.partial(_proj_prompt_kernel, tiles_per_seq=seq // tm),
        grid=(t // tm,),
        in_specs=[row(d), _const_spec(gpre.shape), _const_spec(w1.shape), _const_spec(gq.shape),
                  _const_spec(wq.shape), _const_spec(gkv.shape), _const_spec(wk.shape), _const_spec(wv.shape),
                  _const_spec(wconv.shape), tab, tab, tab, tab],
        out_specs=(row(hs), row(hs), row(hs), row(gkv.shape[1]), row(QK_ROPE), row(GROUP), row(GROUP),
                   pl.BlockSpec((1, CONV_W - 1, GROUP), lambda i: (i // (seq // tm), 0, 0))),
        out_shape=out_shape,
        scratch_shapes=[pltpu.VMEM((8, GROUP), F32)],
        compiler_params=pltpu.CompilerParams(dimension_semantics=("arbitrary",),
                                             vmem_limit_bytes=VMEM_LIMIT_BYTES),
        name="proj_prompt",
    )(x2d, gpre, w1, gq, wq, gkv, wk, wv, wconv, *tabs)


def _attn_kernel(q_ref, k_ref, v_ref, o_ref, *, tq, tk):
    i = pl.program_id(1)
    dn = (((1,), (1,)), ((), ()))
    n_diag = tq // tk
    qpos = i * tq + lax.broadcasted_iota(jnp.int32, (tq, tk), 0)
    kcol = lax.broadcasted_iota(jnp.int32, (tq, tk), 1)
    lane = lax.broadcasted_iota(jnp.int32, (tq, LANES), 1)

    def head(h):
        sl = slice(h * HEAD_SLAB, (h + 1) * HEAD_SLAB)
        q = q_ref[:, sl]

        def tile(j, carry, masked):
            m, l, acc = carry
            rows = pl.ds(pl.multiple_of(j * tk, tk), tk)
            s = lax.dot_general(q, k_ref[rows, sl], dn, preferred_element_type=F32)
            if masked:
                s = jnp.where(j * tk + kcol <= qpos, s, -jnp.inf)
            m_new = jnp.maximum(m, jnp.max(s, axis=-1, keepdims=True))
            alpha = jnp.exp(m - m_new)
            p = jnp.exp(s - m_new)
            l = alpha * l + jnp.sum(p, axis=-1, keepdims=True)
            acc = alpha * acc + jnp.dot(p.astype(BF16), v_ref[rows, sl], preferred_element_type=F32)
            return m_new, l, acc

        carry = (jnp.full((tq, 1), -jnp.inf, F32), jnp.zeros((tq, 1), F32), jnp.zeros((tq, LANES), F32))
        n_full = i * n_diag
        for d in range(n_diag):
            carry = tile(n_full + d, carry, True)
        m, l, acc = lax.fori_loop(0, n_full, lambda j, c: tile(j, c, False), carry)
        return acc * (1.0 / l)

    for hp in range(N_HEADS // 2):
        lo, hi = head(2 * hp), head(2 * hp + 1)
        pair = jnp.where(lane < V_HEAD, lo, pltpu.roll(hi, V_HEAD, 1))
        o_ref[:, hp * LANES:(hp + 1) * LANES] = pair.astype(o_ref.dtype)


def _attention(q, k, v, nb, seq, tq, tk):
    hs = q.shape[1]
    return pl.pallas_call(
        functools.partial(_attn_kernel, tq=tq, tk=tk),
        grid=(nb, seq // tq),
        in_specs=[pl.BlockSpec((tq, hs), lambda b, i: (b * (seq // tq) + i, 0)),
                  pl.BlockSpec((seq, hs), lambda b, i: (b, 0)),
                  pl.BlockSpec((seq, hs), lambda b, i: (b, 0))],
        out_specs=pl.BlockSpec((tq, N_HEADS * V_HEAD), lambda b, i: (b * (seq // tq) + i, 0)),
        out_shape=jax.ShapeDtypeStruct((q.shape[0], N_HEADS * V_HEAD), BF16),
        compiler_params=pltpu.CompilerParams(dimension_semantics=("arbitrary", "arbitrary"),
                                             vmem_limit_bytes=VMEM_LIMIT_BYTES),
        name="prompt_attn",
    )(q, k, v)


def _out_prompt_kernel(o_ref, gm_ref, cv_ref, x_ref, wo_ref, gpost_ref, y_ref):
    half = o_ref.shape[1]
    mla = (o_ref[...].astype(F32) * gm_ref[...].astype(F32)).astype(BF16)
    y = jnp.dot(mla, wo_ref[0:half, :], preferred_element_type=F32)
    y = y + jnp.dot(cv_ref[...], wo_ref[half:, :], preferred_element_type=F32)
    y_ref[...] = x_ref[...] + _rms(y, gpost_ref[...])


def _out_prompt(o, gm, cv, x2d, wo, gpost, tm):
    t, d = x2d.shape
    row = lambda w: pl.BlockSpec((tm, w), lambda i: (i, 0))
    return pl.pallas_call(
        _out_prompt_kernel,
        grid=(t // tm,),
        in_specs=[row(o.shape[1]), row(gm.shape[1]), row(cv.shape[1]), row(d),
                  _const_spec(wo.shape), _const_spec(gpost.shape)],
        out_specs=row(d),
        out_shape=jax.ShapeDtypeStruct((t, d), F32),
        compiler_params=pltpu.CompilerParams(dimension_semantics=("arbitrary",),
                                             vmem_limit_bytes=VMEM_LIMIT_BYTES),
        name="out_prompt",
    )(o, gm, cv, x2d, wo, gpost)


def _proj_sample_kernel(x_ref, gpre_ref, w1_ref, gq_ref, wq_ref, gkv_ref, wukt_ref, wconv_ref,
                        cq1_ref, cq2_ref, ck1_ref, ck2_ref, s0_ref, s1_ref,
                        ql_out, qp_out, ckv_out, kpe_out, gm_out, cv_out, conv_out, *, dec_seq):
    proj, q_heads, ckv, kpe_blk = _project_common(
        x_ref, gpre_ref, w1_ref, gq_ref, wq_ref, gkv_ref, cq1_ref, cq2_ref, ck1_ref, ck2_ref)
    for h in range(N_HEADS):
        qh = q_heads[h]
        ql_out[h] = jnp.dot(qh.astype(BF16), wukt_ref[h], preferred_element_type=F32).astype(BF16)
        qp_out[h] = pltpu.roll(qh, LANES - QK_NOPE, 1)[:, 0:QK_ROPE].astype(BF16)
    ckv_out[...] = ckv
    kpe_out[...] = kpe_blk[:, QK_NOPE:QK_NOPE + QK_ROPE]
    gm_out[...] = _silu(proj(C_GM0, C_GM0 + GROUP))

    u = proj(C_GM0 + 2 * GROUP, C_GM0 + 3 * GROUP) * proj(C_GM0 + 3 * GROUP, C_GM0 + 4 * GROUP)
    t_in_seq = lax.broadcasted_iota(jnp.int32, u.shape, 0) % dec_seq
    s0, s1 = s0_ref[...], s1_ref[...]
    um1 = jnp.where(t_in_seq == 0, s1, pltpu.roll(u, 1, 0))
    um2 = jnp.where(t_in_seq == 0, s0, jnp.where(t_in_seq == 1, s1, pltpu.roll(u, 2, 0)))
    cv_out[...] = _conv_gate(proj, um1, um2, u, wconv_ref)
    conv_out[...] = u


def _proj_sample(x2d, dec_seq, gpre, w1, gq, wq, gkv, wukt, wconv, tabs, s0, s1):
    t, d = x2d.shape
    c = gkv.shape[1]
    out_shape = (
        jax.ShapeDtypeStruct((N_HEADS, t, c), BF16),
        jax.ShapeDtypeStruct((N_HEADS, t, QK_ROPE), BF16),
        jax.ShapeDtypeStruct((t, c), F32),
        jax.ShapeDtypeStruct((t, QK_ROPE), F32),
        jax.ShapeDtypeStruct((t, GROUP), F32),
        jax.ShapeDtypeStruct((t, GROUP), F32),
        jax.ShapeDtypeStruct((t, GROUP), F32),
    )
    return pl.pallas_call(
        functools.partial(_proj_sample_kernel, dec_seq=dec_seq),
        out_shape=out_shape,
        compiler_params=pltpu.CompilerParams(vmem_limit_bytes=VMEM_LIMIT_BYTES),
        name="proj_sample",
    )(x2d, gpre, w1, gq, wq, gkv, wukt, wconv, *tabs, s0, s1)


def _decode_attn_kernel(pt_ref, ql_ref, qp_ref, cnew_ref, knew_ref, y_hbm, x_hbm, o_ref,
                        ybuf, xbuf, sem, m_ref, l_ref, acc_ref, *, pages, slots, sub, layer, dec_seq):
    b = pl.program_id(0)
    n_b = pl.num_programs(0)
    n_chunks = pt_ref.shape[1] // pages
    page = ybuf.shape[1] // pages
    rows = ql_ref.shape[1]
    dn = (((1,), (1,)), ((), ()))
    ahead = slots - 1

    def chunk_copies(bb, ch, slot, page_ids):
        cps = []
        for i in range(pages):
            pid = page_ids(bb, ch * pages + i)
            tok = pl.ds(i * page, page)
            cps.append(pltpu.make_async_copy(y_hbm.at[layer, pid], ybuf.at[slot, tok, :], sem.at[0, slot]))
            cps.append(pltpu.make_async_copy(x_hbm.at[layer, pid], xbuf.at[slot, :, tok], sem.at[1, slot]))
        return cps

    def start(bb, ch, slot):
        for cp in chunk_copies(bb, ch, slot, lambda r, j: pt_ref[r, j]):
            cp.start()

    def wait(slot):
        for cp in chunk_copies(0, 0, slot, lambda r, j: 0):
            cp.wait()

    @pl.when(b == 0)
    def _():
        for k in range(ahead):
            start(0, k, k)

    ql, qp = ql_ref[0], qp_ref[0]
    qlf, qpf = ql.astype(F32), qp.astype(F32)
    tok_of_row = lax.broadcasted_iota(jnp.int32, (rows, 1), 0) // N_HEADS
    s_new = []
    for j in range(dec_seq):
        sj = (jnp.sum(qlf * cnew_ref[0, j:j + 1, :], axis=-1, keepdims=True)
              + jnp.sum(qpf * knew_ref[0, j:j + 1, :], axis=-1, keepdims=True))
        s_new.append(jnp.where(tok_of_row >= j, sj, -jnp.inf))
    m_n = functools.reduce(jnp.maximum, s_new)
    p_n = [jnp.exp(sj - m_n) for sj in s_new]
    m_ref[...] = m_n
    l_ref[...] = functools.reduce(lambda a, e: a + e, p_n)
    acc_ref[0] = functools.reduce(lambda a, e: a + e,
                                  [p_n[j] * cnew_ref[0, j:j + 1, :] for j in range(dec_seq)])
    acc_ref[1] = jnp.zeros(acc_ref.shape[1:], F32)

    def compute(slot):
        m, l = m_ref[...], l_ref[...]
        acc = [acc_ref[h] for h in range(2)]
        half = sub // 2
        for j in range(pages * page // sub):
            toks = [slice(j * sub + h * half, j * sub + (h + 1) * half) for h in range(2)]
            yb = [ybuf[slot, t, :].astype(BF16) for t in toks]
            xb = [xbuf[slot, :, t].astype(BF16) for t in toks]
            s = jnp.concatenate(
                [lax.dot_general(ql, yb[h], dn, preferred_element_type=F32)
                 + jnp.dot(qp, xb[h], preferred_element_type=F32) for h in range(2)], axis=1)
            m_new = jnp.maximum(m, jnp.max(s, axis=-1, keepdims=True))
            alpha = jnp.exp(m - m_new)
            p = jnp.exp(s - m_new)
            l = alpha * l + jnp.sum(p, axis=-1, keepdims=True)
            acc = [alpha * acc[h]
                   + jnp.dot(p[:, h * half:(h + 1) * half].astype(BF16), yb[h], preferred_element_type=F32)
                   for h in range(2)]
            m = m_new
        m_ref[...], l_ref[...] = m, l
        for h in range(2):
            acc_ref[h] = acc[h]

    def trip(it, carry):
        for k in range(slots):
            ch = it * slots + k
            nxt = ch + ahead
            nxt_slot = (k + ahead) % slots

            @pl.when(nxt < n_chunks)
            def _():
                start(b, nxt, nxt_slot)

            @pl.when((nxt >= n_chunks) & (b + 1 < n_b))
            def _():
                start(b + 1, nxt - n_chunks, nxt_slot)

            wait(k)
            compute(k)
        return carry

    lax.fori_loop(0, n_chunks // slots, trip, 0)
    o_ref[0] = (acc_ref[0] + acc_ref[1]) * (1.0 / l_ref[...])


def _decode_attention(page_table, ql, qp, cnew, knew, cache_ckv, cache_kpe, layer, pages, slots, sub):
    nb, rows, c = ql.shape
    dec_seq = cnew.shape[1]
    page, r = cache_kpe.shape[2:]
    n_chunks = page_table.shape[1] // pages
    assert n_chunks * pages == page_table.shape[1] and n_chunks % slots == 0 and slots - 1 <= n_chunks
    assert (pages * page) % sub == 0
    kpe_t = jnp.swapaxes(cache_kpe, 2, 3)
    per_b = lambda a: pl.BlockSpec((1,) + a.shape[1:], lambda b, pt: (b, 0, 0))
    grid_spec = pltpu.PrefetchScalarGridSpec(
        num_scalar_prefetch=1,
        grid=(nb,),
        in_specs=[per_b(ql), per_b(qp), per_b(cnew), per_b(knew),
                  pl.BlockSpec(memory_space=pl.ANY), pl.BlockSpec(memory_space=pl.ANY)],
        out_specs=pl.BlockSpec((1, rows, c), lambda b, pt: (b, 0, 0)),
        scratch_shapes=[pltpu.VMEM((slots, pages * page, c), F32), pltpu.VMEM((slots, r, pages * page), F32),
                        pltpu.SemaphoreType.DMA((2, slots)),
                        pltpu.VMEM((rows, 1), F32), pltpu.VMEM((rows, 1), F32), pltpu.VMEM((2, rows, c), F32)],
    )
    return pl.pallas_call(
        functools.partial(_decode_attn_kernel, pages=pages, slots=slots, sub=sub, layer=layer, dec_seq=dec_seq),
        grid_spec=grid_spec,
        out_shape=jax.ShapeDtypeStruct((nb, rows, c), F32),
        compiler_params=pltpu.CompilerParams(dimension_semantics=("arbitrary",),
                                             vmem_limit_bytes=VMEM_LIMIT_BYTES),
        name="decode_attn",
    )(page_table, ql, qp, cnew, knew, cache_ckv, kpe_t)


def _out_sample_kernel(ol_ref, gm_ref, cv_ref, x_ref, wuv_ref, wo_ref, gpost_ref, y_ref):
    half = gm_ref.shape[1]
    o = jnp.concatenate(
        [jnp.dot(ol_ref[h].astype(BF16), wuv_ref[h], preferred_element_type=F32) for h in range(N_HEADS)], axis=1)
    mla = (o * gm_ref[...]).astype(BF16)
    y = jnp.dot(mla, wo_ref[0:half, :], preferred_element_type=F32)
    y = y + jnp.dot(cv_ref[...].astype(BF16), wo_ref[half:, :], preferred_element_type=F32)
    y_ref[...] = x_ref[...] + _rms(y, gpost_ref[...])


def _out_sample(ol, gm, cv, x2d, wuv, wo, gpost):
    return pl.pallas_call(
        _out_sample_kernel,
        out_shape=jax.ShapeDtypeStruct(x2d.shape, F32),
        compiler_params=pltpu.CompilerParams(vmem_limit_bytes=VMEM_LIMIT_BYTES),
        name="out_sample",
    )(ol, gm, cv, x2d, wuv, wo, gpost)


def _rot_cols(w):
    r = w.shape[-1] // 2
    return jnp.concatenate([-w[..., r:], w[..., :r]], axis=-1)


def _pack_weights(w_in, w_uq, w_ukv, w_out):
    d, q_lora, kv_lora = w_in.shape[0], w_uq.shape[0], w_ukv.shape[0]
    cq, ckv, kpe, rest = (w_in[:, :C_Q1], w_in[:, C_KV0:C_KV1], w_in[:, C_KV1:C_KV1 + QK_ROPE],
                          w_in[:, C_KV1 + QK_ROPE:])
    w1 = jnp.concatenate([cq, ckv, jnp.zeros((d, QK_NOPE), F32), kpe, _rot_cols(kpe), rest], axis=1).astype(BF16)
    uq = w_uq.reshape(q_lora, N_HEADS, QK_NOPE + QK_ROPE)
    pe = uq[:, :, QK_NOPE:]
    wq = jnp.concatenate([uq[:, :, :QK_NOPE], pe, _rot_cols(pe)], axis=-1).reshape(q_lora, N_HEADS * HEAD_SLAB)
    ukv = w_ukv.reshape(kv_lora, N_HEADS, QK_NOPE + V_HEAD)
    uk, uv = ukv[:, :, :QK_NOPE], ukv[:, :, QK_NOPE:]
    pad = jnp.zeros((kv_lora, N_HEADS, HEAD_SLAB - QK_NOPE), F32)
    wk = jnp.concatenate([uk, pad], axis=-1).reshape(kv_lora, N_HEADS * HEAD_SLAB)
    wv = jnp.concatenate([uv, pad], axis=-1).reshape(kv_lora, N_HEADS * HEAD_SLAB)
    wukt = jnp.concatenate([uk.transpose(1, 2, 0), jnp.zeros((N_HEADS, HEAD_SLAB - QK_NOPE, kv_lora), F32)], axis=1)
    wuv = uv.transpose(1, 0, 2)
    return (w1, wq.astype(BF16), wk.astype(BF16), wv.astype(BF16), wukt.astype(BF16), wuv.astype(BF16),
            w_out.astype(BF16))


PROMPT_TILE = 512
ATTN_TQ = 512
ATTN_TK = 512
DECODE_PAGES = 32
DECODE_SLOTS = 4
DECODE_SUB = 4096


def kernel(x_prompt, x_sample, cache_ckv, cache_kpe, state_conv, page_table, g_pre, w_in, g_qnorm, w_uq,
           g_kvnorm, w_ukv, w_conv, w_out, g_post):
    depth = w_in.shape[0]
    nb, seq, d = x_prompt.shape
    db, dec_seq, _ = x_sample.shape
    past_len = page_table.shape[1] * cache_ckv.shape[2]
    c = cache_ckv.shape[3]

    tabs_p = _rope_tables(jnp.arange(seq, dtype=jnp.int32))
    tabs_s = _rope_tables(jnp.tile(past_len + jnp.arange(dec_seq, dtype=jnp.int32), db))

    xp = x_prompt.reshape(nb * seq, d)
    xs = x_sample.reshape(db * dec_seq, d)
    outs = [[] for _ in range(6)]
    for l in range(depth):
        w1, wq, wk, wv, wukt, wuv, wo = _pack_weights(w_in[l], w_uq[l], w_ukv[l], w_out[l])
        gpre, gq, gkv, gpost = g_pre[l][None], g_qnorm[l][None], g_kvnorm[l][None], g_post[l][None]

        q, k, v, ckv_p, kpe_p, gm, cv, conv_p = _proj_prompt(
            xp, seq, gpre, w1, gq, wq, gkv, wk, wv, w_conv[l], tabs_p, PROMPT_TILE)
        o = _attention(q, k, v, nb, seq, ATTN_TQ, ATTN_TK)
        xp = _out_prompt(o, gm, cv, xp, wo, gpost, PROMPT_TILE)
        outs[0].append(ckv_p.reshape(nb, seq, c))
        outs[1].append(kpe_p.reshape(nb, seq, QK_ROPE))
        outs[2].append(conv_p)

        st = state_conv[l].astype(F32)
        s0 = jnp.repeat(st[:, 0], dec_seq, axis=0)
        s1 = jnp.repeat(st[:, 1], dec_seq, axis=0)
        ql, qp, ckv_s, kpe_s, gm_s, cv_s, u_s = _proj_sample(
            xs, dec_seq, gpre, w1, gq, wq, gkv, wukt, w_conv[l], tabs_s, s0, s1)
        rows = dec_seq * N_HEADS
        to_rows = lambda a: a.reshape(N_HEADS, db, dec_seq, -1).transpose(1, 2, 0, 3).reshape(db, rows, -1)
        ol = _decode_attention(page_table, to_rows(ql), to_rows(qp), ckv_s.reshape(db, dec_seq, c),
                               kpe_s.reshape(db, dec_seq, QK_ROPE), cache_ckv, cache_kpe, l,
                               DECODE_PAGES, DECODE_SLOTS, DECODE_SUB)
        ol = ol.reshape(db, dec_seq, N_HEADS, c).transpose(2, 0, 1, 3).reshape(N_HEADS, db * dec_seq, c)
        xs = _out_sample(ol, gm_s, cv_s, xs, wuv, wo, gpost)
        outs[3].append(ckv_s.reshape(db, dec_seq, c))
        outs[4].append(kpe_s.reshape(db, dec_seq, QK_ROPE))
        u_pad = jnp.concatenate([st, u_s.reshape(db, dec_seq, GROUP)], axis=1)
        outs[5].append(u_pad[:, -(CONV_W - 1):])

    return (xp.reshape(nb, seq, d), xs.reshape(db, dec_seq, d), *[jnp.stack(o_) for o_ in outs])
```

```python
import functools

import jax
import jax.numpy as jnp
from jax import lax
from jax.experimental import pallas as pl
from jax.experimental.pallas import tpu as pltpu

N_HEADS = 8
QK_NOPE = 64
QK_ROPE = 32
V_HEAD = 64
CONV_W = 3
ROPE_THETA = 10000.0
EPS = 1e-6
ATTN_SCALE = (QK_NOPE + QK_ROPE) ** -0.5

LANES = 128
HEAD_SLAB = LANES
ONES_ROWS = 16
LOG2_E = 1.4426950408889634
VMEM_LIMIT_BYTES = 56 * 1024 * 1024

F32 = jnp.float32
BF16 = jnp.bfloat16


def _silu(x):
    return x * (1.0 / (1.0 + jnp.exp(-x)))


def _rms(x, g):
    return x * lax.rsqrt(jnp.mean(x * x, axis=-1, keepdims=True) + EPS) * g


def _rope_coeffs(ang, slab_axis, q_scale):
    idx = lax.broadcasted_iota(jnp.int32, ang.shape, slab_axis)
    rope = (idx >= QK_NOPE) & (idx < QK_NOPE + QK_ROPE)
    c = jnp.where(rope, jnp.cos(ang), 0.0)
    s = jnp.where(rope, jnp.sin(ang), 0.0)
    return jnp.where(idx < QK_NOPE, q_scale, c * q_scale), s * q_scale, c, s


def _rope_table_kernel(pos_ref, invf_ref, post_ref, invft_ref, cq1_ref, cq2_ref, ck1_ref, ck2_ref,
                       cq1t_ref, cq2t_ref):
    cq1_ref[...], cq2_ref[...], ck1_ref[...], ck2_ref[...] = _rope_coeffs(
        pos_ref[...] * invf_ref[...], 1, ATTN_SCALE)
    cq1t_ref[...], cq2t_ref[...] = _rope_coeffs(invft_ref[...] * post_ref[...], 0, ATTN_SCALE * LOG2_E)[:2]


def _rope_tables(pos):
    t = pos.shape[0]
    r = QK_ROPE
    inv_freq = ROPE_THETA ** (-jnp.arange(0, r, 2, dtype=F32) / r)
    invf = jnp.zeros((1, LANES), F32).at[0, QK_NOPE:QK_NOPE + r].set(jnp.concatenate([inv_freq, inv_freq]))
    posf = pos.astype(F32)
    out = jax.ShapeDtypeStruct((t, LANES), F32)
    out_t = jax.ShapeDtypeStruct((LANES, t), F32)
    return pl.pallas_call(
        _rope_table_kernel,
        out_shape=(out, out, out, out, out_t, out_t),
        name="rope_tables",
    )(posf.reshape(t, 1), invf, posf.reshape(1, t), invf.reshape(LANES, 1))


C_Q0, C_Q1 = 0, 384
C_KV0, C_KV1 = 384, 640
C_KP0, C_KP1 = 640, 768
C_GM0 = 768
GROUP = 512
D_IN_PACKED = C_GM0 + 5 * GROUP


def _rope_slab(blk, c1, c2):
    return blk * c1 + pltpu.roll(blk, LANES - QK_ROPE, 1) * c2


def _project_common(x_ref, gpre_ref, w1_ref, gq_ref, gkv_ref, ck1_ref, ck2_ref):
    x = x_ref[...]
    xn = _rms(x, gpre_ref[...]).astype(BF16)

    def proj(c0, c1):
        return jnp.dot(xn, w1_ref[:, c0:c1], preferred_element_type=F32)

    cqn = _rms(proj(C_Q0, C_Q1), gq_ref[...]).astype(BF16)
    ckv = _rms(proj(C_KV0, C_KV1), gkv_ref[...])
    kpe_blk = _rope_slab(proj(C_KP0, C_KP1), ck1_ref[...], ck2_ref[...])
    return proj, cqn, ckv, kpe_blk


def _conv_gate(proj, um1, um2, u, wconv_ref):
    w = wconv_ref[...]
    y = w[0:1, :] * um2 + w[1:2, :] * um1 + w[2:3, :] * u
    bg = proj(C_GM0 + GROUP, C_GM0 + 2 * GROUP)
    gc = proj(C_GM0 + 4 * GROUP, C_GM0 + 5 * GROUP)
    return bg * y * _silu(gc)


def _proj_prompt_kernel(x_ref, gpre_ref, w1_ref, gq_ref, wqt_ref, gkv_ref, wk_ref, wvt_ref, wconv_ref,
                        cq1t_ref, cq2t_ref, ck1_ref, ck2_ref,
                        qt_out, k_out, vt_out, ckv_out, kpe_out, gm_out, cv_out, conv_out,
                        carry_ref, *, tiles_per_seq):
    tm = x_ref.shape[0]
    step = pl.program_id(0)
    dn = (((1,), (1,)), ((), ()))
    proj, cqn, ckv, kpe_blk = _project_common(x_ref, gpre_ref, w1_ref, gq_ref, gkv_ref, ck1_ref, ck2_ref)
    qt = lax.dot_general(wqt_ref[...], cqn, dn, preferred_element_type=F32)
    cq1t, cq2t = cq1t_ref[...], cq2t_ref[...]
    for h in range(N_HEADS):
        slab = qt[h * HEAD_SLAB:(h + 1) * HEAD_SLAB, :]
        roped = slab * cq1t + pltpu.roll(slab, HEAD_SLAB - QK_ROPE, 0) * cq2t
        qt_out[h * HEAD_SLAB:(h + 1) * HEAD_SLAB, :] = roped.astype(BF16)
    ckv_out[...] = ckv
    kpe_out[...] = kpe_blk[:, QK_NOPE:QK_NOPE + QK_ROPE]
    ckvb = ckv.astype(BF16)
    kfull = jnp.dot(ckvb, wk_ref[...], preferred_element_type=F32)
    for h in range(N_HEADS):
        k_out[:, h * HEAD_SLAB:(h + 1) * HEAD_SLAB] = (kfull[:, h * HEAD_SLAB:(h + 1) * HEAD_SLAB] + kpe_blk).astype(BF16)
    vt_out[...] = lax.dot_general(wvt_ref[...], ckvb, dn, preferred_element_type=F32).astype(BF16)
    gm_out[...] = _silu(proj(C_GM0, C_GM0 + GROUP)).astype(BF16)

    u = proj(C_GM0 + 2 * GROUP, C_GM0 + 3 * GROUP) * proj(C_GM0 + 3 * GROUP, C_GM0 + 4 * GROUP)

    @pl.when(step % tiles_per_seq == 0)
    def _():
        carry_ref[...] = jnp.zeros_like(carry_ref)

    prev = carry_ref[...]
    p1, p2 = prev[7:8, :], prev[6:7, :]
    row = lax.broadcasted_iota(jnp.int32, u.shape, 0)
    um1 = jnp.where(row == 0, p1, pltpu.roll(u, 1, 0))
    um2 = jnp.where(row == 0, p2, jnp.where(row == 1, p1, pltpu.roll(u, 2, 0)))
    cv_out[...] = _conv_gate(proj, um1, um2, u, wconv_ref).astype(BF16)
    carry_ref[...] = u[tm - 8:tm, :]
    conv_out[0] = u[tm - (CONV_W - 1):tm, :]


def _const_spec(shape):
    nd = len(shape)
    return pl.BlockSpec(shape, lambda *_: (0,) * nd, pipeline_mode=pl.Buffered(1))


def _proj_prompt(x2d, seq, gpre, w1, gq, wqt, gkv, wk, wvt, wconv, tabs, tm):
    t, d = x2d.shape
    nb = t // seq
    hs = N_HEADS * HEAD_SLAB
    hv = N_HEADS * V_HEAD
    cq1, cq2, ck1, ck2, cq1t, cq2t = tabs
    row = lambda w: pl.BlockSpec((tm, w), lambda i: (i, 0))
    col = lambda h: pl.BlockSpec((h, tm), lambda i: (0, i))
    tab = pl.BlockSpec((tm, LANES), lambda i: (i % (seq // tm), 0))
    tab_t = pl.BlockSpec((LANES, tm), lambda i: (0, i % (seq // tm)))
    out_shape = (
        jax.ShapeDtypeStruct((hs, t), BF16),
        jax.ShapeDtypeStruct((t, hs), BF16),
        jax.ShapeDtypeStruct((hv, t), BF16),
        jax.ShapeDtypeStruct((t, gkv.shape[1]), F32),
        jax.ShapeDtypeStruct((t, QK_ROPE), F32),
        jax.ShapeDtypeStruct((t, GROUP), BF16),
        jax.ShapeDtypeStruct((t, GROUP), BF16),
        jax.ShapeDtypeStruct((nb, CONV_W - 1, GROUP), F32),
    )
    return pl.pallas_call(
        functools.partial(_proj_prompt_kernel, tiles_per_seq=seq // tm),
        grid=(t // tm,),
        in_specs=[row(d), _const_spec(gpre.shape), _const_spec(w1.shape), _const_spec(gq.shape),
                  _const_spec(wqt.shape), _const_spec(gkv.shape), _const_spec(wk.shape), _const_spec(wvt.shape),
                  _const_spec(wconv.shape), tab_t, tab_t, tab, tab],
        out_specs=(col(hs), row(hs), col(hv), row(gkv.shape[1]), row(QK_ROPE), row(GROUP), row(GROUP),
                   pl.BlockSpec((1, CONV_W - 1, GROUP), lambda i: (i // (seq // tm), 0, 0))),
        out_shape=out_shape,
        scratch_shapes=[pltpu.VMEM((8, GROUP), F32)],
        compiler_params=pltpu.CompilerParams(dimension_semantics=("arbitrary",),
                                             vmem_limit_bytes=VMEM_LIMIT_BYTES),
        name="proj_prompt",
    )(x2d, gpre, w1, gq, wqt, gkv, wk, wvt, wconv, cq1t, cq2t, ck1, ck2)


def _attn_kernel(qt_ref, k_ref, vt_ref, o_ref, *, tq, tk, group):
    i = pl.program_id(1)
    n_diag = tq // tk
    krow = lax.broadcasted_iota(jnp.int32, (tk, tq), 0)
    qcol = i * tq + lax.broadcasted_iota(jnp.int32, (tk, tq), 1)

    def head_group(hs):
        def tile(j, carries, masked):
            toks = pl.ds(pl.multiple_of(j * tk, tk), tk)
            sts = [jnp.dot(k_ref[toks, h * HEAD_SLAB:(h + 1) * HEAD_SLAB],
                           qt_ref[h * HEAD_SLAB:(h + 1) * HEAD_SLAB, :], preferred_element_type=F32) for h in hs]
            if masked:
                keep = j * tk + krow <= qcol
                sts = [jnp.where(keep, st, -jnp.inf) for st in sts]
            stats = []
            for st, (m, _) in zip(sts, carries):
                m_new = jnp.maximum(m, jnp.max(st, axis=0, keepdims=True))
                stats.append((m_new, jnp.exp2(m - m_new), jnp.exp2(st - m_new).astype(BF16)))
            return tuple(
                (m_new, alpha * acc + jnp.dot(jnp.concatenate([vt_ref[h * V_HEAD:(h + 1) * V_HEAD, toks], ones], 0),
                                              pt, preferred_element_type=F32))
                for h, (m_new, alpha, pt), (_, acc) in zip(hs, stats, carries))

        ones = jnp.ones((ONES_ROWS, tk), BF16)
        init = (jnp.full((1, tq), -jnp.inf, F32), jnp.zeros((V_HEAD + ONES_ROWS, tq), F32))
        carries = tuple(init for _ in hs)
        n_full = i * n_diag
        for d in range(n_diag):
            carries = tile(n_full + d, carries, True)
        carries = lax.fori_loop(0, n_full, lambda j, c: tile(j, c, False), carries)
        return [acc[:V_HEAD] * (1.0 / acc[V_HEAD:V_HEAD + 1]) for _, acc in carries]

    for g in range(N_HEADS // group):
        outs = head_group(range(g * group, (g + 1) * group))
        for p in range(group // 2):
            pair_t = jnp.concatenate(outs[2 * p:2 * p + 2], axis=0)
            hp = g * (group // 2) + p
            o_ref[:, hp * LANES:(hp + 1) * LANES] = pair_t.T.astype(o_ref.dtype)


def _attention(qt, k, vt, nb, seq, tq, tk, group):
    hs, t = qt.shape
    hv = vt.shape[0]
    assert group % 2 == 0 and N_HEADS % group == 0
    return pl.pallas_call(
        functools.partial(_attn_kernel, tq=tq, tk=tk, group=group),
        grid=(nb, seq // tq),
        in_specs=[pl.BlockSpec((hs, tq), lambda b, i: (0, b * (seq // tq) + i)),
                  pl.BlockSpec((seq, hs), lambda b, i: (b, 0)),
                  pl.BlockSpec((hv, seq), lambda b, i: (0, b))],
        out_specs=pl.BlockSpec((tq, hv), lambda b, i: (b * (seq // tq) + i, 0)),
        out_shape=jax.ShapeDtypeStruct((t, hv), BF16),
        compiler_params=pltpu.CompilerParams(dimension_semantics=("arbitrary", "arbitrary"),
                                             vmem_limit_bytes=VMEM_LIMIT_BYTES),
        name="prompt_attn",
    )(qt, k, vt)


def _out_prompt_kernel(o_ref, gm_ref, cv_ref, x_ref, wo_ref, gpost_ref, y_ref):
    half = o_ref.shape[1]
    mla = (o_ref[...].astype(F32) * gm_ref[...].astype(F32)).astype(BF16)
    y = jnp.dot(mla, wo_ref[0:half, :], preferred_element_type=F32)
    y = y + jnp.dot(cv_ref[...], wo_ref[half:, :], preferred_element_type=F32)
    y_ref[...] = x_ref[...] + _rms(y, gpost_ref[...])


def _out_prompt(o, gm, cv, x2d, wo, gpost, tm):
    t, d = x2d.shape
    row = lambda w: pl.BlockSpec((tm, w), lambda i: (i, 0))
    return pl.pallas_call(
        _out_prompt_kernel,
        grid=(t // tm,),
        in_specs=[row(o.shape[1]), row(gm.shape[1]), row(cv.shape[1]), row(d),
                  _const_spec(wo.shape), _const_spec(gpost.shape)],
        out_specs=row(d),
        out_shape=jax.ShapeDtypeStruct((t, d), F32),
        compiler_params=pltpu.CompilerParams(dimension_semantics=("arbitrary",),
                                             vmem_limit_bytes=VMEM_LIMIT_BYTES),
        name="out_prompt",
    )(o, gm, cv, x2d, wo, gpost)


def _proj_sample_kernel(x_ref, gpre_ref, w1_ref, gq_ref, wq_ref, gkv_ref, wukt_ref, wconv_ref,
                        cq1_ref, cq2_ref, ck1_ref, ck2_ref, s0_ref, s1_ref,
                        ql_out, qp_out, ckv_out, kpe_out, gm_out, cv_out, conv_out, *, dec_seq):
    proj, cqn, ckv, kpe_blk = _project_common(x_ref, gpre_ref, w1_ref, gq_ref, gkv_ref, ck1_ref, ck2_ref)
    qraw = jnp.dot(cqn, wq_ref[...], preferred_element_type=F32)
    cq1, cq2 = cq1_ref[...], cq2_ref[...]
    for h in range(N_HEADS):
        qh = _rope_slab(qraw[:, h * HEAD_SLAB:(h + 1) * HEAD_SLAB], cq1, cq2)
        ql_out[h] = jnp.dot(qh.astype(BF16), wukt_ref[h], preferred_element_type=F32).astype(BF16)
        qp_out[h] = pltpu.roll(qh, LANES - QK_NOPE, 1)[:, 0:QK_ROPE].astype(BF16)
    ckv_out[...] = ckv
    kpe_out[...] = kpe_blk[:, QK_NOPE:QK_NOPE + QK_ROPE]
    gm_out[...] = _silu(proj(C_GM0, C_GM0 + GROUP))

    u = proj(C_GM0 + 2 * GROUP, C_GM0 + 3 * GROUP) * proj(C_GM0 + 3 * GROUP, C_GM0 + 4 * GROUP)
    t_in_seq = lax.broadcasted_iota(jnp.int32, u.shape, 0) % dec_seq
    s0, s1 = s0_ref[...], s1_ref[...]
    um1 = jnp.where(t_in_seq == 0, s1, pltpu.roll(u, 1, 0))
    um2 = jnp.where(t_in_seq == 0, s0, jnp.where(t_in_seq == 1, s1, pltpu.roll(u, 2, 0)))
    cv_out[...] = _conv_gate(proj, um1, um2, u, wconv_ref)
    conv_out[...] = u


def _proj_sample(x2d, dec_seq, gpre, w1, gq, wq, gkv, wukt, wconv, tabs, s0, s1):
    t, d = x2d.shape
    c = gkv.shape[1]
    out_shape = (
        jax.ShapeDtypeStruct((N_HEADS, t, c), BF16),
        jax.ShapeDtypeStruct((N_HEADS, t, QK_ROPE), BF16),
        jax.ShapeDtypeStruct((t, c), F32),
        jax.ShapeDtypeStruct((t, QK_ROPE), F32),
        jax.ShapeDtypeStruct((t, GROUP), F32),
        jax.ShapeDtypeStruct((t, GROUP), F32),
        jax.ShapeDtypeStruct((t, GROUP), F32),
    )
    return pl.pallas_call(
        functools.partial(_proj_sample_kernel, dec_seq=dec_seq),
        out_shape=out_shape,
        compiler_params=pltpu.CompilerParams(vmem_limit_bytes=VMEM_LIMIT_BYTES),
        name="proj_sample",
    )(x2d, gpre, w1, gq, wq, gkv, wukt, wconv, *tabs, s0, s1)


def _decode_attn_kernel(pt_ref, ql_ref, qp_ref, cnew_ref, knew_ref, y_hbm, x_hbm, o_ref,
                        ybuf, xbuf, sem, m_ref, l_ref, acc_ref, *, pages, slots, sub, layer, dec_seq):
    b = pl.program_id(0)
    n_b = pl.num_programs(0)
    n_chunks = pt_ref.shape[1] // pages
    page = ybuf.shape[1] // pages
    rows = ql_ref.shape[1]
    dn = (((1,), (1,)), ((), ()))
    ahead = slots - 1

    def chunk_copies(bb, ch, slot, page_ids):
        cps = []
        for i in range(pages):
            pid = page_ids(bb, ch * pages + i)
            tok = pl.ds(i * page, page)
            cps.append(pltpu.make_async_copy(y_hbm.at[layer, pid], ybuf.at[slot, tok, :], sem.at[0, slot]))
            cps.append(pltpu.make_async_copy(x_hbm.at[layer, pid], xbuf.at[slot, :, tok], sem.at[1, slot]))
        return cps

    def start(bb, ch, slot):
        for cp in chunk_copies(bb, ch, slot, lambda r, j: pt_ref[r, j]):
            cp.start()

    def wait(slot):
        for cp in chunk_copies(0, 0, slot, lambda r, j: 0):
            cp.wait()

    @pl.when(b == 0)
    def _():
        for k in range(ahead):
            start(0, k, k)

    ql, qp = ql_ref[0], qp_ref[0]
    qlf, qpf = ql.astype(F32), qp.astype(F32)
    tok_of_row = lax.broadcasted_iota(jnp.int32, (rows, 1), 0) // N_HEADS
    s_new = []
    for j in range(dec_seq):
        sj = (jnp.sum(qlf * cnew_ref[0, j:j + 1, :], axis=-1, keepdims=True)
              + jnp.sum(qpf * knew_ref[0, j:j + 1, :], axis=-1, keepdims=True))
        s_new.append(jnp.where(tok_of_row >= j, sj, -jnp.inf))
    m_n = functools.reduce(jnp.maximum, s_new)
    p_n = [jnp.exp(sj - m_n) for sj in s_new]
    m_ref[...] = m_n
    l_ref[...] = functools.reduce(lambda a, e: a + e, p_n)
    acc_ref[0] = functools.reduce(lambda a, e: a + e,
                                  [p_n[j] * cnew_ref[0, j:j + 1, :] for j in range(dec_seq)])
    acc_ref[1] = jnp.zeros(acc_ref.shape[1:], F32)

    def compute(slot):
        m, l = m_ref[...], l_ref[...]
        acc = [acc_ref[h] for h in range(2)]
        half = sub // 2
        for j in range(pages * page // sub):
            toks = [slice(j * sub + h * half, j * sub + (h + 1) * half) for h in range(2)]
            yb = [ybuf[slot, t, :].astype(BF16) for t in toks]
            xb = [xbuf[slot, :, t].astype(BF16) for t in toks]
            s = jnp.concatenate(
                [lax.dot_general(ql, yb[h], dn, preferred_element_type=F32)
                 + jnp.dot(qp, xb[h], preferred_element_type=F32) for h in range(2)], axis=1)
            m_new = jnp.maximum(m, jnp.max(s, axis=-1, keepdims=True))
            alpha = jnp.exp(m - m_new)
            p = jnp.exp(s - m_new)
            l = alpha * l + jnp.sum(p, axis=-1, keepdims=True)
            acc = [alpha * acc[h]
                   + jnp.dot(p[:, h * half:(h + 1) * half].astype(BF16), yb[h], preferred_element_type=F32)
                   for h in range(2)]
            m = m_new
        m_ref[...], l_ref[...] = m, l
        for h in range(2):
            acc_ref[h] = acc[h]

    def trip(it, carry):
        for k in range(slots):
            ch = it * slots + k
            nxt = ch + ahead
            nxt_slot = (k + ahead) % slots

            @pl.when(nxt < n_chunks)
            def _():
                start(b, nxt, nxt_slot)

            @pl.when((nxt >= n_chunks) & (b + 1 < n_b))
            def _():
                start(b + 1, nxt - n_chunks, nxt_slot)

            wait(k)
            compute(k)
        return carry

    lax.fori_loop(0, n_chunks // slots, trip, 0)
    o_ref[0] = (acc_ref[0] + acc_ref[1]) * (1.0 / l_ref[...])


def _decode_attention(page_table, ql, qp, cnew, knew, cache_ckv, cache_kpe, layer, pages, slots, sub):
    nb, rows, c = ql.shape
    dec_seq = cnew.shape[1]
    page, r = cache_kpe.shape[2:]
    n_chunks = page_table.shape[1] // pages
    assert n_chunks * pages == page_table.shape[1] and n_chunks % slots == 0 and slots - 1 <= n_chunks
    assert (pages * page) % sub == 0
    kpe_t = jnp.swapaxes(cache_kpe, 2, 3)
    per_b = lambda a: pl.BlockSpec((1,) + a.shape[1:], lambda b, pt: (b, 0, 0))
    grid_spec = pltpu.PrefetchScalarGridSpec(
        num_scalar_prefetch=1,
        grid=(nb,),
        in_specs=[per_b(ql), per_b(qp), per_b(cnew), per_b(knew),
                  pl.BlockSpec(memory_space=pl.ANY), pl.BlockSpec(memory_space=pl.ANY)],
        out_specs=pl.BlockSpec((1, rows, c), lambda b, pt: (b, 0, 0)),
        scratch_shapes=[pltpu.VMEM((slots, pages * page, c), F32), pltpu.VMEM((slots, r, pages * page), F32),
                        pltpu.SemaphoreType.DMA((2, slots)),
                        pltpu.VMEM((rows, 1), F32), pltpu.VMEM((rows, 1), F32), pltpu.VMEM((2, rows, c), F32)],
    )
    return pl.pallas_call(
        functools.partial(_decode_attn_kernel, pages=pages, slots=slots, sub=sub, layer=layer, dec_seq=dec_seq),
        grid_spec=grid_spec,
        out_shape=jax.ShapeDtypeStruct((nb, rows, c), F32),
        compiler_params=pltpu.CompilerParams(dimension_semantics=("arbitrary",),
                                             vmem_limit_bytes=VMEM_LIMIT_BYTES),
        name="decode_attn",
    )(page_table, ql, qp, cnew, knew, cache_ckv, kpe_t)


def _out_sample_kernel(ol_ref, gm_ref, cv_ref, x_ref, wuv_ref, wo_ref, gpost_ref, y_ref):
    half = gm_ref.shape[1]
    o = jnp.concatenate(
        [jnp.dot(ol_ref[h].astype(BF16), wuv_ref[h], preferred_element_type=F32) for h in range(N_HEADS)], axis=1)
    mla = (o * gm_ref[...]).astype(BF16)
    y = jnp.dot(mla, wo_ref[0:half, :], preferred_element_type=F32)
    y = y + jnp.dot(cv_ref[...].astype(BF16), wo_ref[half:, :], preferred_element_type=F32)
    y_ref[...] = x_ref[...] + _rms(y, gpost_ref[...])


def _out_sample(ol, gm, cv, x2d, wuv, wo, gpost):
    return pl.pallas_call(
        _out_sample_kernel,
        out_shape=jax.ShapeDtypeStruct(x2d.shape, F32),
        compiler_params=pltpu.CompilerParams(vmem_limit_bytes=VMEM_LIMIT_BYTES),
        name="out_sample",
    )(ol, gm, cv, x2d, wuv, wo, gpost)


def _rot_cols(w):
    r = w.shape[-1] // 2
    return jnp.concatenate([-w[..., r:], w[..., :r]], axis=-1)


def _pack_weights(w_in, w_uq, w_ukv, w_out):
    d, q_lora, kv_lora = w_in.shape[0], w_uq.shape[0], w_ukv.shape[0]
    cq, ckv, kpe, rest = (w_in[:, :C_Q1], w_in[:, C_KV0:C_KV1], w_in[:, C_KV1:C_KV1 + QK_ROPE],
                          w_in[:, C_KV1 + QK_ROPE:])
    w1 = jnp.concatenate([cq, ckv, jnp.zeros((d, QK_NOPE), F32), kpe, _rot_cols(kpe), rest], axis=1).astype(BF16)
    uq = w_uq.reshape(q_lora, N_HEADS, QK_NOPE + QK_ROPE)
    pe = uq[:, :, QK_NOPE:]
    wq = jnp.concatenate([uq[:, :, :QK_NOPE], pe, _rot_cols(pe)], axis=-1).reshape(q_lora, N_HEADS * HEAD_SLAB)
    ukv = w_ukv.reshape(kv_lora, N_HEADS, QK_NOPE + V_HEAD)
    uk, uv = ukv[:, :, :QK_NOPE], ukv[:, :, QK_NOPE:]
    pad = jnp.zeros((kv_lora, N_HEADS, HEAD_SLAB - QK_NOPE), F32)
    wk = jnp.concatenate([uk, pad], axis=-1).reshape(kv_lora, N_HEADS * HEAD_SLAB)
    wvt = uv.reshape(kv_lora, N_HEADS * V_HEAD).T
    wukt = jnp.concatenate([uk.transpose(1, 2, 0), jnp.zeros((N_HEADS, HEAD_SLAB - QK_NOPE, kv_lora), F32)], axis=1)
    wuv = uv.transpose(1, 0, 2)
    return (w1, wq.astype(BF16), wq.T.astype(BF16), wk.astype(BF16), wvt.astype(BF16), wukt.astype(BF16),
            wuv.astype(BF16), w_out.astype(BF16))


PROMPT_TILE = 512
ATTN_TQ = 512
ATTN_TK = 512
ATTN_HEAD_GROUP = 4
DECODE_PAGES = 32
DECODE_SLOTS = 4
DECODE_SUB = 4096


def kernel(x_prompt, x_sample, cache_ckv, cache_kpe, state_conv, page_table, g_pre, w_in, g_qnorm, w_uq,
           g_kvnorm, w_ukv, w_conv, w_out, g_post):
    depth = w_in.shape[0]
    nb, seq, d = x_prompt.shape
    db, dec_seq, _ = x_sample.shape
    past_len = page_table.shape[1] * cache_ckv.shape[2]
    c = cache_ckv.shape[3]

    tabs_p = _rope_tables(jnp.arange(seq, dtype=jnp.int32))
    tabs_s = _rope_tables(jnp.tile(past_len + jnp.arange(dec_seq, dtype=jnp.int32), db))

    xp = x_prompt.reshape(nb * seq, d)
    xs = x_sample.reshape(db * dec_seq, d)
    outs = [[] for _ in range(6)]
    for l in range(depth):
        w1, wq, wqt, wk, wvt, wukt, wuv, wo = _pack_weights(w_in[l], w_uq[l], w_ukv[l], w_out[l])
        gpre, gq, gkv, gpost = g_pre[l][None], g_qnorm[l][None], g_kvnorm[l][None], g_post[l][None]

        qt, k, vt, ckv_p, kpe_p, gm, cv, conv_p = _proj_prompt(
            xp, seq, gpre, w1, gq, wqt, gkv, wk, wvt, w_conv[l], tabs_p, PROMPT_TILE)
        o = _attention(qt, k, vt, nb, seq, ATTN_TQ, ATTN_TK, ATTN_HEAD_GROUP)
        xp = _out_prompt(o, gm, cv, xp, wo, gpost, PROMPT_TILE)
        outs[0].append(ckv_p.reshape(nb, seq, c))
        outs[1].append(kpe_p.reshape(nb, seq, QK_ROPE))
        outs[2].append(conv_p)

        st = state_conv[l].astype(F32)
        s0 = jnp.repeat(st[:, 0], dec_seq, axis=0)
        s1 = jnp.repeat(st[:, 1], dec_seq, axis=0)
        ql, qp, ckv_s, kpe_s, gm_s, cv_s, u_s = _proj_sample(
            xs, dec_seq, gpre, w1, gq, wq, gkv, wukt, w_conv[l], tabs_s[:4], s0, s1)
        rows = dec_seq * N_HEADS
        to_rows = lambda a: a.reshape(N_HEADS, db, dec_seq, -1).transpose(1, 2, 0, 3).reshape(db, rows, -1)
        ol = _decode_attention(page_table, to_rows(ql), to_rows(qp), ckv_s.reshape(db, dec_seq, c),
                               kpe_s.reshape(db, dec_seq, QK_ROPE), cache_ckv, cache_kpe, l,
                               DECODE_PAGES, DECODE_SLOTS, DECODE_SUB)
        ol = ol.reshape(db, dec_seq, N_HEADS, c).transpose(2, 0, 1, 3).reshape(N_HEADS, db * dec_seq, c)
        xs = _out_sample(ol, gm_s, cv_s, xs, wuv, wo, gpost)
        outs[3].append(ckv_s.reshape(db, dec_seq, c))
        outs[4].append(kpe_s.reshape(db, dec_seq, QK_ROPE))
        u_pad = jnp.concatenate([st, u_s.reshape(db, dec_seq, GROUP)], axis=1)
        outs[5].append(u_pad[:, -(CONV_W - 1):])

    return (xp.reshape(nb, seq, d), xs.reshape(db, dec_seq, d), *[jnp.stack(o_) for o_ in outs])
```

```python
import functools

import jax
import jax.numpy as jnp
from jax import lax
from jax.experimental import pallas as pl
from jax.experimental.pallas import tpu as pltpu

N_HEADS = 8
QK_NOPE = 64
QK_ROPE = 32
V_HEAD = 64
CONV_W = 3
ROPE_THETA = 10000.0
EPS = 1e-6
ATTN_SCALE = (QK_NOPE + QK_ROPE) ** -0.5

LANES = 128
HEAD_SLAB = LANES
ONES_ROWS = 16
LOG2_E = 1.4426950408889634
VMEM_LIMIT_BYTES = 56 * 1024 * 1024

F32 = jnp.float32
BF16 = jnp.bfloat16


def _silu(x):
    return x * (1.0 / (1.0 + jnp.exp(-x)))


def _rms(x, g):
    return x * lax.rsqrt(jnp.mean(x * x, axis=-1, keepdims=True) + EPS) * g


def _rope_coeffs(ang, slab_axis, q_scale):
    idx = lax.broadcasted_iota(jnp.int32, ang.shape, slab_axis)
    rope = (idx >= QK_NOPE) & (idx < QK_NOPE + QK_ROPE)
    c = jnp.where(rope, jnp.cos(ang), 0.0)
    s = jnp.where(rope, jnp.sin(ang), 0.0)
    return jnp.where(idx < QK_NOPE, q_scale, c * q_scale), s * q_scale, c, s


def _rope_table_kernel(pos_ref, invf_ref, post_ref, invft_ref, cq1_ref, cq2_ref, ck1_ref, ck2_ref,
                       cq1t_ref, cq2t_ref):
    cq1_ref[...], cq2_ref[...], ck1_ref[...], ck2_ref[...] = _rope_coeffs(
        pos_ref[...] * invf_ref[...], 1, ATTN_SCALE)
    cq1t_ref[...], cq2t_ref[...] = _rope_coeffs(invft_ref[...] * post_ref[...], 0, ATTN_SCALE * LOG2_E)[:2]


def _rope_tables(pos):
    t = pos.shape[0]
    r = QK_ROPE
    inv_freq = ROPE_THETA ** (-jnp.arange(0, r, 2, dtype=F32) / r)
    invf = jnp.zeros((1, LANES), F32).at[0, QK_NOPE:QK_NOPE + r].set(jnp.concatenate([inv_freq, inv_freq]))
    posf = pos.astype(F32)
    out = jax.ShapeDtypeStruct((t, LANES), F32)
    out_t = jax.ShapeDtypeStruct((LANES, t), F32)
    return pl.pallas_call(
        _rope_table_kernel,
        out_shape=(out, out, out, out, out_t, out_t),
        name="rope_tables",
    )(posf.reshape(t, 1), invf, posf.reshape(1, t), invf.reshape(LANES, 1))


C_Q0, C_Q1 = 0, 384
C_KV0, C_KV1 = 384, 640
C_KP0, C_KP1 = 640, 768
C_GM0 = 768
GROUP = 512
D_IN_PACKED = C_GM0 + 5 * GROUP


def _rope_slab(blk, c1, c2):
    return blk * c1 + pltpu.roll(blk, LANES - QK_ROPE, 1) * c2


def _project_common(x_ref, gpre_ref, w1_ref, gq_ref, gkv_ref, ck1_ref, ck2_ref):
    x = x_ref[...]
    xn = _rms(x, gpre_ref[...]).astype(BF16)

    def proj(c0, c1):
        return jnp.dot(xn, w1_ref[:, c0:c1], preferred_element_type=F32)

    cqn = _rms(proj(C_Q0, C_Q1), gq_ref[...]).astype(BF16)
    ckv = _rms(proj(C_KV0, C_KV1), gkv_ref[...])
    kpe_blk = _rope_slab(proj(C_KP0, C_KP1), ck1_ref[...], ck2_ref[...])
    return proj, cqn, ckv, kpe_blk


def _conv_gate(proj, um1, um2, u, wconv_ref):
    w = wconv_ref[...]
    y = w[0:1, :] * um2 + w[1:2, :] * um1 + w[2:3, :] * u
    bg = proj(C_GM0 + GROUP, C_GM0 + 2 * GROUP)
    gc = proj(C_GM0 + 4 * GROUP, C_GM0 + 5 * GROUP)
    return bg * y * _silu(gc)


def _proj_prompt_kernel(x_ref, gpre_ref, w1_ref, gq_ref, wqt_ref, gkv_ref, wk_ref, wvt_ref, wconv_ref,
                        cq1t_ref, cq2t_ref, ck1_ref, ck2_ref,
                        qt_out, k_out, vt_out, ckv_out, kpe_out, gm_out, cv_out, conv_out,
                        carry_ref, *, tiles_per_seq):
    tm = x_ref.shape[0]
    step = pl.program_id(0)
    dn = (((1,), (1,)), ((), ()))
    proj, cqn, ckv, kpe_blk = _project_common(x_ref, gpre_ref, w1_ref, gq_ref, gkv_ref, ck1_ref, ck2_ref)
    qt = lax.dot_general(wqt_ref[...], cqn, dn, preferred_element_type=F32)
    cq1t, cq2t = cq1t_ref[...], cq2t_ref[...]
    for h in range(N_HEADS):
        slab = qt[h * HEAD_SLAB:(h + 1) * HEAD_SLAB, :]
        roped = slab * cq1t + pltpu.roll(slab, HEAD_SLAB - QK_ROPE, 0) * cq2t
        qt_out[h * HEAD_SLAB:(h + 1) * HEAD_SLAB, :] = roped.astype(BF16)
    ckv_out[...] = ckv
    kpe_out[...] = kpe_blk[:, QK_NOPE:QK_NOPE + QK_ROPE]
    ckvb = ckv.astype(BF16)
    kfull = jnp.dot(ckvb, wk_ref[...], preferred_element_type=F32)
    for h in range(N_HEADS):
        k_out[:, h * HEAD_SLAB:(h + 1) * HEAD_SLAB] = (kfull[:, h * HEAD_SLAB:(h + 1) * HEAD_SLAB] + kpe_blk).astype(BF16)
    vt_out[...] = lax.dot_general(wvt_ref[...], ckvb, dn, preferred_element_type=F32).astype(BF16)
    gm_out[...] = _silu(proj(C_GM0, C_GM0 + GROUP)).astype(BF16)

    u = proj(C_GM0 + 2 * GROUP, C_GM0 + 3 * GROUP) * proj(C_GM0 + 3 * GROUP, C_GM0 + 4 * GROUP)

    @pl.when(step % tiles_per_seq == 0)
    def _():
        carry_ref[...] = jnp.zeros_like(carry_ref)

    prev = carry_ref[...]
    p1, p2 = prev[7:8, :], prev[6:7, :]
    row = lax.broadcasted_iota(jnp.int32, u.shape, 0)
    um1 = jnp.where(row == 0, p1, pltpu.roll(u, 1, 0))
    um2 = jnp.where(row == 0, p2, jnp.where(row == 1, p1, pltpu.roll(u, 2, 0)))
    cv_out[...] = _conv_gate(proj, um1, um2, u, wconv_ref).astype(BF16)
    carry_ref[...] = u[tm - 8:tm, :]
    conv_out[0] = u[tm - (CONV_W - 1):tm, :]


def _const_spec(shape):
    nd = len(shape)
    return pl.BlockSpec(shape, lambda *_: (0,) * nd, pipeline_mode=pl.Buffered(1))


def _proj_prompt(x2d, seq, gpre, w1, gq, wqt, gkv, wk, wvt, wconv, tabs, tm):
    t, d = x2d.shape
    nb = t // seq
    hs = N_HEADS * HEAD_SLAB
    hv = N_HEADS * V_HEAD
    cq1, cq2, ck1, ck2, cq1t, cq2t = tabs
    row = lambda w: pl.BlockSpec((tm, w), lambda i: (i, 0))
    col = lambda h: pl.BlockSpec((h, tm), lambda i: (0, i))
    tab = pl.BlockSpec((tm, LANES), lambda i: (i % (seq // tm), 0))
    tab_t = pl.BlockSpec((LANES, tm), lambda i: (0, i % (seq // tm)))
    out_shape = (
        jax.ShapeDtypeStruct((hs, t), BF16),
        jax.ShapeDtypeStruct((t, hs), BF16),
        jax.ShapeDtypeStruct((hv, t), BF16),
        jax.ShapeDtypeStruct((t, gkv.shape[1]), F32),
        jax.ShapeDtypeStruct((t, QK_ROPE), F32),
        jax.ShapeDtypeStruct((t, GROUP), BF16),
        jax.ShapeDtypeStruct((t, GROUP), BF16),
        jax.ShapeDtypeStruct((nb, CONV_W - 1, GROUP), F32),
    )
    return pl.pallas_call(
        functools.partial(_proj_prompt_kernel, tiles_per_seq=seq // tm),
        grid=(t // tm,),
        in_specs=[row(d), _const_spec(gpre.shape), _const_spec(w1.shape), _const_spec(gq.shape),
                  _const_spec(wqt.shape), _const_spec(gkv.shape), _const_spec(wk.shape), _const_spec(wvt.shape),
                  _const_spec(wconv.shape), tab_t, tab_t, tab, tab],
        out_specs=(col(hs), row(hs), col(hv), row(gkv.shape[1]), row(QK_ROPE), row(GROUP), row(GROUP),
                   pl.BlockSpec((1, CONV_W - 1, GROUP), lambda i: (i // (seq // tm), 0, 0))),
        out_shape=out_shape,
        scratch_shapes=[pltpu.VMEM((8, GROUP), F32)],
        compiler_params=pltpu.CompilerParams(dimension_semantics=("arbitrary",),
                                             vmem_limit_bytes=VMEM_LIMIT_BYTES),
        name="proj_prompt",
    )(x2d, gpre, w1, gq, wqt, gkv, wk, wvt, wconv, cq1t, cq2t, ck1, ck2)


def _attn_kernel(qt_ref, k_ref, vt_ref, gm_ref, cv_ref, x_ref, wo_ref, gpost_ref, y_ref, *, tq, tk, group):
    i = pl.program_id(1)
    n_diag = tq // tk
    krow = lax.broadcasted_iota(jnp.int32, (tk, tq), 0)
    qcol = i * tq + lax.broadcasted_iota(jnp.int32, (tk, tq), 1)

    def head_group(hs):
        def tile(j, carries, masked):
            toks = pl.ds(pl.multiple_of(j * tk, tk), tk)
            sts = [jnp.dot(k_ref[toks, h * HEAD_SLAB:(h + 1) * HEAD_SLAB],
                           qt_ref[h * HEAD_SLAB:(h + 1) * HEAD_SLAB, :], preferred_element_type=F32) for h in hs]
            if masked:
                keep = j * tk + krow <= qcol
                sts = [jnp.where(keep, st, -jnp.inf) for st in sts]
            stats = []
            for st, (m, _) in zip(sts, carries):
                m_new = jnp.maximum(m, jnp.max(st, axis=0, keepdims=True))
                stats.append((m_new, jnp.exp2(m - m_new), jnp.exp2(st - m_new).astype(BF16)))
            return tuple(
                (m_new, alpha * acc + jnp.dot(jnp.concatenate([vt_ref[h * V_HEAD:(h + 1) * V_HEAD, toks], ones], 0),
                                              pt, preferred_element_type=F32))
                for h, (m_new, alpha, pt), (_, acc) in zip(hs, stats, carries))

        ones = jnp.ones((ONES_ROWS, tk), BF16)
        init = (jnp.full((1, tq), -jnp.inf, F32), jnp.zeros((V_HEAD + ONES_ROWS, tq), F32))
        carries = tuple(init for _ in hs)
        n_full = i * n_diag
        for d in range(n_diag):
            carries = tile(n_full + d, carries, True)
        carries = lax.fori_loop(0, n_full, lambda j, c: tile(j, c, False), carries)
        return [acc[:V_HEAD] * (1.0 / acc[V_HEAD:V_HEAD + 1]) for _, acc in carries]

    pairs = []
    for g in range(N_HEADS // group):
        outs = head_group(range(g * group, (g + 1) * group))
        for p in range(group // 2):
            pairs.append(jnp.concatenate(outs[2 * p:2 * p + 2], axis=0).T)

    half = gm_ref.shape[1]
    mla = (jnp.concatenate(pairs, axis=1) * gm_ref[...].astype(F32)).astype(BF16)
    y = jnp.dot(mla, wo_ref[0:half, :], preferred_element_type=F32)
    y = y + jnp.dot(cv_ref[...], wo_ref[half:, :], preferred_element_type=F32)
    y_ref[...] = x_ref[...] + _rms(y, gpost_ref[...])


def _attention_out(qt, k, vt, gm, cv, x2d, wo, gpost, nb, seq, tq, tk, group):
    hs, t = qt.shape
    hv = vt.shape[0]
    d = x2d.shape[1]
    assert group % 2 == 0 and N_HEADS % group == 0
    row = lambda w: pl.BlockSpec((tq, w), lambda b, i: (b * (seq // tq) + i, 0))
    return pl.pallas_call(
        functools.partial(_attn_kernel, tq=tq, tk=tk, group=group),
        grid=(nb, seq // tq),
        in_specs=[pl.BlockSpec((hs, tq), lambda b, i: (0, b * (seq // tq) + i)),
                  pl.BlockSpec((seq, hs), lambda b, i: (b, 0)),
                  pl.BlockSpec((hv, seq), lambda b, i: (0, b)),
                  row(gm.shape[1]), row(cv.shape[1]), row(d), _const_spec(wo.shape), _const_spec(gpost.shape)],
        out_specs=row(d),
        out_shape=jax.ShapeDtypeStruct((t, d), F32),
        compiler_params=pltpu.CompilerParams(dimension_semantics=("arbitrary", "arbitrary"),
                                             vmem_limit_bytes=VMEM_LIMIT_BYTES),
        name="prompt_attn",
    )(qt, k, vt, gm, cv, x2d, wo, gpost)


def _proj_sample_kernel(x_ref, gpre_ref, w1_ref, gq_ref, wq_ref, gkv_ref, wukt_ref, wconv_ref,
                        cq1_ref, cq2_ref, ck1_ref, ck2_ref, s0_ref, s1_ref,
                        ql_out, qp_out, ckv_out, kpe_out, gm_out, cv_out, conv_out, *, dec_seq):
    proj, cqn, ckv, kpe_blk = _project_common(x_ref, gpre_ref, w1_ref, gq_ref, gkv_ref, ck1_ref, ck2_ref)
    qraw = jnp.dot(cqn, wq_ref[...], preferred_element_type=F32)
    cq1, cq2 = cq1_ref[...], cq2_ref[...]
    for h in range(N_HEADS):
        qh = _rope_slab(qraw[:, h * HEAD_SLAB:(h + 1) * HEAD_SLAB], cq1, cq2)
        ql_out[h] = jnp.dot(qh.astype(BF16), wukt_ref[h], preferred_element_type=F32).astype(BF16)
        qp_out[h] = pltpu.roll(qh, LANES - QK_NOPE, 1)[:, 0:QK_ROPE].astype(BF16)
    ckv_out[...] = ckv
    kpe_out[...] = kpe_blk[:, QK_NOPE:QK_NOPE + QK_ROPE]
    gm_out[...] = _silu(proj(C_GM0, C_GM0 + GROUP))

    u = proj(C_GM0 + 2 * GROUP, C_GM0 + 3 * GROUP) * proj(C_GM0 + 3 * GROUP, C_GM0 + 4 * GROUP)
    t_in_seq = lax.broadcasted_iota(jnp.int32, u.shape, 0) % dec_seq
    s0, s1 = s0_ref[...], s1_ref[...]
    um1 = jnp.where(t_in_seq == 0, s1, pltpu.roll(u, 1, 0))
    um2 = jnp.where(t_in_seq == 0, s0, jnp.where(t_in_seq == 1, s1, pltpu.roll(u, 2, 0)))
    cv_out[...] = _conv_gate(proj, um1, um2, u, wconv_ref)
    conv_out[...] = u


def _proj_sample(x2d, dec_seq, gpre, w1, gq, wq, gkv, wukt, wconv, tabs, s0, s1):
    t, d = x2d.shape
    c = gkv.shape[1]
    out_shape = (
        jax.ShapeDtypeStruct((N_HEADS, t, c), BF16),
        jax.ShapeDtypeStruct((N_HEADS, t, QK_ROPE), BF16),
        jax.ShapeDtypeStruct((t, c), F32),
        jax.ShapeDtypeStruct((t, QK_ROPE), F32),
        jax.ShapeDtypeStruct((t, GROUP), F32),
        jax.ShapeDtypeStruct((t, GROUP), F32),
        jax.ShapeDtypeStruct((t, GROUP), F32),
    )
    return pl.pallas_call(
        functools.partial(_proj_sample_kernel, dec_seq=dec_seq),
        out_shape=out_shape,
        compiler_params=pltpu.CompilerParams(vmem_limit_bytes=VMEM_LIMIT_BYTES),
        name="proj_sample",
    )(x2d, gpre, w1, gq, wq, gkv, wukt, wconv, *tabs, s0, s1)


def _decode_attn_kernel(pt_ref, ql_ref, qp_ref, cnew_ref, knew_ref, y_hbm, x_hbm, o_ref,
                        ybuf, xbuf, sem, m_ref, l_ref, acc_ref, *, pages, slots, sub, layer, dec_seq):
    b = pl.program_id(0)
    n_b = pl.num_programs(0)
    n_chunks = pt_ref.shape[1] // pages
    page = ybuf.shape[1] // pages
    rows = ql_ref.shape[1]
    dn = (((1,), (1,)), ((), ()))
    ahead = slots - 1

    def chunk_copies(bb, ch, slot, page_ids):
        cps = []
        for i in range(pages):
            pid = page_ids(bb, ch * pages + i)
            tok = pl.ds(i * page, page)
            cps.append(pltpu.make_async_copy(y_hbm.at[layer, pid], ybuf.at[slot, tok, :], sem.at[0, slot]))
            cps.append(pltpu.make_async_copy(x_hbm.at[layer, pid], xbuf.at[slot, :, tok], sem.at[1, slot]))
        return cps

    def start(bb, ch, slot):
        for cp in chunk_copies(bb, ch, slot, lambda r, j: pt_ref[r, j]):
            cp.start()

    def wait(slot):
        for cp in chunk_copies(0, 0, slot, lambda r, j: 0):
            cp.wait()

    @pl.when(b == 0)
    def _():
        for k in range(ahead):
            start(0, k, k)

    ql, qp = ql_ref[0], qp_ref[0]
    qlf, qpf = ql.astype(F32), qp.astype(F32)
    tok_of_row = lax.broadcasted_iota(jnp.int32, (rows, 1), 0) // N_HEADS
    s_new = []
    for j in range(dec_seq):
        sj = (jnp.sum(qlf * cnew_ref[0, j:j + 1, :], axis=-1, keepdims=True)
              + jnp.sum(qpf * knew_ref[0, j:j + 1, :], axis=-1, keepdims=True))
        s_new.append(jnp.where(tok_of_row >= j, sj, -jnp.inf))
    m_n = functools.reduce(jnp.maximum, s_new)
    p_n = [jnp.exp(sj - m_n) for sj in s_new]
    m_ref[...] = m_n
    l_ref[...] = functools.reduce(lambda a, e: a + e, p_n)
    acc_ref[0] = functools.reduce(lambda a, e: a + e,
                                  [p_n[j] * cnew_ref[0, j:j + 1, :] for j in range(dec_seq)])
    acc_ref[1] = jnp.zeros(acc_ref.shape[1:], F32)

    def compute(slot):
        m, l = m_ref[...], l_ref[...]
        acc = [acc_ref[h] for h in range(2)]
        half = sub // 2
        for j in range(pages * page // sub):
            toks = [slice(j * sub + h * half, j * sub + (h + 1) * half) for h in range(2)]
            yb = [ybuf[slot, t, :].astype(BF16) for t in toks]
            xb = [xbuf[slot, :, t].astype(BF16) for t in toks]
            s = jnp.concatenate(
                [lax.dot_general(ql, yb[h], dn, preferred_element_type=F32)
                 + jnp.dot(qp, xb[h], preferred_element_type=F32) for h in range(2)], axis=1)
            m_new = jnp.maximum(m, jnp.max(s, axis=-1, keepdims=True))
            alpha = jnp.exp(m - m_new)
            p = jnp.exp(s - m_new)
            l = alpha * l + jnp.sum(p, axis=-1, keepdims=True)
            acc = [alpha * acc[h]
                   + jnp.dot(p[:, h * half:(h + 1) * half].astype(BF16), yb[h], preferred_element_type=F32)
                   for h in range(2)]
            m = m_new
        m_ref[...], l_ref[...] = m, l
        for h in range(2):
            acc_ref[h] = acc[h]

    def trip(it, carry):
        for k in range(slots):
            ch = it * slots + k
            nxt = ch + ahead
            nxt_slot = (k + ahead) % slots

            @pl.when(nxt < n_chunks)
            def _():
                start(b, nxt, nxt_slot)

            @pl.when((nxt >= n_chunks) & (b + 1 < n_b))
            def _():
                start(b + 1, nxt - n_chunks, nxt_slot)

            wait(k)
            compute(k)
        return carry

    lax.fori_loop(0, n_chunks // slots, trip, 0)
    o_ref[0] = (acc_ref[0] + acc_ref[1]) * (1.0 / l_ref[...])


def _decode_attention(page_table, ql, qp, cnew, knew, cache_ckv, cache_kpe, layer, pages, slots, sub):
    nb, rows, c = ql.shape
    dec_seq = cnew.shape[1]
    page, r = cache_kpe.shape[2:]
    n_chunks = page_table.shape[1] // pages
    assert n_chunks * pages == page_table.shape[1] and n_chunks % slots == 0 and slots - 1 <= n_chunks
    assert (pages * page) % sub == 0
    kpe_t = jnp.swapaxes(cache_kpe, 2, 3)
    per_b = lambda a: pl.BlockSpec((1,) + a.shape[1:], lambda b, pt: (b, 0, 0))
    grid_spec = pltpu.PrefetchScalarGridSpec(
        num_scalar_prefetch=1,
        grid=(nb,),
        in_specs=[per_b(ql), per_b(qp), per_b(cnew), per_b(knew),
                  pl.BlockSpec(memory_space=pl.ANY), pl.BlockSpec(memory_space=pl.ANY)],
        out_specs=pl.BlockSpec((1, rows, c), lambda b, pt: (b, 0, 0)),
        scratch_shapes=[pltpu.VMEM((slots, pages * page, c), F32), pltpu.VMEM((slots, r, pages * page), F32),
                        pltpu.SemaphoreType.DMA((2, slots)),
                        pltpu.VMEM((rows, 1), F32), pltpu.VMEM((rows, 1), F32), pltpu.VMEM((2, rows, c), F32)],
    )
    return pl.pallas_call(
        functools.partial(_decode_attn_kernel, pages=pages, slots=slots, sub=sub, layer=layer, dec_seq=dec_seq),
        grid_spec=grid_spec,
        out_shape=jax.ShapeDtypeStruct((nb, rows, c), F32),
        compiler_params=pltpu.CompilerParams(dimension_semantics=("arbitrary",),
                                             vmem_limit_bytes=VMEM_LIMIT_BYTES),
        name="decode_attn",
    )(page_table, ql, qp, cnew, knew, cache_ckv, kpe_t)


def _out_sample_kernel(ol_ref, gm_ref, cv_ref, x_ref, wuv_ref, wo_ref, gpost_ref, y_ref):
    half = gm_ref.shape[1]
    o = jnp.concatenate(
        [jnp.dot(ol_ref[h].astype(BF16), wuv_ref[h], preferred_element_type=F32) for h in range(N_HEADS)], axis=1)
    mla = (o * gm_ref[...]).astype(BF16)
    y = jnp.dot(mla, wo_ref[0:half, :], preferred_element_type=F32)
    y = y + jnp.dot(cv_ref[...].astype(BF16), wo_ref[half:, :], preferred_element_type=F32)
    y_ref[...] = x_ref[...] + _rms(y, gpost_ref[...])


def _out_sample(ol, gm, cv, x2d, wuv, wo, gpost):
    return pl.pallas_call(
        _out_sample_kernel,
        out_shape=jax.ShapeDtypeStruct(x2d.shape, F32),
        compiler_params=pltpu.CompilerParams(vmem_limit_bytes=VMEM_LIMIT_BYTES),
        name="out_sample",
    )(ol, gm, cv, x2d, wuv, wo, gpost)


def _rot_cols(w):
    r = w.shape[-1] // 2
    return jnp.concatenate([-w[..., r:], w[..., :r]], axis=-1)


def _pack_weights(w_in, w_uq, w_ukv, w_out):
    d, q_lora, kv_lora = w_in.shape[0], w_uq.shape[0], w_ukv.shape[0]
    cq, ckv, kpe, rest = (w_in[:, :C_Q1], w_in[:, C_KV0:C_KV1], w_in[:, C_KV1:C_KV1 + QK_ROPE],
                          w_in[:, C_KV1 + QK_ROPE:])
    w1 = jnp.concatenate([cq, ckv, jnp.zeros((d, QK_NOPE), F32), kpe, _rot_cols(kpe), rest], axis=1).astype(BF16)
    uq = w_uq.reshape(q_lora, N_HEADS, QK_NOPE + QK_ROPE)
    pe = uq[:, :, QK_NOPE:]
    wq = jnp.concatenate([uq[:, :, :QK_NOPE], pe, _rot_cols(pe)], axis=-1).reshape(q_lora, N_HEADS * HEAD_SLAB)
    ukv = w_ukv.reshape(kv_lora, N_HEADS, QK_NOPE + V_HEAD)
    uk, uv = ukv[:, :, :QK_NOPE], ukv[:, :, QK_NOPE:]
    pad = jnp.zeros((kv_lora, N_HEADS, HEAD_SLAB - QK_NOPE), F32)
    wk = jnp.concatenate([uk, pad], axis=-1).reshape(kv_lora, N_HEADS * HEAD_SLAB)
    wvt = uv.reshape(kv_lora, N_HEADS * V_HEAD).T
    wukt = jnp.concatenate([uk.transpose(1, 2, 0), jnp.zeros((N_HEADS, HEAD_SLAB - QK_NOPE, kv_lora), F32)], axis=1)
    wuv = uv.transpose(1, 0, 2)
    return (w1, wq.astype(BF16), wq.T.astype(BF16), wk.astype(BF16), wvt.astype(BF16), wukt.astype(BF16),
            wuv.astype(BF16), w_out.astype(BF16))


PROMPT_TILE = 512
ATTN_TQ = 512
ATTN_TK = 512
ATTN_HEAD_GROUP = 4
DECODE_PAGES = 32
DECODE_SLOTS = 4
DECODE_SUB = 4096


def kernel(x_prompt, x_sample, cache_ckv, cache_kpe, state_conv, page_table, g_pre, w_in, g_qnorm, w_uq,
           g_kvnorm, w_ukv, w_conv, w_out, g_post):
    depth = w_in.shape[0]
    nb, seq, d = x_prompt.shape
    db, dec_seq, _ = x_sample.shape
    past_len = page_table.shape[1] * cache_ckv.shape[2]
    c = cache_ckv.shape[3]

    tabs_p = _rope_tables(jnp.arange(seq, dtype=jnp.int32))
    tabs_s = _rope_tables(jnp.tile(past_len + jnp.arange(dec_seq, dtype=jnp.int32), db))

    xp = x_prompt.reshape(nb * seq, d)
    xs = x_sample.reshape(db * dec_seq, d)
    outs = [[] for _ in range(6)]
    for l in range(depth):
        w1, wq, wqt, wk, wvt, wukt, wuv, wo = _pack_weights(w_in[l], w_uq[l], w_ukv[l], w_out[l])
        gpre, gq, gkv, gpost = g_pre[l][None], g_qnorm[l][None], g_kvnorm[l][None], g_post[l][None]

        qt, k, vt, ckv_p, kpe_p, gm, cv, conv_p = _proj_prompt(
            xp, seq, gpre, w1, gq, wqt, gkv, wk, wvt, w_conv[l], tabs_p, PROMPT_TILE)
        xp = _attention_out(qt, k, vt, gm, cv, xp, wo, gpost, nb, seq, ATTN_TQ, ATTN_TK, ATTN_HEAD_GROUP)
        outs[0].append(ckv_p.reshape(nb, seq, c))
        outs[1].append(kpe_p.reshape(nb, seq, QK_ROPE))
        outs[2].append(conv_p)

        st = state_conv[l].astype(F32)
        s0 = jnp.repeat(st[:, 0], dec_seq, axis=0)
        s1 = jnp.repeat(st[:, 1], dec_seq, axis=0)
        ql, qp, ckv_s, kpe_s, gm_s, cv_s, u_s = _proj_sample(
            xs, dec_seq, gpre, w1, gq, wq, gkv, wukt, w_conv[l], tabs_s[:4], s0, s1)
        rows = dec_seq * N_HEADS
        to_rows = lambda a: a.reshape(N_HEADS, db, dec_seq, -1).transpose(1, 2, 0, 3).reshape(db, rows, -1)
        ol = _decode_attention(page_table, to_rows(ql), to_rows(qp), ckv_s.reshape(db, dec_seq, c),
                               kpe_s.reshape(db, dec_seq, QK_ROPE), cache_ckv, cache_kpe, l,
                               DECODE_PAGES, DECODE_SLOTS, DECODE_SUB)
        ol = ol.reshape(db, dec_seq, N_HEADS, c).transpose(2, 0, 1, 3).reshape(N_HEADS, db * dec_seq, c)
        xs = _out_sample(ol, gm_s, cv_s, xs, wuv, wo, gpost)
        outs[3].append(ckv_s.reshape(db, dec_seq, c))
        outs[4].append(kpe_s.reshape(db, dec_seq, QK_ROPE))
        u_pad = jnp.concatenate([st, u_s.reshape(db, dec_seq, GROUP)], axis=1)
        outs[5].append(u_pad[:, -(CONV_W - 1):])

    return (xp.reshape(nb, seq, d), xs.reshape(db, dec_seq, d), *[jnp.stack(o_) for o_ in outs])
```

```python
import functools

import jax
import jax.numpy as jnp
from jax import lax
from jax.experimental import pallas as pl
from jax.experimental.pallas import tpu as pltpu

N_HEADS = 8
QK_NOPE = 64
QK_ROPE = 32
V_HEAD = 64
CONV_W = 3
ROPE_THETA = 10000.0
EPS = 1e-6
ATTN_SCALE = (QK_NOPE + QK_ROPE) ** -0.5

LANES = 128
HEAD_SLAB = LANES
ONES_ROWS = 16
LOG2_E = 1.4426950408889634
VMEM_LIMIT_BYTES = 56 * 1024 * 1024

F32 = jnp.float32
BF16 = jnp.bfloat16


def _silu(x):
    return x * (1.0 / (1.0 + jnp.exp(-x)))


def _rms(x, g):
    return x * lax.rsqrt(jnp.mean(x * x, axis=-1, keepdims=True) + EPS) * g


def _rope_coeffs(ang, slab_axis, q_scale):
    idx = lax.broadcasted_iota(jnp.int32, ang.shape, slab_axis)
    rope = (idx >= QK_NOPE) & (idx < QK_NOPE + QK_ROPE)
    c = jnp.where(rope, jnp.cos(ang), 0.0)
    s = jnp.where(rope, jnp.sin(ang), 0.0)
    return jnp.where(idx < QK_NOPE, q_scale, c * q_scale), s * q_scale, c, s


def _rope_table_kernel(pos_ref, invf_ref, post_ref, invft_ref, cq1_ref, cq2_ref, ck1_ref, ck2_ref,
                       cq1t_ref, cq2t_ref):
    cq1_ref[...], cq2_ref[...], ck1_ref[...], ck2_ref[...] = _rope_coeffs(
        pos_ref[...] * invf_ref[...], 1, ATTN_SCALE)
    cq1t_ref[...], cq2t_ref[...] = _rope_coeffs(invft_ref[...] * post_ref[...], 0, ATTN_SCALE * LOG2_E)[:2]


def _rope_tables(pos):
    t = pos.shape[0]
    r = QK_ROPE
    inv_freq = ROPE_THETA ** (-jnp.arange(0, r, 2, dtype=F32) / r)
    invf = jnp.zeros((1, LANES), F32).at[0, QK_NOPE:QK_NOPE + r].set(jnp.concatenate([inv_freq, inv_freq]))
    posf = pos.astype(F32)
    out = jax.ShapeDtypeStruct((t, LANES), F32)
    out_t = jax.ShapeDtypeStruct((LANES, t), F32)
    return pl.pallas_call(
        _rope_table_kernel,
        out_shape=(out, out, out, out, out_t, out_t),
        name="rope_tables",
    )(posf.reshape(t, 1), invf, posf.reshape(1, t), invf.reshape(LANES, 1))


C_Q0, C_Q1 = 0, 384
C_KV0, C_KV1 = 384, 640
C_KP0, C_KP1 = 640, 768
C_GM0 = 768
GROUP = 512
D_IN_PACKED = C_GM0 + 5 * GROUP


def _rope_slab(blk, c1, c2):
    return blk * c1 + pltpu.roll(blk, LANES - QK_ROPE, 1) * c2


def _project_common(x_ref, gpre_ref, w1_ref, gq_ref, gkv_ref, ck1_ref, ck2_ref):
    x = x_ref[...]
    xn = _rms(x, gpre_ref[...]).astype(BF16)

    def proj(c0, c1):
        return jnp.dot(xn, w1_ref[:, c0:c1], preferred_element_type=F32)

    cqn = _rms(proj(C_Q0, C_Q1), gq_ref[...]).astype(BF16)
    ckv = _rms(proj(C_KV0, C_KV1), gkv_ref[...])
    kpe_blk = _rope_slab(proj(C_KP0, C_KP1), ck1_ref[...], ck2_ref[...])
    return proj, cqn, ckv, kpe_blk


def _conv_gate(proj, um1, um2, u, wconv_ref):
    w = wconv_ref[...]
    y = w[0:1, :] * um2 + w[1:2, :] * um1 + w[2:3, :] * u
    bg = proj(C_GM0 + GROUP, C_GM0 + 2 * GROUP)
    gc = proj(C_GM0 + 4 * GROUP, C_GM0 + 5 * GROUP)
    return bg * y * _silu(gc)


def _proj_prompt_kernel(x_ref, gpre_ref, w1_ref, gq_ref, wqt_ref, gkv_ref, wk_ref, wvt_ref, wconv_ref,
                        cq1t_ref, cq2t_ref, ck1_ref, ck2_ref,
                        qt_out, k_out, vt_out, ckv_out, kpe_out, gm_out, cv_out, conv_out,
                        carry_ref, *, tiles_per_seq):
    tm = x_ref.shape[0]
    step = pl.program_id(0)
    dn = (((1,), (1,)), ((), ()))
    proj, cqn, ckv, kpe_blk = _project_common(x_ref, gpre_ref, w1_ref, gq_ref, gkv_ref, ck1_ref, ck2_ref)
    qt = lax.dot_general(wqt_ref[...], cqn, dn, preferred_element_type=F32)
    cq1t, cq2t = cq1t_ref[...], cq2t_ref[...]
    for h in range(N_HEADS):
        slab = qt[h * HEAD_SLAB:(h + 1) * HEAD_SLAB, :]
        roped = slab * cq1t + pltpu.roll(slab, HEAD_SLAB - QK_ROPE, 0) * cq2t
        qt_out[h * HEAD_SLAB:(h + 1) * HEAD_SLAB, :] = roped.astype(BF16)
    ckv_out[...] = ckv
    kpe_out[...] = kpe_blk[:, QK_NOPE:QK_NOPE + QK_ROPE]
    ckvb = ckv.astype(BF16)
    kfull = jnp.dot(ckvb, wk_ref[...], preferred_element_type=F32)
    for h in range(N_HEADS):
        k_out[:, h * HEAD_SLAB:(h + 1) * HEAD_SLAB] = (kfull[:, h * HEAD_SLAB:(h + 1) * HEAD_SLAB] + kpe_blk).astype(BF16)
    vt_out[...] = lax.dot_general(wvt_ref[...], ckvb, dn, preferred_element_type=F32).astype(BF16)
    gm_out[...] = _silu(proj(C_GM0, C_GM0 + GROUP)).astype(BF16)

    u = proj(C_GM0 + 2 * GROUP, C_GM0 + 3 * GROUP) * proj(C_GM0 + 3 * GROUP, C_GM0 + 4 * GROUP)

    @pl.when(step % tiles_per_seq == 0)
    def _():
        carry_ref[...] = jnp.zeros_like(carry_ref)

    prev = carry_ref[...]
    p1, p2 = prev[7:8, :], prev[6:7, :]
    row = lax.broadcasted_iota(jnp.int32, u.shape, 0)
    um1 = jnp.where(row == 0, p1, pltpu.roll(u, 1, 0))
    um2 = jnp.where(row == 0, p2, jnp.where(row == 1, p1, pltpu.roll(u, 2, 0)))
    cv_out[...] = _conv_gate(proj, um1, um2, u, wconv_ref).astype(BF16)
    carry_ref[...] = u[tm - 8:tm, :]
    conv_out[0] = u[tm - (CONV_W - 1):tm, :]


def _const_spec(shape):
    nd = len(shape)
    return pl.BlockSpec(shape, lambda *_: (0,) * nd, pipeline_mode=pl.Buffered(1))


def _proj_prompt(x2d, seq, gpre, w1, gq, wqt, gkv, wk, wvt, wconv, tabs, tm):
    t, d = x2d.shape
    nb = t // seq
    hs = N_HEADS * HEAD_SLAB
    hv = N_HEADS * V_HEAD
    cq1, cq2, ck1, ck2, cq1t, cq2t = tabs
    row = lambda w: pl.BlockSpec((tm, w), lambda i: (i, 0))
    col = lambda h: pl.BlockSpec((h, tm), lambda i: (0, i))
    tab = pl.BlockSpec((tm, LANES), lambda i: (i % (seq // tm), 0))
    tab_t = pl.BlockSpec((LANES, tm), lambda i: (0, i % (seq // tm)))
    out_shape = (
        jax.ShapeDtypeStruct((hs, t), BF16),
        jax.ShapeDtypeStruct((t, hs), BF16),
        jax.ShapeDtypeStruct((hv, t), BF16),
        jax.ShapeDtypeStruct((t, gkv.shape[1]), F32),
        jax.ShapeDtypeStruct((t, QK_ROPE), F32),
        jax.ShapeDtypeStruct((t, GROUP), BF16),
        jax.ShapeDtypeStruct((t, GROUP), BF16),
        jax.ShapeDtypeStruct((nb, CONV_W - 1, GROUP), F32),
    )
    return pl.pallas_call(
        functools.partial(_proj_prompt_kernel, tiles_per_seq=seq // tm),
        grid=(t // tm,),
        in_specs=[row(d), _const_spec(gpre.shape), _const_spec(w1.shape), _const_spec(gq.shape),
                  _const_spec(wqt.shape), _const_spec(gkv.shape), _const_spec(wk.shape), _const_spec(wvt.shape),
                  _const_spec(wconv.shape), tab_t, tab_t, tab, tab],
        out_specs=(col(hs), row(hs), col(hv), row(gkv.shape[1]), row(QK_ROPE), row(GROUP), row(GROUP),
                   pl.BlockSpec((1, CONV_W - 1, GROUP), lambda i: (i // (seq // tm), 0, 0))),
        out_shape=out_shape,
        scratch_shapes=[pltpu.VMEM((8, GROUP), F32)],
        compiler_params=pltpu.CompilerParams(dimension_semantics=("arbitrary",),
                                             vmem_limit_bytes=VMEM_LIMIT_BYTES),
        name="proj_prompt",
    )(x2d, gpre, w1, gq, wqt, gkv, wk, wvt, wconv, cq1t, cq2t, ck1, ck2)


def _attn_kernel(qt_ref, k_ref, vt_ref, gm_ref, cv_ref, x_ref, wo_ref, gpost_ref, y_ref, *, tq, tk, group):
    i = pl.program_id(1)
    n_diag = tq // tk
    krow = lax.broadcasted_iota(jnp.int32, (tk, tq), 0)
    qcol = i * tq + lax.broadcasted_iota(jnp.int32, (tk, tq), 1)

    def head_group(hs):
        def tile(j, carries, masked):
            toks = pl.ds(pl.multiple_of(j * tk, tk), tk)
            sts = [jnp.dot(k_ref[toks, h * HEAD_SLAB:(h + 1) * HEAD_SLAB],
                           qt_ref[h * HEAD_SLAB:(h + 1) * HEAD_SLAB, :], preferred_element_type=F32) for h in hs]
            if masked:
                keep = j * tk + krow <= qcol
                sts = [jnp.where(keep, st, -jnp.inf) for st in sts]
            stats = []
            for st, (m, _) in zip(sts, carries):
                m_new = jnp.maximum(m, jnp.max(st, axis=0, keepdims=True))
                stats.append((m_new, jnp.exp2(m - m_new), jnp.exp2(st - m_new).astype(BF16)))
            return tuple(
                (m_new, alpha * acc + jnp.dot(jnp.concatenate([vt_ref[h * V_HEAD:(h + 1) * V_HEAD, toks], ones], 0),
                                              pt, preferred_element_type=F32))
                for h, (m_new, alpha, pt), (_, acc) in zip(hs, stats, carries))

        ones = jnp.ones((ONES_ROWS, tk), BF16)
        init = (jnp.full((1, tq), -jnp.inf, F32), jnp.zeros((V_HEAD + ONES_ROWS, tq), F32))
        carries = tuple(init for _ in hs)
        n_full = i * n_diag
        for d in range(n_diag):
            carries = tile(n_full + d, carries, True)
        carries = lax.fori_loop(0, n_full, lambda j, c: tile(j, c, False), carries)
        return [acc[:V_HEAD] * (1.0 / acc[V_HEAD:V_HEAD + 1]) for _, acc in carries]

    pairs = []
    for g in range(N_HEADS // group):
        outs = head_group(range(g * group, (g + 1) * group))
        for p in range(group // 2):
            pairs.append(jnp.concatenate(outs[2 * p:2 * p + 2], axis=0).T)

    half = gm_ref.shape[1]
    mla = (jnp.concatenate(pairs, axis=1) * gm_ref[...].astype(F32)).astype(BF16)
    y = jnp.dot(mla, wo_ref[0:half, :], preferred_element_type=F32)
    y = y + jnp.dot(cv_ref[...], wo_ref[half:, :], preferred_element_type=F32)
    y_ref[...] = x_ref[...] + _rms(y, gpost_ref[...])


def _attention_out(qt, k, vt, gm, cv, x2d, wo, gpost, nb, seq, tq, tk, group):
    hs, t = qt.shape
    hv = vt.shape[0]
    d = x2d.shape[1]
    assert group % 2 == 0 and N_HEADS % group == 0
    row = lambda w: pl.BlockSpec((tq, w), lambda b, i: (b * (seq // tq) + i, 0))
    return pl.pallas_call(
        functools.partial(_attn_kernel, tq=tq, tk=tk, group=group),
        grid=(nb, seq // tq),
        in_specs=[pl.BlockSpec((hs, tq), lambda b, i: (0, b * (seq // tq) + i)),
                  pl.BlockSpec((seq, hs), lambda b, i: (b, 0)),
                  pl.BlockSpec((hv, seq), lambda b, i: (0, b)),
                  row(gm.shape[1]), row(cv.shape[1]), row(d), _const_spec(wo.shape), _const_spec(gpost.shape)],
        out_specs=row(d),
        out_shape=jax.ShapeDtypeStruct((t, d), F32),
        compiler_params=pltpu.CompilerParams(dimension_semantics=("arbitrary", "arbitrary"),
                                             vmem_limit_bytes=VMEM_LIMIT_BYTES),
        name="prompt_attn",
    )(qt, k, vt, gm, cv, x2d, wo, gpost)


def _proj_sample_kernel(x_ref, gpre_ref, w1_ref, gq_ref, wq_ref, gkv_ref, wukt_ref, wconv_ref,
                        cq1_ref, cq2_ref, ck1_ref, ck2_ref, s0_ref, s1_ref,
                        ql_out, qp_out, ckv_out, kpe_out, gm_out, cv_out, conv_out, *, dec_seq):
    proj, cqn, ckv, kpe_blk = _project_common(x_ref, gpre_ref, w1_ref, gq_ref, gkv_ref, ck1_ref, ck2_ref)
    qraw = jnp.dot(cqn, wq_ref[...], preferred_element_type=F32)
    cq1, cq2 = cq1_ref[...], cq2_ref[...]
    for h in range(N_HEADS):
        qh = _rope_slab(qraw[:, h * HEAD_SLAB:(h + 1) * HEAD_SLAB], cq1, cq2)
        ql_out[h] = jnp.dot(qh.astype(BF16), wukt_ref[h], preferred_element_type=F32).astype(BF16)
        qp_out[h] = pltpu.roll(qh, LANES - QK_NOPE, 1)[:, 0:QK_ROPE].astype(BF16)
    ckv_out[...] = ckv
    kpe_out[...] = kpe_blk[:, QK_NOPE:QK_NOPE + QK_ROPE]
    gm_out[...] = _silu(proj(C_GM0, C_GM0 + GROUP))

    u = proj(C_GM0 + 2 * GROUP, C_GM0 + 3 * GROUP) * proj(C_GM0 + 3 * GROUP, C_GM0 + 4 * GROUP)
    t_in_seq = lax.broadcasted_iota(jnp.int32, u.shape, 0) % dec_seq
    s0, s1 = s0_ref[...], s1_ref[...]
    um1 = jnp.where(t_in_seq == 0, s1, pltpu.roll(u, 1, 0))
    um2 = jnp.where(t_in_seq == 0, s0, jnp.where(t_in_seq == 1, s1, pltpu.roll(u, 2, 0)))
    cv_out[...] = _conv_gate(proj, um1, um2, u, wconv_ref)
    conv_out[...] = u


def _proj_sample(x2d, dec_seq, gpre, w1, gq, wq, gkv, wukt, wconv, tabs, s0, s1):
    t, d = x2d.shape
    c = gkv.shape[1]
    out_shape = (
        jax.ShapeDtypeStruct((N_HEADS, t, c), BF16),
        jax.ShapeDtypeStruct((N_HEADS, t, QK_ROPE), BF16),
        jax.ShapeDtypeStruct((t, c), F32),
        jax.ShapeDtypeStruct((t, QK_ROPE), F32),
        jax.ShapeDtypeStruct((t, GROUP), F32),
        jax.ShapeDtypeStruct((t, GROUP), F32),
        jax.ShapeDtypeStruct((t, GROUP), F32),
    )
    return pl.pallas_call(
        functools.partial(_proj_sample_kernel, dec_seq=dec_seq),
        out_shape=out_shape,
        compiler_params=pltpu.CompilerParams(vmem_limit_bytes=VMEM_LIMIT_BYTES),
        name="proj_sample",
    )(x2d, gpre, w1, gq, wq, gkv, wukt, wconv, *tabs, s0, s1)


def _decode_attn_kernel(pt_ref, ql_ref, qp_ref, cnew_ref, knew_ref, y_hbm, x_hbm, o_ref,
                        ybuf, xbuf, sem, m_ref, l_ref, acc_ref, *, pages, slots, layer, dec_seq):
    b = pl.program_id(0)
    n_b = pl.num_programs(0)
    n_chunks = pt_ref.shape[1] // pages
    page = ybuf.shape[1] // pages
    rows = ql_ref.shape[1]
    dn = (((1,), (1,)), ((), ()))
    ahead = slots - 1

    def chunk_copies(bb, ch, slot, page_ids):
        cps = []
        for i in range(pages):
            pid = page_ids(bb, ch * pages + i)
            tok = pl.ds(i * page, page)
            cps.append(pltpu.make_async_copy(y_hbm.at[layer, pid], ybuf.at[slot, tok, :], sem.at[0, slot]))
            cps.append(pltpu.make_async_copy(x_hbm.at[layer, pid], xbuf.at[slot, :, tok], sem.at[1, slot]))
        return cps

    def start(bb, ch, slot):
        for cp in chunk_copies(bb, ch, slot, lambda r, j: pt_ref[r, j]):
            cp.start()

    def wait(slot):
        for cp in chunk_copies(0, 0, slot, lambda r, j: 0):
            cp.wait()

    @pl.when(b == 0)
    def _():
        for k in range(ahead):
            start(0, k, k)

    ql, qp = ql_ref[0], qp_ref[0]
    qlf, qpf = ql.astype(F32), qp.astype(F32)
    tok_of_row = lax.broadcasted_iota(jnp.int32, (rows, 1), 0) // N_HEADS
    s_new = []
    for j in range(dec_seq):
        sj = (jnp.sum(qlf * cnew_ref[0, j:j + 1, :], axis=-1, keepdims=True)
              + jnp.sum(qpf * knew_ref[0, j:j + 1, :], axis=-1, keepdims=True))
        s_new.append(jnp.where(tok_of_row >= j, sj, -jnp.inf))
    m_n = functools.reduce(jnp.maximum, s_new)
    p_n = [jnp.exp(sj - m_n) for sj in s_new]
    m_ref[...] = m_n
    l_ref[...] = functools.reduce(lambda a, e: a + e, p_n)
    acc_ref[0] = functools.reduce(lambda a, e: a + e,
                                  [p_n[j] * cnew_ref[0, j:j + 1, :] for j in range(dec_seq)])
    acc_ref[1] = jnp.zeros(acc_ref.shape[1:], F32)

    half = pages * page // 2
    toks = [slice(h * half, (h + 1) * half) for h in range(2)]

    def scores(slot):
        yb = [ybuf[slot, t, :].astype(BF16) for t in toks]
        xb = [xbuf[slot, :, t].astype(BF16) for t in toks]
        s = jnp.concatenate(
            [lax.dot_general(ql, yb[h], dn, preferred_element_type=F32)
             + jnp.dot(qp, xb[h], preferred_element_type=F32) for h in range(2)], axis=1)
        return s, yb

    def softmax(s):
        m = m_ref[...]
        m_new = jnp.maximum(m, jnp.max(s, axis=-1, keepdims=True))
        alpha = jnp.exp(m - m_new)
        p = jnp.exp(s - m_new)
        l_ref[...] = alpha * l_ref[...] + jnp.sum(p, axis=-1, keepdims=True)
        m_ref[...] = m_new
        return alpha, p.astype(BF16)

    def values(alpha, p, yb):
        for h in range(2):
            acc_ref[h] = alpha * acc_ref[h] + jnp.dot(p[:, toks[h]], yb[h], preferred_element_type=F32)

    def trip(it, carry):
        pending = None
        for k in range(slots):
            ch = it * slots + k
            nxt = ch + ahead
            nxt_slot = (k + ahead) % slots

            @pl.when(nxt < n_chunks)
            def _():
                start(b, nxt, nxt_slot)

            @pl.when((nxt >= n_chunks) & (b + 1 < n_b))
            def _():
                start(b + 1, nxt - n_chunks, nxt_slot)

            wait(k)
            s, yb = scores(k)
            if pending is not None:
                values(*pending)
            pending = softmax(s) + (yb,)
        values(*pending)
        return carry

    lax.fori_loop(0, n_chunks // slots, trip, 0)
    o_ref[0] = (acc_ref[0] + acc_ref[1]) * (1.0 / l_ref[...])


def _decode_attention(page_table, ql, qp, cnew, knew, cache_ckv, cache_kpe, layer, pages, slots):
    nb, rows, c = ql.shape
    dec_seq = cnew.shape[1]
    page, r = cache_kpe.shape[2:]
    n_chunks = page_table.shape[1] // pages
    assert n_chunks * pages == page_table.shape[1] and n_chunks % slots == 0 and slots - 1 <= n_chunks
    kpe_t = jnp.swapaxes(cache_kpe, 2, 3)
    per_b = lambda a: pl.BlockSpec((1,) + a.shape[1:], lambda b, pt: (b, 0, 0))
    grid_spec = pltpu.PrefetchScalarGridSpec(
        num_scalar_prefetch=1,
        grid=(nb,),
        in_specs=[per_b(ql), per_b(qp), per_b(cnew), per_b(knew),
                  pl.BlockSpec(memory_space=pl.ANY), pl.BlockSpec(memory_space=pl.ANY)],
        out_specs=pl.BlockSpec((1, rows, c), lambda b, pt: (b, 0, 0)),
        scratch_shapes=[pltpu.VMEM((slots, pages * page, c), F32), pltpu.VMEM((slots, r, pages * page), F32),
                        pltpu.SemaphoreType.DMA((2, slots)),
                        pltpu.VMEM((rows, 1), F32), pltpu.VMEM((rows, 1), F32), pltpu.VMEM((2, rows, c), F32)],
    )
    return pl.pallas_call(
        functools.partial(_decode_attn_kernel, pages=pages, slots=slots, layer=layer, dec_seq=dec_seq),
        grid_spec=grid_spec,
        out_shape=jax.ShapeDtypeStruct((nb, rows, c), F32),
        compiler_params=pltpu.CompilerParams(dimension_semantics=("arbitrary",),
                                             vmem_limit_bytes=VMEM_LIMIT_BYTES),
        name="decode_attn",
    )(page_table, ql, qp, cnew, knew, cache_ckv, kpe_t)


def _out_sample_kernel(ol_ref, gm_ref, cv_ref, x_ref, wuv_ref, wo_ref, gpost_ref, y_ref):
    half = gm_ref.shape[1]
    o = jnp.concatenate(
        [jnp.dot(ol_ref[h].astype(BF16), wuv_ref[h], preferred_element_type=F32) for h in range(N_HEADS)], axis=1)
    mla = (o * gm_ref[...]).astype(BF16)
    y = jnp.dot(mla, wo_ref[0:half, :], preferred_element_type=F32)
    y = y + jnp.dot(cv_ref[...].astype(BF16), wo_ref[half:, :], preferred_element_type=F32)
    y_ref[...] = x_ref[...] + _rms(y, gpost_ref[...])


def _out_sample(ol, gm, cv, x2d, wuv, wo, gpost):
    return pl.pallas_call(
        _out_sample_kernel,
        out_shape=jax.ShapeDtypeStruct(x2d.shape, F32),
        compiler_params=pltpu.CompilerParams(vmem_limit_bytes=VMEM_LIMIT_BYTES),
        name="out_sample",
    )(ol, gm, cv, x2d, wuv, wo, gpost)


def _rot_cols(w):
    r = w.shape[-1] // 2
    return jnp.concatenate([-w[..., r:], w[..., :r]], axis=-1)


def _pack_weights(w_in, w_uq, w_ukv, w_out):
    d, q_lora, kv_lora = w_in.shape[0], w_uq.shape[0], w_ukv.shape[0]
    cq, ckv, kpe, rest = (w_in[:, :C_Q1], w_in[:, C_KV0:C_KV1], w_in[:, C_KV1:C_KV1 + QK_ROPE],
                          w_in[:, C_KV1 + QK_ROPE:])
    w1 = jnp.concatenate([cq, ckv, jnp.zeros((d, QK_NOPE), F32), kpe, _rot_cols(kpe), rest], axis=1).astype(BF16)
    uq = w_uq.reshape(q_lora, N_HEADS, QK_NOPE + QK_ROPE)
    pe = uq[:, :, QK_NOPE:]
    wq = jnp.concatenate([uq[:, :, :QK_NOPE], pe, _rot_cols(pe)], axis=-1).reshape(q_lora, N_HEADS * HEAD_SLAB)
    ukv = w_ukv.reshape(kv_lora, N_HEADS, QK_NOPE + V_HEAD)
    uk, uv = ukv[:, :, :QK_NOPE], ukv[:, :, QK_NOPE:]
    pad = jnp.zeros((kv_lora, N_HEADS, HEAD_SLAB - QK_NOPE), F32)
    wk = jnp.concatenate([uk, pad], axis=-1).reshape(kv_lora, N_HEADS * HEAD_SLAB)
    wvt = uv.reshape(kv_lora, N_HEADS * V_HEAD).T
    wukt = jnp.concatenate([uk.transpose(1, 2, 0), jnp.zeros((N_HEADS, HEAD_SLAB - QK_NOPE, kv_lora), F32)], axis=1)
    wuv = uv.transpose(1, 0, 2)
    return (w1, wq.astype(BF16), wq.T.astype(BF16), wk.astype(BF16), wvt.astype(BF16), wukt.astype(BF16),
            wuv.astype(BF16), w_out.astype(BF16))


PROMPT_TILE = 512
ATTN_TQ = 512
ATTN_TK = 512
ATTN_HEAD_GROUP = 4
DECODE_PAGES = 32
DECODE_SLOTS = 4


def kernel(x_prompt, x_sample, cache_ckv, cache_kpe, state_conv, page_table, g_pre, w_in, g_qnorm, w_uq,
           g_kvnorm, w_ukv, w_conv, w_out, g_post):
    depth = w_in.shape[0]
    nb, seq, d = x_prompt.shape
    db, dec_seq, _ = x_sample.shape
    past_len = page_table.shape[1] * cache_ckv.shape[2]
    c = cache_ckv.shape[3]

    tabs_p = _rope_tables(jnp.arange(seq, dtype=jnp.int32))
    tabs_s = _rope_tables(jnp.tile(past_len + jnp.arange(dec_seq, dtype=jnp.int32), db))

    xp = x_prompt.reshape(nb * seq, d)
    xs = x_sample.reshape(db * dec_seq, d)
    outs = [[] for _ in range(6)]
    for l in range(depth):
        w1, wq, wqt, wk, wvt, wukt, wuv, wo = _pack_weights(w_in[l], w_uq[l], w_ukv[l], w_out[l])
        gpre, gq, gkv, gpost = g_pre[l][None], g_qnorm[l][None], g_kvnorm[l][None], g_post[l][None]

        qt, k, vt, ckv_p, kpe_p, gm, cv, conv_p = _proj_prompt(
            xp, seq, gpre, w1, gq, wqt, gkv, wk, wvt, w_conv[l], tabs_p, PROMPT_TILE)
        xp = _attention_out(qt, k, vt, gm, cv, xp, wo, gpost, nb, seq, ATTN_TQ, ATTN_TK, ATTN_HEAD_GROUP)
        outs[0].append(ckv_p.reshape(nb, seq, c))
        outs[1].append(kpe_p.reshape(nb, seq, QK_ROPE))
        outs[2].append(conv_p)

        st = state_conv[l].astype(F32)
        s0 = jnp.repeat(st[:, 0], dec_seq, axis=0)
        s1 = jnp.repeat(st[:, 1], dec_seq, axis=0)
        ql, qp, ckv_s, kpe_s, gm_s, cv_s, u_s = _proj_sample(
            xs, dec_seq, gpre, w1, gq, wq, gkv, wukt, w_conv[l], tabs_s[:4], s0, s1)
        rows = dec_seq * N_HEADS
        to_rows = lambda a: a.reshape(N_HEADS, db, dec_seq, -1).transpose(1, 2, 0, 3).reshape(db, rows, -1)
        ol = _decode_attention(page_table, to_rows(ql), to_rows(qp), ckv_s.reshape(db, dec_seq, c),
                               kpe_s.reshape(db, dec_seq, QK_ROPE), cache_ckv, cache_kpe, l,
                               DECODE_PAGES, DECODE_SLOTS)
        ol = ol.reshape(db, dec_seq, N_HEADS, c).transpose(2, 0, 1, 3).reshape(N_HEADS, db * dec_seq, c)
        xs = _out_sample(ol, gm_s, cv_s, xs, wuv, wo, gpost)
        outs[3].append(ckv_s.reshape(db, dec_seq, c))
        outs[4].append(kpe_s.reshape(db, dec_seq, QK_ROPE))
        u_pad = jnp.concatenate([st, u_s.reshape(db, dec_seq, GROUP)], axis=1)
        outs[5].append(u_pad[:, -(CONV_W - 1):])

    return (xp.reshape(nb, seq, d), xs.reshape(db, dec_seq, d), *[jnp.stack(o_) for o_ in outs])
```

```python
import functools

import jax
import jax.numpy as jnp
from jax import lax
from jax.experimental import pallas as pl
from jax.experimental.pallas import tpu as pltpu

N_HEADS = 8
QK_NOPE = 64
QK_ROPE = 32
V_HEAD = 64
CONV_W = 3
ROPE_THETA = 10000.0
EPS = 1e-6
ATTN_SCALE = (QK_NOPE + QK_ROPE) ** -0.5

LANES = 128
HEAD_SLAB = LANES
ONES_ROWS = 16
LOG2_E = 1.4426950408889634
VMEM_LIMIT_BYTES = 56 * 1024 * 1024

F32 = jnp.float32
BF16 = jnp.bfloat16


def _silu(x):
    return x * (1.0 / (1.0 + jnp.exp(-x)))


def _rms(x, g):
    return x * lax.rsqrt(jnp.mean(x * x, axis=-1, keepdims=True) + EPS) * g


def _rope_coeffs(ang, slab_axis, q_scale):
    idx = lax.broadcasted_iota(jnp.int32, ang.shape, slab_axis)
    rope = (idx >= QK_NOPE) & (idx < QK_NOPE + QK_ROPE)
    c = jnp.where(rope, jnp.cos(ang), 0.0)
    s = jnp.where(rope, jnp.sin(ang), 0.0)
    return jnp.where(idx < QK_NOPE, q_scale, c * q_scale), s * q_scale, c, s


def _rope_table_kernel(pos_ref, invf_ref, post_ref, invft_ref, cq1_ref, cq2_ref, ck1_ref, ck2_ref,
                       cq1t_ref, cq2t_ref):
    cq1_ref[...], cq2_ref[...], ck1_ref[...], ck2_ref[...] = _rope_coeffs(
        pos_ref[...] * invf_ref[...], 1, ATTN_SCALE)
    cq1t_ref[...], cq2t_ref[...] = _rope_coeffs(invft_ref[...] * post_ref[...], 0, ATTN_SCALE * LOG2_E)[:2]


def _rope_tables(pos):
    t = pos.shape[0]
    r = QK_ROPE
    inv_freq = ROPE_THETA ** (-jnp.arange(0, r, 2, dtype=F32) / r)
    invf = jnp.zeros((1, LANES), F32).at[0, QK_NOPE:QK_NOPE + r].set(jnp.concatenate([inv_freq, inv_freq]))
    posf = pos.astype(F32)
    out = jax.ShapeDtypeStruct((t, LANES), F32)
    out_t = jax.ShapeDtypeStruct((LANES, t), F32)
    return pl.pallas_call(
        _rope_table_kernel,
        out_shape=(out, out, out, out, out_t, out_t),
        name="rope_tables",
    )(posf.reshape(t, 1), invf, posf.reshape(1, t), invf.reshape(LANES, 1))


C_Q0, C_Q1 = 0, 384
C_KV0, C_KV1 = 384, 640
C_KP0, C_KP1 = 640, 768
C_GM0 = 768
GROUP = 512
D_IN_PACKED = C_GM0 + 5 * GROUP


def _rope_slab(blk, c1, c2):
    return blk * c1 + pltpu.roll(blk, LANES - QK_ROPE, 1) * c2


def _project_common(x_ref, gpre_ref, w1_ref, gq_ref, gkv_ref, ck1_ref, ck2_ref):
    x = x_ref[...]
    xn = _rms(x, gpre_ref[...]).astype(BF16)

    def proj(c0, c1):
        return jnp.dot(xn, w1_ref[:, c0:c1], preferred_element_type=F32)

    cqn = _rms(proj(C_Q0, C_Q1), gq_ref[...]).astype(BF16)
    ckv = _rms(proj(C_KV0, C_KV1), gkv_ref[...])
    kpe_blk = _rope_slab(proj(C_KP0, C_KP1), ck1_ref[...], ck2_ref[...])
    return proj, cqn, ckv, kpe_blk


def _conv_gate(proj, um1, um2, u, wconv_ref):
    w = wconv_ref[...]
    y = w[0:1, :] * um2 + w[1:2, :] * um1 + w[2:3, :] * u
    bg = proj(C_GM0 + GROUP, C_GM0 + 2 * GROUP)
    gc = proj(C_GM0 + 4 * GROUP, C_GM0 + 5 * GROUP)
    return bg * y * _silu(gc)


def _proj_prompt_kernel(x_ref, gpre_ref, w1_ref, gq_ref, wqt_ref, gkv_ref, wk_ref, wvt_ref, wconv_ref,
                        cq1t_ref, cq2t_ref, ck1_ref, ck2_ref,
                        qt_out, k_out, vt_out, ckv_out, kpet_out, gm_out, cv_out, conv_out,
                        carry_ref, *, tiles_per_seq):
    tm = x_ref.shape[0]
    step = pl.program_id(0)
    dn = (((1,), (1,)), ((), ()))
    proj, cqn, ckv, kpe_blk = _project_common(x_ref, gpre_ref, w1_ref, gq_ref, gkv_ref, ck1_ref, ck2_ref)
    qt = lax.dot_general(wqt_ref[...], cqn, dn, preferred_element_type=F32)
    cq1t, cq2t = cq1t_ref[...], cq2t_ref[...]
    for h in range(N_HEADS):
        slab = qt[h * HEAD_SLAB:(h + 1) * HEAD_SLAB, :]
        roped = slab * cq1t + pltpu.roll(slab, HEAD_SLAB - QK_ROPE, 0) * cq2t
        qt_out[h * HEAD_SLAB:(h + 1) * HEAD_SLAB, :] = roped.astype(BF16)
    ckv_out[...] = ckv
    kpet_out[0] = kpe_blk.T[QK_NOPE:QK_NOPE + QK_ROPE, :]
    ckvb = ckv.astype(BF16)
    kfull = jnp.dot(ckvb, wk_ref[...], preferred_element_type=F32)
    for h in range(N_HEADS):
        k_out[:, h * HEAD_SLAB:(h + 1) * HEAD_SLAB] = (kfull[:, h * HEAD_SLAB:(h + 1) * HEAD_SLAB] + kpe_blk).astype(BF16)
    vt_out[...] = lax.dot_general(wvt_ref[...], ckvb, dn, preferred_element_type=F32).astype(BF16)
    gm_out[...] = _silu(proj(C_GM0, C_GM0 + GROUP)).astype(BF16)

    u = proj(C_GM0 + 2 * GROUP, C_GM0 + 3 * GROUP) * proj(C_GM0 + 3 * GROUP, C_GM0 + 4 * GROUP)

    @pl.when(step % tiles_per_seq == 0)
    def _():
        carry_ref[...] = jnp.zeros_like(carry_ref)

    prev = carry_ref[...]
    p1, p2 = prev[7:8, :], prev[6:7, :]
    row = lax.broadcasted_iota(jnp.int32, u.shape, 0)
    um1 = jnp.where(row == 0, p1, pltpu.roll(u, 1, 0))
    um2 = jnp.where(row == 0, p2, jnp.where(row == 1, p1, pltpu.roll(u, 2, 0)))
    cv_out[...] = _conv_gate(proj, um1, um2, u, wconv_ref).astype(BF16)
    carry_ref[...] = u[tm - 8:tm, :]
    conv_out[0] = u[tm - (CONV_W - 1):tm, :]


def _const_spec(shape):
    nd = len(shape)
    return pl.BlockSpec(shape, lambda *_: (0,) * nd, pipeline_mode=pl.Buffered(1))


def _proj_prompt(x2d, seq, gpre, w1, gq, wqt, gkv, wk, wvt, wconv, tabs, tm):
    t, d = x2d.shape
    nb = t // seq
    hs = N_HEADS * HEAD_SLAB
    hv = N_HEADS * V_HEAD
    cq1, cq2, ck1, ck2, cq1t, cq2t = tabs
    row = lambda w: pl.BlockSpec((tm, w), lambda i: (i, 0))
    col = lambda h: pl.BlockSpec((h, tm), lambda i: (0, i))
    tab = pl.BlockSpec((tm, LANES), lambda i: (i % (seq // tm), 0))
    tab_t = pl.BlockSpec((LANES, tm), lambda i: (0, i % (seq // tm)))
    out_shape = (
        jax.ShapeDtypeStruct((hs, t), BF16),
        jax.ShapeDtypeStruct((t, hs), BF16),
        jax.ShapeDtypeStruct((hv, t), BF16),
        jax.ShapeDtypeStruct((t, gkv.shape[1]), F32),
        jax.ShapeDtypeStruct((nb, QK_ROPE, seq), F32),
        jax.ShapeDtypeStruct((t, GROUP), BF16),
        jax.ShapeDtypeStruct((t, GROUP), BF16),
        jax.ShapeDtypeStruct((nb, CONV_W - 1, GROUP), F32),
    )
    return pl.pallas_call(
        functools.partial(_proj_prompt_kernel, tiles_per_seq=seq // tm),
        grid=(t // tm,),
        in_specs=[row(d), _const_spec(gpre.shape), _const_spec(w1.shape), _const_spec(gq.shape),
                  _const_spec(wqt.shape), _const_spec(gkv.shape), _const_spec(wk.shape), _const_spec(wvt.shape),
                  _const_spec(wconv.shape), tab_t, tab_t, tab, tab],
        out_specs=(col(hs), row(hs), col(hv), row(gkv.shape[1]),
                   pl.BlockSpec((1, QK_ROPE, tm), lambda i: (i // (seq // tm), 0, i % (seq // tm))),
                   row(GROUP), row(GROUP),
                   pl.BlockSpec((1, CONV_W - 1, GROUP), lambda i: (i // (seq // tm), 0, 0))),
        out_shape=out_shape,
        scratch_shapes=[pltpu.VMEM((8, GROUP), F32)],
        compiler_params=pltpu.CompilerParams(dimension_semantics=("arbitrary",),
                                             vmem_limit_bytes=VMEM_LIMIT_BYTES),
        name="proj_prompt",
    )(x2d, gpre, w1, gq, wqt, gkv, wk, wvt, wconv, cq1t, cq2t, ck1, ck2)


def _attn_kernel(qt_ref, k_ref, vt_ref, gm_ref, cv_ref, x_ref, wo_ref, gpost_ref, y_ref, *, tq, tk, group):
    i = pl.program_id(1)
    hq = tq // 2
    tri = (lax.broadcasted_iota(jnp.int32, (hq, hq), 0) <= lax.broadcasted_iota(jnp.int32, (hq, hq), 1))

    def slab(h):
        return slice(h * HEAD_SLAB, (h + 1) * HEAD_SLAB)

    def values(h, toks):
        ones = jnp.ones((ONES_ROWS, toks.size), BF16)
        return jnp.concatenate([vt_ref[h * V_HEAD:(h + 1) * V_HEAD, toks], ones], 0)

    def head_group(hs):
        def full_tile(j, carries):
            toks = pl.ds(pl.multiple_of(j * tk, tk), tk)
            sts = [jnp.dot(k_ref[toks, slab(h)], qt_ref[slab(h), :], preferred_element_type=F32) for h in hs]
            stats = []
            for st, (m, _) in zip(sts, carries):
                m_new = jnp.maximum(m, jnp.max(st, axis=0, keepdims=True))
                stats.append((m_new, jnp.exp2(m - m_new), jnp.exp2(st - m_new).astype(BF16)))
            return tuple((m_new, alpha * acc + jnp.dot(values(h, toks), pt, preferred_element_type=F32))
                         for h, (m_new, alpha, pt), (_, acc) in zip(hs, stats, carries))

        def diagonal_tile():
            base = pl.multiple_of(i * tq, tq)
            ka, kb = pl.ds(base, hq), pl.ds(base + hq, hq)
            sa = [jnp.dot(k_ref[ka, slab(h)], qt_ref[slab(h), :], preferred_element_type=F32) for h in hs]
            sb = [jnp.dot(k_ref[kb, slab(h)], qt_ref[slab(h), hq:], preferred_element_type=F32) for h in hs]
            out = []
            stats = []
            for a, b_ in zip(sa, sb):
                a_lo = jnp.where(tri, a[:, :hq], -jnp.inf)
                a_hi = a[:, hq:]
                b_ = jnp.where(tri, b_, -jnp.inf)
                m_lo = jnp.max(a_lo, axis=0, keepdims=True)
                m_hi = jnp.maximum(jnp.max(a_hi, axis=0, keepdims=True), jnp.max(b_, axis=0, keepdims=True))
                pa = jnp.concatenate([jnp.exp2(a_lo - m_lo), jnp.exp2(a_hi - m_hi)], axis=1).astype(BF16)
                pb = jnp.exp2(b_ - m_hi).astype(BF16)
                stats.append((jnp.concatenate([m_lo, m_hi], axis=1), pa, pb))
            for h, (m_new, pa, pb) in zip(hs, stats):
                acc_a = jnp.dot(values(h, ka), pa, preferred_element_type=F32)
                acc_b = jnp.dot(values(h, kb), pb, preferred_element_type=F32)
                out.append((m_new, jnp.concatenate([acc_a[:, :hq], acc_a[:, hq:] + acc_b], axis=1)))
            return tuple(out)

        carries = lax.fori_loop(0, i * (tq // tk), full_tile, diagonal_tile())
        return [acc[:V_HEAD] * (1.0 / acc[V_HEAD:V_HEAD + 1]) for _, acc in carries]

    pairs = []
    for g in range(N_HEADS // group):
        outs = head_group(range(g * group, (g + 1) * group))
        for p in range(group // 2):
            pairs.append(jnp.concatenate(outs[2 * p:2 * p + 2], axis=0).T)

    half = gm_ref.shape[1]
    mla = (jnp.concatenate(pairs, axis=1) * gm_ref[...].astype(F32)).astype(BF16)
    y = jnp.dot(mla, wo_ref[0:half, :], preferred_element_type=F32)
    y = y + jnp.dot(cv_ref[...], wo_ref[half:, :], preferred_element_type=F32)
    y_ref[...] = x_ref[...] + _rms(y, gpost_ref[...])


def _attention_out(qt, k, vt, gm, cv, x2d, wo, gpost, nb, seq, tq, tk, group):
    hs, t = qt.shape
    hv = vt.shape[0]
    d = x2d.shape[1]
    assert group % 2 == 0 and N_HEADS % group == 0
    row = lambda w: pl.BlockSpec((tq, w), lambda b, i: (b * (seq // tq) + i, 0))
    return pl.pallas_call(
        functools.partial(_attn_kernel, tq=tq, tk=tk, group=group),
        grid=(nb, seq // tq),
        in_specs=[pl.BlockSpec((hs, tq), lambda b, i: (0, b * (seq // tq) + i)),
                  pl.BlockSpec((seq, hs), lambda b, i: (b, 0)),
                  pl.BlockSpec((hv, seq), lambda b, i: (0, b)),
                  row(gm.shape[1]), row(cv.shape[1]), row(d), _const_spec(wo.shape), _const_spec(gpost.shape)],
        out_specs=row(d),
        out_shape=jax.ShapeDtypeStruct((t, d), F32),
        compiler_params=pltpu.CompilerParams(dimension_semantics=("arbitrary", "arbitrary"),
                                             vmem_limit_bytes=VMEM_LIMIT_BYTES),
        name="prompt_attn",
    )(qt, k, vt, gm, cv, x2d, wo, gpost)


def _proj_sample_kernel(x_ref, gpre_ref, w1_ref, gq_ref, wq_ref, gkv_ref, wukt_ref, wconv_ref,
                        cq1_ref, cq2_ref, ck1_ref, ck2_ref, s0_ref, s1_ref,
                        ql_out, qp_out, ckv_out, kpe_out, gm_out, cv_out, conv_out, *, dec_seq):
    proj, cqn, ckv, kpe_blk = _project_common(x_ref, gpre_ref, w1_ref, gq_ref, gkv_ref, ck1_ref, ck2_ref)
    qraw = jnp.dot(cqn, wq_ref[...], preferred_element_type=F32)
    cq1, cq2 = cq1_ref[...], cq2_ref[...]
    for h in range(N_HEADS):
        qh = _rope_slab(qraw[:, h * HEAD_SLAB:(h + 1) * HEAD_SLAB], cq1, cq2)
        ql_out[h] = jnp.dot(qh.astype(BF16), wukt_ref[h], preferred_element_type=F32).astype(BF16)
        qp_out[h] = pltpu.roll(qh, LANES - QK_NOPE, 1)[:, 0:QK_ROPE].astype(BF16)
    ckv_out[...] = ckv
    kpe_out[...] = kpe_blk[:, QK_NOPE:QK_NOPE + QK_ROPE]
    gm_out[...] = _silu(proj(C_GM0, C_GM0 + GROUP))

    u = proj(C_GM0 + 2 * GROUP, C_GM0 + 3 * GROUP) * proj(C_GM0 + 3 * GROUP, C_GM0 + 4 * GROUP)
    t_in_seq = lax.broadcasted_iota(jnp.int32, u.shape, 0) % dec_seq
    s0, s1 = s0_ref[...], s1_ref[...]
    um1 = jnp.where(t_in_seq == 0, s1, pltpu.roll(u, 1, 0))
    um2 = jnp.where(t_in_seq == 0, s0, jnp.where(t_in_seq == 1, s1, pltpu.roll(u, 2, 0)))
    cv_out[...] = _conv_gate(proj, um1, um2, u, wconv_ref)
    conv_out[...] = u


def _proj_sample(x2d, dec_seq, gpre, w1, gq, wq, gkv, wukt, wconv, tabs, s0, s1):
    t, d = x2d.shape
    c = gkv.shape[1]
    out_shape = (
        jax.ShapeDtypeStruct((N_HEADS, t, c), BF16),
        jax.ShapeDtypeStruct((N_HEADS, t, QK_ROPE), BF16),
        jax.ShapeDtypeStruct((t, c), F32),
        jax.ShapeDtypeStruct((t, QK_ROPE), F32),
        jax.ShapeDtypeStruct((t, GROUP), F32),
        jax.ShapeDtypeStruct((t, GROUP), F32),
        jax.ShapeDtypeStruct((t, GROUP), F32),
    )
    return pl.pallas_call(
        functools.partial(_proj_sample_kernel, dec_seq=dec_seq),
        out_shape=out_shape,
        compiler_params=pltpu.CompilerParams(vmem_limit_bytes=VMEM_LIMIT_BYTES),
        name="proj_sample",
    )(x2d, gpre, w1, gq, wq, gkv, wukt, wconv, *tabs, s0, s1)


def _decode_attn_kernel(pt_ref, ql_ref, qp_ref, cnew_ref, knew_ref, y_hbm, x_hbm, o_ref,
                        ybuf, xbuf, sem, m_ref, l_ref, acc_ref, *, pages, slots, layer, dec_seq):
    b = pl.program_id(0)
    n_b = pl.num_programs(0)
    n_chunks = pt_ref.shape[1] // pages
    page = ybuf.shape[1] // pages
    rows = ql_ref.shape[1]
    dn = (((1,), (1,)), ((), ()))
    ahead = slots - 1

    def chunk_copies(bb, ch, slot, page_ids):
        cps = []
        for i in range(pages):
            pid = page_ids(bb, ch * pages + i)
            tok = pl.ds(i * page, page)
            cps.append(pltpu.make_async_copy(y_hbm.at[layer, pid], ybuf.at[slot, tok, :], sem.at[0, slot]))
            cps.append(pltpu.make_async_copy(x_hbm.at[layer, pid], xbuf.at[slot, :, tok], sem.at[1, slot]))
        return cps

    def start(bb, ch, slot):
        for cp in chunk_copies(bb, ch, slot, lambda r, j: pt_ref[r, j]):
            cp.start()

    def wait(slot):
        for cp in chunk_copies(0, 0, slot, lambda r, j: 0):
            cp.wait()

    @pl.when(b == 0)
    def _():
        for k in range(ahead):
            start(0, k, k)

    ql, qp = ql_ref[0], qp_ref[0]
    qlf, qpf = ql.astype(F32), qp.astype(F32)
    tok_of_row = lax.broadcasted_iota(jnp.int32, (rows, 1), 0) // N_HEADS
    s_new = []
    for j in range(dec_seq):
        sj = (jnp.sum(qlf * cnew_ref[0, j:j + 1, :], axis=-1, keepdims=True)
              + jnp.sum(qpf * knew_ref[0, j:j + 1, :], axis=-1, keepdims=True))
        s_new.append(jnp.where(tok_of_row >= j, sj, -jnp.inf))
    m_n = functools.reduce(jnp.maximum, s_new)
    p_n = [jnp.exp(sj - m_n) for sj in s_new]
    m_ref[...] = m_n
    l_ref[...] = functools.reduce(lambda a, e: a + e, p_n)
    acc_ref[0] = functools.reduce(lambda a, e: a + e,
                                  [p_n[j] * cnew_ref[0, j:j + 1, :] for j in range(dec_seq)])
    acc_ref[1] = jnp.zeros(acc_ref.shape[1:], F32)

    half = pages * page // 2
    toks = [slice(h * half, (h + 1) * half) for h in range(2)]

    def scores(slot):
        yb = [ybuf[slot, t, :].astype(BF16) for t in toks]
        xb = [xbuf[slot, :, t].astype(BF16) for t in toks]
        s = jnp.concatenate(
            [lax.dot_general(ql, yb[h], dn, preferred_element_type=F32)
             + jnp.dot(qp, xb[h], preferred_element_type=F32) for h in range(2)], axis=1)
        return s, yb

    def softmax(s):
        m = m_ref[...]
        m_new = jnp.maximum(m, jnp.max(s, axis=-1, keepdims=True))
        alpha = jnp.exp(m - m_new)
        p = jnp.exp(s - m_new)
        l_ref[...] = alpha * l_ref[...] + jnp.sum(p, axis=-1, keepdims=True)
        m_ref[...] = m_new
        return alpha, p.astype(BF16)

    def values(alpha, p, yb):
        for h in range(2):
            acc_ref[h] = alpha * acc_ref[h] + jnp.dot(p[:, toks[h]], yb[h], preferred_element_type=F32)

    def trip(it, carry):
        pending = None
        for k in range(slots):
            ch = it * slots + k
            nxt = ch + ahead
            nxt_slot = (k + ahead) % slots

            @pl.when(nxt < n_chunks)
            def _():
                start(b, nxt, nxt_slot)

            @pl.when((nxt >= n_chunks) & (b + 1 < n_b))
            def _():
                start(b + 1, nxt - n_chunks, nxt_slot)

            wait(k)
            s, yb = scores(k)
            if pending is not None:
                values(*pending)
            pending = softmax(s) + (yb,)
        values(*pending)
        return carry

    lax.fori_loop(0, n_chunks // slots, trip, 0)
    o_ref[0] = (acc_ref[0] + acc_ref[1]) * (1.0 / l_ref[...])


def _decode_attention(page_table, ql, qp, cnew, knew, cache_ckv, cache_kpe, layer, pages, slots):
    nb, rows, c = ql.shape
    dec_seq = cnew.shape[1]
    page, r = cache_kpe.shape[2:]
    n_chunks = page_table.shape[1] // pages
    assert n_chunks * pages == page_table.shape[1] and n_chunks % slots == 0 and slots - 1 <= n_chunks
    kpe_t = jnp.swapaxes(cache_kpe, 2, 3)
    per_b = lambda a: pl.BlockSpec((1,) + a.shape[1:], lambda b, pt: (b, 0, 0))
    grid_spec = pltpu.PrefetchScalarGridSpec(
        num_scalar_prefetch=1,
        grid=(nb,),
        in_specs=[per_b(ql), per_b(qp), per_b(cnew), per_b(knew),
                  pl.BlockSpec(memory_space=pl.ANY), pl.BlockSpec(memory_space=pl.ANY)],
        out_specs=pl.BlockSpec((1, rows, c), lambda b, pt: (b, 0, 0)),
        scratch_shapes=[pltpu.VMEM((slots, pages * page, c), F32), pltpu.VMEM((slots, r, pages * page), F32),
                        pltpu.SemaphoreType.DMA((2, slots)),
                        pltpu.VMEM((rows, 1), F32), pltpu.VMEM((rows, 1), F32), pltpu.VMEM((2, rows, c), F32)],
    )
    return pl.pallas_call(
        functools.partial(_decode_attn_kernel, pages=pages, slots=slots, layer=layer, dec_seq=dec_seq),
        grid_spec=grid_spec,
        out_shape=jax.ShapeDtypeStruct((nb, rows, c), F32),
        compiler_params=pltpu.CompilerParams(dimension_semantics=("arbitrary",),
                                             vmem_limit_bytes=VMEM_LIMIT_BYTES),
        name="decode_attn",
    )(page_table, ql, qp, cnew, knew, cache_ckv, kpe_t)


def _out_sample_kernel(ol_ref, gm_ref, cv_ref, x_ref, wuv_ref, wo_ref, gpost_ref, y_ref):
    half = gm_ref.shape[1]
    o = jnp.concatenate(
        [jnp.dot(ol_ref[h].astype(BF16), wuv_ref[h], preferred_element_type=F32) for h in range(N_HEADS)], axis=1)
    mla = (o * gm_ref[...]).astype(BF16)
    y = jnp.dot(mla, wo_ref[0:half, :], preferred_element_type=F32)
    y = y + jnp.dot(cv_ref[...].astype(BF16), wo_ref[half:, :], preferred_element_type=F32)
    y_ref[...] = x_ref[...] + _rms(y, gpost_ref[...])


def _out_sample(ol, gm, cv, x2d, wuv, wo, gpost):
    return pl.pallas_call(
        _out_sample_kernel,
        out_shape=jax.ShapeDtypeStruct(x2d.shape, F32),
        compiler_params=pltpu.CompilerParams(vmem_limit_bytes=VMEM_LIMIT_BYTES),
        name="out_sample",
    )(ol, gm, cv, x2d, wuv, wo, gpost)


def _rot_cols(w):
    r = w.shape[-1] // 2
    return jnp.concatenate([-w[..., r:], w[..., :r]], axis=-1)


def _pack_weights(w_in, w_uq, w_ukv, w_out):
    d, q_lora, kv_lora = w_in.shape[0], w_uq.shape[0], w_ukv.shape[0]
    cq, ckv, kpe, rest = (w_in[:, :C_Q1], w_in[:, C_KV0:C_KV1], w_in[:, C_KV1:C_KV1 + QK_ROPE],
                          w_in[:, C_KV1 + QK_ROPE:])
    w1 = jnp.concatenate([cq, ckv, jnp.zeros((d, QK_NOPE), F32), kpe, _rot_cols(kpe), rest], axis=1).astype(BF16)
    uq = w_uq.reshape(q_lora, N_HEADS, QK_NOPE + QK_ROPE)
    pe = uq[:, :, QK_NOPE:]
    wq = jnp.concatenate([uq[:, :, :QK_NOPE], pe, _rot_cols(pe)], axis=-1).reshape(q_lora, N_HEADS * HEAD_SLAB)
    ukv = w_ukv.reshape(kv_lora, N_HEADS, QK_NOPE + V_HEAD)
    uk, uv = ukv[:, :, :QK_NOPE], ukv[:, :, QK_NOPE:]
    pad = jnp.zeros((kv_lora, N_HEADS, HEAD_SLAB - QK_NOPE), F32)
    wk = jnp.concatenate([uk, pad], axis=-1).reshape(kv_lora, N_HEADS * HEAD_SLAB)
    wvt = uv.reshape(kv_lora, N_HEADS * V_HEAD).T
    wukt = jnp.concatenate([uk.transpose(1, 2, 0), jnp.zeros((N_HEADS, HEAD_SLAB - QK_NOPE, kv_lora), F32)], axis=1)
    wuv = uv.transpose(1, 0, 2)
    return (w1, wq.astype(BF16), wq.T.astype(BF16), wk.astype(BF16), wvt.astype(BF16), wukt.astype(BF16),
            wuv.astype(BF16), w_out.astype(BF16))


PROMPT_TILE = 512
ATTN_TQ = 512
ATTN_TK = 512
ATTN_HEAD_GROUP = 4
DECODE_PAGES = 32
DECODE_SLOTS = 4


def kernel(x_prompt, x_sample, cache_ckv, cache_kpe, state_conv, page_table, g_pre, w_in, g_qnorm, w_uq,
           g_kvnorm, w_ukv, w_conv, w_out, g_post):
    depth = w_in.shape[0]
    nb, seq, d = x_prompt.shape
    db, dec_seq, _ = x_sample.shape
    past_len = page_table.shape[1] * cache_ckv.shape[2]
    c = cache_ckv.shape[3]

    tabs_p = _rope_tables(jnp.arange(seq, dtype=jnp.int32))
    tabs_s = _rope_tables(jnp.tile(past_len + jnp.arange(dec_seq, dtype=jnp.int32), db))

    xp = x_prompt.reshape(nb * seq, d)
    xs = x_sample.reshape(db * dec_seq, d)
    outs = [[] for _ in range(6)]
    for l in range(depth):
        w1, wq, wqt, wk, wvt, wukt, wuv, wo = _pack_weights(w_in[l], w_uq[l], w_ukv[l], w_out[l])
        gpre, gq, gkv, gpost = g_pre[l][None], g_qnorm[l][None], g_kvnorm[l][None], g_post[l][None]

        qt, k, vt, ckv_p, kpe_p, gm, cv, conv_p = _proj_prompt(
            xp, seq, gpre, w1, gq, wqt, gkv, wk, wvt, w_conv[l], tabs_p, PROMPT_TILE)
        xp = _attention_out(qt, k, vt, gm, cv, xp, wo, gpost, nb, seq, ATTN_TQ, ATTN_TK, ATTN_HEAD_GROUP)
        outs[0].append(ckv_p.reshape(nb, seq, c))
        outs[1].append(jnp.swapaxes(kpe_p, 1, 2))
        outs[2].append(conv_p)

        st = state_conv[l].astype(F32)
        s0 = jnp.repeat(st[:, 0], dec_seq, axis=0)
        s1 = jnp.repeat(st[:, 1], dec_seq, axis=0)
        ql, qp, ckv_s, kpe_s, gm_s, cv_s, u_s = _proj_sample(
            xs, dec_seq, gpre, w1, gq, wq, gkv, wukt, w_conv[l], tabs_s[:4], s0, s1)
        rows = dec_seq * N_HEADS
        to_rows = lambda a: a.reshape(N_HEADS, db, dec_seq, -1).transpose(1, 2, 0, 3).reshape(db, rows, -1)
        ol = _decode_attention(page_table, to_rows(ql), to_rows(qp), ckv_s.reshape(db, dec_seq, c),
                               kpe_s.reshape(db, dec_seq, QK_ROPE), cache_ckv, cache_kpe, l,
                               DECODE_PAGES, DECODE_SLOTS)
        ol = ol.reshape(db, dec_seq, N_HEADS, c).transpose(2, 0, 1, 3).reshape(N_HEADS, db * dec_seq, c)
        xs = _out_sample(ol, gm_s, cv_s, xs, wuv, wo, gpost)
        outs[3].append(ckv_s.reshape(db, dec_seq, c))
        outs[4].append(kpe_s.reshape(db, dec_seq, QK_ROPE))
        u_pad = jnp.concatenate([st, u_s.reshape(db, dec_seq, GROUP)], axis=1)
        outs[5].append(u_pad[:, -(CONV_W - 1):])

    return (xp.reshape(nb, seq, d), xs.reshape(db, dec_seq, d), *[jnp.stack(o_) for o_ in outs])
```

```python
import functools

import jax
import jax.numpy as jnp
from jax import lax
from jax.experimental import pallas as pl
from jax.experimental.pallas import tpu as pltpu

N_HEADS = 8
QK_NOPE = 64
QK_ROPE = 32
V_HEAD = 64
CONV_W = 3
ROPE_THETA = 10000.0
EPS = 1e-6
ATTN_SCALE = (QK_NOPE + QK_ROPE) ** -0.5

LANES = 128
HEAD_SLAB = LANES
ONES_ROWS = 16
LOG2_E = 1.4426950408889634
VMEM_LIMIT_BYTES = 56 * 1024 * 1024

F32 = jnp.float32
BF16 = jnp.bfloat16


def _silu(x):
    return x * (1.0 / (1.0 + jnp.exp(-x)))


def _rms(x, g):
    return x * lax.rsqrt(jnp.mean(x * x, axis=-1, keepdims=True) + EPS) * g


def _rope_coeffs(ang, slab_axis, q_scale):
    idx = lax.broadcasted_iota(jnp.int32, ang.shape, slab_axis)
    rope = (idx >= QK_NOPE) & (idx < QK_NOPE + QK_ROPE)
    c = jnp.where(rope, jnp.cos(ang), 0.0)
    s = jnp.where(rope, jnp.sin(ang), 0.0)
    return jnp.where(idx < QK_NOPE, q_scale, c * q_scale), s * q_scale, c, s


def _rope_table_kernel(pos_ref, invf_ref, post_ref, invft_ref, cq1_ref, cq2_ref, ck1_ref, ck2_ref,
                       cq1t_ref, cq2t_ref):
    cq1_ref[...], cq2_ref[...], ck1_ref[...], ck2_ref[...] = _rope_coeffs(
        pos_ref[...] * invf_ref[...], 1, ATTN_SCALE)
    cq1t_ref[...], cq2t_ref[...] = _rope_coeffs(invft_ref[...] * post_ref[...], 0, ATTN_SCALE * LOG2_E)[:2]


def _rope_tables(pos):
    t = pos.shape[0]
    r = QK_ROPE
    inv_freq = ROPE_THETA ** (-jnp.arange(0, r, 2, dtype=F32) / r)
    invf = jnp.zeros((1, LANES), F32).at[0, QK_NOPE:QK_NOPE + r].set(jnp.concatenate([inv_freq, inv_freq]))
    posf = pos.astype(F32)
    out = jax.ShapeDtypeStruct((t, LANES), F32)
    out_t = jax.ShapeDtypeStruct((LANES, t), F32)
    return pl.pallas_call(
        _rope_table_kernel,
        out_shape=(out, out, out, out, out_t, out_t),
        name="rope_tables",
    )(posf.reshape(t, 1), invf, posf.reshape(1, t), invf.reshape(LANES, 1))


C_Q0, C_Q1 = 0, 384
C_KV0, C_KV1 = 384, 640
C_KP0, C_KP1 = 640, 768
C_GM0 = 768
GROUP = 512
D_IN_PACKED = C_GM0 + 5 * GROUP


def _rope_slab(blk, c1, c2):
    return blk * c1 + pltpu.roll(blk, LANES - QK_ROPE, 1) * c2


def _project_common(x_ref, gpre_ref, w1_ref, gq_ref, gkv_ref, ck1_ref, ck2_ref):
    x = x_ref[...]
    xn = _rms(x, gpre_ref[...]).astype(BF16)

    def proj(c0, c1):
        return jnp.dot(xn, w1_ref[:, c0:c1], preferred_element_type=F32)

    cqn = _rms(proj(C_Q0, C_Q1), gq_ref[...]).astype(BF16)
    ckv = _rms(proj(C_KV0, C_KV1), gkv_ref[...])
    kpe_blk = _rope_slab(proj(C_KP0, C_KP1), ck1_ref[...], ck2_ref[...])
    return proj, cqn, ckv, kpe_blk


def _conv_gate(proj, um1, um2, u, wconv_ref):
    w = wconv_ref[...]
    y = w[0:1, :] * um2 + w[1:2, :] * um1 + w[2:3, :] * u
    bg = proj(C_GM0 + GROUP, C_GM0 + 2 * GROUP)
    gc = proj(C_GM0 + 4 * GROUP, C_GM0 + 5 * GROUP)
    return bg * y * _silu(gc)


def _proj_prompt_kernel(x_ref, gpre_ref, w1_ref, gq_ref, wqt_ref, gkv_ref, wk_ref, wvt_ref, wconv_ref,
                        cq1t_ref, cq2t_ref, ck1_ref, ck2_ref,
                        qt_out, k_out, vt_out, ckv_out, kpet_out, gm_out, cv_out, conv_out,
                        carry_ref, *, tiles_per_seq):
    tm = x_ref.shape[0]
    step = pl.program_id(0)
    dn = (((1,), (1,)), ((), ()))
    proj, cqn, ckv, kpe_blk = _project_common(x_ref, gpre_ref, w1_ref, gq_ref, gkv_ref, ck1_ref, ck2_ref)
    qt = lax.dot_general(wqt_ref[...], cqn, dn, preferred_element_type=F32)
    cq1t, cq2t = cq1t_ref[...], cq2t_ref[...]
    for h in range(N_HEADS):
        slab = qt[h * HEAD_SLAB:(h + 1) * HEAD_SLAB, :]
        roped = slab * cq1t + pltpu.roll(slab, HEAD_SLAB - QK_ROPE, 0) * cq2t
        qt_out[h * HEAD_SLAB:(h + 1) * HEAD_SLAB, :] = roped.astype(BF16)
    ckv_out[...] = ckv
    kpet_out[0] = kpe_blk.T[QK_NOPE:QK_NOPE + QK_ROPE, :]
    ckvb = ckv.astype(BF16)
    kfull = jnp.dot(ckvb, wk_ref[...], preferred_element_type=F32)
    for h in range(N_HEADS):
        k_out[:, h * HEAD_SLAB:(h + 1) * HEAD_SLAB] = (kfull[:, h * HEAD_SLAB:(h + 1) * HEAD_SLAB] + kpe_blk).astype(BF16)
    vt_out[...] = lax.dot_general(wvt_ref[...], ckvb, dn, preferred_element_type=F32).astype(BF16)
    gm_out[...] = _silu(proj(C_GM0, C_GM0 + GROUP)).astype(BF16)

    u = proj(C_GM0 + 2 * GROUP, C_GM0 + 3 * GROUP) * proj(C_GM0 + 3 * GROUP, C_GM0 + 4 * GROUP)

    @pl.when(step % tiles_per_seq == 0)
    def _():
        carry_ref[...] = jnp.zeros_like(carry_ref)

    prev = carry_ref[...]
    p1, p2 = prev[7:8, :], prev[6:7, :]
    row = lax.broadcasted_iota(jnp.int32, u.shape, 0)
    um1 = jnp.where(row == 0, p1, pltpu.roll(u, 1, 0))
    um2 = jnp.where(row == 0, p2, jnp.where(row == 1, p1, pltpu.roll(u, 2, 0)))
    cv_out[...] = _conv_gate(proj, um1, um2, u, wconv_ref).astype(BF16)
    carry_ref[...] = u[tm - 8:tm, :]
    conv_out[0] = u[tm - (CONV_W - 1):tm, :]


def _const_spec(shape):
    nd = len(shape)
    return pl.BlockSpec(shape, lambda *_: (0,) * nd, pipeline_mode=pl.Buffered(1))


def _proj_prompt(x2d, seq, gpre, w1, gq, wqt, gkv, wk, wvt, wconv, tabs, tm):
    t, d = x2d.shape
    nb = t // seq
    hs = N_HEADS * HEAD_SLAB
    hv = N_HEADS * V_HEAD
    cq1, cq2, ck1, ck2, cq1t, cq2t = tabs
    row = lambda w: pl.BlockSpec((tm, w), lambda i: (i, 0))
    col = lambda h: pl.BlockSpec((h, tm), lambda i: (0, i))
    tab = pl.BlockSpec((tm, LANES), lambda i: (i % (seq // tm), 0))
    tab_t = pl.BlockSpec((LANES, tm), lambda i: (0, i % (seq // tm)))
    out_shape = (
        jax.ShapeDtypeStruct((hs, t), BF16),
        jax.ShapeDtypeStruct((t, hs), BF16),
        jax.ShapeDtypeStruct((hv, t), BF16),
        jax.ShapeDtypeStruct((t, gkv.shape[1]), F32),
        jax.ShapeDtypeStruct((nb, QK_ROPE, seq), F32),
        jax.ShapeDtypeStruct((t, GROUP), BF16),
        jax.ShapeDtypeStruct((t, GROUP), BF16),
        jax.ShapeDtypeStruct((nb, CONV_W - 1, GROUP), F32),
    )
    return pl.pallas_call(
        functools.partial(_proj_prompt_kernel, tiles_per_seq=seq // tm),
        grid=(t // tm,),
        in_specs=[row(d), _const_spec(gpre.shape), _const_spec(w1.shape), _const_spec(gq.shape),
                  _const_spec(wqt.shape), _const_spec(gkv.shape), _const_spec(wk.shape), _const_spec(wvt.shape),
                  _const_spec(wconv.shape), tab_t, tab_t, tab, tab],
        out_specs=(col(hs), row(hs), col(hv), row(gkv.shape[1]),
                   pl.BlockSpec((1, QK_ROPE, tm), lambda i: (i // (seq // tm), 0, i % (seq // tm))),
                   row(GROUP), row(GROUP),
                   pl.BlockSpec((1, CONV_W - 1, GROUP), lambda i: (i // (seq // tm), 0, 0))),
        out_shape=out_shape,
        scratch_shapes=[pltpu.VMEM((8, GROUP), F32)],
        compiler_params=pltpu.CompilerParams(dimension_semantics=("arbitrary",),
                                             vmem_limit_bytes=VMEM_LIMIT_BYTES),
        name="proj_prompt",
    )(x2d, gpre, w1, gq, wqt, gkv, wk, wvt, wconv, cq1t, cq2t, ck1, ck2)


def _attn_kernel(qt_ref, k_ref, vt_ref, gm_ref, cv_ref, x_ref, wo_ref, gpost_ref, y_ref, *, tq, tk, group):
    i = pl.program_id(1)
    hq = tq // 2
    tri = (lax.broadcasted_iota(jnp.int32, (hq, hq), 0) <= lax.broadcasted_iota(jnp.int32, (hq, hq), 1))

    def slab(h):
        return slice(h * HEAD_SLAB, (h + 1) * HEAD_SLAB)

    def values(h, toks):
        ones = jnp.ones((ONES_ROWS, toks.size), BF16)
        return jnp.concatenate([vt_ref[h * V_HEAD:(h + 1) * V_HEAD, toks], ones], 0)

    def head_group(hs):
        def full_tile(j, carries):
            toks = pl.ds(pl.multiple_of(j * tk, tk), tk)
            sts = [jnp.dot(k_ref[toks, slab(h)], qt_ref[slab(h), :], preferred_element_type=F32) for h in hs]
            stats = []
            for st, (m, _) in zip(sts, carries):
                m_new = jnp.maximum(m, jnp.max(st, axis=0, keepdims=True))
                stats.append((m_new, jnp.exp2(m - m_new), jnp.exp2(st - m_new).astype(BF16)))
            return tuple((m_new, alpha * acc + jnp.dot(values(h, toks), pt, preferred_element_type=F32))
                         for h, (m_new, alpha, pt), (_, acc) in zip(hs, stats, carries))

        def diagonal_tile():
            base = pl.multiple_of(i * tq, tq)
            ka, kb = pl.ds(base, hq), pl.ds(base + hq, hq)
            sa = [jnp.dot(k_ref[ka, slab(h)], qt_ref[slab(h), :], preferred_element_type=F32) for h in hs]
            sb = [jnp.dot(k_ref[kb, slab(h)], qt_ref[slab(h), hq:], preferred_element_type=F32) for h in hs]
            out = []
            stats = []
            for a, b_ in zip(sa, sb):
                a_lo = jnp.where(tri, a[:, :hq], -jnp.inf)
                a_hi = a[:, hq:]
                b_ = jnp.where(tri, b_, -jnp.inf)
                m_lo = jnp.max(a_lo, axis=0, keepdims=True)
                m_hi = jnp.maximum(jnp.max(a_hi, axis=0, keepdims=True), jnp.max(b_, axis=0, keepdims=True))
                pa = jnp.concatenate([jnp.exp2(a_lo - m_lo), jnp.exp2(a_hi - m_hi)], axis=1).astype(BF16)
                pb = jnp.exp2(b_ - m_hi).astype(BF16)
                stats.append((jnp.concatenate([m_lo, m_hi], axis=1), pa, pb))
            for h, (m_new, pa, pb) in zip(hs, stats):
                acc_a = jnp.dot(values(h, ka), pa, preferred_element_type=F32)
                acc_b = jnp.dot(values(h, kb), pb, preferred_element_type=F32)
                out.append((m_new, jnp.concatenate([acc_a[:, :hq], acc_a[:, hq:] + acc_b], axis=1)))
            return tuple(out)

        carries = lax.fori_loop(0, i * (tq // tk), full_tile, diagonal_tile())
        return [acc[:V_HEAD] * (1.0 / acc[V_HEAD:V_HEAD + 1]) for _, acc in carries]

    pairs = []
    for g in range(N_HEADS // group):
        outs = head_group(range(g * group, (g + 1) * group))
        for p in range(group // 2):
            pairs.append(jnp.concatenate(outs[2 * p:2 * p + 2], axis=0).T)

    half = gm_ref.shape[1]
    mla = (jnp.concatenate(pairs, axis=1) * gm_ref[...].astype(F32)).astype(BF16)
    y = jnp.dot(mla, wo_ref[0:half, :], preferred_element_type=F32)
    y = y + jnp.dot(cv_ref[...], wo_ref[half:, :], preferred_element_type=F32)
    y_ref[...] = x_ref[...] + _rms(y, gpost_ref[...])


def _attention_out(qt, k, vt, gm, cv, x2d, wo, gpost, nb, seq, tq, tk, group):
    hs, t = qt.shape
    hv = vt.shape[0]
    d = x2d.shape[1]
    assert group % 2 == 0 and N_HEADS % group == 0
    row = lambda w: pl.BlockSpec((tq, w), lambda b, i: (b * (seq // tq) + i, 0))
    return pl.pallas_call(
        functools.partial(_attn_kernel, tq=tq, tk=tk, group=group),
        grid=(nb, seq // tq),
        in_specs=[pl.BlockSpec((hs, tq), lambda b, i: (0, b * (seq // tq) + i)),
                  pl.BlockSpec((seq, hs), lambda b, i: (b, 0)),
                  pl.BlockSpec((hv, seq), lambda b, i: (0, b)),
                  row(gm.shape[1]), row(cv.shape[1]), row(d), _const_spec(wo.shape), _const_spec(gpost.shape)],
        out_specs=row(d),
        out_shape=jax.ShapeDtypeStruct((t, d), F32),
        compiler_params=pltpu.CompilerParams(dimension_semantics=("arbitrary", "arbitrary"),
                                             vmem_limit_bytes=VMEM_LIMIT_BYTES),
        name="prompt_attn",
    )(qt, k, vt, gm, cv, x2d, wo, gpost)


def _proj_sample_kernel(x_ref, gpre_ref, w1_ref, gq_ref, wq_ref, gkv_ref, wukt_ref, wconv_ref,
                        cq1_ref, cq2_ref, ck1_ref, ck2_ref, s0_ref, s1_ref,
                        ql_out, qp_out, ckv_out, kpe_out, gm_out, cv_out, conv_out, *, dec_seq):
    proj, cqn, ckv, kpe_blk = _project_common(x_ref, gpre_ref, w1_ref, gq_ref, gkv_ref, ck1_ref, ck2_ref)
    qraw = jnp.dot(cqn, wq_ref[...], preferred_element_type=F32)
    cq1, cq2 = cq1_ref[...], cq2_ref[...]
    for h in range(N_HEADS):
        qh = _rope_slab(qraw[:, h * HEAD_SLAB:(h + 1) * HEAD_SLAB], cq1, cq2)
        ql_out[h] = jnp.dot(qh.astype(BF16), wukt_ref[h], preferred_element_type=F32).astype(BF16)
        qp_out[h] = pltpu.roll(qh, LANES - QK_NOPE, 1)[:, 0:QK_ROPE].astype(BF16)
    ckv_out[...] = ckv
    kpe_out[...] = kpe_blk[:, QK_NOPE:QK_NOPE + QK_ROPE]
    gm_out[...] = _silu(proj(C_GM0, C_GM0 + GROUP))

    u = proj(C_GM0 + 2 * GROUP, C_GM0 + 3 * GROUP) * proj(C_GM0 + 3 * GROUP, C_GM0 + 4 * GROUP)
    t_in_seq = lax.broadcasted_iota(jnp.int32, u.shape, 0) % dec_seq
    s0, s1 = s0_ref[...], s1_ref[...]
    um1 = jnp.where(t_in_seq == 0, s1, pltpu.roll(u, 1, 0))
    um2 = jnp.where(t_in_seq == 0, s0, jnp.where(t_in_seq == 1, s1, pltpu.roll(u, 2, 0)))
    cv_out[...] = _conv_gate(proj, um1, um2, u, wconv_ref)
    conv_out[...] = u


def _proj_sample(x2d, dec_seq, gpre, w1, gq, wq, gkv, wukt, wconv, tabs, s0, s1):
    t, d = x2d.shape
    c = gkv.shape[1]
    out_shape = (
        jax.ShapeDtypeStruct((N_HEADS, t, c), BF16),
        jax.ShapeDtypeStruct((N_HEADS, t, QK_ROPE), BF16),
        jax.ShapeDtypeStruct((t, c), F32),
        jax.ShapeDtypeStruct((t, QK_ROPE), F32),
        jax.ShapeDtypeStruct((t, GROUP), F32),
        jax.ShapeDtypeStruct((t, GROUP), F32),
        jax.ShapeDtypeStruct((t, GROUP), F32),
    )
    return pl.pallas_call(
        functools.partial(_proj_sample_kernel, dec_seq=dec_seq),
        out_shape=out_shape,
        compiler_params=pltpu.CompilerParams(vmem_limit_bytes=VMEM_LIMIT_BYTES),
        name="proj_sample",
    )(x2d, gpre, w1, gq, wq, gkv, wukt, wconv, *tabs, s0, s1)


def _decode_attn_kernel(pt_ref, ql_ref, qp_ref, cnew_ref, knew_ref, y_hbm, x_hbm, o_ref,
                        ybuf, xbuf, sem, m_ref, l_ref, acc_ref, *, pages, slots, layer, dec_seq):
    b = pl.program_id(0)
    n_b = pl.num_programs(0)
    n_chunks = pt_ref.shape[1] // pages
    page = ybuf.shape[1] // pages
    rows = ql_ref.shape[1]
    dn = (((1,), (1,)), ((), ()))
    ahead = slots - 1

    def chunk_copies(bb, ch, slot, page_ids):
        cps = []
        for i in range(pages):
            pid = page_ids(bb, ch * pages + i)
            tok = pl.ds(i * page, page)
            cps.append(pltpu.make_async_copy(y_hbm.at[layer, pid], ybuf.at[slot, tok, :], sem.at[0, slot]))
            cps.append(pltpu.make_async_copy(x_hbm.at[layer, pid], xbuf.at[slot, :, tok], sem.at[1, slot]))
        return cps

    def start(bb, ch, slot):
        for cp in chunk_copies(bb, ch, slot, lambda r, j: pt_ref[r, j]):
            cp.start()

    def wait(slot):
        for cp in chunk_copies(0, 0, slot, lambda r, j: 0):
            cp.wait()

    @pl.when(b == 0)
    def _():
        for k in range(ahead):
            start(0, k, k)

    ql, qp = ql_ref[0], qp_ref[0]
    qlf, qpf = ql.astype(F32), qp.astype(F32)
    tok_of_row = lax.broadcasted_iota(jnp.int32, (rows, 1), 0) // N_HEADS
    s_new = []
    for j in range(dec_seq):
        sj = (jnp.sum(qlf * cnew_ref[0, j:j + 1, :], axis=-1, keepdims=True)
              + jnp.sum(qpf * knew_ref[0, j:j + 1, :], axis=-1, keepdims=True))
        s_new.append(jnp.where(tok_of_row >= j, sj, -jnp.inf))
    m_n = functools.reduce(jnp.maximum, s_new)
    p_n = [jnp.exp(sj - m_n) for sj in s_new]
    m_ref[...] = m_n
    l_ref[...] = functools.reduce(lambda a, e: a + e, p_n)
    acc_ref[0] = functools.reduce(lambda a, e: a + e,
                                  [p_n[j] * cnew_ref[0, j:j + 1, :] for j in range(dec_seq)])
    acc_ref[1] = jnp.zeros(acc_ref.shape[1:], F32)

    half = pages * page // 2
    toks = [slice(h * half, (h + 1) * half) for h in range(2)]

    def scores(slot):
        yb = [ybuf[slot, t, :].astype(BF16) for t in toks]
        xb = [xbuf[slot, :, t].astype(BF16) for t in toks]
        s = jnp.concatenate(
            [lax.dot_general(ql, yb[h], dn, preferred_element_type=F32)
             + jnp.dot(qp, xb[h], preferred_element_type=F32) for h in range(2)], axis=1)
        return s, yb

    def softmax(s):
        m = m_ref[...]
        m_new = jnp.maximum(m, jnp.max(s, axis=-1, keepdims=True))
        alpha = jnp.exp(m - m_new)
        p = jnp.exp(s - m_new)
        l_ref[...] = alpha * l_ref[...] + jnp.sum(p, axis=-1, keepdims=True)
        m_ref[...] = m_new
        return alpha, p.astype(BF16)

    def values(alpha, p, yb):
        for h in range(2):
            acc_ref[h] = alpha * acc_ref[h] + jnp.dot(p[:, toks[h]], yb[h], preferred_element_type=F32)

    def trip(it, carry):
        pending = None
        for k in range(slots):
            ch = it * slots + k
            nxt = ch + ahead
            nxt_slot = (k + ahead) % slots

            @pl.when(nxt < n_chunks)
            def _():
                start(b, nxt, nxt_slot)

            @pl.when((nxt >= n_chunks) & (b + 1 < n_b))
            def _():
                start(b + 1, nxt - n_chunks, nxt_slot)

            wait(k)
            s, yb = scores(k)
            if pending is not None:
                values(*pending)
            pending = softmax(s) + (yb,)
        values(*pending)
        return carry

    lax.fori_loop(0, n_chunks // slots, trip, 0)
    o_ref[0] = (acc_ref[0] + acc_ref[1]) * (1.0 / l_ref[...])


def _decode_attention(page_table, ql, qp, cnew, knew, cache_ckv, cache_kpe, layer, pages, slots):
    nb, rows, c = ql.shape
    dec_seq = cnew.shape[1]
    page, r = cache_kpe.shape[2:]
    n_chunks = page_table.shape[1] // pages
    assert n_chunks * pages == page_table.shape[1] and n_chunks % slots == 0 and slots - 1 <= n_chunks
    kpe_t = jnp.swapaxes(cache_kpe, 2, 3)
    per_b = lambda a: pl.BlockSpec((1,) + a.shape[1:], lambda b, pt: (b, 0, 0))
    grid_spec = pltpu.PrefetchScalarGridSpec(
        num_scalar_prefetch=1,
        grid=(nb,),
        in_specs=[per_b(ql), per_b(qp), per_b(cnew), per_b(knew),
                  pl.BlockSpec(memory_space=pl.ANY), pl.BlockSpec(memory_space=pl.ANY)],
        out_specs=pl.BlockSpec((1, rows, c), lambda b, pt: (b, 0, 0)),
        scratch_shapes=[pltpu.VMEM((slots, pages * page, c), F32), pltpu.VMEM((slots, r, pages * page), F32),
                        pltpu.SemaphoreType.DMA((2, slots)),
                        pltpu.VMEM((rows, 1), F32), pltpu.VMEM((rows, 1), F32), pltpu.VMEM((2, rows, c), F32)],
    )
    return pl.pallas_call(
        functools.partial(_decode_attn_kernel, pages=pages, slots=slots, layer=layer, dec_seq=dec_seq),
        grid_spec=grid_spec,
        out_shape=jax.ShapeDtypeStruct((nb, rows, c), F32),
        compiler_params=pltpu.CompilerParams(dimension_semantics=("arbitrary",),
                                             vmem_limit_bytes=VMEM_LIMIT_BYTES),
        name="decode_attn",
    )(page_table, ql, qp, cnew, knew, cache_ckv, kpe_t)


def _out_sample_kernel(ol_ref, gm_ref, cv_ref, x_ref, wuvt_ref, wo_ref, gpost_ref, y_ref):
    half = gm_ref.shape[1]
    dn = (((1,), (1,)), ((), ()))
    o = jnp.concatenate(
        [lax.dot_general(ol_ref[h].astype(BF16), wuvt_ref[h], dn, preferred_element_type=F32)
         for h in range(N_HEADS)], axis=1)
    mla = (o * gm_ref[...]).astype(BF16)
    y = jnp.dot(mla, wo_ref[0:half, :], preferred_element_type=F32)
    y = y + jnp.dot(cv_ref[...].astype(BF16), wo_ref[half:, :], preferred_element_type=F32)
    y_ref[...] = x_ref[...] + _rms(y, gpost_ref[...])


def _out_sample(ol, gm, cv, x2d, wuv, wo, gpost):
    return pl.pallas_call(
        _out_sample_kernel,
        out_shape=jax.ShapeDtypeStruct(x2d.shape, F32),
        compiler_params=pltpu.CompilerParams(vmem_limit_bytes=VMEM_LIMIT_BYTES),
        name="out_sample",
    )(ol, gm, cv, x2d, wuv, wo, gpost)


def _rot_rows(w):
    r = w.shape[0] // 2
    return jnp.concatenate([-w[r:], w[:r]], axis=0)


def _pack_kernel(wint_ref, wuq_ref, wukv_ref, wout_ref,
                 w1_ref, wq_ref, wqt_ref, wk_ref, wvt_ref, wukt_ref, wuvt_ref, wo_ref):
    d = wint_ref.shape[1]
    blk = LANES

    def put_w1(col0, rows):
        w1_ref[:, col0:col0 + blk] = rows.T.astype(BF16)

    for r0 in range(0, C_KV1, blk):
        put_w1(r0, wint_ref[r0:r0 + blk, :])
    kpe = wint_ref[C_KV1:C_KV1 + QK_ROPE, :]
    put_w1(C_KP0, jnp.concatenate([jnp.zeros((QK_NOPE, d), F32), kpe, _rot_rows(kpe)], axis=0))
    src0 = C_KV1 + QK_ROPE
    for j in range(5 * GROUP // blk):
        put_w1(C_GM0 + j * blk, wint_ref[src0 + j * blk:src0 + (j + 1) * blk, :])

    uqt = wuq_ref[...].T
    ukvt = wukv_ref[...].T
    kv_lora = ukvt.shape[1]
    qk = QK_NOPE + QK_ROPE
    for h in range(N_HEADS):
        pe = uqt[h * qk + QK_NOPE:(h + 1) * qk]
        q_slab = jnp.concatenate([uqt[h * qk:h * qk + QK_NOPE], pe, _rot_rows(pe)], axis=0)
        wqt_ref[h * HEAD_SLAB:(h + 1) * HEAD_SLAB, :] = q_slab.astype(BF16)
        wq_ref[:, h * HEAD_SLAB:(h + 1) * HEAD_SLAB] = q_slab.T.astype(BF16)
        ukt = ukvt[h * (QK_NOPE + V_HEAD):h * (QK_NOPE + V_HEAD) + QK_NOPE]
        uvt = ukvt[h * (QK_NOPE + V_HEAD) + QK_NOPE:(h + 1) * (QK_NOPE + V_HEAD)]
        k_slab = jnp.concatenate([ukt, jnp.zeros((HEAD_SLAB - QK_NOPE, kv_lora), F32)], axis=0)
        wukt_ref[h] = k_slab.astype(BF16)
        wk_ref[:, h * HEAD_SLAB:(h + 1) * HEAD_SLAB] = k_slab.T.astype(BF16)
        wvt_ref[h * V_HEAD:(h + 1) * V_HEAD, :] = uvt.astype(BF16)
        wuvt_ref[h] = uvt.astype(BF16)
    wo_ref[...] = wout_ref[...].astype(BF16)


def _pack_weights(w_in, w_uq, w_ukv, w_out):
    d = w_in.shape[0]
    q_lora, kv_lora = w_uq.shape[0], w_ukv.shape[0]
    assert w_in.shape[1] == C_KV1 + QK_ROPE + 5 * GROUP
    sds = lambda *s: jax.ShapeDtypeStruct(s, BF16)
    return pl.pallas_call(
        _pack_kernel,
        out_shape=(sds(d, D_IN_PACKED), sds(q_lora, N_HEADS * HEAD_SLAB), sds(N_HEADS * HEAD_SLAB, q_lora),
                   sds(kv_lora, N_HEADS * HEAD_SLAB), sds(N_HEADS * V_HEAD, kv_lora),
                   sds(N_HEADS, HEAD_SLAB, kv_lora), sds(N_HEADS, V_HEAD, kv_lora), sds(*w_out.shape)),
        compiler_params=pltpu.CompilerParams(vmem_limit_bytes=VMEM_LIMIT_BYTES),
        name="pack_weights",
    )(w_in.T, w_uq, w_ukv, w_out)


PROMPT_TILE = 512
ATTN_TQ = 512
ATTN_TK = 512
ATTN_HEAD_GROUP = 4
DECODE_PAGES = 32
DECODE_SLOTS = 4


def kernel(x_prompt, x_sample, cache_ckv, cache_kpe, state_conv, page_table, g_pre, w_in, g_qnorm, w_uq,
           g_kvnorm, w_ukv, w_conv, w_out, g_post):
    depth = w_in.shape[0]
    nb, seq, d = x_prompt.shape
    db, dec_seq, _ = x_sample.shape
    past_len = page_table.shape[1] * cache_ckv.shape[2]
    c = cache_ckv.shape[3]

    tabs_p = _rope_tables(jnp.arange(seq, dtype=jnp.int32))
    tabs_s = _rope_tables(jnp.tile(past_len + jnp.arange(dec_seq, dtype=jnp.int32), db))

    xp = x_prompt.reshape(nb * seq, d)
    xs = x_sample.reshape(db * dec_seq, d)
    outs = [[] for _ in range(6)]
    for l in range(depth):
        w1, wq, wqt, wk, wvt, wukt, wuvt, wo = _pack_weights(w_in[l], w_uq[l], w_ukv[l], w_out[l])
        gpre, gq, gkv, gpost = g_pre[l][None], g_qnorm[l][None], g_kvnorm[l][None], g_post[l][None]

        qt, k, vt, ckv_p, kpe_p, gm, cv, conv_p = _proj_prompt(
            xp, seq, gpre, w1, gq, wqt, gkv, wk, wvt, w_conv[l], tabs_p, PROMPT_TILE)
        xp = _attention_out(qt, k, vt, gm, cv, xp, wo, gpost, nb, seq, ATTN_TQ, ATTN_TK, ATTN_HEAD_GROUP)
        outs[0].append(ckv_p.reshape(nb, seq, c))
        outs[1].append(jnp.swapaxes(kpe_p, 1, 2))
        outs[2].append(conv_p)

        st = state_conv[l].astype(F32)
        s0 = jnp.repeat(st[:, 0], dec_seq, axis=0)
        s1 = jnp.repeat(st[:, 1], dec_seq, axis=0)
        ql, qp, ckv_s, kpe_s, gm_s, cv_s, u_s = _proj_sample(
            xs, dec_seq, gpre, w1, gq, wq, gkv, wukt, w_conv[l], tabs_s[:4], s0, s1)
        rows = dec_seq * N_HEADS
        to_rows = lambda a: a.reshape(N_HEADS, db, dec_seq, -1).transpose(1, 2, 0, 3).reshape(db, rows, -1)
        ol = _decode_attention(page_table, to_rows(ql), to_rows(qp), ckv_s.reshape(db, dec_seq, c),
                               kpe_s.reshape(db, dec_seq, QK_ROPE), cache_ckv, cache_kpe, l,
                               DECODE_PAGES, DECODE_SLOTS)
        ol = ol.reshape(db, dec_seq, N_HEADS, c).transpose(2, 0, 1, 3).reshape(N_HEADS, db * dec_seq, c)
        xs = _out_sample(ol, gm_s, cv_s, xs, wuvt, wo, gpost)
        outs[3].append(ckv_s.reshape(db, dec_seq, c))
        outs[4].append(kpe_s.reshape(db, dec_seq, QK_ROPE))
        u_pad = jnp.concatenate([st, u_s.reshape(db, dec_seq, GROUP)], axis=1)
        outs[5].append(u_pad[:, -(CONV_W - 1):])

    return (xp.reshape(nb, seq, d), xs.reshape(db, dec_seq, d), *[jnp.stack(o_) for o_ in outs])
```

```python
import functools

import jax
import jax.numpy as jnp
from jax import lax
from jax.experimental import pallas as pl
from jax.experimental.pallas import tpu as pltpu

N_HEADS = 8
QK_NOPE = 64
QK_ROPE = 32
V_HEAD = 64
CONV_W = 3
ROPE_THETA = 10000.0
EPS = 1e-6
ATTN_SCALE = (QK_NOPE + QK_ROPE) ** -0.5

LANES = 128
HEAD_SLAB = LANES
ONES_ROWS = 16
LOG2_E = 1.4426950408889634
VMEM_LIMIT_BYTES = 56 * 1024 * 1024

F32 = jnp.float32
BF16 = jnp.bfloat16


def _silu(x):
    return x * (1.0 / (1.0 + jnp.exp(-x)))


def _rms(x, g):
    return x * lax.rsqrt(jnp.mean(x * x, axis=-1, keepdims=True) + EPS) * g


def _rope_table_kernel(pos_ref, invf_ref, cq1_ref, cq2_ref, ck1_ref, ck2_ref, cq1t_ref, cq2t_ref):
    ang = invf_ref[...] * pos_ref[...]
    c, s = jnp.cos(ang), jnp.sin(ang)
    t = ang.shape[1]
    nope = jnp.ones((QK_NOPE, t), F32)
    zn = jnp.zeros((QK_NOPE, t), F32)
    zr = jnp.zeros((HEAD_SLAB - QK_NOPE - QK_ROPE, t), F32)

    def slab(first, mid):
        return jnp.concatenate([first, mid, zr], axis=0)

    cq1_ref[...] = slab(nope * ATTN_SCALE, c * ATTN_SCALE).T
    cq2_ref[...] = slab(zn, s * ATTN_SCALE).T
    ck1_ref[...] = slab(zn, c).T
    ck2_ref[...] = slab(zn, s).T
    cq1t_ref[...] = slab(nope * (ATTN_SCALE * LOG2_E), c * (ATTN_SCALE * LOG2_E))
    cq2t_ref[...] = slab(zn, s * (ATTN_SCALE * LOG2_E))


def _rope_tables(pos):
    t = pos.shape[0]
    r = QK_ROPE
    inv_freq = ROPE_THETA ** (-jnp.arange(0, r, 2, dtype=F32) / r)
    invf = jnp.concatenate([inv_freq, inv_freq]).reshape(r, 1)
    out = jax.ShapeDtypeStruct((t, LANES), F32)
    out_t = jax.ShapeDtypeStruct((LANES, t), F32)
    return pl.pallas_call(
        _rope_table_kernel,
        out_shape=(out, out, out, out, out_t, out_t),
        name="rope_tables",
    )(pos.astype(F32).reshape(1, t), invf)


C_Q0, C_Q1 = 0, 384
C_KV0, C_KV1 = 384, 640
C_KP0, C_KP1 = 640, 768
C_GM0 = 768
GROUP = 512
D_IN_PACKED = C_GM0 + 5 * GROUP


def _rope_slab(blk, c1, c2):
    return blk * c1 + pltpu.roll(blk, LANES - QK_ROPE, 1) * c2


def _project_common(x_ref, gpre_ref, w1_ref, gq_ref, gkv_ref, ck1_ref, ck2_ref):
    x = x_ref[...]
    xn = _rms(x, gpre_ref[...]).astype(BF16)

    def proj(c0, c1):
        return jnp.dot(xn, w1_ref[:, c0:c1], preferred_element_type=F32)

    cqn = _rms(proj(C_Q0, C_Q1), gq_ref[...]).astype(BF16)
    ckv = _rms(proj(C_KV0, C_KV1), gkv_ref[...])
    kpe_blk = _rope_slab(proj(C_KP0, C_KP1), ck1_ref[...], ck2_ref[...])
    return proj, cqn, ckv, kpe_blk


def _conv_gate(proj, um1, um2, u, wconv_ref):
    w = wconv_ref[...]
    y = w[0:1, :] * um2 + w[1:2, :] * um1 + w[2:3, :] * u
    bg = proj(C_GM0 + GROUP, C_GM0 + 2 * GROUP)
    gc = proj(C_GM0 + 4 * GROUP, C_GM0 + 5 * GROUP)
    return bg * y * _silu(gc)


def _proj_prompt_kernel(x_ref, gpre_ref, w1_ref, gq_ref, wqt_ref, gkv_ref, wk_ref, wvt_ref, wconv_ref,
                        cq1t_ref, cq2t_ref, ck1_ref, ck2_ref,
                        qt_out, k_out, vt_out, ckv_out, kpet_out, gm_out, cv_out, conv_out,
                        carry_ref, *, tiles_per_seq):
    tm = x_ref.shape[0]
    step = pl.program_id(0)
    dn = (((1,), (1,)), ((), ()))
    proj, cqn, ckv, kpe_blk = _project_common(x_ref, gpre_ref, w1_ref, gq_ref, gkv_ref, ck1_ref, ck2_ref)
    qt = lax.dot_general(wqt_ref[...], cqn, dn, preferred_element_type=F32)
    cq1t, cq2t = cq1t_ref[...], cq2t_ref[...]
    for h in range(N_HEADS):
        slab = qt[h * HEAD_SLAB:(h + 1) * HEAD_SLAB, :]
        roped = slab * cq1t + pltpu.roll(slab, HEAD_SLAB - QK_ROPE, 0) * cq2t
        qt_out[h * HEAD_SLAB:(h + 1) * HEAD_SLAB, :] = roped.astype(BF16)
    ckv_out[...] = ckv
    kpet_out[0] = kpe_blk.T[QK_NOPE:QK_NOPE + QK_ROPE, :]
    ckvb = ckv.astype(BF16)
    kfull = jnp.dot(ckvb, wk_ref[...], preferred_element_type=F32)
    for h in range(N_HEADS):
        k_out[:, h * HEAD_SLAB:(h + 1) * HEAD_SLAB] = (kfull[:, h * HEAD_SLAB:(h + 1) * HEAD_SLAB] + kpe_blk).astype(BF16)
    vt_out[...] = lax.dot_general(wvt_ref[...], ckvb, dn, preferred_element_type=F32).astype(BF16)
    gm_out[...] = _silu(proj(C_GM0, C_GM0 + GROUP)).astype(BF16)

    u = proj(C_GM0 + 2 * GROUP, C_GM0 + 3 * GROUP) * proj(C_GM0 + 3 * GROUP, C_GM0 + 4 * GROUP)

    @pl.when(step % tiles_per_seq == 0)
    def _():
        carry_ref[...] = jnp.zeros_like(carry_ref)

    prev = carry_ref[...]
    p1, p2 = prev[7:8, :], prev[6:7, :]
    row = lax.broadcasted_iota(jnp.int32, u.shape, 0)
    um1 = jnp.where(row == 0, p1, pltpu.roll(u, 1, 0))
    um2 = jnp.where(row == 0, p2, jnp.where(row == 1, p1, pltpu.roll(u, 2, 0)))
    cv_out[...] = _conv_gate(proj, um1, um2, u, wconv_ref).astype(BF16)
    carry_ref[...] = u[tm - 8:tm, :]
    conv_out[0] = u[tm - (CONV_W - 1):tm, :]


def _const_spec(shape):
    nd = len(shape)
    return pl.BlockSpec(shape, lambda *_: (0,) * nd, pipeline_mode=pl.Buffered(1))


def _proj_prompt(x2d, seq, gpre, w1, gq, wqt, gkv, wk, wvt, wconv, tabs, tm):
    t, d = x2d.shape
    nb = t // seq
    hs = N_HEADS * HEAD_SLAB
    hv = N_HEADS * V_HEAD
    cq1, cq2, ck1, ck2, cq1t, cq2t = tabs
    row = lambda w: pl.BlockSpec((tm, w), lambda i: (i, 0))
    col = lambda h: pl.BlockSpec((h, tm), lambda i: (0, i))
    tab = pl.BlockSpec((tm, LANES), lambda i: (i % (seq // tm), 0))
    tab_t = pl.BlockSpec((LANES, tm), lambda i: (0, i % (seq // tm)))
    out_shape = (
        jax.ShapeDtypeStruct((hs, t), BF16),
        jax.ShapeDtypeStruct((t, hs), BF16),
        jax.ShapeDtypeStruct((hv, t), BF16),
        jax.ShapeDtypeStruct((t, gkv.shape[1]), F32),
        jax.ShapeDtypeStruct((nb, QK_ROPE, seq), F32),
        jax.ShapeDtypeStruct((t, GROUP), BF16),
        jax.ShapeDtypeStruct((t, GROUP), BF16),
        jax.ShapeDtypeStruct((nb, CONV_W - 1, GROUP), F32),
    )
    return pl.pallas_call(
        functools.partial(_proj_prompt_kernel, tiles_per_seq=seq // tm),
        grid=(t // tm,),
        in_specs=[row(d), _const_spec(gpre.shape), _const_spec(w1.shape), _const_spec(gq.shape),
                  _const_spec(wqt.shape), _const_spec(gkv.shape), _const_spec(wk.shape), _const_spec(wvt.shape),
                  _const_spec(wconv.shape), tab_t, tab_t, tab, tab],
        out_specs=(col(hs), row(hs), col(hv), row(gkv.shape[1]),
                   pl.BlockSpec((1, QK_ROPE, tm), lambda i: (i // (seq // tm), 0, i % (seq // tm))),
                   row(GROUP), row(GROUP),
                   pl.BlockSpec((1, CONV_W - 1, GROUP), lambda i: (i // (seq // tm), 0, 0))),
        out_shape=out_shape,
        scratch_shapes=[pltpu.VMEM((8, GROUP), F32)],
        compiler_params=pltpu.CompilerParams(dimension_semantics=("arbitrary",),
                                             vmem_limit_bytes=VMEM_LIMIT_BYTES),
        name="proj_prompt",
    )(x2d, gpre, w1, gq, wqt, gkv, wk, wvt, wconv, cq1t, cq2t, ck1, ck2)


def _attn_kernel(qt_ref, k_ref, vt_ref, gm_ref, cv_ref, x_ref, wo_ref, gpost_ref, y_ref, *, tq, tk, group):
    i = pl.program_id(1)
    hq = tq // 2
    tri = (lax.broadcasted_iota(jnp.int32, (hq, hq), 0) <= lax.broadcasted_iota(jnp.int32, (hq, hq), 1))

    def slab(h):
        return slice(h * HEAD_SLAB, (h + 1) * HEAD_SLAB)

    def values(h, toks):
        ones = jnp.ones((ONES_ROWS, toks.size), BF16)
        return jnp.concatenate([vt_ref[h * V_HEAD:(h + 1) * V_HEAD, toks], ones], 0)

    def head_group(hs):
        def full_tile(j, carries):
            toks = pl.ds(pl.multiple_of(j * tk, tk), tk)
            sts = [jnp.dot(k_ref[toks, slab(h)], qt_ref[slab(h), :], preferred_element_type=F32) for h in hs]
            stats = []
            for st, (m, _) in zip(sts, carries):
                m_new = jnp.maximum(m, jnp.max(st, axis=0, keepdims=True))
                stats.append((m_new, jnp.exp2(m - m_new), jnp.exp2(st - m_new).astype(BF16)))
            return tuple((m_new, alpha * acc + jnp.dot(values(h, toks), pt, preferred_element_type=F32))
                         for h, (m_new, alpha, pt), (_, acc) in zip(hs, stats, carries))

        def diagonal_tile():
            base = pl.multiple_of(i * tq, tq)
            ka, kb = pl.ds(base, hq), pl.ds(base + hq, hq)
            sa = [jnp.dot(k_ref[ka, slab(h)], qt_ref[slab(h), :], preferred_element_type=F32) for h in hs]
            sb = [jnp.dot(k_ref[kb, slab(h)], qt_ref[slab(h), hq:], preferred_element_type=F32) for h in hs]
            out = []
            stats = []
            for a, b_ in zip(sa, sb):
                a_lo = jnp.where(tri, a[:, :hq], -jnp.inf)
                a_hi = a[:, hq:]
                b_ = jnp.where(tri, b_, -jnp.inf)
                m_lo = jnp.max(a_lo, axis=0, keepdims=True)
                m_hi = jnp.maximum(jnp.max(a_hi, axis=0, keepdims=True), jnp.max(b_, axis=0, keepdims=True))
                pa = jnp.concatenate([jnp.exp2(a_lo - m_lo), jnp.exp2(a_hi - m_hi)], axis=1).astype(BF16)
                pb = jnp.exp2(b_ - m_hi).astype(BF16)
                stats.append((jnp.concatenate([m_lo, m_hi], axis=1), pa, pb))
            for h, (m_new, pa, pb) in zip(hs, stats):
                acc_a = jnp.dot(values(h, ka), pa, preferred_element_type=F32)
                acc_b = jnp.dot(values(h, kb), pb, preferred_element_type=F32)
                out.append((m_new, jnp.concatenate([acc_a[:, :hq], acc_a[:, hq:] + acc_b], axis=1)))
            return tuple(out)

        carries = lax.fori_loop(0, i * (tq // tk), full_tile, diagonal_tile())
        return [acc[:V_HEAD] * (1.0 / acc[V_HEAD:V_HEAD + 1]) for _, acc in carries]

    pairs = []
    for g in range(N_HEADS // group):
        outs = head_group(range(g * group, (g + 1) * group))
        for p in range(group // 2):
            pairs.append(jnp.concatenate(outs[2 * p:2 * p + 2], axis=0).T)

    half = gm_ref.shape[1]
    mla = (jnp.concatenate(pairs, axis=1) * gm_ref[...].astype(F32)).astype(BF16)
    y = jnp.dot(mla, wo_ref[0:half, :], preferred_element_type=F32)
    y = y + jnp.dot(cv_ref[...], wo_ref[half:, :], preferred_element_type=F32)
    y_ref[...] = x_ref[...] + _rms(y, gpost_ref[...])


def _attention_out(qt, k, vt, gm, cv, x2d, wo, gpost, nb, seq, tq, tk, group):
    hs, t = qt.shape
    hv = vt.shape[0]
    d = x2d.shape[1]
    assert group % 2 == 0 and N_HEADS % group == 0
    row = lambda w: pl.BlockSpec((tq, w), lambda b, i: (b * (seq // tq) + i, 0))
    return pl.pallas_call(
        functools.partial(_attn_kernel, tq=tq, tk=tk, group=group),
        grid=(nb, seq // tq),
        in_specs=[pl.BlockSpec((hs, tq), lambda b, i: (0, b * (seq // tq) + i)),
                  pl.BlockSpec((seq, hs), lambda b, i: (b, 0)),
                  pl.BlockSpec((hv, seq), lambda b, i: (0, b)),
                  row(gm.shape[1]), row(cv.shape[1]), row(d), _const_spec(wo.shape), _const_spec(gpost.shape)],
        out_specs=row(d),
        out_shape=jax.ShapeDtypeStruct((t, d), F32),
        compiler_params=pltpu.CompilerParams(dimension_semantics=("arbitrary", "arbitrary"),
                                             vmem_limit_bytes=VMEM_LIMIT_BYTES),
        name="prompt_attn",
    )(qt, k, vt, gm, cv, x2d, wo, gpost)


def _proj_sample_kernel(x_ref, gpre_ref, w1_ref, gq_ref, wq_ref, gkv_ref, wukt_ref, wconv_ref,
                        cq1_ref, cq2_ref, ck1_ref, ck2_ref, s0_ref, s1_ref,
                        ql_out, qp_out, ckv_out, kpe_out, gm_out, cv_out, conv_out, *, dec_seq):
    proj, cqn, ckv, kpe_blk = _project_common(x_ref, gpre_ref, w1_ref, gq_ref, gkv_ref, ck1_ref, ck2_ref)
    qraw = jnp.dot(cqn, wq_ref[...], preferred_element_type=F32)
    cq1, cq2 = cq1_ref[...], cq2_ref[...]
    for h in range(N_HEADS):
        qh = _rope_slab(qraw[:, h * HEAD_SLAB:(h + 1) * HEAD_SLAB], cq1, cq2)
        ql_out[:, h, :] = jnp.dot(qh.astype(BF16), wukt_ref[h], preferred_element_type=F32)
        qp_out[:, h, :] = pltpu.roll(qh, LANES - QK_NOPE, 1)[:, 0:QK_ROPE]
    ckv_out[...] = ckv
    kpe_out[...] = kpe_blk[:, QK_NOPE:QK_NOPE + QK_ROPE]
    gm_out[...] = _silu(proj(C_GM0, C_GM0 + GROUP))

    u = proj(C_GM0 + 2 * GROUP, C_GM0 + 3 * GROUP) * proj(C_GM0 + 3 * GROUP, C_GM0 + 4 * GROUP)
    t_in_seq = lax.broadcasted_iota(jnp.int32, u.shape, 0) % dec_seq
    s0, s1 = s0_ref[...], s1_ref[...]
    um1 = jnp.where(t_in_seq == 0, s1, pltpu.roll(u, 1, 0))
    um2 = jnp.where(t_in_seq == 0, s0, jnp.where(t_in_seq == 1, s1, pltpu.roll(u, 2, 0)))
    cv_out[...] = _conv_gate(proj, um1, um2, u, wconv_ref)
    conv_out[...] = u


def _proj_sample(x2d, dec_seq, gpre, w1, gq, wq, gkv, wukt, wconv, tabs, s0, s1):
    t, d = x2d.shape
    c = gkv.shape[1]
    out_shape = (
        jax.ShapeDtypeStruct((t, N_HEADS, c), F32),
        jax.ShapeDtypeStruct((t, N_HEADS, QK_ROPE), F32),
        jax.ShapeDtypeStruct((t, c), F32),
        jax.ShapeDtypeStruct((t, QK_ROPE), F32),
        jax.ShapeDtypeStruct((t, GROUP), F32),
        jax.ShapeDtypeStruct((t, GROUP), F32),
        jax.ShapeDtypeStruct((t, GROUP), F32),
    )
    return pl.pallas_call(
        functools.partial(_proj_sample_kernel, dec_seq=dec_seq),
        out_shape=out_shape,
        compiler_params=pltpu.CompilerParams(vmem_limit_bytes=VMEM_LIMIT_BYTES),
        name="proj_sample",
    )(x2d, gpre, w1, gq, wq, gkv, wukt, wconv, *tabs, s0, s1)


def _decode_attn_kernel(pt_ref, ql_ref, qp_ref, cnew_ref, knew_ref, y_hbm, x_hbm, o_ref,
                        ybuf, xbuf, sem, m_ref, l_ref, acc_ref, *, pages, slots, layer, dec_seq):
    b = pl.program_id(0)
    n_b = pl.num_programs(0)
    n_chunks = pt_ref.shape[1] // pages
    page = ybuf.shape[1] // pages
    rows = ql_ref.shape[1]
    dn = (((1,), (1,)), ((), ()))
    ahead = slots - 1

    def chunk_copies(bb, ch, slot, page_ids):
        cps = []
        for i in range(pages):
            pid = page_ids(bb, ch * pages + i)
            tok = pl.ds(i * page, page)
            cps.append(pltpu.make_async_copy(y_hbm.at[layer, pid], ybuf.at[slot, tok, :], sem.at[0, slot]))
            cps.append(pltpu.make_async_copy(x_hbm.at[layer, pid], xbuf.at[slot, :, tok], sem.at[1, slot]))
        return cps

    def start(bb, ch, slot):
        for cp in chunk_copies(bb, ch, slot, lambda r, j: pt_ref[r, j]):
            cp.start()

    def wait(slot):
        for cp in chunk_copies(0, 0, slot, lambda r, j: 0):
            cp.wait()

    @pl.when(b == 0)
    def _():
        for k in range(ahead):
            start(0, k, k)

    qlf, qpf = ql_ref[0], qp_ref[0]
    ql, qp = qlf.astype(BF16), qpf.astype(BF16)
    tok_of_row = lax.broadcasted_iota(jnp.int32, (rows, 1), 0) // N_HEADS
    s_new = []
    for j in range(dec_seq):
        sj = (jnp.sum(qlf * cnew_ref[0, j:j + 1, :], axis=-1, keepdims=True)
              + jnp.sum(qpf * knew_ref[0, j:j + 1, :], axis=-1, keepdims=True))
        s_new.append(jnp.where(tok_of_row >= j, sj, -jnp.inf))
    m_n = functools.reduce(jnp.maximum, s_new)
    p_n = [jnp.exp(sj - m_n) for sj in s_new]
    m_ref[...] = m_n
    l_ref[...] = functools.reduce(lambda a, e: a + e, p_n)
    acc_ref[0] = functools.reduce(lambda a, e: a + e,
                                  [p_n[j] * cnew_ref[0, j:j + 1, :] for j in range(dec_seq)])
    acc_ref[1] = jnp.zeros(acc_ref.shape[1:], F32)

    half = pages * page // 2
    toks = [slice(h * half, (h + 1) * half) for h in range(2)]

    def scores(slot):
        yb = [ybuf[slot, t, :].astype(BF16) for t in toks]
        xb = [xbuf[slot, :, t].astype(BF16) for t in toks]
        s = jnp.concatenate(
            [lax.dot_general(ql, yb[h], dn, preferred_element_type=F32)
             + jnp.dot(qp, xb[h], preferred_element_type=F32) for h in range(2)], axis=1)
        return s, yb

    def softmax(s):
        m = m_ref[...]
        m_new = jnp.maximum(m, jnp.max(s, axis=-1, keepdims=True))
        alpha = jnp.exp(m - m_new)
        p = jnp.exp(s - m_new)
        l_ref[...] = alpha * l_ref[...] + jnp.sum(p, axis=-1, keepdims=True)
        m_ref[...] = m_new
        return alpha, p.astype(BF16)

    def values(alpha, p, yb):
        for h in range(2):
            acc_ref[h] = alpha * acc_ref[h] + jnp.dot(p[:, toks[h]], yb[h], preferred_element_type=F32)

    def trip(it, carry):
        scored = None
        weighted = None
        for k in range(slots):
            ch = it * slots + k
            nxt = ch + ahead
            nxt_slot = (k + ahead) % slots

            @pl.when(nxt < n_chunks)
            def _():
                start(b, nxt, nxt_slot)

            @pl.when((nxt >= n_chunks) & (b + 1 < n_b))
            def _():
                start(b + 1, nxt - n_chunks, nxt_slot)

            wait(k)
            fresh = scores(k)
            if weighted is not None:
                values(*weighted)
            if scored is not None:
                weighted = softmax(scored[0]) + (scored[1],)
            scored = fresh
        if weighted is not None:
            values(*weighted)
        values(*(softmax(scored[0]) + (scored[1],)))
        return carry

    lax.fori_loop(0, n_chunks // slots, trip, 0)
    o_ref[0] = (acc_ref[0] + acc_ref[1]) * (1.0 / l_ref[...])


def _decode_attention(page_table, ql, qp, cnew, knew, cache_ckv, cache_kpe, layer, pages, slots):
    nb, rows, c = ql.shape
    dec_seq = cnew.shape[1]
    page, r = cache_kpe.shape[2:]
    n_chunks = page_table.shape[1] // pages
    assert n_chunks * pages == page_table.shape[1] and n_chunks % slots == 0 and slots - 1 <= n_chunks
    kpe_t = jnp.swapaxes(cache_kpe, 2, 3)
    per_b = lambda a: pl.BlockSpec((1,) + a.shape[1:], lambda b, pt: (b, 0, 0))
    grid_spec = pltpu.PrefetchScalarGridSpec(
        num_scalar_prefetch=1,
        grid=(nb,),
        in_specs=[per_b(ql), per_b(qp), per_b(cnew), per_b(knew),
                  pl.BlockSpec(memory_space=pl.ANY), pl.BlockSpec(memory_space=pl.ANY)],
        out_specs=pl.BlockSpec((1, rows, c), lambda b, pt: (b, 0, 0)),
        scratch_shapes=[pltpu.VMEM((slots, pages * page, c), F32), pltpu.VMEM((slots, r, pages * page), F32),
                        pltpu.SemaphoreType.DMA((2, slots)),
                        pltpu.VMEM((rows, 1), F32), pltpu.VMEM((rows, 1), F32), pltpu.VMEM((2, rows, c), F32)],
    )
    return pl.pallas_call(
        functools.partial(_decode_attn_kernel, pages=pages, slots=slots, layer=layer, dec_seq=dec_seq),
        grid_spec=grid_spec,
        out_shape=jax.ShapeDtypeStruct((nb, rows, c), F32),
        compiler_params=pltpu.CompilerParams(dimension_semantics=("arbitrary",),
                                             vmem_limit_bytes=VMEM_LIMIT_BYTES),
        name="decode_attn",
    )(page_table, ql, qp, cnew, knew, cache_ckv, kpe_t)


def _out_sample_kernel(ol_ref, gm_ref, cv_ref, x_ref, wuvt_ref, wo_ref, gpost_ref, y_ref):
    half = gm_ref.shape[1]
    dn = (((1,), (1,)), ((), ()))
    o = jnp.concatenate(
        [lax.dot_general(ol_ref[:, h, :].astype(BF16), wuvt_ref[h], dn, preferred_element_type=F32)
         for h in range(N_HEADS)], axis=1)
    mla = (o * gm_ref[...]).astype(BF16)
    y = jnp.dot(mla, wo_ref[0:half, :], preferred_element_type=F32)
    y = y + jnp.dot(cv_ref[...].astype(BF16), wo_ref[half:, :], preferred_element_type=F32)
    y_ref[...] = x_ref[...] + _rms(y, gpost_ref[...])


def _out_sample(ol, gm, cv, x2d, wuv, wo, gpost):
    return pl.pallas_call(
        _out_sample_kernel,
        out_shape=jax.ShapeDtypeStruct(x2d.shape, F32),
        compiler_params=pltpu.CompilerParams(vmem_limit_bytes=VMEM_LIMIT_BYTES),
        name="out_sample",
    )(ol, gm, cv, x2d, wuv, wo, gpost)


def _rot_rows(w):
    r = w.shape[0] // 2
    return jnp.concatenate([-w[r:], w[:r]], axis=0)


def _pack_kernel(wint_ref, wuq_ref, wukv_ref, wout_ref,
                 w1_ref, wq_ref, wqt_ref, wk_ref, wvt_ref, wukt_ref, wuvt_ref, wo_ref):
    d = wint_ref.shape[1]
    blk = LANES

    def put_w1(col0, rows):
        w1_ref[:, col0:col0 + blk] = rows.T.astype(BF16)

    for r0 in range(0, C_KV1, blk):
        put_w1(r0, wint_ref[r0:r0 + blk, :])
    kpe = wint_ref[C_KV1:C_KV1 + QK_ROPE, :]
    put_w1(C_KP0, jnp.concatenate([jnp.zeros((QK_NOPE, d), F32), kpe, _rot_rows(kpe)], axis=0))
    src0 = C_KV1 + QK_ROPE
    for j in range(5 * GROUP // blk):
        put_w1(C_GM0 + j * blk, wint_ref[src0 + j * blk:src0 + (j + 1) * blk, :])

    uqt = wuq_ref[...].T
    ukvt = wukv_ref[...].T
    kv_lora = ukvt.shape[1]
    qk = QK_NOPE + QK_ROPE
    for h in range(N_HEADS):
        pe = uqt[h * qk + QK_NOPE:(h + 1) * qk]
        q_slab = jnp.concatenate([uqt[h * qk:h * qk + QK_NOPE], pe, _rot_rows(pe)], axis=0)
        wqt_ref[h * HEAD_SLAB:(h + 1) * HEAD_SLAB, :] = q_slab.astype(BF16)
        wq_ref[:, h * HEAD_SLAB:(h + 1) * HEAD_SLAB] = q_slab.T.astype(BF16)
        ukt = ukvt[h * (QK_NOPE + V_HEAD):h * (QK_NOPE + V_HEAD) + QK_NOPE]
        uvt = ukvt[h * (QK_NOPE + V_HEAD) + QK_NOPE:(h + 1) * (QK_NOPE + V_HEAD)]
        k_slab = jnp.concatenate([ukt, jnp.zeros((HEAD_SLAB - QK_NOPE, kv_lora), F32)], axis=0)
        wukt_ref[h] = k_slab.astype(BF16)
        wk_ref[:, h * HEAD_SLAB:(h + 1) * HEAD_SLAB] = k_slab.T.astype(BF16)
        wvt_ref[h * V_HEAD:(h + 1) * V_HEAD, :] = uvt.astype(BF16)
        wuvt_ref[h] = uvt.astype(BF16)
    wo_ref[...] = wout_ref[...].astype(BF16)


def _pack_weights(w_in, w_uq, w_ukv, w_out):
    d = w_in.shape[0]
    q_lora, kv_lora = w_uq.shape[0], w_ukv.shape[0]
    assert w_in.shape[1] == C_KV1 + QK_ROPE + 5 * GROUP
    sds = lambda *s: jax.ShapeDtypeStruct(s, BF16)
    return pl.pallas_call(
        _pack_kernel,
        out_shape=(sds(d, D_IN_PACKED), sds(q_lora, N_HEADS * HEAD_SLAB), sds(N_HEADS * HEAD_SLAB, q_lora),
                   sds(kv_lora, N_HEADS * HEAD_SLAB), sds(N_HEADS * V_HEAD, kv_lora),
                   sds(N_HEADS, HEAD_SLAB, kv_lora), sds(N_HEADS, V_HEAD, kv_lora), sds(*w_out.shape)),
        compiler_params=pltpu.CompilerParams(vmem_limit_bytes=VMEM_LIMIT_BYTES),
        name="pack_weights",
    )(w_in.T, w_uq, w_ukv, w_out)


PROMPT_TILE = 512
ATTN_TQ = 512
ATTN_TK = 512
ATTN_HEAD_GROUP = 4
DECODE_PAGES = 32
DECODE_SLOTS = 4


def kernel(x_prompt, x_sample, cache_ckv, cache_kpe, state_conv, page_table, g_pre, w_in, g_qnorm, w_uq,
           g_kvnorm, w_ukv, w_conv, w_out, g_post):
    depth = w_in.shape[0]
    nb, seq, d = x_prompt.shape
    db, dec_seq, _ = x_sample.shape
    past_len = page_table.shape[1] * cache_ckv.shape[2]
    c = cache_ckv.shape[3]

    tabs_p = _rope_tables(jnp.arange(seq, dtype=jnp.int32))
    tabs_s = _rope_tables(jnp.tile(past_len + jnp.arange(dec_seq, dtype=jnp.int32), db))

    xp = x_prompt.reshape(nb * seq, d)
    xs = x_sample.reshape(db * dec_seq, d)
    outs = [[] for _ in range(6)]
    for l in range(depth):
        w1, wq, wqt, wk, wvt, wukt, wuvt, wo = _pack_weights(w_in[l], w_uq[l], w_ukv[l], w_out[l])
        gpre, gq, gkv, gpost = g_pre[l][None], g_qnorm[l][None], g_kvnorm[l][None], g_post[l][None]

        qt, k, vt, ckv_p, kpe_p, gm, cv, conv_p = _proj_prompt(
            xp, seq, gpre, w1, gq, wqt, gkv, wk, wvt, w_conv[l], tabs_p, PROMPT_TILE)
        xp = _attention_out(qt, k, vt, gm, cv, xp, wo, gpost, nb, seq, ATTN_TQ, ATTN_TK, ATTN_HEAD_GROUP)
        outs[0].append(ckv_p.reshape(nb, seq, c))
        outs[1].append(jnp.swapaxes(kpe_p, 1, 2))
        outs[2].append(conv_p)

        st = state_conv[l].astype(F32)
        s0 = jnp.repeat(st[:, 0], dec_seq, axis=0)
        s1 = jnp.repeat(st[:, 1], dec_seq, axis=0)
        ql, qp, ckv_s, kpe_s, gm_s, cv_s, u_s = _proj_sample(
            xs, dec_seq, gpre, w1, gq, wq, gkv, wukt, w_conv[l], tabs_s[:4], s0, s1)
        rows = dec_seq * N_HEADS
        ol = _decode_attention(page_table, ql.reshape(db, rows, c), qp.reshape(db, rows, QK_ROPE),
                               ckv_s.reshape(db, dec_seq, c), kpe_s.reshape(db, dec_seq, QK_ROPE),
                               cache_ckv, cache_kpe, l, DECODE_PAGES, DECODE_SLOTS)
        xs = _out_sample(ol.reshape(db * dec_seq, N_HEADS, c), gm_s, cv_s, xs, wuvt, wo, gpost)
        outs[3].append(ckv_s.reshape(db, dec_seq, c))
        outs[4].append(kpe_s.reshape(db, dec_seq, QK_ROPE))
        u_pad = jnp.concatenate([st, u_s.reshape(db, dec_seq, GROUP)], axis=1)
        outs[5].append(u_pad[:, -(CONV_W - 1):])

    return (xp.reshape(nb, seq, d), xs.reshape(db, dec_seq, d), *[jnp.stack(o_) for o_ in outs])
```

```python
import functools

import jax
import jax.numpy as jnp
from jax import lax
from jax.experimental import pallas as pl
from jax.experimental.pallas import tpu as pltpu

N_HEADS = 8
QK_NOPE = 64
QK_ROPE = 32
V_HEAD = 64
CONV_W = 3
ROPE_THETA = 10000.0
EPS = 1e-6
ATTN_SCALE = (QK_NOPE + QK_ROPE) ** -0.5

LANES = 128
HEAD_SLAB = LANES
ONES_ROWS = 16
LOG2_E = 1.4426950408889634
VMEM_LIMIT_BYTES = 56 * 1024 * 1024

F32 = jnp.float32
BF16 = jnp.bfloat16


def _silu(x):
    return x * (1.0 / (1.0 + jnp.exp(-x)))


def _rms(x, g):
    return x * lax.rsqrt(jnp.mean(x * x, axis=-1, keepdims=True) + EPS) * g


def _rope_table_kernel(pos_ref, invf_ref, cq1_ref, cq2_ref, ck1_ref, ck2_ref, cq1t_ref, cq2t_ref):
    ang = invf_ref[...] * pos_ref[...]
    c, s = jnp.cos(ang), jnp.sin(ang)
    t = ang.shape[1]
    nope = jnp.ones((QK_NOPE, t), F32)
    zn = jnp.zeros((QK_NOPE, t), F32)
    zr = jnp.zeros((HEAD_SLAB - QK_NOPE - QK_ROPE, t), F32)

    def slab(first, mid):
        return jnp.concatenate([first, mid, zr], axis=0)

    cq1_ref[...] = slab(nope * ATTN_SCALE, c * ATTN_SCALE).T
    cq2_ref[...] = slab(zn, s * ATTN_SCALE).T
    ck1_ref[...] = slab(zn, c).T
    ck2_ref[...] = slab(zn, s).T
    cq1t_ref[...] = slab(nope * (ATTN_SCALE * LOG2_E), c * (ATTN_SCALE * LOG2_E))
    cq2t_ref[...] = slab(zn, s * (ATTN_SCALE * LOG2_E))


def _rope_tables(pos):
    t = pos.shape[0]
    r = QK_ROPE
    inv_freq = ROPE_THETA ** (-jnp.arange(0, r, 2, dtype=F32) / r)
    invf = jnp.concatenate([inv_freq, inv_freq]).reshape(r, 1)
    out = jax.ShapeDtypeStruct((t, LANES), F32)
    out_t = jax.ShapeDtypeStruct((LANES, t), F32)
    return pl.pallas_call(
        _rope_table_kernel,
        out_shape=(out, out, out, out, out_t, out_t),
        name="rope_tables",
    )(pos.astype(F32).reshape(1, t), invf)


C_Q0, C_Q1 = 0, 384
C_KV0, C_KV1 = 384, 640
C_KP0, C_KP1 = 640, 768
C_GM0 = 768
GROUP = 512
D_IN_PACKED = C_GM0 + 5 * GROUP


def _rope_slab(blk, c1, c2):
    return blk * c1 + pltpu.roll(blk, LANES - QK_ROPE, 1) * c2


def _project_common(x_ref, gpre_ref, w1_ref, gq_ref, gkv_ref, ck1_ref, ck2_ref):
    x = x_ref[...]
    xn = _rms(x, gpre_ref[...]).astype(BF16)

    def proj(c0, c1):
        return jnp.dot(xn, w1_ref[:, c0:c1], preferred_element_type=F32)

    cqn = _rms(proj(C_Q0, C_Q1), gq_ref[...]).astype(BF16)
    ckv = _rms(proj(C_KV0, C_KV1), gkv_ref[...])
    kpe_blk = _rope_slab(proj(C_KP0, C_KP1), ck1_ref[...], ck2_ref[...])
    return proj, cqn, ckv, kpe_blk


def _conv_gate(proj, um1, um2, u, wconv_ref):
    w = wconv_ref[...]
    y = w[0:1, :] * um2 + w[1:2, :] * um1 + w[2:3, :] * u
    bg = proj(C_GM0 + GROUP, C_GM0 + 2 * GROUP)
    gc = proj(C_GM0 + 4 * GROUP, C_GM0 + 5 * GROUP)
    return bg * y * _silu(gc)


def _proj_prompt_kernel(x_ref, gpre_ref, w1_ref, gq_ref, wqt_ref, gkv_ref, wk_ref, wvt_ref, wconv_ref,
                        cq1t_ref, cq2t_ref, ck1_ref, ck2_ref,
                        qt_out, k_out, vt_out, ckv_out, kpet_out, gm_out, cv_out, conv_out,
                        carry_ref, *, tiles_per_seq):
    tm = x_ref.shape[0]
    step = pl.program_id(0)
    dn = (((1,), (1,)), ((), ()))
    proj, cqn, ckv, kpe_blk = _project_common(x_ref, gpre_ref, w1_ref, gq_ref, gkv_ref, ck1_ref, ck2_ref)
    qt = lax.dot_general(wqt_ref[...], cqn, dn, preferred_element_type=F32)
    cq1t, cq2t = cq1t_ref[...], cq2t_ref[...]
    for h in range(N_HEADS):
        slab = qt[h * HEAD_SLAB:(h + 1) * HEAD_SLAB, :]
        roped = slab * cq1t + pltpu.roll(slab, HEAD_SLAB - QK_ROPE, 0) * cq2t
        qt_out[h * HEAD_SLAB:(h + 1) * HEAD_SLAB, :] = roped.astype(BF16)
    ckv_out[...] = ckv
    kpet_out[0] = kpe_blk.T[QK_NOPE:QK_NOPE + QK_ROPE, :]
    ckvb = ckv.astype(BF16)
    kfull = jnp.dot(ckvb, wk_ref[...], preferred_element_type=F32)
    for h in range(N_HEADS):
        k_out[:, h * HEAD_SLAB:(h + 1) * HEAD_SLAB] = (kfull[:, h * HEAD_SLAB:(h + 1) * HEAD_SLAB] + kpe_blk).astype(BF16)
    vt_out[...] = lax.dot_general(wvt_ref[...], ckvb, dn, preferred_element_type=F32).astype(BF16)
    gm_out[...] = _silu(proj(C_GM0, C_GM0 + GROUP)).astype(BF16)

    u = proj(C_GM0 + 2 * GROUP, C_GM0 + 3 * GROUP) * proj(C_GM0 + 3 * GROUP, C_GM0 + 4 * GROUP)

    @pl.when(step % tiles_per_seq == 0)
    def _():
        carry_ref[...] = jnp.zeros_like(carry_ref)

    prev = carry_ref[...]
    p1, p2 = prev[7:8, :], prev[6:7, :]
    row = lax.broadcasted_iota(jnp.int32, u.shape, 0)
    um1 = jnp.where(row == 0, p1, pltpu.roll(u, 1, 0))
    um2 = jnp.where(row == 0, p2, jnp.where(row == 1, p1, pltpu.roll(u, 2, 0)))
    cv_out[...] = _conv_gate(proj, um1, um2, u, wconv_ref).astype(BF16)
    carry_ref[...] = u[tm - 8:tm, :]
    conv_out[0] = u[tm - (CONV_W - 1):tm, :]


def _const_spec(shape):
    nd = len(shape)
    return pl.BlockSpec(shape, lambda *_: (0,) * nd, pipeline_mode=pl.Buffered(1))


def _proj_prompt(x2d, seq, gpre, w1, gq, wqt, gkv, wk, wvt, wconv, tabs, tm):
    t, d = x2d.shape
    nb = t // seq
    hs = N_HEADS * HEAD_SLAB
    hv = N_HEADS * V_HEAD
    cq1, cq2, ck1, ck2, cq1t, cq2t = tabs
    row = lambda w: pl.BlockSpec((tm, w), lambda i: (i, 0))
    col = lambda h: pl.BlockSpec((h, tm), lambda i: (0, i))
    tab = pl.BlockSpec((tm, LANES), lambda i: (i % (seq // tm), 0))
    tab_t = pl.BlockSpec((LANES, tm), lambda i: (0, i % (seq // tm)))
    out_shape = (
        jax.ShapeDtypeStruct((hs, t), BF16),
        jax.ShapeDtypeStruct((t, hs), BF16),
        jax.ShapeDtypeStruct((hv, t), BF16),
        jax.ShapeDtypeStruct((t, gkv.shape[1]), F32),
        jax.ShapeDtypeStruct((nb, QK_ROPE, seq), F32),
        jax.ShapeDtypeStruct((t, GROUP), BF16),
        jax.ShapeDtypeStruct((t, GROUP), BF16),
        jax.ShapeDtypeStruct((nb, CONV_W - 1, GROUP), F32),
    )
    return pl.pallas_call(
        functools.partial(_proj_prompt_kernel, tiles_per_seq=seq // tm),
        grid=(t // tm,),
        in_specs=[row(d), _const_spec(gpre.shape), _const_spec(w1.shape), _const_spec(gq.shape),
                  _const_spec(wqt.shape), _const_spec(gkv.shape), _const_spec(wk.shape), _const_spec(wvt.shape),
                  _const_spec(wconv.shape), tab_t, tab_t, tab, tab],
        out_specs=(col(hs), row(hs), col(hv), row(gkv.shape[1]),
                   pl.BlockSpec((1, QK_ROPE, tm), lambda i: (i // (seq // tm), 0, i % (seq // tm))),
                   row(GROUP), row(GROUP),
                   pl.BlockSpec((1, CONV_W - 1, GROUP), lambda i: (i // (seq // tm), 0, 0))),
        out_shape=out_shape,
        scratch_shapes=[pltpu.VMEM((8, GROUP), F32)],
        compiler_params=pltpu.CompilerParams(dimension_semantics=("arbitrary",),
                                             vmem_limit_bytes=VMEM_LIMIT_BYTES),
        name="proj_prompt",
    )(x2d, gpre, w1, gq, wqt, gkv, wk, wvt, wconv, cq1t, cq2t, ck1, ck2)


def _attn_kernel(qt_ref, k_ref, vt_ref, gm_ref, cv_ref, x_ref, wo_ref, gpost_ref, y_ref, s_ref,
                 *, tq, tk):
    i = pl.program_id(1)
    hq = tq // 2
    tri = (lax.broadcasted_iota(jnp.int32, (hq, hq), 0) <= lax.broadcasted_iota(jnp.int32, (hq, hq), 1))

    def slab(h):
        return slice(h * HEAD_SLAB, (h + 1) * HEAD_SLAB)

    def values(h, toks):
        ones = jnp.ones((ONES_ROWS, toks.size), BF16)
        return jnp.concatenate([vt_ref[h * V_HEAD:(h + 1) * V_HEAD, toks], ones], 0)

    def scores(h, toks):
        return jnp.dot(k_ref[toks, slab(h)], qt_ref[slab(h), :], preferred_element_type=F32)

    def update(h, toks, st, carry):
        m, acc = carry
        m_new = jnp.maximum(m, jnp.max(st, axis=0, keepdims=True))
        pt = jnp.exp2(st - m_new).astype(BF16)
        return m_new, jnp.exp2(m - m_new) * acc + jnp.dot(values(h, toks), pt, preferred_element_type=F32)

    base = pl.multiple_of(i * tq, tq)
    ka, kb = pl.ds(base, hq), pl.ds(base + hq, hq)

    def diagonal_scores(h):
        return (jnp.dot(k_ref[ka, slab(h)], qt_ref[slab(h), :], preferred_element_type=F32),
                jnp.dot(k_ref[kb, slab(h)], qt_ref[slab(h), hq:], preferred_element_type=F32))

    def diagonal_init(h, a, b_):
        a_lo = jnp.where(tri, a[:, :hq], -jnp.inf)
        a_hi = a[:, hq:]
        b_ = jnp.where(tri, b_, -jnp.inf)
        m_lo = jnp.max(a_lo, axis=0, keepdims=True)
        m_hi = jnp.maximum(jnp.max(a_hi, axis=0, keepdims=True), jnp.max(b_, axis=0, keepdims=True))
        pa = jnp.concatenate([jnp.exp2(a_lo - m_lo), jnp.exp2(a_hi - m_hi)], axis=1).astype(BF16)
        pb = jnp.exp2(b_ - m_hi).astype(BF16)
        acc_a = jnp.dot(values(h, ka), pa, preferred_element_type=F32)
        acc_b = jnp.dot(values(h, kb), pb, preferred_element_type=F32)
        return (jnp.concatenate([m_lo, m_hi], axis=1),
                jnp.concatenate([acc_a[:, :hq], acc_a[:, hq:] + acc_b], axis=1))

    n_full = i * (tq // tk)

    def tile_tokens(j):
        return pl.ds(pl.multiple_of(j * tk, tk), tk)

    carries = []
    cur = diagonal_scores(0)
    for h in range(N_HEADS):
        if h + 1 < N_HEADS:
            ahead = diagonal_scores(h + 1)
        else:
            s_ref[0] = scores(0, tile_tokens(0))
        carries.append(diagonal_init(h, *cur))
        cur = ahead
    carries = tuple(carries)

    def full_tile(j, carries):
        toks = tile_tokens(j)
        nxt = tile_tokens(jnp.minimum(j + 1, n_full - 1))
        out = []
        for h in range(N_HEADS):
            if h + 1 < N_HEADS:
                s_ref[h + 1] = scores(h + 1, toks)
                out.append(update(h, toks, s_ref[h], carries[h]))
            else:
                cur = s_ref[h]
                s_ref[0] = scores(0, nxt)
                out.append(update(h, toks, cur, carries[h]))
        return tuple(out)

    carries = lax.fori_loop(0, n_full, full_tile, carries)
    outs = [acc[:V_HEAD] * (1.0 / acc[V_HEAD:V_HEAD + 1]) for _, acc in carries]
    pairs = [jnp.concatenate(outs[2 * p:2 * p + 2], axis=0).T for p in range(N_HEADS // 2)]

    half = gm_ref.shape[1]
    mla = (jnp.concatenate(pairs, axis=1) * gm_ref[...].astype(F32)).astype(BF16)
    y = jnp.dot(mla, wo_ref[0:half, :], preferred_element_type=F32)
    y = y + jnp.dot(cv_ref[...], wo_ref[half:, :], preferred_element_type=F32)
    y_ref[...] = x_ref[...] + _rms(y, gpost_ref[...])


def _attention_out(qt, k, vt, gm, cv, x2d, wo, gpost, nb, seq, tq, tk):
    hs, t = qt.shape
    hv = vt.shape[0]
    d = x2d.shape[1]
    assert N_HEADS % 2 == 0 and tq % tk == 0
    row = lambda w: pl.BlockSpec((tq, w), lambda b, i: (b * (seq // tq) + i, 0))
    return pl.pallas_call(
        functools.partial(_attn_kernel, tq=tq, tk=tk),
        grid=(nb, seq // tq),
        in_specs=[pl.BlockSpec((hs, tq), lambda b, i: (0, b * (seq // tq) + i)),
                  pl.BlockSpec((seq, hs), lambda b, i: (b, 0)),
                  pl.BlockSpec((hv, seq), lambda b, i: (0, b)),
                  row(gm.shape[1]), row(cv.shape[1]), row(d), _const_spec(wo.shape), _const_spec(gpost.shape)],
        out_specs=row(d),
        out_shape=jax.ShapeDtypeStruct((t, d), F32),
        scratch_shapes=[pltpu.VMEM((N_HEADS, tk, tq), F32)],
        compiler_params=pltpu.CompilerParams(dimension_semantics=("arbitrary", "arbitrary"),
                                             vmem_limit_bytes=VMEM_LIMIT_BYTES),
        name="prompt_attn",
    )(qt, k, vt, gm, cv, x2d, wo, gpost)


def _proj_sample_kernel(x_ref, gpre_ref, w1_ref, gq_ref, wq_ref, gkv_ref, wukt_ref, wconv_ref,
                        cq1_ref, cq2_ref, ck1_ref, ck2_ref, s0_ref, s1_ref,
                        ql_out, qp_out, ckv_out, kpe_out, gm_out, cv_out, conv_out, *, dec_seq):
    proj, cqn, ckv, kpe_blk = _project_common(x_ref, gpre_ref, w1_ref, gq_ref, gkv_ref, ck1_ref, ck2_ref)
    qraw = jnp.dot(cqn, wq_ref[...], preferred_element_type=F32)
    cq1, cq2 = cq1_ref[...], cq2_ref[...]
    for h in range(N_HEADS):
        qh = _rope_slab(qraw[:, h * HEAD_SLAB:(h + 1) * HEAD_SLAB], cq1, cq2)
        ql_out[:, h, :] = jnp.dot(qh.astype(BF16), wukt_ref[h], preferred_element_type=F32)
        qp_out[:, h, :] = pltpu.roll(qh, LANES - QK_NOPE, 1)[:, 0:QK_ROPE]
    ckv_out[...] = ckv
    kpe_out[...] = kpe_blk[:, QK_NOPE:QK_NOPE + QK_ROPE]
    gm_out[...] = _silu(proj(C_GM0, C_GM0 + GROUP))

    u = proj(C_GM0 + 2 * GROUP, C_GM0 + 3 * GROUP) * proj(C_GM0 + 3 * GROUP, C_GM0 + 4 * GROUP)
    t_in_seq = lax.broadcasted_iota(jnp.int32, u.shape, 0) % dec_seq
    s0, s1 = s0_ref[...], s1_ref[...]
    um1 = jnp.where(t_in_seq == 0, s1, pltpu.roll(u, 1, 0))
    um2 = jnp.where(t_in_seq == 0, s0, jnp.where(t_in_seq == 1, s1, pltpu.roll(u, 2, 0)))
    cv_out[...] = _conv_gate(proj, um1, um2, u, wconv_ref)
    conv_out[...] = u


def _proj_sample(x2d, dec_seq, gpre, w1, gq, wq, gkv, wukt, wconv, tabs, s0, s1):
    t, d = x2d.shape
    c = gkv.shape[1]
    out_shape = (
        jax.ShapeDtypeStruct((t, N_HEADS, c), F32),
        jax.ShapeDtypeStruct((t, N_HEADS, QK_ROPE), F32),
        jax.ShapeDtypeStruct((t, c), F32),
        jax.ShapeDtypeStruct((t, QK_ROPE), F32),
        jax.ShapeDtypeStruct((t, GROUP), F32),
        jax.ShapeDtypeStruct((t, GROUP), F32),
        jax.ShapeDtypeStruct((t, GROUP), F32),
    )
    return pl.pallas_call(
        functools.partial(_proj_sample_kernel, dec_seq=dec_seq),
        out_shape=out_shape,
        compiler_params=pltpu.CompilerParams(vmem_limit_bytes=VMEM_LIMIT_BYTES),
        name="proj_sample",
    )(x2d, gpre, w1, gq, wq, gkv, wukt, wconv, *tabs, s0, s1)


def _decode_attn_kernel(pt_ref, ql_ref, qp_ref, cnew_ref, knew_ref, y_hbm, x_hbm, o_ref,
                        ybuf, xbuf, sem, m_ref, l_ref, acc_ref, *, pages, slots, layer, dec_seq):
    b = pl.program_id(0)
    n_b = pl.num_programs(0)
    n_chunks = pt_ref.shape[1] // pages
    page = ybuf.shape[1] // pages
    rows = ql_ref.shape[1]
    dn = (((1,), (1,)), ((), ()))
    ahead = slots - 1

    def chunk_copies(bb, ch, slot, page_ids):
        cps = []
        for i in range(pages):
            pid = page_ids(bb, ch * pages + i)
            tok = pl.ds(i * page, page)
            cps.append(pltpu.make_async_copy(y_hbm.at[layer, pid], ybuf.at[slot, tok, :], sem.at[0, slot]))
            cps.append(pltpu.make_async_copy(x_hbm.at[layer, pid], xbuf.at[slot, :, tok], sem.at[1, slot]))
        return cps

    def start(bb, ch, slot):
        for cp in chunk_copies(bb, ch, slot, lambda r, j: pt_ref[r, j]):
            cp.start()

    def wait(slot):
        for cp in chunk_copies(0, 0, slot, lambda r, j: 0):
            cp.wait()

    @pl.when(b == 0)
    def _():
        for k in range(ahead):
            start(0, k, k)

    qlf, qpf = ql_ref[0], qp_ref[0]
    ql, qp = qlf.astype(BF16), qpf.astype(BF16)
    tok_of_row = lax.broadcasted_iota(jnp.int32, (rows, 1), 0) // N_HEADS
    s_new = []
    for j in range(dec_seq):
        sj = (jnp.sum(qlf * cnew_ref[0, j:j + 1, :], axis=-1, keepdims=True)
              + jnp.sum(qpf * knew_ref[0, j:j + 1, :], axis=-1, keepdims=True))
        s_new.append(jnp.where(tok_of_row >= j, sj, -jnp.inf))
    m_n = functools.reduce(jnp.maximum, s_new)
    p_n = [jnp.exp(sj - m_n) for sj in s_new]
    m_ref[...] = m_n
    l_ref[...] = functools.reduce(lambda a, e: a + e, p_n)
    acc_ref[0] = functools.reduce(lambda a, e: a + e,
                                  [p_n[j] * cnew_ref[0, j:j + 1, :] for j in range(dec_seq)])
    acc_ref[1] = jnp.zeros(acc_ref.shape[1:], F32)

    half = pages * page // 2
    toks = [slice(h * half, (h + 1) * half) for h in range(2)]

    def scores(slot):
        yb = [ybuf[slot, t, :].astype(BF16) for t in toks]
        xb = [xbuf[slot, :, t].astype(BF16) for t in toks]
        s = jnp.concatenate(
            [lax.dot_general(ql, yb[h], dn, preferred_element_type=F32)
             + jnp.dot(qp, xb[h], preferred_element_type=F32) for h in range(2)], axis=1)
        return s, yb

    def softmax(s):
        m = m_ref[...]
        m_new = jnp.maximum(m, jnp.max(s, axis=-1, keepdims=True))
        alpha = jnp.exp(m - m_new)
        p = jnp.exp(s - m_new)
        l_ref[...] = alpha * l_ref[...] + jnp.sum(p, axis=-1, keepdims=True)
        m_ref[...] = m_new
        return alpha, p.astype(BF16)

    def values(alpha, p, yb):
        for h in range(2):
            acc_ref[h] = alpha * acc_ref[h] + jnp.dot(p[:, toks[h]], yb[h], preferred_element_type=F32)

    def trip(it, carry):
        scored = None
        weighted = None
        for k in range(slots):
            ch = it * slots + k
            nxt = ch + ahead
            nxt_slot = (k + ahead) % slots

            @pl.when(nxt < n_chunks)
            def _():
                start(b, nxt, nxt_slot)

            @pl.when((nxt >= n_chunks) & (b + 1 < n_b))
            def _():
                start(b + 1, nxt - n_chunks, nxt_slot)

            wait(k)
            fresh = scores(k)
            if weighted is not None:
                values(*weighted)
            if scored is not None:
                weighted = softmax(scored[0]) + (scored[1],)
            scored = fresh
        if weighted is not None:
            values(*weighted)
        values(*(softmax(scored[0]) + (scored[1],)))
        return carry

    lax.fori_loop(0, n_chunks // slots, trip, 0)
    o_ref[0] = (acc_ref[0] + acc_ref[1]) * (1.0 / l_ref[...])


def _decode_attention(page_table, ql, qp, cnew, knew, cache_ckv, cache_kpe, layer, pages, slots):
    nb, rows, c = ql.shape
    dec_seq = cnew.shape[1]
    page, r = cache_kpe.shape[2:]
    n_chunks = page_table.shape[1] // pages
    assert n_chunks * pages == page_table.shape[1] and n_chunks % slots == 0 and slots - 1 <= n_chunks
    kpe_t = jnp.swapaxes(cache_kpe, 2, 3)
    per_b = lambda a: pl.BlockSpec((1,) + a.shape[1:], lambda b, pt: (b, 0, 0))
    grid_spec = pltpu.PrefetchScalarGridSpec(
        num_scalar_prefetch=1,
        grid=(nb,),
        in_specs=[per_b(ql), per_b(qp), per_b(cnew), per_b(knew),
                  pl.BlockSpec(memory_space=pl.ANY), pl.BlockSpec(memory_space=pl.ANY)],
        out_specs=pl.BlockSpec((1, rows, c), lambda b, pt: (b, 0, 0)),
        scratch_shapes=[pltpu.VMEM((slots, pages * page, c), F32), pltpu.VMEM((slots, r, pages * page), F32),
                        pltpu.SemaphoreType.DMA((2, slots)),
                        pltpu.VMEM((rows, 1), F32), pltpu.VMEM((rows, 1), F32), pltpu.VMEM((2, rows, c), F32)],
    )
    return pl.pallas_call(
        functools.partial(_decode_attn_kernel, pages=pages, slots=slots, layer=layer, dec_seq=dec_seq),
        grid_spec=grid_spec,
        out_shape=jax.ShapeDtypeStruct((nb, rows, c), F32),
        compiler_params=pltpu.CompilerParams(dimension_semantics=("arbitrary",),
                                             vmem_limit_bytes=VMEM_LIMIT_BYTES),
        name="decode_attn",
    )(page_table, ql, qp, cnew, knew, cache_ckv, kpe_t)


def _out_sample_kernel(ol_ref, gm_ref, cv_ref, x_ref, wuvt_ref, wo_ref, gpost_ref, y_ref):
    half = gm_ref.shape[1]
    dn = (((1,), (1,)), ((), ()))
    o = jnp.concatenate(
        [lax.dot_general(ol_ref[:, h, :].astype(BF16), wuvt_ref[h], dn, preferred_element_type=F32)
         for h in range(N_HEADS)], axis=1)
    mla = (o * gm_ref[...]).astype(BF16)
    y = jnp.dot(mla, wo_ref[0:half, :], preferred_element_type=F32)
    y = y + jnp.dot(cv_ref[...].astype(BF16), wo_ref[half:, :], preferred_element_type=F32)
    y_ref[...] = x_ref[...] + _rms(y, gpost_ref[...])


def _out_sample(ol, gm, cv, x2d, wuv, wo, gpost):
    return pl.pallas_call(
        _out_sample_kernel,
        out_shape=jax.ShapeDtypeStruct(x2d.shape, F32),
        compiler_params=pltpu.CompilerParams(vmem_limit_bytes=VMEM_LIMIT_BYTES),
        name="out_sample",
    )(ol, gm, cv, x2d, wuv, wo, gpost)


def _rot_rows(w):
    r = w.shape[0] // 2
    return jnp.concatenate([-w[r:], w[:r]], axis=0)


def _pack_kernel(wint_ref, wuq_ref, wukv_ref, wout_ref,
                 w1_ref, wq_ref, wqt_ref, wk_ref, wvt_ref, wukt_ref, wuvt_ref, wo_ref):
    d = wint_ref.shape[1]
    blk = LANES

    def put_w1(col0, rows):
        w1_ref[:, col0:col0 + blk] = rows.T.astype(BF16)

    for r0 in range(0, C_KV1, blk):
        put_w1(r0, wint_ref[r0:r0 + blk, :])
    kpe = wint_ref[C_KV1:C_KV1 + QK_ROPE, :]
    put_w1(C_KP0, jnp.concatenate([jnp.zeros((QK_NOPE, d), F32), kpe, _rot_rows(kpe)], axis=0))
    src0 = C_KV1 + QK_ROPE
    for j in range(5 * GROUP // blk):
        put_w1(C_GM0 + j * blk, wint_ref[src0 + j * blk:src0 + (j + 1) * blk, :])

    uqt = wuq_ref[...].T
    ukvt = wukv_ref[...].T
    kv_lora = ukvt.shape[1]
    qk = QK_NOPE + QK_ROPE
    for h in range(N_HEADS):
        pe = uqt[h * qk + QK_NOPE:(h + 1) * qk]
        q_slab = jnp.concatenate([uqt[h * qk:h * qk + QK_NOPE], pe, _rot_rows(pe)], axis=0)
        wqt_ref[h * HEAD_SLAB:(h + 1) * HEAD_SLAB, :] = q_slab.astype(BF16)
        wq_ref[:, h * HEAD_SLAB:(h + 1) * HEAD_SLAB] = q_slab.T.astype(BF16)
        ukt = ukvt[h * (QK_NOPE + V_HEAD):h * (QK_NOPE + V_HEAD) + QK_NOPE]
        uvt = ukvt[h * (QK_NOPE + V_HEAD) + QK_NOPE:(h + 1) * (QK_NOPE + V_HEAD)]
        k_slab = jnp.concatenate([ukt, jnp.zeros((HEAD_SLAB - QK_NOPE, kv_lora), F32)], axis=0)
        wukt_ref[h] = k_slab.astype(BF16)
        wk_ref[:, h * HEAD_SLAB:(h + 1) * HEAD_SLAB] = k_slab.T.astype(BF16)
        wvt_ref[h * V_HEAD:(h + 1) * V_HEAD, :] = uvt.astype(BF16)
        wuvt_ref[h] = uvt.astype(BF16)
    wo_ref[...] = wout_ref[...].astype(BF16)


def _pack_weights(w_in, w_uq, w_ukv, w_out):
    d = w_in.shape[0]
    q_lora, kv_lora = w_uq.shape[0], w_ukv.shape[0]
    assert w_in.shape[1] == C_KV1 + QK_ROPE + 5 * GROUP
    sds = lambda *s: jax.ShapeDtypeStruct(s, BF16)
    return pl.pallas_call(
        _pack_kernel,
        out_shape=(sds(d, D_IN_PACKED), sds(q_lora, N_HEADS * HEAD_SLAB), sds(N_HEADS * HEAD_SLAB, q_lora),
                   sds(kv_lora, N_HEADS * HEAD_SLAB), sds(N_HEADS * V_HEAD, kv_lora),
                   sds(N_HEADS, HEAD_SLAB, kv_lora), sds(N_HEADS, V_HEAD, kv_lora), sds(*w_out.shape)),
        compiler_params=pltpu.CompilerParams(vmem_limit_bytes=VMEM_LIMIT_BYTES),
        name="pack_weights",
    )(w_in.T, w_uq, w_ukv, w_out)


PROMPT_TILE = 512
ATTN_TQ = 512
ATTN_TK = 512
DECODE_PAGES = 32
DECODE_SLOTS = 4


def kernel(x_prompt, x_sample, cache_ckv, cache_kpe, state_conv, page_table, g_pre, w_in, g_qnorm, w_uq,
           g_kvnorm, w_ukv, w_conv, w_out, g_post):
    depth = w_in.shape[0]
    nb, seq, d = x_prompt.shape
    db, dec_seq, _ = x_sample.shape
    past_len = page_table.shape[1] * cache_ckv.shape[2]
    c = cache_ckv.shape[3]

    tabs_p = _rope_tables(jnp.arange(seq, dtype=jnp.int32))
    tabs_s = _rope_tables(jnp.tile(past_len + jnp.arange(dec_seq, dtype=jnp.int32), db))

    xp = x_prompt.reshape(nb * seq, d)
    xs = x_sample.reshape(db * dec_seq, d)
    outs = [[] for _ in range(6)]
    for l in range(depth):
        w1, wq, wqt, wk, wvt, wukt, wuvt, wo = _pack_weights(w_in[l], w_uq[l], w_ukv[l], w_out[l])
        gpre, gq, gkv, gpost = g_pre[l][None], g_qnorm[l][None], g_kvnorm[l][None], g_post[l][None]

        qt, k, vt, ckv_p, kpe_p, gm, cv, conv_p = _proj_prompt(
            xp, seq, gpre, w1, gq, wqt, gkv, wk, wvt, w_conv[l], tabs_p, PROMPT_TILE)
        xp = _attention_out(qt, k, vt, gm, cv, xp, wo, gpost, nb, seq, ATTN_TQ, ATTN_TK)
        outs[0].append(ckv_p.reshape(nb, seq, c))
        outs[1].append(jnp.swapaxes(kpe_p, 1, 2))
        outs[2].append(conv_p)

        st = state_conv[l].astype(F32)
        s0 = jnp.repeat(st[:, 0], dec_seq, axis=0)
        s1 = jnp.repeat(st[:, 1], dec_seq, axis=0)
        ql, qp, ckv_s, kpe_s, gm_s, cv_s, u_s = _proj_sample(
            xs, dec_seq, gpre, w1, gq, wq, gkv, wukt, w_conv[l], tabs_s[:4], s0, s1)
        rows = dec_seq * N_HEADS
        ol = _decode_attention(page_table, ql.reshape(db, rows, c), qp.reshape(db, rows, QK_ROPE),
                               ckv_s.reshape(db, dec_seq, c), kpe_s.reshape(db, dec_seq, QK_ROPE),
                               cache_ckv, cache_kpe, l, DECODE_PAGES, DECODE_SLOTS)
        xs = _out_sample(ol.reshape(db * dec_seq, N_HEADS, c), gm_s, cv_s, xs, wuvt, wo, gpost)
        outs[3].append(ckv_s.reshape(db, dec_seq, c))
        outs[4].append(kpe_s.reshape(db, dec_seq, QK_ROPE))
        u_pad = jnp.concatenate([st, u_s.reshape(db, dec_seq, GROUP)], axis=1)
        outs[5].append(u_pad[:, -(CONV_W - 1):])

    return (xp.reshape(nb, seq, d), xs.reshape(db, dec_seq, d), *[jnp.stack(o_) for o_ in outs])
```

```python
import functools

import jax
import jax.numpy as jnp
from jax import lax
from jax.experimental import pallas as pl
from jax.experimental.pallas import tpu as pltpu

N_HEADS = 8
QK_NOPE = 64
QK_ROPE = 32
V_HEAD = 64
CONV_W = 3
ROPE_THETA = 10000.0
EPS = 1e-6
ATTN_SCALE = (QK_NOPE + QK_ROPE) ** -0.5

LANES = 128
HEAD_SLAB = LANES
ONES_ROWS = 16
LOG2_E = 1.4426950408889634
VMEM_LIMIT_BYTES = 56 * 1024 * 1024

F32 = jnp.float32
BF16 = jnp.bfloat16


def _silu(x):
    return x * (1.0 / (1.0 + jnp.exp(-x)))


def _rms(x, g):
    return x * lax.rsqrt(jnp.mean(x * x, axis=-1, keepdims=True) + EPS) * g


def _rope_table_kernel(pos_ref, invf_ref, cq1_ref, cq2_ref, ck1_ref, ck2_ref, cq1t_ref, cq2t_ref):
    ang = invf_ref[...] * pos_ref[...]
    c, s = jnp.cos(ang), jnp.sin(ang)
    t = ang.shape[1]
    nope = jnp.ones((QK_NOPE, t), F32)
    zn = jnp.zeros((QK_NOPE, t), F32)
    zr = jnp.zeros((HEAD_SLAB - QK_NOPE - QK_ROPE, t), F32)

    def slab(first, mid):
        return jnp.concatenate([first, mid, zr], axis=0)

    cq1_ref[...] = slab(nope * ATTN_SCALE, c * ATTN_SCALE).T
    cq2_ref[...] = slab(zn, s * ATTN_SCALE).T
    ck1_ref[...] = slab(zn, c).T
    ck2_ref[...] = slab(zn, s).T
    cq1t_ref[...] = slab(nope * (ATTN_SCALE * LOG2_E), c * (ATTN_SCALE * LOG2_E))
    cq2t_ref[...] = slab(zn, s * (ATTN_SCALE * LOG2_E))


def _rope_tables(pos):
    t = pos.shape[0]
    r = QK_ROPE
    inv_freq = ROPE_THETA ** (-jnp.arange(0, r, 2, dtype=F32) / r)
    invf = jnp.concatenate([inv_freq, inv_freq]).reshape(r, 1)
    out = jax.ShapeDtypeStruct((t, LANES), F32)
    out_t = jax.ShapeDtypeStruct((LANES, t), F32)
    return pl.pallas_call(
        _rope_table_kernel,
        out_shape=(out, out, out, out, out_t, out_t),
        name="rope_tables",
    )(pos.astype(F32).reshape(1, t), invf)


C_Q0, C_Q1 = 0, 384
C_KV0, C_KV1 = 384, 640
C_KP0, C_KP1 = 640, 768
C_GM0 = 768
GROUP = 512
D_IN_PACKED = C_GM0 + 5 * GROUP


def _rope_slab(blk, c1, c2):
    return blk * c1 + pltpu.roll(blk, LANES - QK_ROPE, 1) * c2


def _project_common(x_ref, gpre_ref, w1_ref, gq_ref, gkv_ref, ck1_ref, ck2_ref):
    x = x_ref[...]
    xn = _rms(x, gpre_ref[...]).astype(BF16)

    def proj(c0, c1):
        return jnp.dot(xn, w1_ref[:, c0:c1], preferred_element_type=F32)

    cqn = _rms(proj(C_Q0, C_Q1), gq_ref[...]).astype(BF16)
    ckv = _rms(proj(C_KV0, C_KV1), gkv_ref[...])
    kpe_blk = _rope_slab(proj(C_KP0, C_KP1), ck1_ref[...], ck2_ref[...])
    return proj, cqn, ckv, kpe_blk


def _conv_gate(proj, um1, um2, u, wconv_ref):
    w = wconv_ref[...]
    y = w[0:1, :] * um2 + w[1:2, :] * um1 + w[2:3, :] * u
    bg = proj(C_GM0 + GROUP, C_GM0 + 2 * GROUP)
    gc = proj(C_GM0 + 4 * GROUP, C_GM0 + 5 * GROUP)
    return bg * y * _silu(gc)


def _proj_prompt_kernel(x_ref, gpre_ref, w1_ref, gq_ref, wqt_ref, gkv_ref, wk_ref, wvt_ref, wconv_ref,
                        cq1t_ref, cq2t_ref, ck1_ref, ck2_ref,
                        qt_out, k_out, vt_out, ckv_out, kpet_out, gm_out, cv_out, conv_out,
                        carry_ref, *, tiles_per_seq):
    tm = x_ref.shape[0]
    step = pl.program_id(0)
    dn = (((1,), (1,)), ((), ()))

    @pl.when(step % tiles_per_seq == 0)
    def _():
        carry_ref[...] = jnp.zeros_like(carry_ref)

    xn = _rms(x_ref[...], gpre_ref[...]).astype(BF16)

    def proj(c0, c1):
        return jnp.dot(xn, w1_ref[:, c0:c1], preferred_element_type=F32)

    z_cq = proj(C_Q0, C_Q1)
    z_ckv = proj(C_KV0, C_KV1)
    cqn = _rms(z_cq, gq_ref[...]).astype(BF16)
    qt = lax.dot_general(wqt_ref[...], cqn, dn, preferred_element_type=F32)
    ckv = _rms(z_ckv, gkv_ref[...])
    ckv_out[...] = ckv
    ckvb = ckv.astype(BF16)
    z_kp = proj(C_KP0, C_KP1)
    kfull = jnp.dot(ckvb, wk_ref[...], preferred_element_type=F32)
    qk = QK_NOPE + QK_ROPE
    nope_scale = cq1t_ref[0:QK_NOPE, :]
    cos_t, sin_t = cq1t_ref[QK_NOPE:qk, :], cq2t_ref[QK_NOPE:qk, :]
    for h in range(N_HEADS):
        nope, pe = qt[h * qk:h * qk + QK_NOPE, :], qt[h * qk + QK_NOPE:(h + 1) * qk, :]
        qt_out[h * HEAD_SLAB:h * HEAD_SLAB + QK_NOPE, :] = (nope * nope_scale).astype(BF16)
        qt_out[h * HEAD_SLAB + QK_NOPE:h * HEAD_SLAB + qk, :] = (pe * cos_t + _rot_rows(pe) * sin_t).astype(BF16)
        qt_out[h * HEAD_SLAB + qk:(h + 1) * HEAD_SLAB, :] = jnp.zeros((HEAD_SLAB - qk, tm), BF16)
    vt = lax.dot_general(wvt_ref[...], ckvb, dn, preferred_element_type=F32)
    z_gm = proj(C_GM0, C_GM0 + GROUP)
    kpe_blk = _rope_slab(z_kp, ck1_ref[...], ck2_ref[...])
    kpet_out[0] = kpe_blk.T[QK_NOPE:QK_NOPE + QK_ROPE, :]
    for h in range(N_HEADS):
        k_out[:, h * HEAD_SLAB:(h + 1) * HEAD_SLAB] = (kfull[:, h * HEAD_SLAB:(h + 1) * HEAD_SLAB] + kpe_blk).astype(BF16)
    vt_out[...] = vt.astype(BF16)
    z_cg = proj(C_GM0 + 2 * GROUP, C_GM0 + 3 * GROUP)
    z_h = proj(C_GM0 + 3 * GROUP, C_GM0 + 4 * GROUP)
    gm_out[...] = _silu(z_gm).astype(BF16)
    z_bg = proj(C_GM0 + GROUP, C_GM0 + 2 * GROUP)
    z_gc = proj(C_GM0 + 4 * GROUP, C_GM0 + 5 * GROUP)

    u = z_cg * z_h
    prev = carry_ref[...]
    p1, p2 = prev[7:8, :], prev[6:7, :]
    row = lax.broadcasted_iota(jnp.int32, u.shape, 0)
    um1 = jnp.where(row == 0, p1, pltpu.roll(u, 1, 0))
    um2 = jnp.where(row == 0, p2, jnp.where(row == 1, p1, pltpu.roll(u, 2, 0)))
    w = wconv_ref[...]
    y = w[0:1, :] * um2 + w[1:2, :] * um1 + w[2:3, :] * u
    cv_out[...] = (z_bg * y * _silu(z_gc)).astype(BF16)
    carry_ref[...] = u[tm - 8:tm, :]
    conv_out[0] = u[tm - (CONV_W - 1):tm, :]


def _const_spec(shape):
    nd = len(shape)
    return pl.BlockSpec(shape, lambda *_: (0,) * nd, pipeline_mode=pl.Buffered(1))


def _proj_prompt(x2d, seq, gpre, w1, gq, wqt, gkv, wk, wvt, wconv, tabs, tm):
    t, d = x2d.shape
    nb = t // seq
    hs = N_HEADS * HEAD_SLAB
    hv = N_HEADS * V_HEAD
    cq1, cq2, ck1, ck2, cq1t, cq2t = tabs
    row = lambda w: pl.BlockSpec((tm, w), lambda i: (i, 0))
    col = lambda h: pl.BlockSpec((h, tm), lambda i: (0, i))
    tab = pl.BlockSpec((tm, LANES), lambda i: (i % (seq // tm), 0))
    tab_t = pl.BlockSpec((LANES, tm), lambda i: (0, i % (seq // tm)))
    out_shape = (
        jax.ShapeDtypeStruct((hs, t), BF16),
        jax.ShapeDtypeStruct((t, hs), BF16),
        jax.ShapeDtypeStruct((hv, t), BF16),
        jax.ShapeDtypeStruct((t, gkv.shape[1]), F32),
        jax.ShapeDtypeStruct((nb, QK_ROPE, seq), F32),
        jax.ShapeDtypeStruct((t, GROUP), BF16),
        jax.ShapeDtypeStruct((t, GROUP), BF16),
        jax.ShapeDtypeStruct((nb, CONV_W - 1, GROUP), F32),
    )
    return pl.pallas_call(
        functools.partial(_proj_prompt_kernel, tiles_per_seq=seq // tm),
        grid=(t // tm,),
        in_specs=[row(d), _const_spec(gpre.shape), _const_spec(w1.shape), _const_spec(gq.shape),
                  _const_spec(wqt.shape), _const_spec(gkv.shape), _const_spec(wk.shape), _const_spec(wvt.shape),
                  _const_spec(wconv.shape), tab_t, tab_t, tab, tab],
        out_specs=(col(hs), row(hs), col(hv), row(gkv.shape[1]),
                   pl.BlockSpec((1, QK_ROPE, tm), lambda i: (i // (seq // tm), 0, i % (seq // tm))),
                   row(GROUP), row(GROUP),
                   pl.BlockSpec((1, CONV_W - 1, GROUP), lambda i: (i // (seq // tm), 0, 0))),
        out_shape=out_shape,
        scratch_shapes=[pltpu.VMEM((8, GROUP), F32)],
        compiler_params=pltpu.CompilerParams(dimension_semantics=("arbitrary",),
                                             vmem_limit_bytes=VMEM_LIMIT_BYTES),
        name="proj_prompt",
    )(x2d, gpre, w1, gq, wqt, gkv, wk, wvt, wconv, cq1t, cq2t, ck1, ck2)


def _attn_kernel(qt_ref, k_ref, vt_ref, gm_ref, cv_ref, x_ref, wo_ref, gpost_ref, y_ref, s_ref,
                 *, tq, tk):
    i = pl.program_id(1)
    hq = tq // 2
    tri = (lax.broadcasted_iota(jnp.int32, (hq, hq), 0) <= lax.broadcasted_iota(jnp.int32, (hq, hq), 1))

    def slab(h):
        return slice(h * HEAD_SLAB, (h + 1) * HEAD_SLAB)

    def values(h, toks):
        ones = jnp.ones((ONES_ROWS, toks.size), BF16)
        return jnp.concatenate([vt_ref[h * V_HEAD:(h + 1) * V_HEAD, toks], ones], 0)

    def scores(h, toks):
        return jnp.dot(k_ref[toks, slab(h)], qt_ref[slab(h), :], preferred_element_type=F32)

    def update(h, toks, st, carry):
        m, acc = carry
        m_new = jnp.maximum(m, jnp.max(st, axis=0, keepdims=True))
        pt = jnp.exp2(st - m_new).astype(BF16)
        return m_new, jnp.exp2(m - m_new) * acc + jnp.dot(values(h, toks), pt, preferred_element_type=F32)

    base = pl.multiple_of(i * tq, tq)
    ka, kb = pl.ds(base, hq), pl.ds(base + hq, hq)

    def diagonal_scores(h):
        return (jnp.dot(k_ref[ka, slab(h)], qt_ref[slab(h), :], preferred_element_type=F32),
                jnp.dot(k_ref[kb, slab(h)], qt_ref[slab(h), hq:], preferred_element_type=F32))

    def diagonal_init(h, a, b_):
        a_lo = jnp.where(tri, a[:, :hq], -jnp.inf)
        a_hi = a[:, hq:]
        b_ = jnp.where(tri, b_, -jnp.inf)
        m_lo = jnp.max(a_lo, axis=0, keepdims=True)
        m_hi = jnp.maximum(jnp.max(a_hi, axis=0, keepdims=True), jnp.max(b_, axis=0, keepdims=True))
        pa = jnp.concatenate([jnp.exp2(a_lo - m_lo), jnp.exp2(a_hi - m_hi)], axis=1).astype(BF16)
        pb = jnp.exp2(b_ - m_hi).astype(BF16)
        acc_a = jnp.dot(values(h, ka), pa, preferred_element_type=F32)
        acc_b = jnp.dot(values(h, kb), pb, preferred_element_type=F32)
        return (jnp.concatenate([m_lo, m_hi], axis=1),
                jnp.concatenate([acc_a[:, :hq], acc_a[:, hq:] + acc_b], axis=1))

    n_full = i * (tq // tk)

    def tile_tokens(j):
        return pl.ds(pl.multiple_of(j * tk, tk), tk)

    carries = []
    cur = diagonal_scores(0)
    for h in range(N_HEADS):
        if h + 1 < N_HEADS:
            ahead = diagonal_scores(h + 1)
        else:
            s_ref[0] = scores(0, tile_tokens(0))
        carries.append(diagonal_init(h, *cur))
        cur = ahead
    carries = tuple(carries)

    def full_tile(j, carries):
        toks = tile_tokens(j)
        nxt = tile_tokens(jnp.minimum(j + 1, n_full - 1))
        out = []
        for h in range(N_HEADS):
            if h + 1 < N_HEADS:
                s_ref[h + 1] = scores(h + 1, toks)
                out.append(update(h, toks, s_ref[h], carries[h]))
            else:
                cur = s_ref[h]
                s_ref[0] = scores(0, nxt)
                out.append(update(h, toks, cur, carries[h]))
        return tuple(out)

    carries = lax.fori_loop(0, n_full, full_tile, carries)
    outs = [acc[:V_HEAD] * (1.0 / acc[V_HEAD:V_HEAD + 1]) for _, acc in carries]
    pairs = [jnp.concatenate(outs[2 * p:2 * p + 2], axis=0).T for p in range(N_HEADS // 2)]

    half = gm_ref.shape[1]
    mla = (jnp.concatenate(pairs, axis=1) * gm_ref[...].astype(F32)).astype(BF16)
    y = jnp.dot(mla, wo_ref[0:half, :], preferred_element_type=F32)
    y = y + jnp.dot(cv_ref[...], wo_ref[half:, :], preferred_element_type=F32)
    y_ref[...] = x_ref[...] + _rms(y, gpost_ref[...])


def _attention_out(qt, k, vt, gm, cv, x2d, wo, gpost, nb, seq, tq, tk):
    hs, t = qt.shape
    hv = vt.shape[0]
    d = x2d.shape[1]
    assert N_HEADS % 2 == 0 and tq % tk == 0
    row = lambda w: pl.BlockSpec((tq, w), lambda b, i: (b * (seq // tq) + i, 0))
    return pl.pallas_call(
        functools.partial(_attn_kernel, tq=tq, tk=tk),
        grid=(nb, seq // tq),
        in_specs=[pl.BlockSpec((hs, tq), lambda b, i: (0, b * (seq // tq) + i)),
                  pl.BlockSpec((seq, hs), lambda b, i: (b, 0)),
                  pl.BlockSpec((hv, seq), lambda b, i: (0, b)),
                  row(gm.shape[1]), row(cv.shape[1]), row(d), _const_spec(wo.shape), _const_spec(gpost.shape)],
        out_specs=row(d),
        out_shape=jax.ShapeDtypeStruct((t, d), F32),
        scratch_shapes=[pltpu.VMEM((N_HEADS, tk, tq), F32)],
        compiler_params=pltpu.CompilerParams(dimension_semantics=("arbitrary", "arbitrary"),
                                             vmem_limit_bytes=VMEM_LIMIT_BYTES),
        name="prompt_attn",
    )(qt, k, vt, gm, cv, x2d, wo, gpost)


def _proj_sample_kernel(x_ref, gpre_ref, w1_ref, gq_ref, wq_ref, gkv_ref, wukt_ref, wconv_ref,
                        cq1_ref, cq2_ref, ck1_ref, ck2_ref, s0_ref, s1_ref,
                        ql_out, qp_out, ckv_out, kpe_out, gm_out, cv_out, conv_out, *, dec_seq):
    proj, cqn, ckv, kpe_blk = _project_common(x_ref, gpre_ref, w1_ref, gq_ref, gkv_ref, ck1_ref, ck2_ref)
    qraw = jnp.dot(cqn, wq_ref[...], preferred_element_type=F32)
    cq1, cq2 = cq1_ref[...], cq2_ref[...]
    for h in range(N_HEADS):
        qh = _rope_slab(qraw[:, h * HEAD_SLAB:(h + 1) * HEAD_SLAB], cq1, cq2)
        ql_out[:, h, :] = jnp.dot(qh.astype(BF16), wukt_ref[h], preferred_element_type=F32)
        qp_out[:, h, :] = pltpu.roll(qh, LANES - QK_NOPE, 1)[:, 0:QK_ROPE]
    ckv_out[...] = ckv
    kpe_out[...] = kpe_blk[:, QK_NOPE:QK_NOPE + QK_ROPE]
    gm_out[...] = _silu(proj(C_GM0, C_GM0 + GROUP))

    u = proj(C_GM0 + 2 * GROUP, C_GM0 + 3 * GROUP) * proj(C_GM0 + 3 * GROUP, C_GM0 + 4 * GROUP)
    t_in_seq = lax.broadcasted_iota(jnp.int32, u.shape, 0) % dec_seq
    s0, s1 = s0_ref[...], s1_ref[...]
    um1 = jnp.where(t_in_seq == 0, s1, pltpu.roll(u, 1, 0))
    um2 = jnp.where(t_in_seq == 0, s0, jnp.where(t_in_seq == 1, s1, pltpu.roll(u, 2, 0)))
    cv_out[...] = _conv_gate(proj, um1, um2, u, wconv_ref)
    conv_out[...] = u


def _proj_sample(x2d, dec_seq, gpre, w1, gq, wq, gkv, wukt, wconv, tabs, s0, s1):
    t, d = x2d.shape
    c = gkv.shape[1]
    out_shape = (
        jax.ShapeDtypeStruct((t, N_HEADS, c), F32),
        jax.ShapeDtypeStruct((t, N_HEADS, QK_ROPE), F32),
        jax.ShapeDtypeStruct((t, c), F32),
        jax.ShapeDtypeStruct((t, QK_ROPE), F32),
        jax.ShapeDtypeStruct((t, GROUP), F32),
        jax.ShapeDtypeStruct((t, GROUP), F32),
        jax.ShapeDtypeStruct((t, GROUP), F32),
    )
    return pl.pallas_call(
        functools.partial(_proj_sample_kernel, dec_seq=dec_seq),
        out_shape=out_shape,
        compiler_params=pltpu.CompilerParams(vmem_limit_bytes=VMEM_LIMIT_BYTES),
        name="proj_sample",
    )(x2d, gpre, w1, gq, wq, gkv, wukt, wconv, *tabs, s0, s1)


def _decode_attn_kernel(pt_ref, ql_ref, qp_ref, cnew_ref, knew_ref, y_hbm, x_hbm, o_ref,
                        ybuf, xbuf, sem, m_ref, l_ref, acc_ref, *, pages, slots, layer, dec_seq):
    b = pl.program_id(0)
    n_b = pl.num_programs(0)
    n_chunks = pt_ref.shape[1] // pages
    page = ybuf.shape[1] // pages
    rows = ql_ref.shape[1]
    dn = (((1,), (1,)), ((), ()))
    ahead = slots - 1

    def chunk_copies(bb, ch, slot, page_ids):
        cps = []
        for i in range(pages):
            pid = page_ids(bb, ch * pages + i)
            tok = pl.ds(i * page, page)
            cps.append(pltpu.make_async_copy(y_hbm.at[layer, pid], ybuf.at[slot, tok, :], sem.at[0, slot]))
            cps.append(pltpu.make_async_copy(x_hbm.at[layer, pid], xbuf.at[slot, :, tok], sem.at[1, slot]))
        return cps

    def start(bb, ch, slot):
        for cp in chunk_copies(bb, ch, slot, lambda r, j: pt_ref[r, j]):
            cp.start()

    def wait(slot):
        for cp in chunk_copies(0, 0, slot, lambda r, j: 0):
            cp.wait()

    @pl.when(b == 0)
    def _():
        for k in range(ahead):
            start(0, k, k)

    qlf, qpf = ql_ref[0], qp_ref[0]
    ql, qp = qlf.astype(BF16), qpf.astype(BF16)
    tok_of_row = lax.broadcasted_iota(jnp.int32, (rows, 1), 0) // N_HEADS
    s_new = []
    for j in range(dec_seq):
        sj = (jnp.sum(qlf * cnew_ref[0, j:j + 1, :], axis=-1, keepdims=True)
              + jnp.sum(qpf * knew_ref[0, j:j + 1, :], axis=-1, keepdims=True))
        s_new.append(jnp.where(tok_of_row >= j, sj, -jnp.inf))
    m_n = functools.reduce(jnp.maximum, s_new)
    p_n = [jnp.exp(sj - m_n) for sj in s_new]
    m_ref[...] = m_n
    l_ref[...] = functools.reduce(lambda a, e: a + e, p_n)
    acc_ref[0] = functools.reduce(lambda a, e: a + e,
                                  [p_n[j] * cnew_ref[0, j:j + 1, :] for j in range(dec_seq)])
    acc_ref[1] = jnp.zeros(acc_ref.shape[1:], F32)

    half = pages * page // 2
    toks = [slice(h * half, (h + 1) * half) for h in range(2)]

    def scores(slot):
        yb = [ybuf[slot, t, :].astype(BF16) for t in toks]
        xb = [xbuf[slot, :, t].astype(BF16) for t in toks]
        s = jnp.concatenate(
            [lax.dot_general(ql, yb[h], dn, preferred_element_type=F32)
             + jnp.dot(qp, xb[h], preferred_element_type=F32) for h in range(2)], axis=1)
        return s, yb

    def softmax(s):
        m = m_ref[...]
        m_new = jnp.maximum(m, jnp.max(s, axis=-1, keepdims=True))
        alpha = jnp.exp(m - m_new)
        p = jnp.exp(s - m_new)
        l_ref[...] = alpha * l_ref[...] + jnp.sum(p, axis=-1, keepdims=True)
        m_ref[...] = m_new
        return alpha, p.astype(BF16)

    def values(alpha, p, yb):
        for h in range(2):
            acc_ref[h] = alpha * acc_ref[h] + jnp.dot(p[:, toks[h]], yb[h], preferred_element_type=F32)

    def trip(it, carry):
        scored = None
        weighted = None
        for k in range(slots):
            ch = it * slots + k
            nxt = ch + ahead
            nxt_slot = (k + ahead) % slots

            @pl.when(nxt < n_chunks)
            def _():
                start(b, nxt, nxt_slot)

            @pl.when((nxt >= n_chunks) & (b + 1 < n_b))
            def _():
                start(b + 1, nxt - n_chunks, nxt_slot)

            wait(k)
            fresh = scores(k)
            if weighted is not None:
                values(*weighted)
            if scored is not None:
                weighted = softmax(scored[0]) + (scored[1],)
            scored = fresh
        if weighted is not None:
            values(*weighted)
        values(*(softmax(scored[0]) + (scored[1],)))
        return carry

    lax.fori_loop(0, n_chunks // slots, trip, 0)
    o_ref[0] = (acc_ref[0] + acc_ref[1]) * (1.0 / l_ref[...])


def _decode_attention(page_table, ql, qp, cnew, knew, cache_ckv, cache_kpe, layer, pages, slots):
    nb, rows, c = ql.shape
    dec_seq = cnew.shape[1]
    page, r = cache_kpe.shape[2:]
    n_chunks = page_table.shape[1] // pages
    assert n_chunks * pages == page_table.shape[1] and n_chunks % slots == 0 and slots - 1 <= n_chunks
    kpe_t = jnp.swapaxes(cache_kpe, 2, 3)
    per_b = lambda a: pl.BlockSpec((1,) + a.shape[1:], lambda b, pt: (b, 0, 0))
    grid_spec = pltpu.PrefetchScalarGridSpec(
        num_scalar_prefetch=1,
        grid=(nb,),
        in_specs=[per_b(ql), per_b(qp), per_b(cnew), per_b(knew),
                  pl.BlockSpec(memory_space=pl.ANY), pl.BlockSpec(memory_space=pl.ANY)],
        out_specs=pl.BlockSpec((1, rows, c), lambda b, pt: (b, 0, 0)),
        scratch_shapes=[pltpu.VMEM((slots, pages * page, c), F32), pltpu.VMEM((slots, r, pages * page), F32),
                        pltpu.SemaphoreType.DMA((2, slots)),
                        pltpu.VMEM((rows, 1), F32), pltpu.VMEM((rows, 1), F32), pltpu.VMEM((2, rows, c), F32)],
    )
    return pl.pallas_call(
        functools.partial(_decode_attn_kernel, pages=pages, slots=slots, layer=layer, dec_seq=dec_seq),
        grid_spec=grid_spec,
        out_shape=jax.ShapeDtypeStruct((nb, rows, c), F32),
        compiler_params=pltpu.CompilerParams(dimension_semantics=("arbitrary",),
                                             vmem_limit_bytes=VMEM_LIMIT_BYTES),
        name="decode_attn",
    )(page_table, ql, qp, cnew, knew, cache_ckv, kpe_t)


def _out_sample_kernel(ol_ref, gm_ref, cv_ref, x_ref, wuvt_ref, wo_ref, gpost_ref, y_ref):
    half = gm_ref.shape[1]
    dn = (((1,), (1,)), ((), ()))
    o = jnp.concatenate(
        [lax.dot_general(ol_ref[:, h, :].astype(BF16), wuvt_ref[h], dn, preferred_element_type=F32)
         for h in range(N_HEADS)], axis=1)
    mla = (o * gm_ref[...]).astype(BF16)
    y = jnp.dot(mla, wo_ref[0:half, :], preferred_element_type=F32)
    y = y + jnp.dot(cv_ref[...].astype(BF16), wo_ref[half:, :], preferred_element_type=F32)
    y_ref[...] = x_ref[...] + _rms(y, gpost_ref[...])


def _out_sample(ol, gm, cv, x2d, wuv, wo, gpost):
    return pl.pallas_call(
        _out_sample_kernel,
        out_shape=jax.ShapeDtypeStruct(x2d.shape, F32),
        compiler_params=pltpu.CompilerParams(vmem_limit_bytes=VMEM_LIMIT_BYTES),
        name="out_sample",
    )(ol, gm, cv, x2d, wuv, wo, gpost)


def _rot_rows(w):
    r = w.shape[0] // 2
    return jnp.concatenate([-w[r:], w[:r]], axis=0)


def _pack_kernel(wint_ref, wuq_ref, wukv_ref, wout_ref,
                 w1_ref, wq_ref, wqt_ref, wk_ref, wvt_ref, wukt_ref, wuvt_ref, wo_ref):
    d = wint_ref.shape[1]
    blk = LANES

    def put_w1(col0, rows):
        w1_ref[:, col0:col0 + blk] = rows.T.astype(BF16)

    for r0 in range(0, C_KV1, blk):
        put_w1(r0, wint_ref[r0:r0 + blk, :])
    kpe = wint_ref[C_KV1:C_KV1 + QK_ROPE, :]
    put_w1(C_KP0, jnp.concatenate([jnp.zeros((QK_NOPE, d), F32), kpe, _rot_rows(kpe)], axis=0))
    src0 = C_KV1 + QK_ROPE
    for j in range(5 * GROUP // blk):
        put_w1(C_GM0 + j * blk, wint_ref[src0 + j * blk:src0 + (j + 1) * blk, :])

    uqt = wuq_ref[...].T
    ukvt = wukv_ref[...].T
    kv_lora = ukvt.shape[1]
    qk = QK_NOPE + QK_ROPE
    wqt_ref[...] = uqt.astype(BF16)
    for h in range(N_HEADS):
        pe = uqt[h * qk + QK_NOPE:(h + 1) * qk]
        q_slab = jnp.concatenate([uqt[h * qk:h * qk + QK_NOPE], pe, _rot_rows(pe)], axis=0)
        wq_ref[:, h * HEAD_SLAB:(h + 1) * HEAD_SLAB] = q_slab.T.astype(BF16)
        ukt = ukvt[h * (QK_NOPE + V_HEAD):h * (QK_NOPE + V_HEAD) + QK_NOPE]
        uvt = ukvt[h * (QK_NOPE + V_HEAD) + QK_NOPE:(h + 1) * (QK_NOPE + V_HEAD)]
        k_slab = jnp.concatenate([ukt, jnp.zeros((HEAD_SLAB - QK_NOPE, kv_lora), F32)], axis=0)
        wukt_ref[h] = k_slab.astype(BF16)
        wk_ref[:, h * HEAD_SLAB:(h + 1) * HEAD_SLAB] = k_slab.T.astype(BF16)
        wvt_ref[h * V_HEAD:(h + 1) * V_HEAD, :] = uvt.astype(BF16)
        wuvt_ref[h] = uvt.astype(BF16)
    wo_ref[...] = wout_ref[...].astype(BF16)


def _pack_weights(w_in, w_uq, w_ukv, w_out):
    d = w_in.shape[0]
    q_lora, kv_lora = w_uq.shape[0], w_ukv.shape[0]
    assert w_in.shape[1] == C_KV1 + QK_ROPE + 5 * GROUP
    sds = lambda *s: jax.ShapeDtypeStruct(s, BF16)
    return pl.pallas_call(
        _pack_kernel,
        out_shape=(sds(d, D_IN_PACKED), sds(q_lora, N_HEADS * HEAD_SLAB), sds(w_uq.shape[1], q_lora),
                   sds(kv_lora, N_HEADS * HEAD_SLAB), sds(N_HEADS * V_HEAD, kv_lora),
                   sds(N_HEADS, HEAD_SLAB, kv_lora), sds(N_HEADS, V_HEAD, kv_lora), sds(*w_out.shape)),
        compiler_params=pltpu.CompilerParams(vmem_limit_bytes=VMEM_LIMIT_BYTES),
        name="pack_weights",
    )(w_in.T, w_uq, w_ukv, w_out)


PROMPT_TILE = 512
ATTN_TQ = 512
ATTN_TK = 512
DECODE_PAGES = 32
DECODE_SLOTS = 4


def kernel(x_prompt, x_sample, cache_ckv, cache_kpe, state_conv, page_table, g_pre, w_in, g_qnorm, w_uq,
           g_kvnorm, w_ukv, w_conv, w_out, g_post):
    depth = w_in.shape[0]
    nb, seq, d = x_prompt.shape
    db, dec_seq, _ = x_sample.shape
    past_len = page_table.shape[1] * cache_ckv.shape[2]
    c = cache_ckv.shape[3]

    tabs_p = _rope_tables(jnp.arange(seq, dtype=jnp.int32))
    tabs_s = _rope_tables(jnp.tile(past_len + jnp.arange(dec_seq, dtype=jnp.int32), db))

    xp = x_prompt.reshape(nb * seq, d)
    xs = x_sample.reshape(db * dec_seq, d)
    outs = [[] for _ in range(6)]
    for l in range(depth):
        w1, wq, wqt, wk, wvt, wukt, wuvt, wo = _pack_weights(w_in[l], w_uq[l], w_ukv[l], w_out[l])
        gpre, gq, gkv, gpost = g_pre[l][None], g_qnorm[l][None], g_kvnorm[l][None], g_post[l][None]

        qt, k, vt, ckv_p, kpe_p, gm, cv, conv_p = _proj_prompt(
            xp, seq, gpre, w1, gq, wqt, gkv, wk, wvt, w_conv[l], tabs_p, PROMPT_TILE)
        xp = _attention_out(qt, k, vt, gm, cv, xp, wo, gpost, nb, seq, ATTN_TQ, ATTN_TK)
        outs[0].append(ckv_p.reshape(nb, seq, c))
        outs[1].append(jnp.swapaxes(kpe_p, 1, 2))
        outs[2].append(conv_p)

        st = state_conv[l].astype(F32)
        s0 = jnp.repeat(st[:, 0], dec_seq, axis=0)
        s1 = jnp.repeat(st[:, 1], dec_seq, axis=0)
        ql, qp, ckv_s, kpe_s, gm_s, cv_s, u_s = _proj_sample(
            xs, dec_seq, gpre, w1, gq, wq, gkv, wukt, w_conv[l], tabs_s[:4], s0, s1)
        rows = dec_seq * N_HEADS
        ol = _decode_attention(page_table, ql.reshape(db, rows, c), qp.reshape(db, rows, QK_ROPE),
                               ckv_s.reshape(db, dec_seq, c), kpe_s.reshape(db, dec_seq, QK_ROPE),
                               cache_ckv, cache_kpe, l, DECODE_PAGES, DECODE_SLOTS)
        xs = _out_sample(ol.reshape(db * dec_seq, N_HEADS, c), gm_s, cv_s, xs, wuvt, wo, gpost)
        outs[3].append(ckv_s.reshape(db, dec_seq, c))
        outs[4].append(kpe_s.reshape(db, dec_seq, QK_ROPE))
        u_pad = jnp.concatenate([st, u_s.reshape(db, dec_seq, GROUP)], axis=1)
        outs[5].append(u_pad[:, -(CONV_W - 1):])

    return (xp.reshape(nb, seq, d), xs.reshape(db, dec_seq, d), *[jnp.stack(o_) for o_ in outs])
```

```python
import functools

import jax
import jax.numpy as jnp
from jax import lax
from jax.experimental import pallas as pl
from jax.experimental.pallas import tpu as pltpu

N_HEADS = 8
QK_NOPE = 64
QK_ROPE = 32
V_HEAD = 64
CONV_W = 3
ROPE_THETA = 10000.0
EPS = 1e-6
ATTN_SCALE = (QK_NOPE + QK_ROPE) ** -0.5

LANES = 128
HEAD_SLAB = LANES
ONES_ROWS = 16
LOG2_E = 1.4426950408889634
VMEM_LIMIT_BYTES = 56 * 1024 * 1024

F32 = jnp.float32
BF16 = jnp.bfloat16


def _silu(x):
    return x * (1.0 / (1.0 + jnp.exp(-x)))


def _rms(x, g):
    return x * lax.rsqrt(jnp.mean(x * x, axis=-1, keepdims=True) + EPS) * g


def _rope_table_kernel(pos_ref, invf_ref, cq1_ref, cq2_ref, ck1_ref, ck2_ref, cq1t_ref, cq2t_ref):
    ang = invf_ref[...] * pos_ref[...]
    c, s = jnp.cos(ang), jnp.sin(ang)
    t = ang.shape[1]
    nope = jnp.ones((QK_NOPE, t), F32)
    zn = jnp.zeros((QK_NOPE, t), F32)
    zr = jnp.zeros((HEAD_SLAB - QK_NOPE - QK_ROPE, t), F32)

    def slab(first, mid):
        return jnp.concatenate([first, mid, zr], axis=0)

    cq1_ref[...] = slab(nope * ATTN_SCALE, c * ATTN_SCALE).T
    cq2_ref[...] = slab(zn, s * ATTN_SCALE).T
    ck1_ref[...] = slab(zn, c).T
    ck2_ref[...] = slab(zn, s).T
    cq1t_ref[...] = slab(nope * (ATTN_SCALE * LOG2_E), c * (ATTN_SCALE * LOG2_E))
    cq2t_ref[...] = slab(zn, s * (ATTN_SCALE * LOG2_E))


def _rope_tables(pos):
    t = pos.shape[0]
    r = QK_ROPE
    inv_freq = ROPE_THETA ** (-jnp.arange(0, r, 2, dtype=F32) / r)
    invf = jnp.concatenate([inv_freq, inv_freq]).reshape(r, 1)
    out = jax.ShapeDtypeStruct((t, LANES), F32)
    out_t = jax.ShapeDtypeStruct((LANES, t), F32)
    return pl.pallas_call(
        _rope_table_kernel,
        out_shape=(out, out, out, out, out_t, out_t),
        name="rope_tables",
    )(pos.astype(F32).reshape(1, t), invf)


C_Q0, C_Q1 = 0, 384
C_KV0, C_KV1 = 384, 640
C_KP0, C_KP1 = 640, 768
C_GM0 = 768
GROUP = 512
D_IN_PACKED = C_GM0 + 5 * GROUP


def _rope_slab(blk, c1, c2):
    return blk * c1 + pltpu.roll(blk, LANES - QK_ROPE, 1) * c2


def _project_common(x_ref, gpre_ref, w1_ref, gq_ref, gkv_ref, ck1_ref, ck2_ref):
    x = x_ref[...]
    xn = _rms(x, gpre_ref[...]).astype(BF16)

    def proj(c0, c1):
        return jnp.dot(xn, w1_ref[:, c0:c1], preferred_element_type=F32)

    cqn = _rms(proj(C_Q0, C_Q1), gq_ref[...]).astype(BF16)
    ckv = _rms(proj(C_KV0, C_KV1), gkv_ref[...])
    kpe_blk = _rope_slab(proj(C_KP0, C_KP1), ck1_ref[...], ck2_ref[...])
    return proj, cqn, ckv, kpe_blk


def _conv_gate(proj, um1, um2, u, wconv_ref):
    w = wconv_ref[...]
    y = w[0:1, :] * um2 + w[1:2, :] * um1 + w[2:3, :] * u
    bg = proj(C_GM0 + GROUP, C_GM0 + 2 * GROUP)
    gc = proj(C_GM0 + 4 * GROUP, C_GM0 + 5 * GROUP)
    return bg * y * _silu(gc)


def _proj_prompt_kernel(x_ref, gpre_ref, w1_ref, gq_ref, wqt_ref, gkv_ref, wk_ref, wvt_ref, wconv_ref,
                        cq1t_ref, cq2t_ref, ck1_ref, ck2_ref,
                        qt_out, k_out, vt_out, ckv_out, kpet_out, gm_out, cv_out, conv_out,
                        carry_ref, *, tiles_per_seq):
    tm = x_ref.shape[0]
    step = pl.program_id(0)
    dn = (((1,), (1,)), ((), ()))

    @pl.when(step % tiles_per_seq == 0)
    def _():
        carry_ref[...] = jnp.zeros_like(carry_ref)

    xn = _rms(x_ref[...], gpre_ref[...]).astype(BF16)

    def proj(c0, c1):
        return jnp.dot(xn, w1_ref[:, c0:c1], preferred_element_type=F32)

    z_cq = proj(C_Q0, C_Q1)
    z_ckv = proj(C_KV0, C_KV1)
    cqn = _rms(z_cq, gq_ref[...]).astype(BF16)
    qt = lax.dot_general(wqt_ref[...], cqn, dn, preferred_element_type=F32)
    ckv = _rms(z_ckv, gkv_ref[...])
    ckv_out[...] = ckv
    ckvb = ckv.astype(BF16)
    z_kp = proj(C_KP0, C_KP1)
    kfull = jnp.dot(ckvb, wk_ref[...], preferred_element_type=F32)
    qk = QK_NOPE + QK_ROPE
    nope_scale = cq1t_ref[0:QK_NOPE, :]
    cos_t, sin_t = cq1t_ref[QK_NOPE:qk, :], cq2t_ref[QK_NOPE:qk, :]
    for h in range(N_HEADS):
        nope, pe = qt[h * qk:h * qk + QK_NOPE, :], qt[h * qk + QK_NOPE:(h + 1) * qk, :]
        qt_out[h * HEAD_SLAB:h * HEAD_SLAB + QK_NOPE, :] = (nope * nope_scale).astype(BF16)
        qt_out[h * HEAD_SLAB + QK_NOPE:h * HEAD_SLAB + qk, :] = (pe * cos_t + _rot_rows(pe) * sin_t).astype(BF16)
        qt_out[h * HEAD_SLAB + qk:(h + 1) * HEAD_SLAB, :] = jnp.zeros((HEAD_SLAB - qk, tm), BF16)
    vt = lax.dot_general(wvt_ref[...], ckvb, dn, preferred_element_type=F32)
    z_gm = proj(C_GM0, C_GM0 + GROUP)
    kpe_blk = _rope_slab(z_kp, ck1_ref[...], ck2_ref[...])
    kpet_out[0] = kpe_blk.T[QK_NOPE:QK_NOPE + QK_ROPE, :]
    for h in range(N_HEADS):
        k_out[:, h * HEAD_SLAB:(h + 1) * HEAD_SLAB] = (kfull[:, h * HEAD_SLAB:(h + 1) * HEAD_SLAB] + kpe_blk).astype(BF16)
    vt_out[...] = vt.astype(BF16)
    z_cg = proj(C_GM0 + 2 * GROUP, C_GM0 + 3 * GROUP)
    z_h = proj(C_GM0 + 3 * GROUP, C_GM0 + 4 * GROUP)
    gm_out[...] = _silu(z_gm).astype(BF16)
    z_bg = proj(C_GM0 + GROUP, C_GM0 + 2 * GROUP)
    z_gc = proj(C_GM0 + 4 * GROUP, C_GM0 + 5 * GROUP)

    u = z_cg * z_h
    prev = carry_ref[...]
    p1, p2 = prev[7:8, :], prev[6:7, :]
    row = lax.broadcasted_iota(jnp.int32, u.shape, 0)
    um1 = jnp.where(row == 0, p1, pltpu.roll(u, 1, 0))
    um2 = jnp.where(row == 0, p2, jnp.where(row == 1, p1, pltpu.roll(u, 2, 0)))
    w = wconv_ref[...]
    y = w[0:1, :] * um2 + w[1:2, :] * um1 + w[2:3, :] * u
    cv_out[...] = (z_bg * y * _silu(z_gc)).astype(BF16)
    carry_ref[...] = u[tm - 8:tm, :]
    conv_out[0] = u[tm - (CONV_W - 1):tm, :]


def _const_spec(shape):
    nd = len(shape)
    return pl.BlockSpec(shape, lambda *_: (0,) * nd, pipeline_mode=pl.Buffered(1))


def _proj_prompt(x2d, seq, gpre, w1, gq, wqt, gkv, wk, wvt, wconv, tabs, tm):
    t, d = x2d.shape
    nb = t // seq
    hs = N_HEADS * HEAD_SLAB
    hv = N_HEADS * V_HEAD
    cq1, cq2, ck1, ck2, cq1t, cq2t = tabs
    row = lambda w: pl.BlockSpec((tm, w), lambda i: (i, 0))
    col = lambda h: pl.BlockSpec((h, tm), lambda i: (0, i))
    tab = pl.BlockSpec((tm, LANES), lambda i: (i % (seq // tm), 0))
    tab_t = pl.BlockSpec((LANES, tm), lambda i: (0, i % (seq // tm)))
    out_shape = (
        jax.ShapeDtypeStruct((hs, t), BF16),
        jax.ShapeDtypeStruct((t, hs), BF16),
        jax.ShapeDtypeStruct((hv, t), BF16),
        jax.ShapeDtypeStruct((t, gkv.shape[1]), F32),
        jax.ShapeDtypeStruct((nb, QK_ROPE, seq), F32),
        jax.ShapeDtypeStruct((t, GROUP), BF16),
        jax.ShapeDtypeStruct((t, GROUP), BF16),
        jax.ShapeDtypeStruct((nb, CONV_W - 1, GROUP), F32),
    )
    return pl.pallas_call(
        functools.partial(_proj_prompt_kernel, tiles_per_seq=seq // tm),
        grid=(t // tm,),
        in_specs=[row(d), _const_spec(gpre.shape), _const_spec(w1.shape), _const_spec(gq.shape),
                  _const_spec(wqt.shape), _const_spec(gkv.shape), _const_spec(wk.shape), _const_spec(wvt.shape),
                  _const_spec(wconv.shape), tab_t, tab_t, tab, tab],
        out_specs=(col(hs), row(hs), col(hv), row(gkv.shape[1]),
                   pl.BlockSpec((1, QK_ROPE, tm), lambda i: (i // (seq // tm), 0, i % (seq // tm))),
                   row(GROUP), row(GROUP),
                   pl.BlockSpec((1, CONV_W - 1, GROUP), lambda i: (i // (seq // tm), 0, 0))),
        out_shape=out_shape,
        scratch_shapes=[pltpu.VMEM((8, GROUP), F32)],
        compiler_params=pltpu.CompilerParams(dimension_semantics=("arbitrary",),
                                             vmem_limit_bytes=VMEM_LIMIT_BYTES),
        name="proj_prompt",
    )(x2d, gpre, w1, gq, wqt, gkv, wk, wvt, wconv, cq1t, cq2t, ck1, ck2)


def _attn_kernel(qt_ref, k_ref, vt_ref, gm_ref, cv_ref, x_ref, wo_ref, gpost_ref, y_ref, s_ref,
                 *, tq, tk):
    i = pl.program_id(1)
    hq = tq // 2
    tri = (lax.broadcasted_iota(jnp.int32, (hq, hq), 0) <= lax.broadcasted_iota(jnp.int32, (hq, hq), 1))

    def slab(h):
        return slice(h * HEAD_SLAB, (h + 1) * HEAD_SLAB)

    def values(h, toks):
        ones = jnp.ones((ONES_ROWS, toks.size), BF16)
        return jnp.concatenate([vt_ref[h * V_HEAD:(h + 1) * V_HEAD, toks], ones], 0)

    def scores(h, toks):
        return jnp.dot(k_ref[toks, slab(h)], qt_ref[slab(h), :], preferred_element_type=F32)

    def update(h, toks, st, carry):
        m, acc = carry
        m_new = jnp.maximum(m, jnp.max(st, axis=0, keepdims=True))
        pt = jnp.exp2(st - m_new).astype(BF16)
        return m_new, jnp.exp2(m - m_new) * acc + jnp.dot(values(h, toks), pt, preferred_element_type=F32)

    base = pl.multiple_of(i * tq, tq)
    ka, kb = pl.ds(base, hq), pl.ds(base + hq, hq)

    def diagonal_scores(h):
        return (jnp.dot(k_ref[ka, slab(h)], qt_ref[slab(h), :], preferred_element_type=F32),
                jnp.dot(k_ref[kb, slab(h)], qt_ref[slab(h), hq:], preferred_element_type=F32))

    def diagonal_init(h, a, b_):
        a_lo = jnp.where(tri, a[:, :hq], -jnp.inf)
        a_hi = a[:, hq:]
        b_ = jnp.where(tri, b_, -jnp.inf)
        m_lo = jnp.max(a_lo, axis=0, keepdims=True)
        m_hi = jnp.maximum(jnp.max(a_hi, axis=0, keepdims=True), jnp.max(b_, axis=0, keepdims=True))
        pa = jnp.concatenate([jnp.exp2(a_lo - m_lo), jnp.exp2(a_hi - m_hi)], axis=1).astype(BF16)
        pb = jnp.exp2(b_ - m_hi).astype(BF16)
        acc_a = jnp.dot(values(h, ka), pa, preferred_element_type=F32)
        acc_b = jnp.dot(values(h, kb), pb, preferred_element_type=F32)
        return (jnp.concatenate([m_lo, m_hi], axis=1),
                jnp.concatenate([acc_a[:, :hq], acc_a[:, hq:] + acc_b], axis=1))

    n_full = i * (tq // tk)

    def tile_tokens(j):
        return pl.ds(pl.multiple_of(j * tk, tk), tk)

    carries = []
    cur = diagonal_scores(0)
    for h in range(N_HEADS):
        if h + 1 < N_HEADS:
            ahead = diagonal_scores(h + 1)
        else:
            s_ref[0] = scores(0, tile_tokens(0))
        carries.append(diagonal_init(h, *cur))
        cur = ahead
    carries = tuple(carries)

    def full_tile(j, carries):
        toks = tile_tokens(j)
        nxt = tile_tokens(jnp.minimum(j + 1, n_full - 1))
        out = []
        for h in range(N_HEADS):
            if h + 1 < N_HEADS:
                s_ref[h + 1] = scores(h + 1, toks)
                out.append(update(h, toks, s_ref[h], carries[h]))
            else:
                cur = s_ref[h]
                s_ref[0] = scores(0, nxt)
                out.append(update(h, toks, cur, carries[h]))
        return tuple(out)

    carries = lax.fori_loop(0, n_full, full_tile, carries)
    outs = [acc[:V_HEAD] * (1.0 / acc[V_HEAD:V_HEAD + 1]) for _, acc in carries]
    pairs = [jnp.concatenate(outs[2 * p:2 * p + 2], axis=0).T for p in range(N_HEADS // 2)]

    half = gm_ref.shape[1]
    mla = (jnp.concatenate(pairs, axis=1) * gm_ref[...].astype(F32)).astype(BF16)
    y = jnp.dot(mla, wo_ref[0:half, :], preferred_element_type=F32)
    y = y + jnp.dot(cv_ref[...], wo_ref[half:, :], preferred_element_type=F32)
    y_ref[...] = x_ref[...] + _rms(y, gpost_ref[...])


def _attention_out(qt, k, vt, gm, cv, x2d, wo, gpost, nb, seq, tq, tk):
    hs, t = qt.shape
    hv = vt.shape[0]
    d = x2d.shape[1]
    assert N_HEADS % 2 == 0 and tq % tk == 0
    row = lambda w: pl.BlockSpec((tq, w), lambda b, i: (b * (seq // tq) + i, 0))
    return pl.pallas_call(
        functools.partial(_attn_kernel, tq=tq, tk=tk),
        grid=(nb, seq // tq),
        in_specs=[pl.BlockSpec((hs, tq), lambda b, i: (0, b * (seq // tq) + i)),
                  pl.BlockSpec((seq, hs), lambda b, i: (b, 0)),
                  pl.BlockSpec((hv, seq), lambda b, i: (0, b)),
                  row(gm.shape[1]), row(cv.shape[1]), row(d), _const_spec(wo.shape), _const_spec(gpost.shape)],
        out_specs=row(d),
        out_shape=jax.ShapeDtypeStruct((t, d), F32),
        scratch_shapes=[pltpu.VMEM((N_HEADS, tk, tq), F32)],
        compiler_params=pltpu.CompilerParams(dimension_semantics=("arbitrary", "arbitrary"),
                                             vmem_limit_bytes=VMEM_LIMIT_BYTES),
        name="prompt_attn",
    )(qt, k, vt, gm, cv, x2d, wo, gpost)


def _proj_sample_kernel(x_ref, gpre_ref, w1_ref, gq_ref, wq_ref, gkv_ref, wukt_ref, wconv_ref,
                        cq1_ref, cq2_ref, ck1_ref, ck2_ref, s0_ref, s1_ref,
                        ql_out, qp_out, ckv_out, kpe_out, gm_out, cv_out, conv_out, *, dec_seq):
    proj, cqn, ckv, kpe_blk = _project_common(x_ref, gpre_ref, w1_ref, gq_ref, gkv_ref, ck1_ref, ck2_ref)
    qraw = jnp.dot(cqn, wq_ref[...], preferred_element_type=F32)
    cq1, cq2 = cq1_ref[...], cq2_ref[...]
    for h in range(N_HEADS):
        qh = _rope_slab(qraw[:, h * HEAD_SLAB:(h + 1) * HEAD_SLAB], cq1, cq2)
        ql_out[:, h, :] = jnp.dot(qh.astype(BF16), wukt_ref[h], preferred_element_type=F32)
        qp_out[:, h, :] = pltpu.roll(qh, LANES - QK_NOPE, 1)[:, 0:QK_ROPE]
    ckv_out[...] = ckv
    kpe_out[...] = kpe_blk[:, QK_NOPE:QK_NOPE + QK_ROPE]
    gm_out[...] = _silu(proj(C_GM0, C_GM0 + GROUP))

    u = proj(C_GM0 + 2 * GROUP, C_GM0 + 3 * GROUP) * proj(C_GM0 + 3 * GROUP, C_GM0 + 4 * GROUP)
    t_in_seq = lax.broadcasted_iota(jnp.int32, u.shape, 0) % dec_seq
    s0, s1 = s0_ref[...], s1_ref[...]
    um1 = jnp.where(t_in_seq == 0, s1, pltpu.roll(u, 1, 0))
    um2 = jnp.where(t_in_seq == 0, s0, jnp.where(t_in_seq == 1, s1, pltpu.roll(u, 2, 0)))
    cv_out[...] = _conv_gate(proj, um1, um2, u, wconv_ref)
    conv_out[...] = u


def _proj_sample(x2d, dec_seq, gpre, w1, gq, wq, gkv, wukt, wconv, tabs, s0, s1):
    t, d = x2d.shape
    c = gkv.shape[1]
    out_shape = (
        jax.ShapeDtypeStruct((t, N_HEADS, c), F32),
        jax.ShapeDtypeStruct((t, N_HEADS, QK_ROPE), F32),
        jax.ShapeDtypeStruct((t, c), F32),
        jax.ShapeDtypeStruct((t, QK_ROPE), F32),
        jax.ShapeDtypeStruct((t, GROUP), F32),
        jax.ShapeDtypeStruct((t, GROUP), F32),
        jax.ShapeDtypeStruct((t, GROUP), F32),
    )
    return pl.pallas_call(
        functools.partial(_proj_sample_kernel, dec_seq=dec_seq),
        out_shape=out_shape,
        compiler_params=pltpu.CompilerParams(vmem_limit_bytes=VMEM_LIMIT_BYTES),
        name="proj_sample",
    )(x2d, gpre, w1, gq, wq, gkv, wukt, wconv, *tabs, s0, s1)


def _decode_attn_kernel(pt_ref, ql_ref, qp_ref, cnew_ref, knew_ref, y_hbm, x_hbm, o_ref,
                        ybuf, xbuf, sem, m_ref, l_ref, acc_ref, *, pages, slots, layer, dec_seq):
    g = pl.program_id(0)
    n_g = pl.num_programs(0)
    per_step = ql_ref.shape[0]
    n_chunks = pt_ref.shape[1] // pages
    total = per_step * n_chunks
    page = ybuf.shape[1] // pages
    rows = ql_ref.shape[1]
    dn = (((1,), (1,)), ((), ()))
    ahead = slots - 1

    def chunk_copies(bb, ch, slot, page_ids):
        cps = []
        for i in range(pages):
            pid = page_ids(bb, ch * pages + i)
            tok = pl.ds(i * page, page)
            cps.append(pltpu.make_async_copy(y_hbm.at[layer, pid], ybuf.at[slot, tok, :], sem.at[0, slot]))
            cps.append(pltpu.make_async_copy(x_hbm.at[layer, pid], xbuf.at[slot, :, tok], sem.at[1, slot]))
        return cps

    def start(bb, ch, slot):
        for cp in chunk_copies(bb, ch, slot, lambda r, j: pt_ref[r, j]):
            cp.start()

    def wait(slot):
        for cp in chunk_copies(0, 0, slot, lambda r, j: 0):
            cp.wait()

    def start_nth(step, k, slot):
        start(step * per_step + k // n_chunks, k % n_chunks, slot)

    @pl.when(g == 0)
    def _():
        for k in range(ahead):
            start_nth(0, k, k % slots)

    tok_of_row = lax.broadcasted_iota(jnp.int32, (rows, 1), 0) // N_HEADS
    for e in range(per_step):
        qlf, qpf = ql_ref[e], qp_ref[e]
        s_new = []
        for j in range(dec_seq):
            sj = (jnp.sum(qlf * cnew_ref[e, j:j + 1, :], axis=-1, keepdims=True)
                  + jnp.sum(qpf * knew_ref[e, j:j + 1, :], axis=-1, keepdims=True))
            s_new.append(jnp.where(tok_of_row >= j, sj, -jnp.inf))
        m_n = functools.reduce(jnp.maximum, s_new)
        p_n = [jnp.exp(sj - m_n) for sj in s_new]
        m_ref[e] = m_n
        l_ref[e] = functools.reduce(lambda a, v: a + v, p_n)
        acc_ref[e, 0] = functools.reduce(lambda a, v: a + v,
                                         [p_n[j] * cnew_ref[e, j:j + 1, :] for j in range(dec_seq)])
        acc_ref[e, 1] = jnp.zeros(acc_ref.shape[2:], F32)

    half = pages * page // 2
    toks = [slice(h * half, (h + 1) * half) for h in range(2)]

    def scores(e, slot):
        ql, qp = ql_ref[e].astype(BF16), qp_ref[e].astype(BF16)
        yb = [ybuf[slot, t, :].astype(BF16) for t in toks]
        xb = [xbuf[slot, :, t].astype(BF16) for t in toks]
        s = jnp.concatenate(
            [lax.dot_general(ql, yb[h], dn, preferred_element_type=F32)
             + jnp.dot(qp, xb[h], preferred_element_type=F32) for h in range(2)], axis=1)
        return e, s, yb

    def softmax(e, s, yb):
        m = m_ref[e]
        m_new = jnp.maximum(m, jnp.max(s, axis=-1, keepdims=True))
        alpha = jnp.exp(m - m_new)
        p = jnp.exp(s - m_new)
        l_ref[e] = alpha * l_ref[e] + jnp.sum(p, axis=-1, keepdims=True)
        m_ref[e] = m_new
        return e, alpha, p.astype(BF16), yb

    def values(e, alpha, p, yb):
        for h in range(2):
            acc_ref[e, h] = alpha * acc_ref[e, h] + jnp.dot(p[:, toks[h]], yb[h], preferred_element_type=F32)

    scored = None
    weighted = None
    for k in range(total):
        nxt = k + ahead
        if nxt < total:
            start_nth(g, nxt, nxt % slots)
        else:
            @pl.when(g + 1 < n_g)
            def _():
                start_nth(g + 1, nxt - total, nxt % slots)

        wait(k % slots)
        fresh = scores(k // n_chunks, k % slots)
        if weighted is not None:
            values(*weighted)
        if scored is not None:
            weighted = softmax(*scored)
        scored = fresh
    if weighted is not None:
        values(*weighted)
    values(*softmax(*scored))
    for e in range(per_step):
        o_ref[e] = (acc_ref[e, 0] + acc_ref[e, 1]) * (1.0 / l_ref[e])


def _decode_attention(page_table, ql, qp, cnew, knew, cache_ckv, cache_kpe, layer, pages, slots, per_step):
    nb, rows, c = ql.shape
    dec_seq = cnew.shape[1]
    page, r = cache_kpe.shape[2:]
    n_chunks = page_table.shape[1] // pages
    assert n_chunks * pages == page_table.shape[1] and nb % per_step == 0
    assert (per_step * n_chunks) % slots == 0 and slots - 1 <= per_step * n_chunks
    kpe_t = jnp.swapaxes(cache_kpe, 2, 3)
    per_b = lambda a: pl.BlockSpec((per_step,) + a.shape[1:], lambda g, pt: (g, 0, 0))
    grid_spec = pltpu.PrefetchScalarGridSpec(
        num_scalar_prefetch=1,
        grid=(nb // per_step,),
        in_specs=[per_b(ql), per_b(qp), per_b(cnew), per_b(knew),
                  pl.BlockSpec(memory_space=pl.ANY), pl.BlockSpec(memory_space=pl.ANY)],
        out_specs=pl.BlockSpec((per_step, rows, c), lambda g, pt: (g, 0, 0)),
        scratch_shapes=[pltpu.VMEM((slots, pages * page, c), F32), pltpu.VMEM((slots, r, pages * page), F32),
                        pltpu.SemaphoreType.DMA((2, slots)),
                        pltpu.VMEM((per_step, rows, 1), F32), pltpu.VMEM((per_step, rows, 1), F32),
                        pltpu.VMEM((per_step, 2, rows, c), F32)],
    )
    return pl.pallas_call(
        functools.partial(_decode_attn_kernel, pages=pages, slots=slots, layer=layer, dec_seq=dec_seq),
        grid_spec=grid_spec,
        out_shape=jax.ShapeDtypeStruct((nb, rows, c), F32),
        compiler_params=pltpu.CompilerParams(dimension_semantics=("arbitrary",),
                                             vmem_limit_bytes=VMEM_LIMIT_BYTES),
        name="decode_attn",
    )(page_table, ql, qp, cnew, knew, cache_ckv, kpe_t)


def _out_sample_kernel(ol_ref, gm_ref, cv_ref, x_ref, wuvt_ref, wo_ref, gpost_ref, y_ref):
    half = gm_ref.shape[1]
    dn = (((1,), (1,)), ((), ()))
    o = jnp.concatenate(
        [lax.dot_general(ol_ref[:, h, :].astype(BF16), wuvt_ref[h], dn, preferred_element_type=F32)
         for h in range(N_HEADS)], axis=1)
    mla = (o * gm_ref[...]).astype(BF16)
    y = jnp.dot(mla, wo_ref[0:half, :], preferred_element_type=F32)
    y = y + jnp.dot(cv_ref[...].astype(BF16), wo_ref[half:, :], preferred_element_type=F32)
    y_ref[...] = x_ref[...] + _rms(y, gpost_ref[...])


def _out_sample(ol, gm, cv, x2d, wuv, wo, gpost):
    return pl.pallas_call(
        _out_sample_kernel,
        out_shape=jax.ShapeDtypeStruct(x2d.shape, F32),
        compiler_params=pltpu.CompilerParams(vmem_limit_bytes=VMEM_LIMIT_BYTES),
        name="out_sample",
    )(ol, gm, cv, x2d, wuv, wo, gpost)


def _rot_rows(w):
    r = w.shape[0] // 2
    return jnp.concatenate([-w[r:], w[:r]], axis=0)


def _pack_kernel(wint_ref, wuq_ref, wukv_ref, wout_ref,
                 w1_ref, wq_ref, wqt_ref, wk_ref, wvt_ref, wukt_ref, wuvt_ref, wo_ref):
    d = wint_ref.shape[1]
    blk = LANES

    def put_w1(col0, rows):
        w1_ref[:, col0:col0 + blk] = rows.T.astype(BF16)

    for r0 in range(0, C_KV1, blk):
        put_w1(r0, wint_ref[r0:r0 + blk, :])
    kpe = wint_ref[C_KV1:C_KV1 + QK_ROPE, :]
    put_w1(C_KP0, jnp.concatenate([jnp.zeros((QK_NOPE, d), F32), kpe, _rot_rows(kpe)], axis=0))
    src0 = C_KV1 + QK_ROPE
    for j in range(5 * GROUP // blk):
        put_w1(C_GM0 + j * blk, wint_ref[src0 + j * blk:src0 + (j + 1) * blk, :])

    uqt = wuq_ref[...].T
    ukvt = wukv_ref[...].T
    kv_lora = ukvt.shape[1]
    qk = QK_NOPE + QK_ROPE
    wqt_ref[...] = uqt.astype(BF16)
    for h in range(N_HEADS):
        pe = uqt[h * qk + QK_NOPE:(h + 1) * qk]
        q_slab = jnp.concatenate([uqt[h * qk:h * qk + QK_NOPE], pe, _rot_rows(pe)], axis=0)
        wq_ref[:, h * HEAD_SLAB:(h + 1) * HEAD_SLAB] = q_slab.T.astype(BF16)
        ukt = ukvt[h * (QK_NOPE + V_HEAD):h * (QK_NOPE + V_HEAD) + QK_NOPE]
        uvt = ukvt[h * (QK_NOPE + V_HEAD) + QK_NOPE:(h + 1) * (QK_NOPE + V_HEAD)]
        k_slab = jnp.concatenate([ukt, jnp.zeros((HEAD_SLAB - QK_NOPE, kv_lora), F32)], axis=0)
        wukt_ref[h] = k_slab.astype(BF16)
        wk_ref[:, h * HEAD_SLAB:(h + 1) * HEAD_SLAB] = k_slab.T.astype(BF16)
        wvt_ref[h * V_HEAD:(h + 1) * V_HEAD, :] = uvt.astype(BF16)
        wuvt_ref[h] = uvt.astype(BF16)
    wo_ref[...] = wout_ref[...].astype(BF16)


def _pack_weights(w_in, w_uq, w_ukv, w_out):
    d = w_in.shape[0]
    q_lora, kv_lora = w_uq.shape[0], w_ukv.shape[0]
    assert w_in.shape[1] == C_KV1 + QK_ROPE + 5 * GROUP
    sds = lambda *s: jax.ShapeDtypeStruct(s, BF16)
    return pl.pallas_call(
        _pack_kernel,
        out_shape=(sds(d, D_IN_PACKED), sds(q_lora, N_HEADS * HEAD_SLAB), sds(w_uq.shape[1], q_lora),
                   sds(kv_lora, N_HEADS * HEAD_SLAB), sds(N_HEADS * V_HEAD, kv_lora),
                   sds(N_HEADS, HEAD_SLAB, kv_lora), sds(N_HEADS, V_HEAD, kv_lora), sds(*w_out.shape)),
        compiler_params=pltpu.CompilerParams(vmem_limit_bytes=VMEM_LIMIT_BYTES),
        name="pack_weights",
    )(w_in.T, w_uq, w_ukv, w_out)


PROMPT_TILE = 512
ATTN_TQ = 512
ATTN_TK = 512
DECODE_PAGES = 32
DECODE_SLOTS = 4
DECODE_PER_STEP = 1


def kernel(x_prompt, x_sample, cache_ckv, cache_kpe, state_conv, page_table, g_pre, w_in, g_qnorm, w_uq,
           g_kvnorm, w_ukv, w_conv, w_out, g_post):
    depth = w_in.shape[0]
    nb, seq, d = x_prompt.shape
    db, dec_seq, _ = x_sample.shape
    past_len = page_table.shape[1] * cache_ckv.shape[2]
    c = cache_ckv.shape[3]

    tabs_p = _rope_tables(jnp.arange(seq, dtype=jnp.int32))
    tabs_s = _rope_tables(jnp.tile(past_len + jnp.arange(dec_seq, dtype=jnp.int32), db))

    xp = x_prompt.reshape(nb * seq, d)
    xs = x_sample.reshape(db * dec_seq, d)
    outs = [[] for _ in range(6)]
    for l in range(depth):
        w1, wq, wqt, wk, wvt, wukt, wuvt, wo = _pack_weights(w_in[l], w_uq[l], w_ukv[l], w_out[l])
        gpre, gq, gkv, gpost = g_pre[l][None], g_qnorm[l][None], g_kvnorm[l][None], g_post[l][None]

        qt, k, vt, ckv_p, kpe_p, gm, cv, conv_p = _proj_prompt(
            xp, seq, gpre, w1, gq, wqt, gkv, wk, wvt, w_conv[l], tabs_p, PROMPT_TILE)
        xp = _attention_out(qt, k, vt, gm, cv, xp, wo, gpost, nb, seq, ATTN_TQ, ATTN_TK)
        outs[0].append(ckv_p.reshape(nb, seq, c))
        outs[1].append(jnp.swapaxes(kpe_p, 1, 2))
        outs[2].append(conv_p)

        st = state_conv[l].astype(F32)
        s0 = jnp.repeat(st[:, 0], dec_seq, axis=0)
        s1 = jnp.repeat(st[:, 1], dec_seq, axis=0)
        ql, qp, ckv_s, kpe_s, gm_s, cv_s, u_s = _proj_sample(
            xs, dec_seq, gpre, w1, gq, wq, gkv, wukt, w_conv[l], tabs_s[:4], s0, s1)
        rows = dec_seq * N_HEADS
        ol = _decode_attention(page_table, ql.reshape(db, rows, c), qp.reshape(db, rows, QK_ROPE),
                               ckv_s.reshape(db, dec_seq, c), kpe_s.reshape(db, dec_seq, QK_ROPE),
                               cache_ckv, cache_kpe, l, DECODE_PAGES, DECODE_SLOTS, DECODE_PER_STEP)
        xs = _out_sample(ol.reshape(db * dec_seq, N_HEADS, c), gm_s, cv_s, xs, wuvt, wo, gpost)
        outs[3].append(ckv_s.reshape(db, dec_seq, c))
        outs[4].append(kpe_s.reshape(db, dec_seq, QK_ROPE))
        u_pad = jnp.concatenate([st, u_s.reshape(db, dec_seq, GROUP)], axis=1)
        outs[5].append(u_pad[:, -(CONV_W - 1):])

    return (xp.reshape(nb, seq, d), xs.reshape(db, dec_seq, d), *[jnp.stack(o_) for o_ in outs])
```

```python
import functools

import jax
import jax.numpy as jnp
from jax import lax
from jax.experimental import pallas as pl
from jax.experimental.pallas import tpu as pltpu

N_HEADS = 8
QK_NOPE = 64
QK_ROPE = 32
V_HEAD = 64
CONV_W = 3
ROPE_THETA = 10000.0
EPS = 1e-6
ATTN_SCALE = (QK_NOPE + QK_ROPE) ** -0.5

LANES = 128
HEAD_SLAB = LANES
ONES_ROWS = 16
LOG2_E = 1.4426950408889634
VMEM_LIMIT_BYTES = 56 * 1024 * 1024

F32 = jnp.float32
BF16 = jnp.bfloat16


def _silu(x):
    return x * (1.0 / (1.0 + jnp.exp(-x)))


def _rms(x, g):
    return x * lax.rsqrt(jnp.mean(x * x, axis=-1, keepdims=True) + EPS) * g


def _rope_table_kernel(pos_ref, invf_ref, cq1_ref, cq2_ref, ck1_ref, ck2_ref, cq1t_ref, cq2t_ref):
    ang = invf_ref[...] * pos_ref[...]
    c, s = jnp.cos(ang), jnp.sin(ang)
    t = ang.shape[1]
    nope = jnp.ones((QK_NOPE, t), F32)
    zn = jnp.zeros((QK_NOPE, t), F32)
    zr = jnp.zeros((HEAD_SLAB - QK_NOPE - QK_ROPE, t), F32)

    def slab(first, mid):
        return jnp.concatenate([first, mid, zr], axis=0)

    cq1_ref[...] = slab(nope * ATTN_SCALE, c * ATTN_SCALE).T
    cq2_ref[...] = slab(zn, s * ATTN_SCALE).T
    ck1_ref[...] = slab(zn, c).T
    ck2_ref[...] = slab(zn, s).T
    cq1t_ref[...] = slab(nope * (ATTN_SCALE * LOG2_E), c * (ATTN_SCALE * LOG2_E))
    cq2t_ref[...] = slab(zn, s * (ATTN_SCALE * LOG2_E))


def _rope_tables(pos):
    t = pos.shape[0]
    r = QK_ROPE
    inv_freq = ROPE_THETA ** (-jnp.arange(0, r, 2, dtype=F32) / r)
    invf = jnp.concatenate([inv_freq, inv_freq]).reshape(r, 1)
    out = jax.ShapeDtypeStruct((t, LANES), F32)
    out_t = jax.ShapeDtypeStruct((LANES, t), F32)
    return pl.pallas_call(
        _rope_table_kernel,
        out_shape=(out, out, out, out, out_t, out_t),
        name="rope_tables",
    )(pos.astype(F32).reshape(1, t), invf)


C_Q0, C_Q1 = 0, 384
C_KV0, C_KV1 = 384, 640
C_KP0, C_KP1 = 640, 768
C_GM0 = 768
GROUP = 512
D_IN_PACKED = C_GM0 + 5 * GROUP


def _rope_slab(blk, c1, c2):
    return blk * c1 + pltpu.roll(blk, LANES - QK_ROPE, 1) * c2


def _project_common(x_ref, gpre_ref, w1_ref, gq_ref, gkv_ref, ck1_ref, ck2_ref):
    x = x_ref[...]
    xn = _rms(x, gpre_ref[...]).astype(BF16)

    def proj(c0, c1):
        return jnp.dot(xn, w1_ref[:, c0:c1], preferred_element_type=F32)

    cqn = _rms(proj(C_Q0, C_Q1), gq_ref[...]).astype(BF16)
    ckv = _rms(proj(C_KV0, C_KV1), gkv_ref[...])
    kpe_blk = _rope_slab(proj(C_KP0, C_KP1), ck1_ref[...], ck2_ref[...])
    return proj, cqn, ckv, kpe_blk


def _conv_gate(proj, um1, um2, u, wconv_ref):
    w = wconv_ref[...]
    y = w[0:1, :] * um2 + w[1:2, :] * um1 + w[2:3, :] * u
    bg = proj(C_GM0 + GROUP, C_GM0 + 2 * GROUP)
    gc = proj(C_GM0 + 4 * GROUP, C_GM0 + 5 * GROUP)
    return bg * y * _silu(gc)


def _proj_prompt_kernel(x_ref, gpre_ref, w1_ref, gq_ref, wqt_ref, gkv_ref, wk_ref, wvt_ref, wconv_ref,
                        cq1t_ref, cq2t_ref, ck1_ref, ck2_ref,
                        qt_out, k_out, vt_out, ckv_out, kpet_out, gm_out, cv_out, conv_out,
                        carry_ref, *, tiles_per_seq):
    tm = x_ref.shape[0]
    step = pl.program_id(0)
    dn = (((1,), (1,)), ((), ()))

    @pl.when(step % tiles_per_seq == 0)
    def _():
        carry_ref[...] = jnp.zeros_like(carry_ref)

    xn = _rms(x_ref[...], gpre_ref[...]).astype(BF16)

    def proj(c0, c1):
        return jnp.dot(xn, w1_ref[:, c0:c1], preferred_element_type=F32)

    z_cq = proj(C_Q0, C_Q1)
    z_ckv = proj(C_KV0, C_KV1)
    cqn = _rms(z_cq, gq_ref[...]).astype(BF16)
    qt = lax.dot_general(wqt_ref[...], cqn, dn, preferred_element_type=F32)
    ckv = _rms(z_ckv, gkv_ref[...])
    ckv_out[...] = ckv
    ckvb = ckv.astype(BF16)
    z_kp = proj(C_KP0, C_KP1)
    kfull = jnp.dot(ckvb, wk_ref[...], preferred_element_type=F32)
    qk = QK_NOPE + QK_ROPE
    nope_scale = cq1t_ref[0:QK_NOPE, :]
    cos_t, sin_t = cq1t_ref[QK_NOPE:qk, :], cq2t_ref[QK_NOPE:qk, :]
    for h in range(N_HEADS):
        nope, pe = qt[h * qk:h * qk + QK_NOPE, :], qt[h * qk + QK_NOPE:(h + 1) * qk, :]
        qt_out[h * HEAD_SLAB:h * HEAD_SLAB + QK_NOPE, :] = (nope * nope_scale).astype(BF16)
        qt_out[h * HEAD_SLAB + QK_NOPE:h * HEAD_SLAB + qk, :] = (pe * cos_t + _rot_rows(pe) * sin_t).astype(BF16)
        qt_out[h * HEAD_SLAB + qk:(h + 1) * HEAD_SLAB, :] = jnp.zeros((HEAD_SLAB - qk, tm), BF16)
    vt = lax.dot_general(wvt_ref[...], ckvb, dn, preferred_element_type=F32)
    z_gm = proj(C_GM0, C_GM0 + GROUP)
    kpe_blk = _rope_slab(z_kp, ck1_ref[...], ck2_ref[...])
    kpet_out[0] = kpe_blk.T[QK_NOPE:QK_NOPE + QK_ROPE, :]
    for h in range(N_HEADS):
        k_out[:, h * HEAD_SLAB:(h + 1) * HEAD_SLAB] = (kfull[:, h * HEAD_SLAB:(h + 1) * HEAD_SLAB] + kpe_blk).astype(BF16)
    vt_out[...] = vt.astype(BF16)
    z_cg = proj(C_GM0 + 2 * GROUP, C_GM0 + 3 * GROUP)
    z_h = proj(C_GM0 + 3 * GROUP, C_GM0 + 4 * GROUP)
    gm_out[...] = _silu(z_gm).astype(BF16)
    z_bg = proj(C_GM0 + GROUP, C_GM0 + 2 * GROUP)
    z_gc = proj(C_GM0 + 4 * GROUP, C_GM0 + 5 * GROUP)

    u = z_cg * z_h
    prev = carry_ref[...]
    p1, p2 = prev[7:8, :], prev[6:7, :]
    row = lax.broadcasted_iota(jnp.int32, u.shape, 0)
    um1 = jnp.where(row == 0, p1, pltpu.roll(u, 1, 0))
    um2 = jnp.where(row == 0, p2, jnp.where(row == 1, p1, pltpu.roll(u, 2, 0)))
    w = wconv_ref[...]
    y = w[0:1, :] * um2 + w[1:2, :] * um1 + w[2:3, :] * u
    cv_out[...] = (z_bg * y * _silu(z_gc)).astype(BF16)
    carry_ref[...] = u[tm - 8:tm, :]
    conv_out[0] = u[tm - (CONV_W - 1):tm, :]


def _const_spec(shape):
    nd = len(shape)
    return pl.BlockSpec(shape, lambda *_: (0,) * nd, pipeline_mode=pl.Buffered(1))


def _proj_prompt(x2d, seq, gpre, w1, gq, wqt, gkv, wk, wvt, wconv, tabs, tm):
    t, d = x2d.shape
    nb = t // seq
    hs = N_HEADS * HEAD_SLAB
    hv = N_HEADS * V_HEAD
    cq1, cq2, ck1, ck2, cq1t, cq2t = tabs
    row = lambda w: pl.BlockSpec((tm, w), lambda i: (i, 0))
    col = lambda h: pl.BlockSpec((h, tm), lambda i: (0, i))
    tab = pl.BlockSpec((tm, LANES), lambda i: (i % (seq // tm), 0))
    tab_t = pl.BlockSpec((LANES, tm), lambda i: (0, i % (seq // tm)))
    out_shape = (
        jax.ShapeDtypeStruct((hs, t), BF16),
        jax.ShapeDtypeStruct((t, hs), BF16),
        jax.ShapeDtypeStruct((hv, t), BF16),
        jax.ShapeDtypeStruct((t, gkv.shape[1]), F32),
        jax.ShapeDtypeStruct((nb, QK_ROPE, seq), F32),
        jax.ShapeDtypeStruct((t, GROUP), BF16),
        jax.ShapeDtypeStruct((t, GROUP), BF16),
        jax.ShapeDtypeStruct((nb, CONV_W - 1, GROUP), F32),
    )
    return pl.pallas_call(
        functools.partial(_proj_prompt_kernel, tiles_per_seq=seq // tm),
        grid=(t // tm,),
        in_specs=[row(d), _const_spec(gpre.shape), _const_spec(w1.shape), _const_spec(gq.shape),
                  _const_spec(wqt.shape), _const_spec(gkv.shape), _const_spec(wk.shape), _const_spec(wvt.shape),
                  _const_spec(wconv.shape), tab_t, tab_t, tab, tab],
        out_specs=(col(hs), row(hs), col(hv), row(gkv.shape[1]),
                   pl.BlockSpec((1, QK_ROPE, tm), lambda i: (i // (seq // tm), 0, i % (seq // tm))),
                   row(GROUP), row(GROUP),
                   pl.BlockSpec((1, CONV_W - 1, GROUP), lambda i: (i // (seq // tm), 0, 0))),
        out_shape=out_shape,
        scratch_shapes=[pltpu.VMEM((8, GROUP), F32)],
        compiler_params=pltpu.CompilerParams(dimension_semantics=("arbitrary",),
                                             vmem_limit_bytes=VMEM_LIMIT_BYTES),
        name="proj_prompt",
    )(x2d, gpre, w1, gq, wqt, gkv, wk, wvt, wconv, cq1t, cq2t, ck1, ck2)


def _attn_kernel(qt_ref, k_ref, vt_ref, gm_ref, cv_ref, x_ref, wo_ref, gpost_ref, y_ref,
                 s_ref, m_ref, acc_ref, *, tq, tk):
    i = pl.program_id(1)
    hq = tq // 2
    tri = (lax.broadcasted_iota(jnp.int32, (hq, hq), 0) <= lax.broadcasted_iota(jnp.int32, (hq, hq), 1))

    def slab(h):
        return slice(h * HEAD_SLAB, (h + 1) * HEAD_SLAB)

    def values(h, toks):
        ones = jnp.ones((ONES_ROWS, toks.size), BF16)
        return jnp.concatenate([vt_ref[h * V_HEAD:(h + 1) * V_HEAD, toks], ones], 0)

    def scores(h, toks):
        return jnp.dot(k_ref[toks, slab(h)], qt_ref[slab(h), :], preferred_element_type=F32)

    def update(h, toks, st):
        m = m_ref[h]
        m_new = jnp.maximum(m, jnp.max(st, axis=0, keepdims=True))
        pt = jnp.exp2(st - m_new).astype(BF16)
        m_ref[h] = m_new
        acc_ref[h] = jnp.exp2(m - m_new) * acc_ref[h] + jnp.dot(values(h, toks), pt, preferred_element_type=F32)

    base = pl.multiple_of(i * tq, tq)
    ka, kb = pl.ds(base, hq), pl.ds(base + hq, hq)

    def diagonal_scores(h):
        return (jnp.dot(k_ref[ka, slab(h)], qt_ref[slab(h), :], preferred_element_type=F32),
                jnp.dot(k_ref[kb, slab(h)], qt_ref[slab(h), hq:], preferred_element_type=F32))

    def diagonal_init(h, a, b_):
        a_lo = jnp.where(tri, a[:, :hq], -jnp.inf)
        a_hi = a[:, hq:]
        b_ = jnp.where(tri, b_, -jnp.inf)
        m_lo = jnp.max(a_lo, axis=0, keepdims=True)
        m_hi = jnp.maximum(jnp.max(a_hi, axis=0, keepdims=True), jnp.max(b_, axis=0, keepdims=True))
        pa = jnp.concatenate([jnp.exp2(a_lo - m_lo), jnp.exp2(a_hi - m_hi)], axis=1).astype(BF16)
        pb = jnp.exp2(b_ - m_hi).astype(BF16)
        acc_a = jnp.dot(values(h, ka), pa, preferred_element_type=F32)
        acc_b = jnp.dot(values(h, kb), pb, preferred_element_type=F32)
        m_ref[h] = jnp.concatenate([m_lo, m_hi], axis=1)
        acc_ref[h] = jnp.concatenate([acc_a[:, :hq], acc_a[:, hq:] + acc_b], axis=1)

    n_full = i * (tq // tk)

    def tile_tokens(j):
        return pl.ds(pl.multiple_of(j * tk, tk), tk)

    cur = diagonal_scores(0)
    for h in range(N_HEADS):
        if h + 1 < N_HEADS:
            ahead = diagonal_scores(h + 1)
        else:
            s_ref[0] = scores(0, tile_tokens(0))
        diagonal_init(h, *cur)
        cur = ahead

    def full_tile(j, carry):
        toks = tile_tokens(j)
        nxt = tile_tokens(jnp.minimum(j + 1, n_full - 1))
        for h in range(N_HEADS):
            if h + 1 < N_HEADS:
                s_ref[h + 1] = scores(h + 1, toks)
                update(h, toks, s_ref[h])
            else:
                cur = s_ref[h]
                s_ref[0] = scores(0, nxt)
                update(h, toks, cur)
        return carry

    lax.fori_loop(0, n_full, full_tile, 0)
    outs = [acc_ref[h, :V_HEAD] * (1.0 / acc_ref[h, V_HEAD:V_HEAD + 1]) for h in range(N_HEADS)]
    pairs = [jnp.concatenate(outs[2 * p:2 * p + 2], axis=0).T for p in range(N_HEADS // 2)]

    half = gm_ref.shape[1]
    mla = (jnp.concatenate(pairs, axis=1) * gm_ref[...].astype(F32)).astype(BF16)
    y = jnp.dot(mla, wo_ref[0:half, :], preferred_element_type=F32)
    y = y + jnp.dot(cv_ref[...], wo_ref[half:, :], preferred_element_type=F32)
    y_ref[...] = x_ref[...] + _rms(y, gpost_ref[...])


def _attention_out(qt, k, vt, gm, cv, x2d, wo, gpost, nb, seq, tq, tk):
    hs, t = qt.shape
    hv = vt.shape[0]
    d = x2d.shape[1]
    assert N_HEADS % 2 == 0 and tq % tk == 0
    row = lambda w: pl.BlockSpec((tq, w), lambda b, i: (b * (seq // tq) + i, 0))
    return pl.pallas_call(
        functools.partial(_attn_kernel, tq=tq, tk=tk),
        grid=(nb, seq // tq),
        in_specs=[pl.BlockSpec((hs, tq), lambda b, i: (0, b * (seq // tq) + i)),
                  pl.BlockSpec((seq, hs), lambda b, i: (b, 0)),
                  pl.BlockSpec((hv, seq), lambda b, i: (0, b)),
                  row(gm.shape[1]), row(cv.shape[1]), row(d), _const_spec(wo.shape), _const_spec(gpost.shape)],
        out_specs=row(d),
        out_shape=jax.ShapeDtypeStruct((t, d), F32),
        scratch_shapes=[pltpu.VMEM((N_HEADS, tk, tq), F32), pltpu.VMEM((N_HEADS, 1, tq), F32),
                        pltpu.VMEM((N_HEADS, V_HEAD + ONES_ROWS, tq), F32)],
        compiler_params=pltpu.CompilerParams(dimension_semantics=("arbitrary", "arbitrary"),
                                             vmem_limit_bytes=VMEM_LIMIT_BYTES),
        name="prompt_attn",
    )(qt, k, vt, gm, cv, x2d, wo, gpost)


def _proj_sample_kernel(x_ref, gpre_ref, w1_ref, gq_ref, wq_ref, gkv_ref, wukt_ref, wconv_ref,
                        cq1_ref, cq2_ref, ck1_ref, ck2_ref, s0_ref, s1_ref,
                        ql_out, qp_out, ckv_out, kpe_out, gm_out, cv_out, conv_out, *, dec_seq):
    proj, cqn, ckv, kpe_blk = _project_common(x_ref, gpre_ref, w1_ref, gq_ref, gkv_ref, ck1_ref, ck2_ref)
    qraw = jnp.dot(cqn, wq_ref[...], preferred_element_type=F32)
    cq1, cq2 = cq1_ref[...], cq2_ref[...]
    for h in range(N_HEADS):
        qh = _rope_slab(qraw[:, h * HEAD_SLAB:(h + 1) * HEAD_SLAB], cq1, cq2)
        ql_out[:, h, :] = jnp.dot(qh.astype(BF16), wukt_ref[h], preferred_element_type=F32)
        qp_out[:, h, :] = pltpu.roll(qh, LANES - QK_NOPE, 1)[:, 0:QK_ROPE]
    ckv_out[...] = ckv
    kpe_out[...] = kpe_blk[:, QK_NOPE:QK_NOPE + QK_ROPE]
    gm_out[...] = _silu(proj(C_GM0, C_GM0 + GROUP))

    u = proj(C_GM0 + 2 * GROUP, C_GM0 + 3 * GROUP) * proj(C_GM0 + 3 * GROUP, C_GM0 + 4 * GROUP)
    t_in_seq = lax.broadcasted_iota(jnp.int32, u.shape, 0) % dec_seq
    s0, s1 = s0_ref[...], s1_ref[...]
    um1 = jnp.where(t_in_seq == 0, s1, pltpu.roll(u, 1, 0))
    um2 = jnp.where(t_in_seq == 0, s0, jnp.where(t_in_seq == 1, s1, pltpu.roll(u, 2, 0)))
    cv_out[...] = _conv_gate(proj, um1, um2, u, wconv_ref)
    conv_out[...] = u


def _proj_sample(x2d, dec_seq, gpre, w1, gq, wq, gkv, wukt, wconv, tabs, s0, s1):
    t, d = x2d.shape
    c = gkv.shape[1]
    out_shape = (
        jax.ShapeDtypeStruct((t, N_HEADS, c), F32),
        jax.ShapeDtypeStruct((t, N_HEADS, QK_ROPE), F32),
        jax.ShapeDtypeStruct((t, c), F32),
        jax.ShapeDtypeStruct((t, QK_ROPE), F32),
        jax.ShapeDtypeStruct((t, GROUP), F32),
        jax.ShapeDtypeStruct((t, GROUP), F32),
        jax.ShapeDtypeStruct((t, GROUP), F32),
    )
    return pl.pallas_call(
        functools.partial(_proj_sample_kernel, dec_seq=dec_seq),
        out_shape=out_shape,
        compiler_params=pltpu.CompilerParams(vmem_limit_bytes=VMEM_LIMIT_BYTES),
        name="proj_sample",
    )(x2d, gpre, w1, gq, wq, gkv, wukt, wconv, *tabs, s0, s1)


def _decode_attn_kernel(pt_ref, ql_ref, qp_ref, cnew_ref, knew_ref, y_hbm, x_hbm, o_ref,
                        ybuf, xbuf, sem, m_ref, l_ref, acc_ref, *, pages, slots, layer, dec_seq):
    g = pl.program_id(0)
    n_g = pl.num_programs(0)
    per_step = ql_ref.shape[0]
    n_chunks = pt_ref.shape[1] // pages
    total = per_step * n_chunks
    page = ybuf.shape[1] // pages
    rows = ql_ref.shape[1]
    dn = (((1,), (1,)), ((), ()))
    ahead = slots - 1

    def chunk_copies(bb, ch, slot, page_ids):
        cps = []
        for i in range(pages):
            pid = page_ids(bb, ch * pages + i)
            tok = pl.ds(i * page, page)
            cps.append(pltpu.make_async_copy(y_hbm.at[layer, pid], ybuf.at[slot, tok, :], sem.at[0, slot]))
            cps.append(pltpu.make_async_copy(x_hbm.at[layer, pid], xbuf.at[slot, :, tok], sem.at[1, slot]))
        return cps

    def start(bb, ch, slot):
        for cp in chunk_copies(bb, ch, slot, lambda r, j: pt_ref[r, j]):
            cp.start()

    def wait(slot):
        for cp in chunk_copies(0, 0, slot, lambda r, j: 0):
            cp.wait()

    def start_nth(step, k, slot):
        start(step * per_step + k // n_chunks, k % n_chunks, slot)

    @pl.when(g == 0)
    def _():
        for k in range(ahead):
            start_nth(0, k, k % slots)

    tok_of_row = lax.broadcasted_iota(jnp.int32, (rows, 1), 0) // N_HEADS
    for e in range(per_step):
        qlf, qpf = ql_ref[e], qp_ref[e]
        s_new = []
        for j in range(dec_seq):
            sj = (jnp.sum(qlf * cnew_ref[e, j:j + 1, :], axis=-1, keepdims=True)
                  + jnp.sum(qpf * knew_ref[e, j:j + 1, :], axis=-1, keepdims=True))
            s_new.append(jnp.where(tok_of_row >= j, sj, -jnp.inf))
        m_n = functools.reduce(jnp.maximum, s_new)
        p_n = [jnp.exp(sj - m_n) for sj in s_new]
        m_ref[e] = m_n
        l_ref[e] = functools.reduce(lambda a, v: a + v, p_n)
        acc_ref[e, 0] = functools.reduce(lambda a, v: a + v,
                                         [p_n[j] * cnew_ref[e, j:j + 1, :] for j in range(dec_seq)])
        acc_ref[e, 1] = jnp.zeros(acc_ref.shape[2:], F32)

    half = pages * page // 2
    toks = [slice(h * half, (h + 1) * half) for h in range(2)]

    def scores(e, slot):
        ql, qp = ql_ref[e].astype(BF16), qp_ref[e].astype(BF16)
        yb = [ybuf[slot, t, :].astype(BF16) for t in toks]
        xb = [xbuf[slot, :, t].astype(BF16) for t in toks]
        s = jnp.concatenate(
            [lax.dot_general(ql, yb[h], dn, preferred_element_type=F32)
             + jnp.dot(qp, xb[h], preferred_element_type=F32) for h in range(2)], axis=1)
        return e, s, yb

    def softmax(e, s, yb):
        m = m_ref[e]
        m_new = jnp.maximum(m, jnp.max(s, axis=-1, keepdims=True))
        alpha = jnp.exp(m - m_new)
        p = jnp.exp(s - m_new)
        l_ref[e] = alpha * l_ref[e] + jnp.sum(p, axis=-1, keepdims=True)
        m_ref[e] = m_new
        return e, alpha, p.astype(BF16), yb

    def values(e, alpha, p, yb):
        for h in range(2):
            acc_ref[e, h] = alpha * acc_ref[e, h] + jnp.dot(p[:, toks[h]], yb[h], preferred_element_type=F32)

    scored = None
    weighted = None
    for k in range(total):
        nxt = k + ahead
        if nxt < total:
            start_nth(g, nxt, nxt % slots)
        else:
            @pl.when(g + 1 < n_g)
            def _():
                start_nth(g + 1, nxt - total, nxt % slots)

        wait(k % slots)
        fresh = scores(k // n_chunks, k % slots)
        if weighted is not None:
            values(*weighted)
        if scored is not None:
            weighted = softmax(*scored)
        scored = fresh
    if weighted is not None:
        values(*weighted)
    values(*softmax(*scored))
    for e in range(per_step):
        o_ref[e] = (acc_ref[e, 0] + acc_ref[e, 1]) * (1.0 / l_ref[e])


def _decode_attention(page_table, ql, qp, cnew, knew, cache_ckv, cache_kpe, layer, pages, slots, per_step):
    nb, rows, c = ql.shape
    dec_seq = cnew.shape[1]
    page, r = cache_kpe.shape[2:]
    n_chunks = page_table.shape[1] // pages
    assert n_chunks * pages == page_table.shape[1] and nb % per_step == 0
    assert (per_step * n_chunks) % slots == 0 and slots - 1 <= per_step * n_chunks
    kpe_t = jnp.swapaxes(cache_kpe, 2, 3)
    per_b = lambda a: pl.BlockSpec((per_step,) + a.shape[1:], lambda g, pt: (g, 0, 0))
    grid_spec = pltpu.PrefetchScalarGridSpec(
        num_scalar_prefetch=1,
        grid=(nb // per_step,),
        in_specs=[per_b(ql), per_b(qp), per_b(cnew), per_b(knew),
                  pl.BlockSpec(memory_space=pl.ANY), pl.BlockSpec(memory_space=pl.ANY)],
        out_specs=pl.BlockSpec((per_step, rows, c), lambda g, pt: (g, 0, 0)),
        scratch_shapes=[pltpu.VMEM((slots, pages * page, c), F32), pltpu.VMEM((slots, r, pages * page), F32),
                        pltpu.SemaphoreType.DMA((2, slots)),
                        pltpu.VMEM((per_step, rows, 1), F32), pltpu.VMEM((per_step, rows, 1), F32),
                        pltpu.VMEM((per_step, 2, rows, c), F32)],
    )
    return pl.pallas_call(
        functools.partial(_decode_attn_kernel, pages=pages, slots=slots, layer=layer, dec_seq=dec_seq),
        grid_spec=grid_spec,
        out_shape=jax.ShapeDtypeStruct((nb, rows, c), F32),
        compiler_params=pltpu.CompilerParams(dimension_semantics=("arbitrary",),
                                             vmem_limit_bytes=VMEM_LIMIT_BYTES),
        name="decode_attn",
    )(page_table, ql, qp, cnew, knew, cache_ckv, kpe_t)


def _out_sample_kernel(ol_ref, gm_ref, cv_ref, x_ref, wuvt_ref, wo_ref, gpost_ref, y_ref):
    half = gm_ref.shape[1]
    dn = (((1,), (1,)), ((), ()))
    o = jnp.concatenate(
        [lax.dot_general(ol_ref[:, h, :].astype(BF16), wuvt_ref[h], dn, preferred_element_type=F32)
         for h in range(N_HEADS)], axis=1)
    mla = (o * gm_ref[...]).astype(BF16)
    y = jnp.dot(mla, wo_ref[0:half, :], preferred_element_type=F32)
    y = y + jnp.dot(cv_ref[...].astype(BF16), wo_ref[half:, :], preferred_element_type=F32)
    y_ref[...] = x_ref[...] + _rms(y, gpost_ref[...])


def _out_sample(ol, gm, cv, x2d, wuv, wo, gpost):
    return pl.pallas_call(
        _out_sample_kernel,
        out_shape=jax.ShapeDtypeStruct(x2d.shape, F32),
        compiler_params=pltpu.CompilerParams(vmem_limit_bytes=VMEM_LIMIT_BYTES),
        name="out_sample",
    )(ol, gm, cv, x2d, wuv, wo, gpost)


def _rot_rows(w):
    r = w.shape[0] // 2
    return jnp.concatenate([-w[r:], w[:r]], axis=0)


def _pack_kernel(wint_ref, wuq_ref, wukv_ref, wout_ref,
                 w1_ref, wq_ref, wqt_ref, wk_ref, wvt_ref, wukt_ref, wuvt_ref, wo_ref):
    d = wint_ref.shape[1]
    blk = LANES

    def put_w1(col0, rows):
        w1_ref[:, col0:col0 + blk] = rows.T.astype(BF16)

    for r0 in range(0, C_KV1, blk):
        put_w1(r0, wint_ref[r0:r0 + blk, :])
    kpe = wint_ref[C_KV1:C_KV1 + QK_ROPE, :]
    put_w1(C_KP0, jnp.concatenate([jnp.zeros((QK_NOPE, d), F32), kpe, _rot_rows(kpe)], axis=0))
    src0 = C_KV1 + QK_ROPE
    for j in range(5 * GROUP // blk):
        put_w1(C_GM0 + j * blk, wint_ref[src0 + j * blk:src0 + (j + 1) * blk, :])

    uqt = wuq_ref[...].T
    ukvt = wukv_ref[...].T
    kv_lora = ukvt.shape[1]
    qk = QK_NOPE + QK_ROPE
    wqt_ref[...] = uqt.astype(BF16)
    for h in range(N_HEADS):
        pe = uqt[h * qk + QK_NOPE:(h + 1) * qk]
        q_slab = jnp.concatenate([uqt[h * qk:h * qk + QK_NOPE], pe, _rot_rows(pe)], axis=0)
        wq_ref[:, h * HEAD_SLAB:(h + 1) * HEAD_SLAB] = q_slab.T.astype(BF16)
        ukt = ukvt[h * (QK_NOPE + V_HEAD):h * (QK_NOPE + V_HEAD) + QK_NOPE]
        uvt = ukvt[h * (QK_NOPE + V_HEAD) + QK_NOPE:(h + 1) * (QK_NOPE + V_HEAD)]
        k_slab = jnp.concatenate([ukt, jnp.zeros((HEAD_SLAB - QK_NOPE, kv_lora), F32)], axis=0)
        wukt_ref[h] = k_slab.astype(BF16)
        wk_ref[:, h * HEAD_SLAB:(h + 1) * HEAD_SLAB] = k_slab.T.astype(BF16)
        wvt_ref[h * V_HEAD:(h + 1) * V_HEAD, :] = uvt.astype(BF16)
        wuvt_ref[h] = uvt.astype(BF16)
    wo_ref[...] = wout_ref[...].astype(BF16)


def _pack_weights(w_in, w_uq, w_ukv, w_out):
    d = w_in.shape[0]
    q_lora, kv_lora = w_uq.shape[0], w_ukv.shape[0]
    assert w_in.shape[1] == C_KV1 + QK_ROPE + 5 * GROUP
    sds = lambda *s: jax.ShapeDtypeStruct(s, BF16)
    return pl.pallas_call(
        _pack_kernel,
        out_shape=(sds(d, D_IN_PACKED), sds(q_lora, N_HEADS * HEAD_SLAB), sds(w_uq.shape[1], q_lora),
                   sds(kv_lora, N_HEADS * HEAD_SLAB), sds(N_HEADS * V_HEAD, kv_lora),
                   sds(N_HEADS, HEAD_SLAB, kv_lora), sds(N_HEADS, V_HEAD, kv_lora), sds(*w_out.shape)),
        compiler_params=pltpu.CompilerParams(vmem_limit_bytes=VMEM_LIMIT_BYTES),
        name="pack_weights",
    )(w_in.T, w_uq, w_ukv, w_out)


PROMPT_TILE = 512
ATTN_TQ = 512
ATTN_TK = 512
DECODE_PAGES = 32
DECODE_SLOTS = 4
DECODE_PER_STEP = 1


def kernel(x_prompt, x_sample, cache_ckv, cache_kpe, state_conv, page_table, g_pre, w_in, g_qnorm, w_uq,
           g_kvnorm, w_ukv, w_conv, w_out, g_post):
    depth = w_in.shape[0]
    nb, seq, d = x_prompt.shape
    db, dec_seq, _ = x_sample.shape
    past_len = page_table.shape[1] * cache_ckv.shape[2]
    c = cache_ckv.shape[3]

    tabs_p = _rope_tables(jnp.arange(seq, dtype=jnp.int32))
    tabs_s = _rope_tables(jnp.tile(past_len + jnp.arange(dec_seq, dtype=jnp.int32), db))

    xp = x_prompt.reshape(nb * seq, d)
    xs = x_sample.reshape(db * dec_seq, d)
    outs = [[] for _ in range(6)]
    for l in range(depth):
        w1, wq, wqt, wk, wvt, wukt, wuvt, wo = _pack_weights(w_in[l], w_uq[l], w_ukv[l], w_out[l])
        gpre, gq, gkv, gpost = g_pre[l][None], g_qnorm[l][None], g_kvnorm[l][None], g_post[l][None]

        qt, k, vt, ckv_p, kpe_p, gm, cv, conv_p = _proj_prompt(
            xp, seq, gpre, w1, gq, wqt, gkv, wk, wvt, w_conv[l], tabs_p, PROMPT_TILE)
        xp = _attention_out(qt, k, vt, gm, cv, xp, wo, gpost, nb, seq, ATTN_TQ, ATTN_TK)
        outs[0].append(ckv_p.reshape(nb, seq, c))
        outs[1].append(jnp.swapaxes(kpe_p, 1, 2))
        outs[2].append(conv_p)

        st = state_conv[l].astype(F32)
        s0 = jnp.repeat(st[:, 0], dec_seq, axis=0)
        s1 = jnp.repeat(st[:, 1], dec_seq, axis=0)
        ql, qp, ckv_s, kpe_s, gm_s, cv_s, u_s = _proj_sample(
            xs, dec_seq, gpre, w1, gq, wq, gkv, wukt, w_conv[l], tabs_s[:4], s0, s1)
        rows = dec_seq * N_HEADS
        ol = _decode_attention(page_table, ql.reshape(db, rows, c), qp.reshape(db, rows, QK_ROPE),
                               ckv_s.reshape(db, dec_seq, c), kpe_s.reshape(db, dec_seq, QK_ROPE),
                               cache_ckv, cache_kpe, l, DECODE_PAGES, DECODE_SLOTS, DECODE_PER_STEP)
        xs = _out_sample(ol.reshape(db * dec_seq, N_HEADS, c), gm_s, cv_s, xs, wuvt, wo, gpost)
        outs[3].append(ckv_s.reshape(db, dec_seq, c))
        outs[4].append(kpe_s.reshape(db, dec_seq, QK_ROPE))
        u_pad = jnp.concatenate([st, u_s.reshape(db, dec_seq, GROUP)], axis=1)
        outs[5].append(u_pad[:, -(CONV_W - 1):])

    return (xp.reshape(nb, seq, d), xs.reshape(db, dec_seq, d), *[jnp.stack(o_) for o_ in outs])
```

```python
import functools

import jax
import jax.numpy as jnp
from jax import lax
from jax.experimental import pallas as pl
from jax.experimental.pallas import tpu as pltpu

N_HEADS = 8
QK_NOPE = 64
QK_ROPE = 32
V_HEAD = 64
CONV_W = 3
ROPE_THETA = 10000.0
EPS = 1e-6
ATTN_SCALE = (QK_NOPE + QK_ROPE) ** -0.5

LANES = 128
HEAD_SLAB = LANES
ONES_ROWS = 16
LOG2_E = 1.4426950408889634
VMEM_LIMIT_BYTES = 56 * 1024 * 1024

F32 = jnp.float32
BF16 = jnp.bfloat16


def _silu(x):
    return x * (1.0 / (1.0 + jnp.exp(-x)))


def _rms(x, g):
    return x * lax.rsqrt(jnp.mean(x * x, axis=-1, keepdims=True) + EPS) * g


def _rope_table_kernel(pos_ref, invf_ref, cq1_ref, cq2_ref, ck1_ref, ck2_ref, cq1t_ref, cq2t_ref):
    ang = invf_ref[...] * pos_ref[...]
    c, s = jnp.cos(ang), jnp.sin(ang)
    t = ang.shape[1]
    nope = jnp.ones((QK_NOPE, t), F32)
    zn = jnp.zeros((QK_NOPE, t), F32)
    zr = jnp.zeros((HEAD_SLAB - QK_NOPE - QK_ROPE, t), F32)

    def slab(first, mid):
        return jnp.concatenate([first, mid, zr], axis=0)

    cq1_ref[...] = slab(nope * ATTN_SCALE, c * ATTN_SCALE).T
    cq2_ref[...] = slab(zn, s * ATTN_SCALE).T
    ck1_ref[...] = slab(zn, c).T
    ck2_ref[...] = slab(zn, s).T
    cq1t_ref[...] = slab(nope * (ATTN_SCALE * LOG2_E), c * (ATTN_SCALE * LOG2_E))
    cq2t_ref[...] = slab(zn, s * (ATTN_SCALE * LOG2_E))


def _rope_tables(pos):
    t = pos.shape[0]
    r = QK_ROPE
    inv_freq = ROPE_THETA ** (-jnp.arange(0, r, 2, dtype=F32) / r)
    invf = jnp.concatenate([inv_freq, inv_freq]).reshape(r, 1)
    out = jax.ShapeDtypeStruct((t, LANES), F32)
    out_t = jax.ShapeDtypeStruct((LANES, t), F32)
    return pl.pallas_call(
        _rope_table_kernel,
        out_shape=(out, out, out, out, out_t, out_t),
        name="rope_tables",
    )(pos.astype(F32).reshape(1, t), invf)


C_Q0, C_Q1 = 0, 384
C_KV0, C_KV1 = 384, 640
C_KP0, C_KP1 = 640, 768
C_GM0 = 768
GROUP = 512
D_IN_PACKED = C_GM0 + 5 * GROUP


def _rope_slab(blk, c1, c2):
    return blk * c1 + pltpu.roll(blk, LANES - QK_ROPE, 1) * c2


def _project_common(x_ref, gpre_ref, w1_ref, gq_ref, gkv_ref, ck1_ref, ck2_ref):
    x = x_ref[...]
    xn = _rms(x, gpre_ref[...]).astype(BF16)

    def proj(c0, c1):
        return jnp.dot(xn, w1_ref[:, c0:c1], preferred_element_type=F32)

    cqn = _rms(proj(C_Q0, C_Q1), gq_ref[...]).astype(BF16)
    ckv = _rms(proj(C_KV0, C_KV1), gkv_ref[...])
    kpe_blk = _rope_slab(proj(C_KP0, C_KP1), ck1_ref[...], ck2_ref[...])
    return proj, cqn, ckv, kpe_blk


def _conv_gate(proj, um1, um2, u, wconv_ref):
    w = wconv_ref[...]
    y = w[0:1, :] * um2 + w[1:2, :] * um1 + w[2:3, :] * u
    bg = proj(C_GM0 + GROUP, C_GM0 + 2 * GROUP)
    gc = proj(C_GM0 + 4 * GROUP, C_GM0 + 5 * GROUP)
    return bg * y * _silu(gc)


def _proj_prompt_kernel(x_ref, gpre_ref, w1_ref, gq_ref, wqt_ref, gkv_ref, wk_ref, wvt_ref, wconv_ref,
                        cq1t_ref, cq2t_ref, ck1_ref, ck2_ref,
                        qt_out, k_out, vt_out, ckv_out, kpet_out, gm_out, cv_out, conv_out,
                        carry_ref, *, tiles_per_seq):
    tm = x_ref.shape[0]
    step = pl.program_id(0)
    dn = (((1,), (1,)), ((), ()))

    @pl.when(step % tiles_per_seq == 0)
    def _():
        carry_ref[...] = jnp.zeros_like(carry_ref)

    xn = _rms(x_ref[...], gpre_ref[...]).astype(BF16)

    def proj(c0, c1):
        return jnp.dot(xn, w1_ref[:, c0:c1], preferred_element_type=F32)

    z_cq = proj(C_Q0, C_Q1)
    z_ckv = proj(C_KV0, C_KV1)
    cqn = _rms(z_cq, gq_ref[...]).astype(BF16)
    qt = lax.dot_general(wqt_ref[...], cqn, dn, preferred_element_type=F32)
    ckv = _rms(z_ckv, gkv_ref[...])
    ckv_out[...] = ckv
    ckvb = ckv.astype(BF16)
    z_kp = proj(C_KP0, C_KP1)
    kfull = jnp.dot(ckvb, wk_ref[...], preferred_element_type=F32)
    qk = QK_NOPE + QK_ROPE
    nope_scale = cq1t_ref[0:QK_NOPE, :]
    cos_t, sin_t = cq1t_ref[QK_NOPE:qk, :], cq2t_ref[QK_NOPE:qk, :]
    for h in range(N_HEADS):
        nope, pe = qt[h * qk:h * qk + QK_NOPE, :], qt[h * qk + QK_NOPE:(h + 1) * qk, :]
        qt_out[h * HEAD_SLAB:h * HEAD_SLAB + QK_NOPE, :] = (nope * nope_scale).astype(BF16)
        qt_out[h * HEAD_SLAB + QK_NOPE:h * HEAD_SLAB + qk, :] = (pe * cos_t + _rot_rows(pe) * sin_t).astype(BF16)
        qt_out[h * HEAD_SLAB + qk:(h + 1) * HEAD_SLAB, :] = jnp.zeros((HEAD_SLAB - qk, tm), BF16)
    vt = lax.dot_general(wvt_ref[...], ckvb, dn, preferred_element_type=F32)
    z_gm = proj(C_GM0, C_GM0 + GROUP)
    kpe_blk = _rope_slab(z_kp, ck1_ref[...], ck2_ref[...])
    kpet_out[0] = kpe_blk.T[QK_NOPE:QK_NOPE + QK_ROPE, :]
    for h in range(N_HEADS):
        k_out[:, h * HEAD_SLAB:(h + 1) * HEAD_SLAB] = (kfull[:, h * HEAD_SLAB:(h + 1) * HEAD_SLAB] + kpe_blk).astype(BF16)
    vt_out[...] = vt.astype(BF16)
    z_cg = proj(C_GM0 + 2 * GROUP, C_GM0 + 3 * GROUP)
    z_h = proj(C_GM0 + 3 * GROUP, C_GM0 + 4 * GROUP)
    gm_out[...] = _silu(z_gm).astype(BF16)
    z_bg = proj(C_GM0 + GROUP, C_GM0 + 2 * GROUP)
    z_gc = proj(C_GM0 + 4 * GROUP, C_GM0 + 5 * GROUP)

    u = z_cg * z_h
    prev = carry_ref[...]
    p1, p2 = prev[7:8, :], prev[6:7, :]
    row = lax.broadcasted_iota(jnp.int32, u.shape, 0)
    um1 = jnp.where(row == 0, p1, pltpu.roll(u, 1, 0))
    um2 = jnp.where(row == 0, p2, jnp.where(row == 1, p1, pltpu.roll(u, 2, 0)))
    w = wconv_ref[...]
    y = w[0:1, :] * um2 + w[1:2, :] * um1 + w[2:3, :] * u
    cv_out[...] = (z_bg * y * _silu(z_gc)).astype(BF16)
    carry_ref[...] = u[tm - 8:tm, :]
    conv_out[0] = u[tm - (CONV_W - 1):tm, :]


def _const_spec(shape):
    nd = len(shape)
    return pl.BlockSpec(shape, lambda *_: (0,) * nd, pipeline_mode=pl.Buffered(1))


def _proj_prompt(x2d, seq, gpre, w1, gq, wqt, gkv, wk, wvt, wconv, tabs, tm):
    t, d = x2d.shape
    nb = t // seq
    hs = N_HEADS * HEAD_SLAB
    hv = N_HEADS * V_HEAD
    cq1, cq2, ck1, ck2, cq1t, cq2t = tabs
    row = lambda w: pl.BlockSpec((tm, w), lambda i: (i, 0))
    col = lambda h: pl.BlockSpec((h, tm), lambda i: (0, i))
    tab = pl.BlockSpec((tm, LANES), lambda i: (i % (seq // tm), 0))
    tab_t = pl.BlockSpec((LANES, tm), lambda i: (0, i % (seq // tm)))
    out_shape = (
        jax.ShapeDtypeStruct((hs, t), BF16),
        jax.ShapeDtypeStruct((t, hs), BF16),
        jax.ShapeDtypeStruct((hv, t), BF16),
        jax.ShapeDtypeStruct((t, gkv.shape[1]), F32),
        jax.ShapeDtypeStruct((nb, QK_ROPE, seq), F32),
        jax.ShapeDtypeStruct((t, GROUP), BF16),
        jax.ShapeDtypeStruct((t, GROUP), BF16),
        jax.ShapeDtypeStruct((nb, CONV_W - 1, GROUP), F32),
    )
    return pl.pallas_call(
        functools.partial(_proj_prompt_kernel, tiles_per_seq=seq // tm),
        grid=(t // tm,),
        in_specs=[row(d), _const_spec(gpre.shape), _const_spec(w1.shape), _const_spec(gq.shape),
                  _const_spec(wqt.shape), _const_spec(gkv.shape), _const_spec(wk.shape), _const_spec(wvt.shape),
                  _const_spec(wconv.shape), tab_t, tab_t, tab, tab],
        out_specs=(col(hs), row(hs), col(hv), row(gkv.shape[1]),
                   pl.BlockSpec((1, QK_ROPE, tm), lambda i: (i // (seq // tm), 0, i % (seq // tm))),
                   row(GROUP), row(GROUP),
                   pl.BlockSpec((1, CONV_W - 1, GROUP), lambda i: (i // (seq // tm), 0, 0))),
        out_shape=out_shape,
        scratch_shapes=[pltpu.VMEM((8, GROUP), F32)],
        compiler_params=pltpu.CompilerParams(dimension_semantics=("arbitrary",),
                                             vmem_limit_bytes=VMEM_LIMIT_BYTES),
        name="proj_prompt",
    )(x2d, gpre, w1, gq, wqt, gkv, wk, wvt, wconv, cq1t, cq2t, ck1, ck2)


def _attn_kernel(qt_ref, k_ref, vt_ref, gm_ref, cv_ref, x_ref, wo_ref, gpost_ref, y_ref,
                 s_ref, m_ref, acc_ref, *, tq, tk):
    i = pl.program_id(1)
    hq = tq // 2
    tri = (lax.broadcasted_iota(jnp.int32, (hq, hq), 0) <= lax.broadcasted_iota(jnp.int32, (hq, hq), 1))

    def slab(h):
        return slice(h * HEAD_SLAB, (h + 1) * HEAD_SLAB)

    def values(h, toks):
        ones = jnp.ones((ONES_ROWS, toks.size), BF16)
        return jnp.concatenate([vt_ref[h * V_HEAD:(h + 1) * V_HEAD, toks], ones], 0)

    def scores(h, toks):
        return jnp.dot(k_ref[toks, slab(h)], qt_ref[slab(h), :], preferred_element_type=F32)

    def update(h, toks, st):
        m = m_ref[h]
        m_new = jnp.maximum(m, jnp.max(st, axis=0, keepdims=True))
        pt = jnp.exp2(st - m_new).astype(BF16)
        m_ref[h] = m_new
        acc_ref[h] = jnp.exp2(m - m_new) * acc_ref[h] + jnp.dot(values(h, toks), pt, preferred_element_type=F32)

    base = pl.multiple_of(i * tq, tq)
    ka, kb = pl.ds(base, hq), pl.ds(base + hq, hq)

    def diagonal_scores(h):
        return (jnp.dot(k_ref[ka, slab(h)], qt_ref[slab(h), :], preferred_element_type=F32),
                jnp.dot(k_ref[kb, slab(h)], qt_ref[slab(h), hq:], preferred_element_type=F32))

    def diagonal_init(h, a, b_):
        a_lo = jnp.where(tri, a[:, :hq], -jnp.inf)
        a_hi = a[:, hq:]
        b_ = jnp.where(tri, b_, -jnp.inf)
        m_lo = jnp.max(a_lo, axis=0, keepdims=True)
        m_hi = jnp.maximum(jnp.max(a_hi, axis=0, keepdims=True), jnp.max(b_, axis=0, keepdims=True))
        pa = jnp.concatenate([jnp.exp2(a_lo - m_lo), jnp.exp2(a_hi - m_hi)], axis=1).astype(BF16)
        pb = jnp.exp2(b_ - m_hi).astype(BF16)
        acc_a = jnp.dot(values(h, ka), pa, preferred_element_type=F32)
        acc_b = jnp.dot(values(h, kb), pb, preferred_element_type=F32)
        m_ref[h] = jnp.concatenate([m_lo, m_hi], axis=1)
        acc_ref[h] = jnp.concatenate([acc_a[:, :hq], acc_a[:, hq:] + acc_b], axis=1)

    n_full = i * (tq // tk)

    def tile_tokens(j):
        return pl.ds(pl.multiple_of(j * tk, tk), tk)

    cur = diagonal_scores(0)
    for h in range(N_HEADS):
        if h + 1 < N_HEADS:
            ahead = diagonal_scores(h + 1)
        else:
            s_ref[0] = scores(0, tile_tokens(0))
        diagonal_init(h, *cur)
        cur = ahead

    def full_tile(j, carry):
        toks = tile_tokens(j)
        nxt = tile_tokens(jnp.minimum(j + 1, n_full - 1))
        for h in range(N_HEADS):
            if h + 1 < N_HEADS:
                s_ref[h + 1] = scores(h + 1, toks)
                update(h, toks, s_ref[h])
            else:
                cur = s_ref[h]
                s_ref[0] = scores(0, nxt)
                update(h, toks, cur)
        return carry

    lax.fori_loop(0, n_full, full_tile, 0)
    outs = [acc_ref[h, :V_HEAD] * (1.0 / acc_ref[h, V_HEAD:V_HEAD + 1]) for h in range(N_HEADS)]
    pairs = [jnp.concatenate(outs[2 * p:2 * p + 2], axis=0).T for p in range(N_HEADS // 2)]

    half = gm_ref.shape[1]
    mla = (jnp.concatenate(pairs, axis=1) * gm_ref[...].astype(F32)).astype(BF16)
    y = jnp.dot(mla, wo_ref[0:half, :], preferred_element_type=F32)
    y = y + jnp.dot(cv_ref[...], wo_ref[half:, :], preferred_element_type=F32)
    y_ref[...] = x_ref[...] + _rms(y, gpost_ref[...])


def _attention_out(qt, k, vt, gm, cv, x2d, wo, gpost, nb, seq, tq, tk):
    hs, t = qt.shape
    hv = vt.shape[0]
    d = x2d.shape[1]
    assert N_HEADS % 2 == 0 and tq % tk == 0
    row = lambda w: pl.BlockSpec((tq, w), lambda b, i: (b * (seq // tq) + i, 0))
    return pl.pallas_call(
        functools.partial(_attn_kernel, tq=tq, tk=tk),
        grid=(nb, seq // tq),
        in_specs=[pl.BlockSpec((hs, tq), lambda b, i: (0, b * (seq // tq) + i)),
                  pl.BlockSpec((seq, hs), lambda b, i: (b, 0)),
                  pl.BlockSpec((hv, seq), lambda b, i: (0, b)),
                  row(gm.shape[1]), row(cv.shape[1]), row(d), _const_spec(wo.shape), _const_spec(gpost.shape)],
        out_specs=row(d),
        out_shape=jax.ShapeDtypeStruct((t, d), F32),
        scratch_shapes=[pltpu.VMEM((N_HEADS, tk, tq), F32), pltpu.VMEM((N_HEADS, 1, tq), F32),
                        pltpu.VMEM((N_HEADS, V_HEAD + ONES_ROWS, tq), F32)],
        compiler_params=pltpu.CompilerParams(dimension_semantics=("arbitrary", "arbitrary"),
                                             vmem_limit_bytes=VMEM_LIMIT_BYTES),
        name="prompt_attn",
    )(qt, k, vt, gm, cv, x2d, wo, gpost)


def _proj_sample_kernel(x_ref, gpre_ref, w1_ref, gq_ref, wq_ref, gkv_ref, wukt_ref, wconv_ref,
                        cq1_ref, cq2_ref, ck1_ref, ck2_ref, s0_ref, s1_ref,
                        ql_out, qp_out, ckv_out, kpe_out, gm_out, cv_out, conv_out, *, dec_seq):
    proj, cqn, ckv, kpe_blk = _project_common(x_ref, gpre_ref, w1_ref, gq_ref, gkv_ref, ck1_ref, ck2_ref)
    qraw = jnp.dot(cqn, wq_ref[...], preferred_element_type=F32)
    cq1, cq2 = cq1_ref[...], cq2_ref[...]
    for h in range(N_HEADS):
        qh = _rope_slab(qraw[:, h * HEAD_SLAB:(h + 1) * HEAD_SLAB], cq1, cq2)
        ql_out[:, h, :] = jnp.dot(qh.astype(BF16), wukt_ref[h], preferred_element_type=F32)
        qp_out[:, h, :] = pltpu.roll(qh, LANES - QK_NOPE, 1)[:, 0:QK_ROPE]
    ckv_out[...] = ckv
    kpe_out[...] = kpe_blk[:, QK_NOPE:QK_NOPE + QK_ROPE]
    gm_out[...] = _silu(proj(C_GM0, C_GM0 + GROUP))

    u = proj(C_GM0 + 2 * GROUP, C_GM0 + 3 * GROUP) * proj(C_GM0 + 3 * GROUP, C_GM0 + 4 * GROUP)
    t_in_seq = lax.broadcasted_iota(jnp.int32, u.shape, 0) % dec_seq
    s0, s1 = s0_ref[...], s1_ref[...]
    um1 = jnp.where(t_in_seq == 0, s1, pltpu.roll(u, 1, 0))
    um2 = jnp.where(t_in_seq == 0, s0, jnp.where(t_in_seq == 1, s1, pltpu.roll(u, 2, 0)))
    cv_out[...] = _conv_gate(proj, um1, um2, u, wconv_ref)
    conv_out[...] = u


def _proj_sample(x2d, dec_seq, gpre, w1, gq, wq, gkv, wukt, wconv, tabs, s0, s1):
    t, d = x2d.shape
    c = gkv.shape[1]
    out_shape = (
        jax.ShapeDtypeStruct((t, N_HEADS, c), F32),
        jax.ShapeDtypeStruct((t, N_HEADS, QK_ROPE), F32),
        jax.ShapeDtypeStruct((t, c), F32),
        jax.ShapeDtypeStruct((t, QK_ROPE), F32),
        jax.ShapeDtypeStruct((t, GROUP), F32),
        jax.ShapeDtypeStruct((t, GROUP), F32),
        jax.ShapeDtypeStruct((t, GROUP), F32),
    )
    return pl.pallas_call(
        functools.partial(_proj_sample_kernel, dec_seq=dec_seq),
        out_shape=out_shape,
        compiler_params=pltpu.CompilerParams(vmem_limit_bytes=VMEM_LIMIT_BYTES),
        name="proj_sample",
    )(x2d, gpre, w1, gq, wq, gkv, wukt, wconv, *tabs, s0, s1)


def _decode_attn_kernel(pt_ref, ql_ref, qp_ref, cnew_ref, knew_ref, y_hbm, x_hbm, o_ref,
                        ybuf, xbuf, sem, ybb, s_scr, p_scr, alpha_scr, m_ref, l_ref, acc_ref,
                        *, pages, layer, dec_seq):
    n_b = ql_ref.shape[0]
    n_chunks = ybuf.shape[0]
    page = ybuf.shape[1] // pages
    rows = ql_ref.shape[1]
    dn = (((1,), (1,)), ((), ()))
    ahead = n_chunks - 1
    cur, prev = 0, 1

    def chunk_copies(bb, ch, page_ids):
        cps = []
        for i in range(pages):
            pid = page_ids(bb, ch * pages + i)
            tok = pl.ds(i * page, page)
            cps.append(pltpu.make_async_copy(y_hbm.at[layer, pid], ybuf.at[ch, tok, :], sem.at[0, ch]))
            cps.append(pltpu.make_async_copy(x_hbm.at[layer, pid], xbuf.at[ch, :, tok], sem.at[1, ch]))
        return cps

    def start(bb, ch):
        for cp in chunk_copies(bb, ch, lambda r, j: pt_ref[r, j]):
            cp.start()

    def wait(ch):
        for cp in chunk_copies(0, ch, lambda r, j: 0):
            cp.wait()

    half = pages * page // 2
    toks = [slice(h * half, (h + 1) * half) for h in range(2)]
    tok_of_row = lax.broadcasted_iota(jnp.int32, (rows, 1), 0) // N_HEADS

    def init_state(b):
        qlf, qpf = ql_ref[b], qp_ref[b]
        s_new = []
        for j in range(dec_seq):
            sj = (jnp.sum(qlf * cnew_ref[b, j:j + 1, :], axis=-1, keepdims=True)
                  + jnp.sum(qpf * knew_ref[b, j:j + 1, :], axis=-1, keepdims=True))
            s_new.append(jnp.where(tok_of_row >= j, sj, -jnp.inf))
        m_n = functools.reduce(jnp.maximum, s_new)
        p_n = [jnp.exp(sj - m_n) for sj in s_new]
        m_ref[cur] = m_n
        l_ref[cur] = functools.reduce(lambda a, v: a + v, p_n)
        acc_ref[cur, 0] = functools.reduce(lambda a, v: a + v,
                                           [p_n[j] * cnew_ref[b, j:j + 1, :] for j in range(dec_seq)])
        acc_ref[cur, 1] = jnp.zeros(acc_ref.shape[2:], F32)

    def retire_state():
        m_ref[prev], l_ref[prev] = m_ref[cur], l_ref[cur]
        for h in range(2):
            acc_ref[prev, h] = acc_ref[cur, h]

    def stage_a(b, k):
        ql, qp = ql_ref[b].astype(BF16), qp_ref[b].astype(BF16)
        for h in range(2):
            yb = ybuf[k, toks[h], :].astype(BF16)
            xb = xbuf[k, :, toks[h]].astype(BF16)
            ybb[k, toks[h], :] = yb
            s_scr[k % 2, :, toks[h]] = (lax.dot_general(ql, yb, dn, preferred_element_type=F32)
                                        + jnp.dot(qp, xb, preferred_element_type=F32))

    def stage_b(st, k):
        s = s_scr[k % 2]
        m = m_ref[st]
        m_new = jnp.maximum(m, jnp.max(s, axis=-1, keepdims=True))
        alpha = jnp.exp(m - m_new)
        p = jnp.exp(s - m_new)
        l_ref[st] = alpha * l_ref[st] + jnp.sum(p, axis=-1, keepdims=True)
        m_ref[st] = m_new
        alpha_scr[k % 2] = alpha
        p_scr[k % 2] = p.astype(BF16)

    def stage_c(st, k):
        alpha = alpha_scr[k % 2]
        for h in range(2):
            acc_ref[st, h] = alpha * acc_ref[st, h] + jnp.dot(p_scr[k % 2, :, toks[h]], ybb[k, toks[h], :],
                                                              preferred_element_type=F32)

    def finalize(st, b):
        o_ref[b] = (acc_ref[st, 0] + acc_ref[st, 1]) * (1.0 / l_ref[st])

    def element(b, has_prev):
        if has_prev:
            retire_state()
        init_state(b)
        for k in range(n_chunks):
            nxt = k + ahead
            if nxt < n_chunks:
                start(b, nxt)
            else:
                @pl.when(b + 1 < n_b)
                def _():
                    start(b + 1, nxt - n_chunks)

            wait(k)
            stage_a(b, k)
            if k >= 2:
                stage_c(cur, k - 2)
            elif has_prev:
                stage_c(prev, k - 2 + n_chunks)
                if k == 1:
                    finalize(prev, b - 1)
            if k >= 1:
                stage_b(cur, k - 1)
            elif has_prev:
                stage_b(prev, n_chunks - 1)

    for k in range(ahead):
        start(0, k)
    element(0, False)
    lax.fori_loop(1, n_b, lambda b, carry: (element(b, True), carry)[1], 0)
    stage_c(cur, n_chunks - 2)
    stage_b(cur, n_chunks - 1)
    stage_c(cur, n_chunks - 1)
    finalize(cur, n_b - 1)


def _decode_attention(page_table, ql, qp, cnew, knew, cache_ckv, cache_kpe, layer, pages):
    nb, rows, c = ql.shape
    dec_seq = cnew.shape[1]
    page, r = cache_kpe.shape[2:]
    n_chunks = page_table.shape[1] // pages
    assert n_chunks * pages == page_table.shape[1] and n_chunks % 2 == 0 and n_chunks >= 4
    kpe_t = jnp.swapaxes(cache_kpe, 2, 3)
    whole = lambda a: pl.BlockSpec(a.shape, lambda g, pt: (0,) * a.ndim)
    chunk = pages * page
    grid_spec = pltpu.PrefetchScalarGridSpec(
        num_scalar_prefetch=1,
        grid=(1,),
        in_specs=[whole(ql), whole(qp), whole(cnew), whole(knew),
                  pl.BlockSpec(memory_space=pl.ANY), pl.BlockSpec(memory_space=pl.ANY)],
        out_specs=pl.BlockSpec((nb, rows, c), lambda g, pt: (0, 0, 0)),
        scratch_shapes=[pltpu.VMEM((n_chunks, chunk, c), F32), pltpu.VMEM((n_chunks, r, chunk), F32),
                        pltpu.SemaphoreType.DMA((2, n_chunks)),
                        pltpu.VMEM((n_chunks, chunk, c), BF16),
                        pltpu.VMEM((2, rows, chunk), F32),
                        pltpu.VMEM((2, rows, chunk), BF16),
                        pltpu.VMEM((2, rows, 1), F32),
                        pltpu.VMEM((2, rows, 1), F32), pltpu.VMEM((2, rows, 1), F32),
                        pltpu.VMEM((2, 2, rows, c), F32)],
    )
    return pl.pallas_call(
        functools.partial(_decode_attn_kernel, pages=pages, layer=layer, dec_seq=dec_seq),
        grid_spec=grid_spec,
        out_shape=jax.ShapeDtypeStruct((nb, rows, c), F32),
        compiler_params=pltpu.CompilerParams(dimension_semantics=("arbitrary",),
                                             vmem_limit_bytes=VMEM_LIMIT_BYTES),
        name="decode_attn",
    )(page_table, ql, qp, cnew, knew, cache_ckv, kpe_t)


def _out_sample_kernel(ol_ref, gm_ref, cv_ref, x_ref, wuvt_ref, wo_ref, gpost_ref, y_ref):
    half = gm_ref.shape[1]
    dn = (((1,), (1,)), ((), ()))
    o = jnp.concatenate(
        [lax.dot_general(ol_ref[:, h, :].astype(BF16), wuvt_ref[h], dn, preferred_element_type=F32)
         for h in range(N_HEADS)], axis=1)
    mla = (o * gm_ref[...]).astype(BF16)
    y = jnp.dot(mla, wo_ref[0:half, :], preferred_element_type=F32)
    y = y + jnp.dot(cv_ref[...].astype(BF16), wo_ref[half:, :], preferred_element_type=F32)
    y_ref[...] = x_ref[...] + _rms(y, gpost_ref[...])


def _out_sample(ol, gm, cv, x2d, wuv, wo, gpost):
    return pl.pallas_call(
        _out_sample_kernel,
        out_shape=jax.ShapeDtypeStruct(x2d.shape, F32),
        compiler_params=pltpu.CompilerParams(vmem_limit_bytes=VMEM_LIMIT_BYTES),
        name="out_sample",
    )(ol, gm, cv, x2d, wuv, wo, gpost)


def _rot_rows(w):
    r = w.shape[0] // 2
    return jnp.concatenate([-w[r:], w[:r]], axis=0)


def _pack_kernel(wint_ref, wuq_ref, wukv_ref, wout_ref,
                 w1_ref, wq_ref, wqt_ref, wk_ref, wvt_ref, wukt_ref, wuvt_ref, wo_ref):
    d = wint_ref.shape[1]
    blk = LANES

    def put_w1(col0, rows):
        w1_ref[:, col0:col0 + blk] = rows.T.astype(BF16)

    for r0 in range(0, C_KV1, blk):
        put_w1(r0, wint_ref[r0:r0 + blk, :])
    kpe = wint_ref[C_KV1:C_KV1 + QK_ROPE, :]
    put_w1(C_KP0, jnp.concatenate([jnp.zeros((QK_NOPE, d), F32), kpe, _rot_rows(kpe)], axis=0))
    src0 = C_KV1 + QK_ROPE
    for j in range(5 * GROUP // blk):
        put_w1(C_GM0 + j * blk, wint_ref[src0 + j * blk:src0 + (j + 1) * blk, :])

    uqt = wuq_ref[...].T
    ukvt = wukv_ref[...].T
    kv_lora = ukvt.shape[1]
    qk = QK_NOPE + QK_ROPE
    wqt_ref[...] = uqt.astype(BF16)
    for h in range(N_HEADS):
        pe = uqt[h * qk + QK_NOPE:(h + 1) * qk]
        q_slab = jnp.concatenate([uqt[h * qk:h * qk + QK_NOPE], pe, _rot_rows(pe)], axis=0)
        wq_ref[:, h * HEAD_SLAB:(h + 1) * HEAD_SLAB] = q_slab.T.astype(BF16)
        ukt = ukvt[h * (QK_NOPE + V_HEAD):h * (QK_NOPE + V_HEAD) + QK_NOPE]
        uvt = ukvt[h * (QK_NOPE + V_HEAD) + QK_NOPE:(h + 1) * (QK_NOPE + V_HEAD)]
        k_slab = jnp.concatenate([ukt, jnp.zeros((HEAD_SLAB - QK_NOPE, kv_lora), F32)], axis=0)
        wukt_ref[h] = k_slab.astype(BF16)
        wk_ref[:, h * HEAD_SLAB:(h + 1) * HEAD_SLAB] = k_slab.T.astype(BF16)
        wvt_ref[h * V_HEAD:(h + 1) * V_HEAD, :] = uvt.astype(BF16)
        wuvt_ref[h] = uvt.astype(BF16)
    wo_ref[...] = wout_ref[...].astype(BF16)


def _pack_weights(w_in, w_uq, w_ukv, w_out):
    d = w_in.shape[0]
    q_lora, kv_lora = w_uq.shape[0], w_ukv.shape[0]
    assert w_in.shape[1] == C_KV1 + QK_ROPE + 5 * GROUP
    sds = lambda *s: jax.ShapeDtypeStruct(s, BF16)
    return pl.pallas_call(
        _pack_kernel,
        out_shape=(sds(d, D_IN_PACKED), sds(q_lora, N_HEADS * HEAD_SLAB), sds(w_uq.shape[1], q_lora),
                   sds(kv_lora, N_HEADS * HEAD_SLAB), sds(N_HEADS * V_HEAD, kv_lora),
                   sds(N_HEADS, HEAD_SLAB, kv_lora), sds(N_HEADS, V_HEAD, kv_lora), sds(*w_out.shape)),
        compiler_params=pltpu.CompilerParams(vmem_limit_bytes=VMEM_LIMIT_BYTES),
        name="pack_weights",
    )(w_in.T, w_uq, w_ukv, w_out)


PROMPT_TILE = 512
ATTN_TQ = 512
ATTN_TK = 512
DECODE_PAGES = 32


def kernel(x_prompt, x_sample, cache_ckv, cache_kpe, state_conv, page_table, g_pre, w_in, g_qnorm, w_uq,
           g_kvnorm, w_ukv, w_conv, w_out, g_post):
    depth = w_in.shape[0]
    nb, seq, d = x_prompt.shape
    db, dec_seq, _ = x_sample.shape
    past_len = page_table.shape[1] * cache_ckv.shape[2]
    c = cache_ckv.shape[3]

    tabs_p = _rope_tables(jnp.arange(seq, dtype=jnp.int32))
    tabs_s = _rope_tables(jnp.tile(past_len + jnp.arange(dec_seq, dtype=jnp.int32), db))

    xp = x_prompt.reshape(nb * seq, d)
    xs = x_sample.reshape(db * dec_seq, d)
    outs = [[] for _ in range(6)]
    for l in range(depth):
        w1, wq, wqt, wk, wvt, wukt, wuvt, wo = _pack_weights(w_in[l], w_uq[l], w_ukv[l], w_out[l])
        gpre, gq, gkv, gpost = g_pre[l][None], g_qnorm[l][None], g_kvnorm[l][None], g_post[l][None]

        qt, k, vt, ckv_p, kpe_p, gm, cv, conv_p = _proj_prompt(
            xp, seq, gpre, w1, gq, wqt, gkv, wk, wvt, w_conv[l], tabs_p, PROMPT_TILE)
        xp = _attention_out(qt, k, vt, gm, cv, xp, wo, gpost, nb, seq, ATTN_TQ, ATTN_TK)
        outs[0].append(ckv_p.reshape(nb, seq, c))
        outs[1].append(jnp.swapaxes(kpe_p, 1, 2))
        outs[2].append(conv_p)

        st = state_conv[l].astype(F32)
        s0 = jnp.repeat(st[:, 0], dec_seq, axis=0)
        s1 = jnp.repeat(st[:, 1], dec_seq, axis=0)
        ql, qp, ckv_s, kpe_s, gm_s, cv_s, u_s = _proj_sample(
            xs, dec_seq, gpre, w1, gq, wq, gkv, wukt, w_conv[l], tabs_s[:4], s0, s1)
        rows = dec_seq * N_HEADS
        ol = _decode_attention(page_table, ql.reshape(db, rows, c), qp.reshape(db, rows, QK_ROPE),
                               ckv_s.reshape(db, dec_seq, c), kpe_s.reshape(db, dec_seq, QK_ROPE),
                               cache_ckv, cache_kpe, l, DECODE_PAGES)
        xs = _out_sample(ol.reshape(db * dec_seq, N_HEADS, c), gm_s, cv_s, xs, wuvt, wo, gpost)
        outs[3].append(ckv_s.reshape(db, dec_seq, c))
        outs[4].append(kpe_s.reshape(db, dec_seq, QK_ROPE))
        u_pad = jnp.concatenate([st, u_s.reshape(db, dec_seq, GROUP)], axis=1)
        outs[5].append(u_pad[:, -(CONV_W - 1):])

    return (xp.reshape(nb, seq, d), xs.reshape(db, dec_seq, d), *[jnp.stack(o_) for o_ in outs])
```

```python
import functools

import jax
import jax.numpy as jnp
from jax import lax
from jax.experimental import pallas as pl
from jax.experimental.pallas import tpu as pltpu

N_HEADS = 8
QK_NOPE = 64
QK_ROPE = 32
V_HEAD = 64
CONV_W = 3
ROPE_THETA = 10000.0
EPS = 1e-6
ATTN_SCALE = (QK_NOPE + QK_ROPE) ** -0.5

LANES = 128
HEAD_SLAB = LANES
ONES_ROWS = 16
LOG2_E = 1.4426950408889634
VMEM_LIMIT_BYTES = 56 * 1024 * 1024

F32 = jnp.float32
BF16 = jnp.bfloat16


def _silu(x):
    return x * (1.0 / (1.0 + jnp.exp(-x)))


def _rms(x, g):
    return x * lax.rsqrt(jnp.mean(x * x, axis=-1, keepdims=True) + EPS) * g


def _rope_table_kernel(pos_ref, invf_ref, cq1_ref, cq2_ref, ck1_ref, ck2_ref, cq1t_ref, cq2t_ref):
    ang = invf_ref[...] * pos_ref[...]
    c, s = jnp.cos(ang), jnp.sin(ang)
    t = ang.shape[1]
    nope = jnp.ones((QK_NOPE, t), F32)
    zn = jnp.zeros((QK_NOPE, t), F32)
    zr = jnp.zeros((HEAD_SLAB - QK_NOPE - QK_ROPE, t), F32)

    def slab(first, mid):
        return jnp.concatenate([first, mid, zr], axis=0)

    cq1_ref[...] = slab(nope * ATTN_SCALE, c * ATTN_SCALE).T
    cq2_ref[...] = slab(zn, s * ATTN_SCALE).T
    ck1_ref[...] = slab(zn, c).T
    ck2_ref[...] = slab(zn, s).T
    cq1t_ref[...] = slab(nope * (ATTN_SCALE * LOG2_E), c * (ATTN_SCALE * LOG2_E))
    cq2t_ref[...] = slab(zn, s * (ATTN_SCALE * LOG2_E))


def _rope_tables(pos):
    t = pos.shape[0]
    r = QK_ROPE
    inv_freq = ROPE_THETA ** (-jnp.arange(0, r, 2, dtype=F32) / r)
    invf = jnp.concatenate([inv_freq, inv_freq]).reshape(r, 1)
    out = jax.ShapeDtypeStruct((t, LANES), F32)
    out_t = jax.ShapeDtypeStruct((LANES, t), F32)
    return pl.pallas_call(
        _rope_table_kernel,
        out_shape=(out, out, out, out, out_t, out_t),
        name="rope_tables",
    )(pos.astype(F32).reshape(1, t), invf)


C_Q0, C_Q1 = 0, 384
C_KV0, C_KV1 = 384, 640
C_KP0, C_KP1 = 640, 768
C_GM0 = 768
GROUP = 512
D_IN_PACKED = C_GM0 + 5 * GROUP


def _rope_slab(blk, c1, c2):
    return blk * c1 + pltpu.roll(blk, LANES - QK_ROPE, 1) * c2


def _project_common(x_ref, gpre_ref, w1_ref, gq_ref, gkv_ref, ck1_ref, ck2_ref):
    x = x_ref[...]
    xn = _rms(x, gpre_ref[...]).astype(BF16)

    def proj(c0, c1):
        return jnp.dot(xn, w1_ref[:, c0:c1], preferred_element_type=F32)

    cqn = _rms(proj(C_Q0, C_Q1), gq_ref[...]).astype(BF16)
    ckv = _rms(proj(C_KV0, C_KV1), gkv_ref[...])
    kpe_blk = _rope_slab(proj(C_KP0, C_KP1), ck1_ref[...], ck2_ref[...])
    return proj, cqn, ckv, kpe_blk


def _conv_gate(proj, um1, um2, u, wconv_ref):
    w = wconv_ref[...]
    y = w[0:1, :] * um2 + w[1:2, :] * um1 + w[2:3, :] * u
    bg = proj(C_GM0 + GROUP, C_GM0 + 2 * GROUP)
    gc = proj(C_GM0 + 4 * GROUP, C_GM0 + 5 * GROUP)
    return bg * y * _silu(gc)


def _proj_prompt_kernel(x_ref, gpre_ref, w1_ref, gq_ref, wqt_ref, gkv_ref, wk_ref, wvt_ref, wconv_ref,
                        cq1t_ref, cq2t_ref, ck1_ref, ck2_ref,
                        qt_out, k_out, vt_out, ckv_out, kpet_out, gm_out, cv_out, conv_out,
                        carry_ref, *, tiles_per_seq):
    tm = x_ref.shape[0]
    step = pl.program_id(0)
    dn = (((1,), (1,)), ((), ()))

    @pl.when(step % tiles_per_seq == 0)
    def _():
        carry_ref[...] = jnp.zeros_like(carry_ref)

    xn = _rms(x_ref[...], gpre_ref[...]).astype(BF16)

    def proj(c0, c1):
        return jnp.dot(xn, w1_ref[:, c0:c1], preferred_element_type=F32)

    z_cq = proj(C_Q0, C_Q1)
    z_ckv = proj(C_KV0, C_KV1)
    cqn = _rms(z_cq, gq_ref[...]).astype(BF16)
    qt = lax.dot_general(wqt_ref[...], cqn, dn, preferred_element_type=F32)
    ckv = _rms(z_ckv, gkv_ref[...])
    ckv_out[...] = ckv
    ckvb = ckv.astype(BF16)
    z_kp = proj(C_KP0, C_KP1)
    knope = jnp.dot(ckvb, wk_ref[...], preferred_element_type=F32)
    qk = QK_NOPE + QK_ROPE
    nope_scale = cq1t_ref[0:QK_NOPE, :]
    cos_t, sin_t = cq1t_ref[QK_NOPE:qk, :], cq2t_ref[QK_NOPE:qk, :]
    for h in range(N_HEADS):
        nope, pe = qt[h * qk:h * qk + QK_NOPE, :], qt[h * qk + QK_NOPE:(h + 1) * qk, :]
        qt_out[h * HEAD_SLAB:h * HEAD_SLAB + QK_NOPE, :] = (nope * nope_scale).astype(BF16)
        qt_out[h * HEAD_SLAB + QK_NOPE:h * HEAD_SLAB + qk, :] = (pe * cos_t + _rot_rows(pe) * sin_t).astype(BF16)
        qt_out[h * HEAD_SLAB + qk:(h + 1) * HEAD_SLAB, :] = jnp.zeros((HEAD_SLAB - qk, tm), BF16)
    vt = lax.dot_general(wvt_ref[...], ckvb, dn, preferred_element_type=F32)
    z_gm = proj(C_GM0, C_GM0 + GROUP)
    kpe_blk = _rope_slab(z_kp, ck1_ref[...], ck2_ref[...])
    kpet_out[0] = kpe_blk.T[QK_NOPE:QK_NOPE + QK_ROPE, :]
    lane = lax.broadcasted_iota(jnp.int32, kpe_blk.shape, 1)
    for p in range(N_HEADS // 2):
        two = knope[:, p * LANES:(p + 1) * LANES]
        for h, nope_first in ((2 * p, two), (2 * p + 1, pltpu.roll(two, LANES - QK_NOPE, 1))):
            k_out[:, h * HEAD_SLAB:(h + 1) * HEAD_SLAB] = jnp.where(lane < QK_NOPE, nope_first, kpe_blk).astype(BF16)
    vt_out[...] = vt.astype(BF16)
    z_cg = proj(C_GM0 + 2 * GROUP, C_GM0 + 3 * GROUP)
    z_h = proj(C_GM0 + 3 * GROUP, C_GM0 + 4 * GROUP)
    gm_out[...] = _silu(z_gm).astype(BF16)
    z_bg = proj(C_GM0 + GROUP, C_GM0 + 2 * GROUP)
    z_gc = proj(C_GM0 + 4 * GROUP, C_GM0 + 5 * GROUP)

    u = z_cg * z_h
    prev = carry_ref[...]
    p1, p2 = prev[7:8, :], prev[6:7, :]
    row = lax.broadcasted_iota(jnp.int32, u.shape, 0)
    um1 = jnp.where(row == 0, p1, pltpu.roll(u, 1, 0))
    um2 = jnp.where(row == 0, p2, jnp.where(row == 1, p1, pltpu.roll(u, 2, 0)))
    w = wconv_ref[...]
    y = w[0:1, :] * um2 + w[1:2, :] * um1 + w[2:3, :] * u
    cv_out[...] = (z_bg * y * _silu(z_gc)).astype(BF16)
    carry_ref[...] = u[tm - 8:tm, :]
    conv_out[0] = u[tm - (CONV_W - 1):tm, :]


def _const_spec(shape):
    nd = len(shape)
    return pl.BlockSpec(shape, lambda *_: (0,) * nd, pipeline_mode=pl.Buffered(1))


def _proj_prompt(x2d, seq, gpre, w1, gq, wqt, gkv, wk, wvt, wconv, tabs, tm):
    t, d = x2d.shape
    nb = t // seq
    hs = N_HEADS * HEAD_SLAB
    hv = N_HEADS * V_HEAD
    cq1, cq2, ck1, ck2, cq1t, cq2t = tabs
    row = lambda w: pl.BlockSpec((tm, w), lambda i: (i, 0))
    col = lambda h: pl.BlockSpec((h, tm), lambda i: (0, i))
    tab = pl.BlockSpec((tm, LANES), lambda i: (i % (seq // tm), 0))
    tab_t = pl.BlockSpec((LANES, tm), lambda i: (0, i % (seq // tm)))
    out_shape = (
        jax.ShapeDtypeStruct((hs, t), BF16),
        jax.ShapeDtypeStruct((t, hs), BF16),
        jax.ShapeDtypeStruct((hv, t), BF16),
        jax.ShapeDtypeStruct((t, gkv.shape[1]), F32),
        jax.ShapeDtypeStruct((nb, QK_ROPE, seq), F32),
        jax.ShapeDtypeStruct((t, GROUP), BF16),
        jax.ShapeDtypeStruct((t, GROUP), BF16),
        jax.ShapeDtypeStruct((nb, CONV_W - 1, GROUP), F32),
    )
    return pl.pallas_call(
        functools.partial(_proj_prompt_kernel, tiles_per_seq=seq // tm),
        grid=(t // tm,),
        in_specs=[row(d), _const_spec(gpre.shape), _const_spec(w1.shape), _const_spec(gq.shape),
                  _const_spec(wqt.shape), _const_spec(gkv.shape), _const_spec(wk.shape), _const_spec(wvt.shape),
                  _const_spec(wconv.shape), tab_t, tab_t, tab, tab],
        out_specs=(col(hs), row(hs), col(hv), row(gkv.shape[1]),
                   pl.BlockSpec((1, QK_ROPE, tm), lambda i: (i // (seq // tm), 0, i % (seq // tm))),
                   row(GROUP), row(GROUP),
                   pl.BlockSpec((1, CONV_W - 1, GROUP), lambda i: (i // (seq // tm), 0, 0))),
        out_shape=out_shape,
        scratch_shapes=[pltpu.VMEM((8, GROUP), F32)],
        compiler_params=pltpu.CompilerParams(dimension_semantics=("arbitrary",),
                                             vmem_limit_bytes=VMEM_LIMIT_BYTES),
        name="proj_prompt",
    )(x2d, gpre, w1, gq, wqt, gkv, wk, wvt, wconv, cq1t, cq2t, ck1, ck2)


def _attn_kernel(qt_ref, k_ref, vt_ref, gm_ref, cv_ref, x_ref, wo_ref, gpost_ref, y_ref,
                 s_ref, m_ref, acc_ref, *, tq, tk):
    i = pl.program_id(1)
    hq = tq // 2
    tri = (lax.broadcasted_iota(jnp.int32, (hq, hq), 0) <= lax.broadcasted_iota(jnp.int32, (hq, hq), 1))

    def slab(h):
        return slice(h * HEAD_SLAB, (h + 1) * HEAD_SLAB)

    def values(h, toks):
        ones = jnp.ones((ONES_ROWS, toks.size), BF16)
        return jnp.concatenate([vt_ref[h * V_HEAD:(h + 1) * V_HEAD, toks], ones], 0)

    def scores(h, toks):
        return jnp.dot(k_ref[toks, slab(h)], qt_ref[slab(h), :], preferred_element_type=F32)

    def update(h, toks, st):
        m = m_ref[h]
        m_new = jnp.maximum(m, jnp.max(st, axis=0, keepdims=True))
        pt = jnp.exp2(st - m_new).astype(BF16)
        m_ref[h] = m_new
        acc_ref[h] = jnp.exp2(m - m_new) * acc_ref[h] + jnp.dot(values(h, toks), pt, preferred_element_type=F32)

    base = pl.multiple_of(i * tq, tq)
    ka, kb = pl.ds(base, hq), pl.ds(base + hq, hq)

    def diagonal_scores(h):
        return (jnp.dot(k_ref[ka, slab(h)], qt_ref[slab(h), :], preferred_element_type=F32),
                jnp.dot(k_ref[kb, slab(h)], qt_ref[slab(h), hq:], preferred_element_type=F32))

    def diagonal_init(h, a, b_):
        a_lo = jnp.where(tri, a[:, :hq], -jnp.inf)
        a_hi = a[:, hq:]
        b_ = jnp.where(tri, b_, -jnp.inf)
        m_lo = jnp.max(a_lo, axis=0, keepdims=True)
        m_hi = jnp.maximum(jnp.max(a_hi, axis=0, keepdims=True), jnp.max(b_, axis=0, keepdims=True))
        pa = jnp.concatenate([jnp.exp2(a_lo - m_lo), jnp.exp2(a_hi - m_hi)], axis=1).astype(BF16)
        pb = jnp.exp2(b_ - m_hi).astype(BF16)
        acc_a = jnp.dot(values(h, ka), pa, preferred_element_type=F32)
        acc_b = jnp.dot(values(h, kb), pb, preferred_element_type=F32)
        m_ref[h] = jnp.concatenate([m_lo, m_hi], axis=1)
        acc_ref[h] = jnp.concatenate([acc_a[:, :hq], acc_a[:, hq:] + acc_b], axis=1)

    n_full = i * (tq // tk)

    def tile_tokens(j):
        return pl.ds(pl.multiple_of(j * tk, tk), tk)

    cur = diagonal_scores(0)
    for h in range(N_HEADS):
        if h + 1 < N_HEADS:
            ahead = diagonal_scores(h + 1)
        else:
            s_ref[0] = scores(0, tile_tokens(0))
        diagonal_init(h, *cur)
        cur = ahead

    def full_tile(j, carry):
        toks = tile_tokens(j)
        nxt = tile_tokens(jnp.minimum(j + 1, n_full - 1))
        for h in range(N_HEADS):
            if h + 1 < N_HEADS:
                s_ref[h + 1] = scores(h + 1, toks)
                update(h, toks, s_ref[h])
            else:
                cur = s_ref[h]
                s_ref[0] = scores(0, nxt)
                update(h, toks, cur)
        return carry

    lax.fori_loop(0, n_full, full_tile, 0)
    outs = [acc_ref[h, :V_HEAD] * (1.0 / acc_ref[h, V_HEAD:V_HEAD + 1]) for h in range(N_HEADS)]
    pairs = [jnp.concatenate(outs[2 * p:2 * p + 2], axis=0).T for p in range(N_HEADS // 2)]

    half = gm_ref.shape[1]
    mla = (jnp.concatenate(pairs, axis=1) * gm_ref[...].astype(F32)).astype(BF16)
    y = jnp.dot(mla, wo_ref[0:half, :], preferred_element_type=F32)
    y = y + jnp.dot(cv_ref[...], wo_ref[half:, :], preferred_element_type=F32)
    y_ref[...] = x_ref[...] + _rms(y, gpost_ref[...])


def _attention_out(qt, k, vt, gm, cv, x2d, wo, gpost, nb, seq, tq, tk):
    hs, t = qt.shape
    hv = vt.shape[0]
    d = x2d.shape[1]
    assert N_HEADS % 2 == 0 and tq % tk == 0
    row = lambda w: pl.BlockSpec((tq, w), lambda b, i: (b * (seq // tq) + i, 0))
    return pl.pallas_call(
        functools.partial(_attn_kernel, tq=tq, tk=tk),
        grid=(nb, seq // tq),
        in_specs=[pl.BlockSpec((hs, tq), lambda b, i: (0, b * (seq // tq) + i)),
                  pl.BlockSpec((seq, hs), lambda b, i: (b, 0)),
                  pl.BlockSpec((hv, seq), lambda b, i: (0, b)),
                  row(gm.shape[1]), row(cv.shape[1]), row(d), _const_spec(wo.shape), _const_spec(gpost.shape)],
        out_specs=row(d),
        out_shape=jax.ShapeDtypeStruct((t, d), F32),
        scratch_shapes=[pltpu.VMEM((N_HEADS, tk, tq), F32), pltpu.VMEM((N_HEADS, 1, tq), F32),
                        pltpu.VMEM((N_HEADS, V_HEAD + ONES_ROWS, tq), F32)],
        compiler_params=pltpu.CompilerParams(dimension_semantics=("arbitrary", "arbitrary"),
                                             vmem_limit_bytes=VMEM_LIMIT_BYTES),
        name="prompt_attn",
    )(qt, k, vt, gm, cv, x2d, wo, gpost)


def _proj_sample_kernel(x_ref, gpre_ref, w1_ref, gq_ref, wq_ref, gkv_ref, wukt_ref, wconv_ref,
                        cq1_ref, cq2_ref, ck1_ref, ck2_ref, s0_ref, s1_ref,
                        ql_out, qp_out, ckv_out, kpe_out, gm_out, cv_out, conv_out, *, dec_seq):
    proj, cqn, ckv, kpe_blk = _project_common(x_ref, gpre_ref, w1_ref, gq_ref, gkv_ref, ck1_ref, ck2_ref)
    qraw = jnp.dot(cqn, wq_ref[...], preferred_element_type=F32)
    cq1, cq2 = cq1_ref[...], cq2_ref[...]
    for h in range(N_HEADS):
        qh = _rope_slab(qraw[:, h * HEAD_SLAB:(h + 1) * HEAD_SLAB], cq1, cq2)
        ql_out[:, h, :] = jnp.dot(qh.astype(BF16), wukt_ref[h], preferred_element_type=F32)
        qp_out[:, h, :] = pltpu.roll(qh, LANES - QK_NOPE, 1)[:, 0:QK_ROPE]
    ckv_out[...] = ckv
    kpe_out[...] = kpe_blk[:, QK_NOPE:QK_NOPE + QK_ROPE]
    gm_out[...] = _silu(proj(C_GM0, C_GM0 + GROUP))

    u = proj(C_GM0 + 2 * GROUP, C_GM0 + 3 * GROUP) * proj(C_GM0 + 3 * GROUP, C_GM0 + 4 * GROUP)
    t_in_seq = lax.broadcasted_iota(jnp.int32, u.shape, 0) % dec_seq
    s0, s1 = s0_ref[...], s1_ref[...]
    um1 = jnp.where(t_in_seq == 0, s1, pltpu.roll(u, 1, 0))
    um2 = jnp.where(t_in_seq == 0, s0, jnp.where(t_in_seq == 1, s1, pltpu.roll(u, 2, 0)))
    cv_out[...] = _conv_gate(proj, um1, um2, u, wconv_ref)
    conv_out[...] = u


def _proj_sample(x2d, dec_seq, gpre, w1, gq, wq, gkv, wukt, wconv, tabs, s0, s1):
    t, d = x2d.shape
    c = gkv.shape[1]
    out_shape = (
        jax.ShapeDtypeStruct((t, N_HEADS, c), F32),
        jax.ShapeDtypeStruct((t, N_HEADS, QK_ROPE), F32),
        jax.ShapeDtypeStruct((t, c), F32),
        jax.ShapeDtypeStruct((t, QK_ROPE), F32),
        jax.ShapeDtypeStruct((t, GROUP), F32),
        jax.ShapeDtypeStruct((t, GROUP), F32),
        jax.ShapeDtypeStruct((t, GROUP), F32),
    )
    return pl.pallas_call(
        functools.partial(_proj_sample_kernel, dec_seq=dec_seq),
        out_shape=out_shape,
        compiler_params=pltpu.CompilerParams(vmem_limit_bytes=VMEM_LIMIT_BYTES),
        name="proj_sample",
    )(x2d, gpre, w1, gq, wq, gkv, wukt, wconv, *tabs, s0, s1)


def _decode_attn_kernel(pt_ref, ql_ref, qp_ref, cnew_ref, knew_ref, y_hbm, x_hbm, o_ref,
                        ybuf, xbuf, sem, m_ref, l_ref, acc_ref, *, pages, slots, layer, dec_seq):
    g = pl.program_id(0)
    n_g = pl.num_programs(0)
    per_step = ql_ref.shape[0]
    n_chunks = pt_ref.shape[1] // pages
    total = per_step * n_chunks
    page = ybuf.shape[1] // pages
    rows = ql_ref.shape[1]
    dn = (((1,), (1,)), ((), ()))
    ahead = slots - 1

    def chunk_copies(bb, ch, slot, page_ids):
        cps = []
        for i in range(pages):
            pid = page_ids(bb, ch * pages + i)
            tok = pl.ds(i * page, page)
            cps.append(pltpu.make_async_copy(y_hbm.at[layer, pid], ybuf.at[slot, tok, :], sem.at[0, slot]))
            cps.append(pltpu.make_async_copy(x_hbm.at[layer, pid], xbuf.at[slot, :, tok], sem.at[1, slot]))
        return cps

    def start(bb, ch, slot):
        for cp in chunk_copies(bb, ch, slot, lambda r, j: pt_ref[r, j]):
            cp.start()

    def wait(slot):
        for cp in chunk_copies(0, 0, slot, lambda r, j: 0):
            cp.wait()

    def start_nth(step, k, slot):
        start(step * per_step + k // n_chunks, k % n_chunks, slot)

    @pl.when(g == 0)
    def _():
        for k in range(ahead):
            start_nth(0, k, k % slots)

    tok_of_row = lax.broadcasted_iota(jnp.int32, (rows, 1), 0) // N_HEADS
    for e in range(per_step):
        qlf, qpf = ql_ref[e], qp_ref[e]
        s_new = []
        for j in range(dec_seq):
            sj = (jnp.sum(qlf * cnew_ref[e, j:j + 1, :], axis=-1, keepdims=True)
                  + jnp.sum(qpf * knew_ref[e, j:j + 1, :], axis=-1, keepdims=True))
            s_new.append(jnp.where(tok_of_row >= j, sj, -jnp.inf))
        m_n = functools.reduce(jnp.maximum, s_new)
        p_n = [jnp.exp(sj - m_n) for sj in s_new]
        m_ref[e] = m_n
        l_ref[e] = functools.reduce(lambda a, v: a + v, p_n)
        acc_ref[e, 0] = functools.reduce(lambda a, v: a + v,
                                         [p_n[j] * cnew_ref[e, j:j + 1, :] for j in range(dec_seq)])
        acc_ref[e, 1] = jnp.zeros(acc_ref.shape[2:], F32)

    half = pages * page // 2
    toks = [slice(h * half, (h + 1) * half) for h in range(2)]

    def scores(e, slot):
        ql, qp = ql_ref[e].astype(BF16), qp_ref[e].astype(BF16)
        yb = [ybuf[slot, t, :].astype(BF16) for t in toks]
        xb = [xbuf[slot, :, t].astype(BF16) for t in toks]
        s = jnp.concatenate(
            [lax.dot_general(ql, yb[h], dn, preferred_element_type=F32)
             + jnp.dot(qp, xb[h], preferred_element_type=F32) for h in range(2)], axis=1)
        return e, s, yb

    def softmax(e, s, yb):
        m = m_ref[e]
        m_new = jnp.maximum(m, jnp.max(s, axis=-1, keepdims=True))
        alpha = jnp.exp(m - m_new)
        p = jnp.exp(s - m_new)
        l_ref[e] = alpha * l_ref[e] + jnp.sum(p, axis=-1, keepdims=True)
        m_ref[e] = m_new
        return e, alpha, p.astype(BF16), yb

    def values(e, alpha, p, yb):
        for h in range(2):
            acc_ref[e, h] = alpha * acc_ref[e, h] + jnp.dot(p[:, toks[h]], yb[h], preferred_element_type=F32)

    scored = None
    weighted = None
    for k in range(total):
        nxt = k + ahead
        if nxt < total:
            start_nth(g, nxt, nxt % slots)
        else:
            @pl.when(g + 1 < n_g)
            def _():
                start_nth(g + 1, nxt - total, nxt % slots)

        wait(k % slots)
        fresh = scores(k // n_chunks, k % slots)
        if weighted is not None:
            values(*weighted)
        if scored is not None:
            weighted = softmax(*scored)
        scored = fresh
    if weighted is not None:
        values(*weighted)
    values(*softmax(*scored))
    for e in range(per_step):
        o_ref[e] = (acc_ref[e, 0] + acc_ref[e, 1]) * (1.0 / l_ref[e])


def _decode_attention(page_table, ql, qp, cnew, knew, cache_ckv, cache_kpe, layer, pages, slots, per_step):
    nb, rows, c = ql.shape
    dec_seq = cnew.shape[1]
    page, r = cache_kpe.shape[2:]
    n_chunks = page_table.shape[1] // pages
    assert n_chunks * pages == page_table.shape[1] and nb % per_step == 0
    assert (per_step * n_chunks) % slots == 0 and slots - 1 <= per_step * n_chunks
    kpe_t = jnp.swapaxes(cache_kpe, 2, 3)
    per_b = lambda a: pl.BlockSpec((per_step,) + a.shape[1:], lambda g, pt: (g, 0, 0))
    grid_spec = pltpu.PrefetchScalarGridSpec(
        num_scalar_prefetch=1,
        grid=(nb // per_step,),
        in_specs=[per_b(ql), per_b(qp), per_b(cnew), per_b(knew),
                  pl.BlockSpec(memory_space=pl.ANY), pl.BlockSpec(memory_space=pl.ANY)],
        out_specs=pl.BlockSpec((per_step, rows, c), lambda g, pt: (g, 0, 0)),
        scratch_shapes=[pltpu.VMEM((slots, pages * page, c), F32), pltpu.VMEM((slots, r, pages * page), F32),
                        pltpu.SemaphoreType.DMA((2, slots)),
                        pltpu.VMEM((per_step, rows, 1), F32), pltpu.VMEM((per_step, rows, 1), F32),
                        pltpu.VMEM((per_step, 2, rows, c), F32)],
    )
    return pl.pallas_call(
        functools.partial(_decode_attn_kernel, pages=pages, slots=slots, layer=layer, dec_seq=dec_seq),
        grid_spec=grid_spec,
        out_shape=jax.ShapeDtypeStruct((nb, rows, c), F32),
        compiler_params=pltpu.CompilerParams(dimension_semantics=("arbitrary",),
                                             vmem_limit_bytes=VMEM_LIMIT_BYTES),
        name="decode_attn",
    )(page_table, ql, qp, cnew, knew, cache_ckv, kpe_t)


def _out_sample_kernel(ol_ref, gm_ref, cv_ref, x_ref, wuvt_ref, wo_ref, gpost_ref, y_ref):
    half = gm_ref.shape[1]
    dn = (((1,), (1,)), ((), ()))
    o = jnp.concatenate(
        [lax.dot_general(ol_ref[:, h, :].astype(BF16), wuvt_ref[h], dn, preferred_element_type=F32)
         for h in range(N_HEADS)], axis=1)
    mla = (o * gm_ref[...]).astype(BF16)
    y = jnp.dot(mla, wo_ref[0:half, :], preferred_element_type=F32)
    y = y + jnp.dot(cv_ref[...].astype(BF16), wo_ref[half:, :], preferred_element_type=F32)
    y_ref[...] = x_ref[...] + _rms(y, gpost_ref[...])


def _out_sample(ol, gm, cv, x2d, wuv, wo, gpost):
    return pl.pallas_call(
        _out_sample_kernel,
        out_shape=jax.ShapeDtypeStruct(x2d.shape, F32),
        compiler_params=pltpu.CompilerParams(vmem_limit_bytes=VMEM_LIMIT_BYTES),
        name="out_sample",
    )(ol, gm, cv, x2d, wuv, wo, gpost)


def _rot_rows(w):
    r = w.shape[0] // 2
    return jnp.concatenate([-w[r:], w[:r]], axis=0)


def _pack_kernel(wint_ref, wuq_ref, wukv_ref, wout_ref,
                 w1_ref, wq_ref, wqt_ref, wk_ref, wvt_ref, wukt_ref, wuvt_ref, wo_ref):
    d = wint_ref.shape[1]
    blk = LANES

    def put_w1(col0, rows):
        w1_ref[:, col0:col0 + blk] = rows.T.astype(BF16)

    for r0 in range(0, C_KV1, blk):
        put_w1(r0, wint_ref[r0:r0 + blk, :])
    kpe = wint_ref[C_KV1:C_KV1 + QK_ROPE, :]
    put_w1(C_KP0, jnp.concatenate([jnp.zeros((QK_NOPE, d), F32), kpe, _rot_rows(kpe)], axis=0))
    src0 = C_KV1 + QK_ROPE
    for j in range(5 * GROUP // blk):
        put_w1(C_GM0 + j * blk, wint_ref[src0 + j * blk:src0 + (j + 1) * blk, :])

    uqt = wuq_ref[...].T
    ukvt = wukv_ref[...].T
    kv_lora = ukvt.shape[1]
    qk = QK_NOPE + QK_ROPE
    wqt_ref[...] = uqt.astype(BF16)
    for h in range(N_HEADS):
        pe = uqt[h * qk + QK_NOPE:(h + 1) * qk]
        q_slab = jnp.concatenate([uqt[h * qk:h * qk + QK_NOPE], pe, _rot_rows(pe)], axis=0)
        wq_ref[:, h * HEAD_SLAB:(h + 1) * HEAD_SLAB] = q_slab.T.astype(BF16)
        ukt = ukvt[h * (QK_NOPE + V_HEAD):h * (QK_NOPE + V_HEAD) + QK_NOPE]
        uvt = ukvt[h * (QK_NOPE + V_HEAD) + QK_NOPE:(h + 1) * (QK_NOPE + V_HEAD)]
        k_slab = jnp.concatenate([ukt, jnp.zeros((HEAD_SLAB - QK_NOPE, kv_lora), F32)], axis=0)
        wukt_ref[h] = k_slab.astype(BF16)
        wvt_ref[h * V_HEAD:(h + 1) * V_HEAD, :] = uvt.astype(BF16)
        wuvt_ref[h] = uvt.astype(BF16)
    for p in range(N_HEADS // 2):
        two = ukvt[2 * p * (QK_NOPE + V_HEAD):2 * (p + 1) * (QK_NOPE + V_HEAD)]
        pair = jnp.concatenate([two[0:QK_NOPE], two[QK_NOPE + V_HEAD:2 * QK_NOPE + V_HEAD]], axis=0)
        wk_ref[:, p * LANES:(p + 1) * LANES] = pair.T.astype(BF16)
    wo_ref[...] = wout_ref[...].astype(BF16)


def _pack_weights(w_in, w_uq, w_ukv, w_out):
    d = w_in.shape[0]
    q_lora, kv_lora = w_uq.shape[0], w_ukv.shape[0]
    assert w_in.shape[1] == C_KV1 + QK_ROPE + 5 * GROUP
    sds = lambda *s: jax.ShapeDtypeStruct(s, BF16)
    return pl.pallas_call(
        _pack_kernel,
        out_shape=(sds(d, D_IN_PACKED), sds(q_lora, N_HEADS * HEAD_SLAB), sds(w_uq.shape[1], q_lora),
                   sds(kv_lora, N_HEADS * QK_NOPE), sds(N_HEADS * V_HEAD, kv_lora),
                   sds(N_HEADS, HEAD_SLAB, kv_lora), sds(N_HEADS, V_HEAD, kv_lora), sds(*w_out.shape)),
        compiler_params=pltpu.CompilerParams(vmem_limit_bytes=VMEM_LIMIT_BYTES),
        name="pack_weights",
    )(w_in.T, w_uq, w_ukv, w_out)


PROMPT_TILE = 512
ATTN_TQ = 512
ATTN_TK = 512
DECODE_PAGES = 32
DECODE_SLOTS = 4
DECODE_PER_STEP = 1


def kernel(x_prompt, x_sample, cache_ckv, cache_kpe, state_conv, page_table, g_pre, w_in, g_qnorm, w_uq,
           g_kvnorm, w_ukv, w_conv, w_out, g_post):
    depth = w_in.shape[0]
    nb, seq, d = x_prompt.shape
    db, dec_seq, _ = x_sample.shape
    past_len = page_table.shape[1] * cache_ckv.shape[2]
    c = cache_ckv.shape[3]

    tabs_p = _rope_tables(jnp.arange(seq, dtype=jnp.int32))
    tabs_s = _rope_tables(jnp.tile(past_len + jnp.arange(dec_seq, dtype=jnp.int32), db))

    xp = x_prompt.reshape(nb * seq, d)
    xs = x_sample.reshape(db * dec_seq, d)
    outs = [[] for _ in range(6)]
    for l in range(depth):
        w1, wq, wqt, wk, wvt, wukt, wuvt, wo = _pack_weights(w_in[l], w_uq[l], w_ukv[l], w_out[l])
        gpre, gq, gkv, gpost = g_pre[l][None], g_qnorm[l][None], g_kvnorm[l][None], g_post[l][None]

        qt, k, vt, ckv_p, kpe_p, gm, cv, conv_p = _proj_prompt(
            xp, seq, gpre, w1, gq, wqt, gkv, wk, wvt, w_conv[l], tabs_p, PROMPT_TILE)
        xp = _attention_out(qt, k, vt, gm, cv, xp, wo, gpost, nb, seq, ATTN_TQ, ATTN_TK)
        outs[0].append(ckv_p.reshape(nb, seq, c))
        outs[1].append(jnp.swapaxes(kpe_p, 1, 2))
        outs[2].append(conv_p)

        st = state_conv[l].astype(F32)
        s0 = jnp.repeat(st[:, 0], dec_seq, axis=0)
        s1 = jnp.repeat(st[:, 1], dec_seq, axis=0)
        ql, qp, ckv_s, kpe_s, gm_s, cv_s, u_s = _proj_sample(
            xs, dec_seq, gpre, w1, gq, wq, gkv, wukt, w_conv[l], tabs_s[:4], s0, s1)
        rows = dec_seq * N_HEADS
        ol = _decode_attention(page_table, ql.reshape(db, rows, c), qp.reshape(db, rows, QK_ROPE),
                               ckv_s.reshape(db, dec_seq, c), kpe_s.reshape(db, dec_seq, QK_ROPE),
                               cache_ckv, cache_kpe, l, DECODE_PAGES, DECODE_SLOTS, DECODE_PER_STEP)
        xs = _out_sample(ol.reshape(db * dec_seq, N_HEADS, c), gm_s, cv_s, xs, wuvt, wo, gpost)
        outs[3].append(ckv_s.reshape(db, dec_seq, c))
        outs[4].append(kpe_s.reshape(db, dec_seq, QK_ROPE))
        u_pad = jnp.concatenate([st, u_s.reshape(db, dec_seq, GROUP)], axis=1)
        outs[5].append(u_pad[:, -(CONV_W - 1):])

    return (xp.reshape(nb, seq, d), xs.reshape(db, dec_seq, d), *[jnp.stack(o_) for o_ in outs])
```

```python
import functools

import jax
import jax.numpy as jnp
from jax import lax
from jax.experimental import pallas as pl
from jax.experimental.pallas import tpu as pltpu

N_HEADS = 8
QK_NOPE = 64
QK_ROPE = 32
V_HEAD = 64
CONV_W = 3
ROPE_THETA = 10000.0
EPS = 1e-6
ATTN_SCALE = (QK_NOPE + QK_ROPE) ** -0.5

LANES = 128
HEAD_SLAB = LANES
ONES_ROWS = 16
LOG2_E = 1.4426950408889634
VMEM_LIMIT_BYTES = 56 * 1024 * 1024

F32 = jnp.float32
BF16 = jnp.bfloat16


def _silu(x):
    return x * (1.0 / (1.0 + jnp.exp(-x)))


def _rms(x, g):
    return x * lax.rsqrt(jnp.mean(x * x, axis=-1, keepdims=True) + EPS) * g


def _rope_table_kernel(pos_ref, invf_ref, cq1_ref, cq2_ref, ck1_ref, ck2_ref, cq1t_ref, cq2t_ref):
    ang = invf_ref[...] * pos_ref[...]
    c, s = jnp.cos(ang), jnp.sin(ang)
    t = ang.shape[1]
    nope = jnp.ones((QK_NOPE, t), F32)
    zn = jnp.zeros((QK_NOPE, t), F32)
    zr = jnp.zeros((HEAD_SLAB - QK_NOPE - QK_ROPE, t), F32)

    def slab(first, mid):
        return jnp.concatenate([first, mid, zr], axis=0)

    cq1_ref[...] = slab(nope * ATTN_SCALE, c * ATTN_SCALE).T
    cq2_ref[...] = slab(zn, s * ATTN_SCALE).T
    ck1_ref[...] = slab(zn, c).T
    ck2_ref[...] = slab(zn, s).T
    cq1t_ref[...] = slab(nope * (ATTN_SCALE * LOG2_E), c * (ATTN_SCALE * LOG2_E))
    cq2t_ref[...] = slab(zn, s * (ATTN_SCALE * LOG2_E))


def _rope_tables(pos):
    t = pos.shape[0]
    r = QK_ROPE
    inv_freq = ROPE_THETA ** (-jnp.arange(0, r, 2, dtype=F32) / r)
    invf = jnp.concatenate([inv_freq, inv_freq]).reshape(r, 1)
    out = jax.ShapeDtypeStruct((t, LANES), F32)
    out_t = jax.ShapeDtypeStruct((LANES, t), F32)
    return pl.pallas_call(
        _rope_table_kernel,
        out_shape=(out, out, out, out, out_t, out_t),
        name="rope_tables",
    )(pos.astype(F32).reshape(1, t), invf)


C_Q0, C_Q1 = 0, 384
C_KV0, C_KV1 = 384, 640
C_KP0, C_KP1 = 640, 768
C_GM0 = 768
GROUP = 512
D_IN_PACKED = C_GM0 + 5 * GROUP


def _rope_slab(blk, c1, c2):
    return blk * c1 + pltpu.roll(blk, LANES - QK_ROPE, 1) * c2


def _project_common(x_ref, gpre_ref, w1_ref, gq_ref, gkv_ref, ck1_ref, ck2_ref):
    x = x_ref[...]
    xn = _rms(x, gpre_ref[...]).astype(BF16)

    def proj(c0, c1):
        return jnp.dot(xn, w1_ref[:, c0:c1], preferred_element_type=F32)

    cqn = _rms(proj(C_Q0, C_Q1), gq_ref[...]).astype(BF16)
    ckv = _rms(proj(C_KV0, C_KV1), gkv_ref[...])
    kpe_blk = _rope_slab(proj(C_KP0, C_KP1), ck1_ref[...], ck2_ref[...])
    return proj, cqn, ckv, kpe_blk


def _conv_gate(proj, um1, um2, u, wconv_ref):
    w = wconv_ref[...]
    y = w[0:1, :] * um2 + w[1:2, :] * um1 + w[2:3, :] * u
    bg = proj(C_GM0 + GROUP, C_GM0 + 2 * GROUP)
    gc = proj(C_GM0 + 4 * GROUP, C_GM0 + 5 * GROUP)
    return bg * y * _silu(gc)


def _proj_prompt_kernel(x_ref, gpre_ref, w1_ref, gq_ref, wqt_ref, gkv_ref, wk_ref, wvt_ref, wconv_ref,
                        cq1t_ref, cq2t_ref, ck1_ref, ck2_ref,
                        qt_out, k_out, vt_out, ckv_out, kpet_out, gm_out, cv_out, conv_out,
                        carry_ref, *, tiles_per_seq):
    tm = x_ref.shape[0]
    step = pl.program_id(0)
    dn = (((1,), (1,)), ((), ()))

    @pl.when(step % tiles_per_seq == 0)
    def _():
        carry_ref[...] = jnp.zeros_like(carry_ref)

    xn = _rms(x_ref[...], gpre_ref[...]).astype(BF16)

    def proj(c0, c1):
        return jnp.dot(xn, w1_ref[:, c0:c1], preferred_element_type=F32)

    z_cq = proj(C_Q0, C_Q1)
    z_ckv = proj(C_KV0, C_KV1)
    cqn = _rms(z_cq, gq_ref[...]).astype(BF16)
    qt = lax.dot_general(wqt_ref[...], cqn, dn, preferred_element_type=F32)
    ckv = _rms(z_ckv, gkv_ref[...])
    ckv_out[...] = ckv
    ckvb = ckv.astype(BF16)
    z_kp = proj(C_KP0, C_KP1)
    knope = jnp.dot(ckvb, wk_ref[...], preferred_element_type=F32)
    qk = QK_NOPE + QK_ROPE
    nope_scale = cq1t_ref[0:QK_NOPE, :]
    cos_t, sin_t = cq1t_ref[QK_NOPE:qk, :], cq2t_ref[QK_NOPE:qk, :]
    for h in range(N_HEADS):
        nope, pe = qt[h * qk:h * qk + QK_NOPE, :], qt[h * qk + QK_NOPE:(h + 1) * qk, :]
        qt_out[h * HEAD_SLAB:h * HEAD_SLAB + QK_NOPE, :] = (nope * nope_scale).astype(BF16)
        qt_out[h * HEAD_SLAB + QK_NOPE:h * HEAD_SLAB + qk, :] = (pe * cos_t + _rot_rows(pe) * sin_t).astype(BF16)
        qt_out[h * HEAD_SLAB + qk:(h + 1) * HEAD_SLAB, :] = jnp.zeros((HEAD_SLAB - qk, tm), BF16)
    vt = lax.dot_general(wvt_ref[...], ckvb, dn, preferred_element_type=F32)
    z_gm = proj(C_GM0, C_GM0 + GROUP)
    kpe_blk = _rope_slab(z_kp, ck1_ref[...], ck2_ref[...])
    kpet_out[0] = kpe_blk.T[QK_NOPE:QK_NOPE + QK_ROPE, :]
    lane = lax.broadcasted_iota(jnp.int32, kpe_blk.shape, 1)
    for p in range(N_HEADS // 2):
        two = knope[:, p * LANES:(p + 1) * LANES]
        for h, nope_first in ((2 * p, two), (2 * p + 1, pltpu.roll(two, LANES - QK_NOPE, 1))):
            k_out[:, h * HEAD_SLAB:(h + 1) * HEAD_SLAB] = jnp.where(lane < QK_NOPE, nope_first, kpe_blk).astype(BF16)
    vt_out[...] = vt.astype(BF16)
    z_cg = proj(C_GM0 + 2 * GROUP, C_GM0 + 3 * GROUP)
    z_h = proj(C_GM0 + 3 * GROUP, C_GM0 + 4 * GROUP)
    gm_out[...] = _silu(z_gm).astype(BF16)
    z_bg = proj(C_GM0 + GROUP, C_GM0 + 2 * GROUP)
    z_gc = proj(C_GM0 + 4 * GROUP, C_GM0 + 5 * GROUP)

    u = z_cg * z_h
    prev = carry_ref[...]
    p1, p2 = prev[7:8, :], prev[6:7, :]
    row = lax.broadcasted_iota(jnp.int32, u.shape, 0)
    um1 = jnp.where(row == 0, p1, pltpu.roll(u, 1, 0))
    um2 = jnp.where(row == 0, p2, jnp.where(row == 1, p1, pltpu.roll(u, 2, 0)))
    w = wconv_ref[...]
    y = w[0:1, :] * um2 + w[1:2, :] * um1 + w[2:3, :] * u
    cv_out[...] = (z_bg * y * _silu(z_gc)).astype(BF16)
    carry_ref[...] = u[tm - 8:tm, :]
    conv_out[0] = u[tm - (CONV_W - 1):tm, :]


def _const_spec(shape):
    nd = len(shape)
    return pl.BlockSpec(shape, lambda *_: (0,) * nd, pipeline_mode=pl.Buffered(1))


def _proj_prompt(x2d, seq, gpre, w1, gq, wqt, gkv, wk, wvt, wconv, tabs, tm):
    t, d = x2d.shape
    nb = t // seq
    hs = N_HEADS * HEAD_SLAB
    hv = N_HEADS * V_HEAD
    cq1, cq2, ck1, ck2, cq1t, cq2t = tabs
    row = lambda w: pl.BlockSpec((tm, w), lambda i: (i, 0))
    col = lambda h: pl.BlockSpec((h, tm), lambda i: (0, i))
    tab = pl.BlockSpec((tm, LANES), lambda i: (i % (seq // tm), 0))
    tab_t = pl.BlockSpec((LANES, tm), lambda i: (0, i % (seq // tm)))
    out_shape = (
        jax.ShapeDtypeStruct((hs, t), BF16),
        jax.ShapeDtypeStruct((t, hs), BF16),
        jax.ShapeDtypeStruct((hv, t), BF16),
        jax.ShapeDtypeStruct((t, gkv.shape[1]), F32),
        jax.ShapeDtypeStruct((nb, QK_ROPE, seq), F32),
        jax.ShapeDtypeStruct((t, GROUP), BF16),
        jax.ShapeDtypeStruct((t, GROUP), BF16),
        jax.ShapeDtypeStruct((nb, CONV_W - 1, GROUP), F32),
    )
    return pl.pallas_call(
        functools.partial(_proj_prompt_kernel, tiles_per_seq=seq // tm),
        grid=(t // tm,),
        in_specs=[row(d), _const_spec(gpre.shape), _const_spec(w1.shape), _const_spec(gq.shape),
                  _const_spec(wqt.shape), _const_spec(gkv.shape), _const_spec(wk.shape), _const_spec(wvt.shape),
                  _const_spec(wconv.shape), tab_t, tab_t, tab, tab],
        out_specs=(col(hs), row(hs), col(hv), row(gkv.shape[1]),
                   pl.BlockSpec((1, QK_ROPE, tm), lambda i: (i // (seq // tm), 0, i % (seq // tm))),
                   row(GROUP), row(GROUP),
                   pl.BlockSpec((1, CONV_W - 1, GROUP), lambda i: (i // (seq // tm), 0, 0))),
        out_shape=out_shape,
        scratch_shapes=[pltpu.VMEM((8, GROUP), F32)],
        compiler_params=pltpu.CompilerParams(dimension_semantics=("arbitrary",),
                                             vmem_limit_bytes=VMEM_LIMIT_BYTES),
        name="proj_prompt",
    )(x2d, gpre, w1, gq, wqt, gkv, wk, wvt, wconv, cq1t, cq2t, ck1, ck2)


def _attn_kernel(qt_ref, k_ref, vt_ref, gm_ref, cv_ref, x_ref, wo_ref, gpost_ref, y_ref,
                 s_ref, m_ref, acc_ref, *, tq, tk):
    i = pl.program_id(1)
    hq = tq // 2
    tri = (lax.broadcasted_iota(jnp.int32, (hq, hq), 0) <= lax.broadcasted_iota(jnp.int32, (hq, hq), 1))

    def slab(h):
        return slice(h * HEAD_SLAB, (h + 1) * HEAD_SLAB)

    def values(h, toks):
        ones = jnp.ones((ONES_ROWS, toks.size), BF16)
        return jnp.concatenate([vt_ref[h * V_HEAD:(h + 1) * V_HEAD, toks], ones], 0)

    def scores(h, toks):
        return jnp.dot(k_ref[toks, slab(h)], qt_ref[slab(h), :], preferred_element_type=F32)

    def update(h, toks, st):
        m = m_ref[h]
        m_new = jnp.maximum(m, jnp.max(st, axis=0, keepdims=True))
        pt = jnp.exp2(st - m_new).astype(BF16)
        m_ref[h] = m_new
        acc_ref[h] = jnp.exp2(m - m_new) * acc_ref[h] + jnp.dot(values(h, toks), pt, preferred_element_type=F32)

    base = pl.multiple_of(i * tq, tq)
    ka, kb = pl.ds(base, hq), pl.ds(base + hq, hq)

    def diagonal_scores(h):
        return (jnp.dot(k_ref[ka, slab(h)], qt_ref[slab(h), :], preferred_element_type=F32),
                jnp.dot(k_ref[kb, slab(h)], qt_ref[slab(h), hq:], preferred_element_type=F32))

    def diagonal_init(h, a, b_):
        a_lo = jnp.where(tri, a[:, :hq], -jnp.inf)
        a_hi = a[:, hq:]
        b_ = jnp.where(tri, b_, -jnp.inf)
        m_lo = jnp.max(a_lo, axis=0, keepdims=True)
        m_hi = jnp.maximum(jnp.max(a_hi, axis=0, keepdims=True), jnp.max(b_, axis=0, keepdims=True))
        pa = jnp.concatenate([jnp.exp2(a_lo - m_lo), jnp.exp2(a_hi - m_hi)], axis=1).astype(BF16)
        pb = jnp.exp2(b_ - m_hi).astype(BF16)
        acc_a = jnp.dot(values(h, ka), pa, preferred_element_type=F32)
        acc_b = jnp.dot(values(h, kb), pb, preferred_element_type=F32)
        m_ref[h] = jnp.concatenate([m_lo, m_hi], axis=1)
        acc_ref[h] = jnp.concatenate([acc_a[:, :hq], acc_a[:, hq:] + acc_b], axis=1)

    n_full = i * (tq // tk)

    def tile_tokens(j):
        return pl.ds(pl.multiple_of(j * tk, tk), tk)

    cur = diagonal_scores(0)
    for h in range(N_HEADS):
        if h + 1 < N_HEADS:
            ahead = diagonal_scores(h + 1)
        else:
            s_ref[0] = scores(0, tile_tokens(0))
        diagonal_init(h, *cur)
        cur = ahead

    def full_tile(j, carry):
        toks = tile_tokens(j)
        nxt = tile_tokens(jnp.minimum(j + 1, n_full - 1))
        for h in range(N_HEADS):
            if h + 1 < N_HEADS:
                s_ref[h + 1] = scores(h + 1, toks)
                update(h, toks, s_ref[h])
            else:
                cur = s_ref[h]
                s_ref[0] = scores(0, nxt)
                update(h, toks, cur)
        return carry

    lax.fori_loop(0, n_full, full_tile, 0)
    outs = [acc_ref[h, :V_HEAD] * (1.0 / acc_ref[h, V_HEAD:V_HEAD + 1]) for h in range(N_HEADS)]
    pairs = [jnp.concatenate(outs[2 * p:2 * p + 2], axis=0).T for p in range(N_HEADS // 2)]

    half = gm_ref.shape[1]
    mla = (jnp.concatenate(pairs, axis=1) * gm_ref[...].astype(F32)).astype(BF16)
    y = jnp.dot(mla, wo_ref[0:half, :], preferred_element_type=F32)
    y = y + jnp.dot(cv_ref[...], wo_ref[half:, :], preferred_element_type=F32)
    y_ref[...] = x_ref[...] + _rms(y, gpost_ref[...])


def _attention_out(qt, k, vt, gm, cv, x2d, wo, gpost, nb, seq, tq, tk):
    hs, t = qt.shape
    hv = vt.shape[0]
    d = x2d.shape[1]
    assert N_HEADS % 2 == 0 and tq % tk == 0
    row = lambda w: pl.BlockSpec((tq, w), lambda b, i: (b * (seq // tq) + i, 0))
    return pl.pallas_call(
        functools.partial(_attn_kernel, tq=tq, tk=tk),
        grid=(nb, seq // tq),
        in_specs=[pl.BlockSpec((hs, tq), lambda b, i: (0, b * (seq // tq) + i)),
                  pl.BlockSpec((seq, hs), lambda b, i: (b, 0)),
                  pl.BlockSpec((hv, seq), lambda b, i: (0, b)),
                  row(gm.shape[1]), row(cv.shape[1]), row(d), _const_spec(wo.shape), _const_spec(gpost.shape)],
        out_specs=row(d),
        out_shape=jax.ShapeDtypeStruct((t, d), F32),
        scratch_shapes=[pltpu.VMEM((N_HEADS, tk, tq), F32), pltpu.VMEM((N_HEADS, 1, tq), F32),
                        pltpu.VMEM((N_HEADS, V_HEAD + ONES_ROWS, tq), F32)],
        compiler_params=pltpu.CompilerParams(dimension_semantics=("arbitrary", "arbitrary"),
                                             vmem_limit_bytes=VMEM_LIMIT_BYTES),
        name="prompt_attn",
    )(qt, k, vt, gm, cv, x2d, wo, gpost)


def _proj_sample_kernel(x_ref, gpre_ref, w1_ref, gq_ref, wq_ref, gkv_ref, wukt_ref, wconv_ref,
                        cq1_ref, cq2_ref, ck1_ref, ck2_ref, s0_ref, s1_ref,
                        ql_out, qp_out, ckv_out, kpe_out, gm_out, cv_out, conv_out, *, dec_seq):
    proj, cqn, ckv, kpe_blk = _project_common(x_ref, gpre_ref, w1_ref, gq_ref, gkv_ref, ck1_ref, ck2_ref)
    qraw = jnp.dot(cqn, wq_ref[...], preferred_element_type=F32)
    cq1, cq2 = cq1_ref[...], cq2_ref[...]
    for h in range(N_HEADS):
        qh = _rope_slab(qraw[:, h * HEAD_SLAB:(h + 1) * HEAD_SLAB], cq1, cq2)
        ql_out[:, h, :] = jnp.dot(qh.astype(BF16), wukt_ref[h], preferred_element_type=F32)
        qp_out[:, h, :] = pltpu.roll(qh, LANES - QK_NOPE, 1)[:, 0:QK_ROPE]
    ckv_out[...] = ckv
    kpe_out[...] = kpe_blk[:, QK_NOPE:QK_NOPE + QK_ROPE]
    gm_out[...] = _silu(proj(C_GM0, C_GM0 + GROUP))

    u = proj(C_GM0 + 2 * GROUP, C_GM0 + 3 * GROUP) * proj(C_GM0 + 3 * GROUP, C_GM0 + 4 * GROUP)
    t_in_seq = lax.broadcasted_iota(jnp.int32, u.shape, 0) % dec_seq
    s0, s1 = s0_ref[...], s1_ref[...]
    um1 = jnp.where(t_in_seq == 0, s1, pltpu.roll(u, 1, 0))
    um2 = jnp.where(t_in_seq == 0, s0, jnp.where(t_in_seq == 1, s1, pltpu.roll(u, 2, 0)))
    cv_out[...] = _conv_gate(proj, um1, um2, u, wconv_ref)
    conv_out[...] = u


def _proj_sample(x2d, dec_seq, gpre, w1, gq, wq, gkv, wukt, wconv, tabs, s0, s1):
    t, d = x2d.shape
    c = gkv.shape[1]
    out_shape = (
        jax.ShapeDtypeStruct((t, N_HEADS, c), F32),
        jax.ShapeDtypeStruct((t, N_HEADS, QK_ROPE), F32),
        jax.ShapeDtypeStruct((t, c), F32),
        jax.ShapeDtypeStruct((t, QK_ROPE), F32),
        jax.ShapeDtypeStruct((t, GROUP), F32),
        jax.ShapeDtypeStruct((t, GROUP), F32),
        jax.ShapeDtypeStruct((t, GROUP), F32),
    )
    return pl.pallas_call(
        functools.partial(_proj_sample_kernel, dec_seq=dec_seq),
        out_shape=out_shape,
        compiler_params=pltpu.CompilerParams(vmem_limit_bytes=VMEM_LIMIT_BYTES),
        name="proj_sample",
    )(x2d, gpre, w1, gq, wq, gkv, wukt, wconv, *tabs, s0, s1)


def _decode_attn_kernel(pt_ref, ql_ref, qp_ref, cnew_ref, knew_ref, y_hbm, x_hbm, o_ref,
                        ybuf, xbuf, sem, m_ref, l_ref, acc_ref, *, pages, slots, layer, dec_seq):
    g = pl.program_id(0)
    n_g = pl.num_programs(0)
    per_step = ql_ref.shape[0]
    n_chunks = pt_ref.shape[1] // pages
    total = per_step * n_chunks
    page = ybuf.shape[1] // pages
    rows = ql_ref.shape[1]
    dn = (((1,), (1,)), ((), ()))
    ahead = slots - 1

    def chunk_copies(bb, ch, slot, page_ids):
        cps = []
        for i in range(pages):
            pid = page_ids(bb, ch * pages + i)
            tok = pl.ds(i * page, page)
            cps.append(pltpu.make_async_copy(y_hbm.at[layer, pid], ybuf.at[slot, tok, :], sem.at[0, slot]))
            cps.append(pltpu.make_async_copy(x_hbm.at[layer, pid], xbuf.at[slot, :, tok], sem.at[1, slot]))
        return cps

    def start(bb, ch, slot):
        for n, cp in enumerate(chunk_copies(bb, ch, slot, lambda r, j: pt_ref[r, j])):
            cp.start(priority=n % 2)

    def wait(slot):
        for cp in chunk_copies(0, 0, slot, lambda r, j: 0):
            cp.wait()

    def start_nth(step, k, slot):
        start(step * per_step + k // n_chunks, k % n_chunks, slot)

    @pl.when(g == 0)
    def _():
        for k in range(ahead):
            start_nth(0, k, k % slots)

    tok_of_row = lax.broadcasted_iota(jnp.int32, (rows, 1), 0) // N_HEADS
    for e in range(per_step):
        qlf, qpf = ql_ref[e], qp_ref[e]
        s_new = []
        for j in range(dec_seq):
            sj = (jnp.sum(qlf * cnew_ref[e, j:j + 1, :], axis=-1, keepdims=True)
                  + jnp.sum(qpf * knew_ref[e, j:j + 1, :], axis=-1, keepdims=True))
            s_new.append(jnp.where(tok_of_row >= j, sj, -jnp.inf))
        m_n = functools.reduce(jnp.maximum, s_new)
        p_n = [jnp.exp(sj - m_n) for sj in s_new]
        m_ref[e] = m_n
        l_ref[e] = functools.reduce(lambda a, v: a + v, p_n)
        acc_ref[e, 0] = functools.reduce(lambda a, v: a + v,
                                         [p_n[j] * cnew_ref[e, j:j + 1, :] for j in range(dec_seq)])
        acc_ref[e, 1] = jnp.zeros(acc_ref.shape[2:], F32)

    half = pages * page // 2
    toks = [slice(h * half, (h + 1) * half) for h in range(2)]

    def scores(e, slot):
        ql, qp = ql_ref[e].astype(BF16), qp_ref[e].astype(BF16)
        yb = [ybuf[slot, t, :].astype(BF16) for t in toks]
        xb = [xbuf[slot, :, t].astype(BF16) for t in toks]
        s = jnp.concatenate(
            [lax.dot_general(ql, yb[h], dn, preferred_element_type=F32)
             + jnp.dot(qp, xb[h], preferred_element_type=F32) for h in range(2)], axis=1)
        return e, s, yb

    def softmax(e, s, yb):
        m = m_ref[e]
        m_new = jnp.maximum(m, jnp.max(s, axis=-1, keepdims=True))
        alpha = jnp.exp(m - m_new)
        p = jnp.exp(s - m_new)
        l_ref[e] = alpha * l_ref[e] + jnp.sum(p, axis=-1, keepdims=True)
        m_ref[e] = m_new
        return e, alpha, p.astype(BF16), yb

    def values(e, alpha, p, yb):
        for h in range(2):
            acc_ref[e, h] = alpha * acc_ref[e, h] + jnp.dot(p[:, toks[h]], yb[h], preferred_element_type=F32)

    scored = None
    weighted = None
    for k in range(total):
        nxt = k + ahead
        if nxt < total:
            start_nth(g, nxt, nxt % slots)
        else:
            @pl.when(g + 1 < n_g)
            def _():
                start_nth(g + 1, nxt - total, nxt % slots)

        wait(k % slots)
        fresh = scores(k // n_chunks, k % slots)
        if weighted is not None:
            values(*weighted)
        if scored is not None:
            weighted = softmax(*scored)
        scored = fresh
    if weighted is not None:
        values(*weighted)
    values(*softmax(*scored))
    for e in range(per_step):
        o_ref[e] = (acc_ref[e, 0] + acc_ref[e, 1]) * (1.0 / l_ref[e])


def _decode_attention(page_table, ql, qp, cnew, knew, cache_ckv, cache_kpe, layer, pages, slots, per_step):
    nb, rows, c = ql.shape
    dec_seq = cnew.shape[1]
    page, r = cache_kpe.shape[2:]
    n_chunks = page_table.shape[1] // pages
    assert n_chunks * pages == page_table.shape[1] and nb % per_step == 0
    assert (per_step * n_chunks) % slots == 0 and slots - 1 <= per_step * n_chunks
    kpe_t = jnp.swapaxes(cache_kpe, 2, 3)
    per_b = lambda a: pl.BlockSpec((per_step,) + a.shape[1:], lambda g, pt: (g, 0, 0))
    grid_spec = pltpu.PrefetchScalarGridSpec(
        num_scalar_prefetch=1,
        grid=(nb // per_step,),
        in_specs=[per_b(ql), per_b(qp), per_b(cnew), per_b(knew),
                  pl.BlockSpec(memory_space=pl.ANY), pl.BlockSpec(memory_space=pl.ANY)],
        out_specs=pl.BlockSpec((per_step, rows, c), lambda g, pt: (g, 0, 0)),
        scratch_shapes=[pltpu.VMEM((slots, pages * page, c), F32), pltpu.VMEM((slots, r, pages * page), F32),
                        pltpu.SemaphoreType.DMA((2, slots)),
                        pltpu.VMEM((per_step, rows, 1), F32), pltpu.VMEM((per_step, rows, 1), F32),
                        pltpu.VMEM((per_step, 2, rows, c), F32)],
    )
    return pl.pallas_call(
        functools.partial(_decode_attn_kernel, pages=pages, slots=slots, layer=layer, dec_seq=dec_seq),
        grid_spec=grid_spec,
        out_shape=jax.ShapeDtypeStruct((nb, rows, c), F32),
        compiler_params=pltpu.CompilerParams(dimension_semantics=("arbitrary",),
                                             vmem_limit_bytes=VMEM_LIMIT_BYTES),
        name="decode_attn",
    )(page_table, ql, qp, cnew, knew, cache_ckv, kpe_t)


def _out_sample_kernel(ol_ref, gm_ref, cv_ref, x_ref, wuvt_ref, wo_ref, gpost_ref, y_ref):
    half = gm_ref.shape[1]
    dn = (((1,), (1,)), ((), ()))
    o = jnp.concatenate(
        [lax.dot_general(ol_ref[:, h, :].astype(BF16), wuvt_ref[h], dn, preferred_element_type=F32)
         for h in range(N_HEADS)], axis=1)
    mla = (o * gm_ref[...]).astype(BF16)
    y = jnp.dot(mla, wo_ref[0:half, :], preferred_element_type=F32)
    y = y + jnp.dot(cv_ref[...].astype(BF16), wo_ref[half:, :], preferred_element_type=F32)
    y_ref[...] = x_ref[...] + _rms(y, gpost_ref[...])


def _out_sample(ol, gm, cv, x2d, wuv, wo, gpost):
    return pl.pallas_call(
        _out_sample_kernel,
        out_shape=jax.ShapeDtypeStruct(x2d.shape, F32),
        compiler_params=pltpu.CompilerParams(vmem_limit_bytes=VMEM_LIMIT_BYTES),
        name="out_sample",
    )(ol, gm, cv, x2d, wuv, wo, gpost)


def _rot_rows(w):
    r = w.shape[0] // 2
    return jnp.concatenate([-w[r:], w[:r]], axis=0)


def _pack_kernel(wint_ref, wuq_ref, wukv_ref, wout_ref,
                 w1_ref, wq_ref, wqt_ref, wk_ref, wvt_ref, wukt_ref, wuvt_ref, wo_ref):
    d = wint_ref.shape[1]
    blk = LANES

    def put_w1(col0, rows):
        w1_ref[:, col0:col0 + blk] = rows.T.astype(BF16)

    for r0 in range(0, C_KV1, blk):
        put_w1(r0, wint_ref[r0:r0 + blk, :])
    kpe = wint_ref[C_KV1:C_KV1 + QK_ROPE, :]
    put_w1(C_KP0, jnp.concatenate([jnp.zeros((QK_NOPE, d), F32), kpe, _rot_rows(kpe)], axis=0))
    src0 = C_KV1 + QK_ROPE
    for j in range(5 * GROUP // blk):
        put_w1(C_GM0 + j * blk, wint_ref[src0 + j * blk:src0 + (j + 1) * blk, :])

    uqt = wuq_ref[...].T
    ukvt = wukv_ref[...].T
    kv_lora = ukvt.shape[1]
    qk = QK_NOPE + QK_ROPE
    wqt_ref[...] = uqt.astype(BF16)
    for h in range(N_HEADS):
        pe = uqt[h * qk + QK_NOPE:(h + 1) * qk]
        q_slab = jnp.concatenate([uqt[h * qk:h * qk + QK_NOPE], pe, _rot_rows(pe)], axis=0)
        wq_ref[:, h * HEAD_SLAB:(h + 1) * HEAD_SLAB] = q_slab.T.astype(BF16)
        ukt = ukvt[h * (QK_NOPE + V_HEAD):h * (QK_NOPE + V_HEAD) + QK_NOPE]
        uvt = ukvt[h * (QK_NOPE + V_HEAD) + QK_NOPE:(h + 1) * (QK_NOPE + V_HEAD)]
        k_slab = jnp.concatenate([ukt, jnp.zeros((HEAD_SLAB - QK_NOPE, kv_lora), F32)], axis=0)
        wukt_ref[h] = k_slab.astype(BF16)
        wvt_ref[h * V_HEAD:(h + 1) * V_HEAD, :] = uvt.astype(BF16)
        wuvt_ref[h] = uvt.astype(BF16)
    for p in range(N_HEADS // 2):
        two = ukvt[2 * p * (QK_NOPE + V_HEAD):2 * (p + 1) * (QK_NOPE + V_HEAD)]
        pair = jnp.concatenate([two[0:QK_NOPE], two[QK_NOPE + V_HEAD:2 * QK_NOPE + V_HEAD]], axis=0)
        wk_ref[:, p * LANES:(p + 1) * LANES] = pair.T.astype(BF16)
    wo_ref[...] = wout_ref[...].astype(BF16)


def _pack_weights(w_in, w_uq, w_ukv, w_out):
    d = w_in.shape[0]
    q_lora, kv_lora = w_uq.shape[0], w_ukv.shape[0]
    assert w_in.shape[1] == C_KV1 + QK_ROPE + 5 * GROUP
    sds = lambda *s: jax.ShapeDtypeStruct(s, BF16)
    return pl.pallas_call(
        _pack_kernel,
        out_shape=(sds(d, D_IN_PACKED), sds(q_lora, N_HEADS * HEAD_SLAB), sds(w_uq.shape[1], q_lora),
                   sds(kv_lora, N_HEADS * QK_NOPE), sds(N_HEADS * V_HEAD, kv_lora),
                   sds(N_HEADS, HEAD_SLAB, kv_lora), sds(N_HEADS, V_HEAD, kv_lora), sds(*w_out.shape)),
        compiler_params=pltpu.CompilerParams(vmem_limit_bytes=VMEM_LIMIT_BYTES),
        name="pack_weights",
    )(w_in.T, w_uq, w_ukv, w_out)


PROMPT_TILE = 512
ATTN_TQ = 512
ATTN_TK = 512
DECODE_PAGES = 32
DECODE_SLOTS = 4
DECODE_PER_STEP = 1


def kernel(x_prompt, x_sample, cache_ckv, cache_kpe, state_conv, page_table, g_pre, w_in, g_qnorm, w_uq,
           g_kvnorm, w_ukv, w_conv, w_out, g_post):
    depth = w_in.shape[0]
    nb, seq, d = x_prompt.shape
    db, dec_seq, _ = x_sample.shape
    past_len = page_table.shape[1] * cache_ckv.shape[2]
    c = cache_ckv.shape[3]

    tabs_p = _rope_tables(jnp.arange(seq, dtype=jnp.int32))
    tabs_s = _rope_tables(jnp.tile(past_len + jnp.arange(dec_seq, dtype=jnp.int32), db))

    xp = x_prompt.reshape(nb * seq, d)
    xs = x_sample.reshape(db * dec_seq, d)
    outs = [[] for _ in range(6)]
    for l in range(depth):
        w1, wq, wqt, wk, wvt, wukt, wuvt, wo = _pack_weights(w_in[l], w_uq[l], w_ukv[l], w_out[l])
        gpre, gq, gkv, gpost = g_pre[l][None], g_qnorm[l][None], g_kvnorm[l][None], g_post[l][None]

        qt, k, vt, ckv_p, kpe_p, gm, cv, conv_p = _proj_prompt(
            xp, seq, gpre, w1, gq, wqt, gkv, wk, wvt, w_conv[l], tabs_p, PROMPT_TILE)
        xp = _attention_out(qt, k, vt, gm, cv, xp, wo, gpost, nb, seq, ATTN_TQ, ATTN_TK)
        outs[0].append(ckv_p.reshape(nb, seq, c))
        outs[1].append(jnp.swapaxes(kpe_p, 1, 2))
        outs[2].append(conv_p)

        st = state_conv[l].astype(F32)
        s0 = jnp.repeat(st[:, 0], dec_seq, axis=0)
        s1 = jnp.repeat(st[:, 1], dec_seq, axis=0)
        ql, qp, ckv_s, kpe_s, gm_s, cv_s, u_s = _proj_sample(
            xs, dec_seq, gpre, w1, gq, wq, gkv, wukt, w_conv[l], tabs_s[:4], s0, s1)
        rows = dec_seq * N_HEADS
        ol = _decode_attention(page_table, ql.reshape(db, rows, c), qp.reshape(db, rows, QK_ROPE),
                               ckv_s.reshape(db, dec_seq, c), kpe_s.reshape(db, dec_seq, QK_ROPE),
                               cache_ckv, cache_kpe, l, DECODE_PAGES, DECODE_SLOTS, DECODE_PER_STEP)
        xs = _out_sample(ol.reshape(db * dec_seq, N_HEADS, c), gm_s, cv_s, xs, wuvt, wo, gpost)
        outs[3].append(ckv_s.reshape(db, dec_seq, c))
        outs[4].append(kpe_s.reshape(db, dec_seq, QK_ROPE))
        u_pad = jnp.concatenate([st, u_s.reshape(db, dec_seq, GROUP)], axis=1)
        outs[5].append(u_pad[:, -(CONV_W - 1):])

    return (xp.reshape(nb, seq, d), xs.reshape(db, dec_seq, d), *[jnp.stack(o_) for o_ in outs])
```

```python
import functools

import jax
import jax.numpy as jnp
from jax import lax
from jax.experimental import pallas as pl
from jax.experimental.pallas import tpu as pltpu

N_HEADS = 8
QK_NOPE = 64
QK_ROPE = 32
V_HEAD = 64
CONV_W = 3
ROPE_THETA = 10000.0
EPS = 1e-6
ATTN_SCALE = (QK_NOPE + QK_ROPE) ** -0.5

LANES = 128
HEAD_SLAB = LANES
ONES_ROWS = 16
LOG2_E = 1.4426950408889634
VMEM_LIMIT_BYTES = 56 * 1024 * 1024

F32 = jnp.float32
BF16 = jnp.bfloat16


def _silu(x):
    return x * (1.0 / (1.0 + jnp.exp(-x)))


def _rms(x, g):
    return x * lax.rsqrt(jnp.mean(x * x, axis=-1, keepdims=True) + EPS) * g


def _rope_table_kernel(pos_ref, invf_ref, cq1_ref, cq2_ref, ck1_ref, ck2_ref, cq1t_ref, cq2t_ref):
    ang = invf_ref[...] * pos_ref[...]
    c, s = jnp.cos(ang), jnp.sin(ang)
    t = ang.shape[1]
    nope = jnp.ones((QK_NOPE, t), F32)
    zn = jnp.zeros((QK_NOPE, t), F32)
    zr = jnp.zeros((HEAD_SLAB - QK_NOPE - QK_ROPE, t), F32)

    def slab(first, mid):
        return jnp.concatenate([first, mid, zr], axis=0)

    cq1_ref[...] = slab(nope * ATTN_SCALE, c * ATTN_SCALE).T
    cq2_ref[...] = slab(zn, s * ATTN_SCALE).T
    ck1_ref[...] = slab(zn, c).T
    ck2_ref[...] = slab(zn, s).T
    cq1t_ref[...] = slab(nope * (ATTN_SCALE * LOG2_E), c * (ATTN_SCALE * LOG2_E))
    cq2t_ref[...] = slab(zn, s * (ATTN_SCALE * LOG2_E))


def _rope_tables(pos):
    t = pos.shape[0]
    r = QK_ROPE
    inv_freq = ROPE_THETA ** (-jnp.arange(0, r, 2, dtype=F32) / r)
    invf = jnp.concatenate([inv_freq, inv_freq]).reshape(r, 1)
    out = jax.ShapeDtypeStruct((t, LANES), F32)
    out_t = jax.ShapeDtypeStruct((LANES, t), F32)
    return pl.pallas_call(
        _rope_table_kernel,
        out_shape=(out, out, out, out, out_t, out_t),
        name="rope_tables",
    )(pos.astype(F32).reshape(1, t), invf)


C_Q0, C_Q1 = 0, 384
C_KV0, C_KV1 = 384, 640
C_KP0, C_KP1 = 640, 768
C_GM0 = 768
GROUP = 512
D_IN_PACKED = C_GM0 + 5 * GROUP


def _rope_slab(blk, c1, c2):
    return blk * c1 + pltpu.roll(blk, LANES - QK_ROPE, 1) * c2


def _project_common(x_ref, gpre_ref, w1_ref, gq_ref, gkv_ref, ck1_ref, ck2_ref):
    x = x_ref[...]
    xn = _rms(x, gpre_ref[...]).astype(BF16)

    def proj(c0, c1):
        return jnp.dot(xn, w1_ref[:, c0:c1], preferred_element_type=F32)

    cqn = _rms(proj(C_Q0, C_Q1), gq_ref[...]).astype(BF16)
    ckv = _rms(proj(C_KV0, C_KV1), gkv_ref[...])
    kpe_blk = _rope_slab(proj(C_KP0, C_KP1), ck1_ref[...], ck2_ref[...])
    return proj, cqn, ckv, kpe_blk


def _conv_gate(proj, um1, um2, u, wconv_ref):
    w = wconv_ref[...]
    y = w[0:1, :] * um2 + w[1:2, :] * um1 + w[2:3, :] * u
    bg = proj(C_GM0 + GROUP, C_GM0 + 2 * GROUP)
    gc = proj(C_GM0 + 4 * GROUP, C_GM0 + 5 * GROUP)
    return bg * y * _silu(gc)


def _proj_prompt_kernel(x_ref, gpre_ref, w1_ref, gq_ref, wqt_ref, gkv_ref, wk_ref, wvt_ref, wconv_ref,
                        cq1t_ref, cq2t_ref, ck1_ref, ck2_ref,
                        qt_out, k_out, vt_out, ckv_out, kpet_out, gm_out, cv_out, conv_out,
                        carry_ref, *, tiles_per_seq):
    tm = x_ref.shape[0]
    step = pl.program_id(0)
    dn = (((1,), (1,)), ((), ()))

    @pl.when(step % tiles_per_seq == 0)
    def _():
        carry_ref[...] = jnp.zeros_like(carry_ref)

    xn = _rms(x_ref[...], gpre_ref[...]).astype(BF16)

    def proj(c0, c1):
        return jnp.dot(xn, w1_ref[:, c0:c1], preferred_element_type=F32)

    z_cq = proj(C_Q0, C_Q1)
    z_ckv = proj(C_KV0, C_KV1)
    cqn = _rms(z_cq, gq_ref[...]).astype(BF16)
    qt = lax.dot_general(wqt_ref[...], cqn, dn, preferred_element_type=F32)
    ckv = _rms(z_ckv, gkv_ref[...])
    ckv_out[...] = ckv
    ckvb = ckv.astype(BF16)
    z_kp = proj(C_KP0, C_KP1)
    knope = jnp.dot(ckvb, wk_ref[...], preferred_element_type=F32)
    qk = QK_NOPE + QK_ROPE
    nope_scale = cq1t_ref[0:QK_NOPE, :]
    cos_t, sin_t = cq1t_ref[QK_NOPE:qk, :], cq2t_ref[QK_NOPE:qk, :]
    for h in range(N_HEADS):
        nope, pe = qt[h * qk:h * qk + QK_NOPE, :], qt[h * qk + QK_NOPE:(h + 1) * qk, :]
        qt_out[h * HEAD_SLAB:h * HEAD_SLAB + QK_NOPE, :] = (nope * nope_scale).astype(BF16)
        qt_out[h * HEAD_SLAB + QK_NOPE:h * HEAD_SLAB + qk, :] = (pe * cos_t + _rot_rows(pe) * sin_t).astype(BF16)
        qt_out[h * HEAD_SLAB + qk:(h + 1) * HEAD_SLAB, :] = jnp.zeros((HEAD_SLAB - qk, tm), BF16)
    vt = lax.dot_general(wvt_ref[...], ckvb, dn, preferred_element_type=F32)
    z_gm = proj(C_GM0, C_GM0 + GROUP)
    kpe_blk = _rope_slab(z_kp, ck1_ref[...], ck2_ref[...])
    kpet_out[0] = kpe_blk.T[QK_NOPE:QK_NOPE + QK_ROPE, :]
    lane = lax.broadcasted_iota(jnp.int32, kpe_blk.shape, 1)
    for p in range(N_HEADS // 2):
        two = knope[:, p * LANES:(p + 1) * LANES]
        for h, nope_first in ((2 * p, two), (2 * p + 1, pltpu.roll(two, LANES - QK_NOPE, 1))):
            k_out[:, h * HEAD_SLAB:(h + 1) * HEAD_SLAB] = jnp.where(lane < QK_NOPE, nope_first, kpe_blk).astype(BF16)
    vt_out[...] = vt.astype(BF16)
    z_cg = proj(C_GM0 + 2 * GROUP, C_GM0 + 3 * GROUP)
    z_h = proj(C_GM0 + 3 * GROUP, C_GM0 + 4 * GROUP)
    gm_out[...] = _silu(z_gm).astype(BF16)
    z_bg = proj(C_GM0 + GROUP, C_GM0 + 2 * GROUP)
    z_gc = proj(C_GM0 + 4 * GROUP, C_GM0 + 5 * GROUP)

    u = z_cg * z_h
    prev = carry_ref[...]
    p1, p2 = prev[7:8, :], prev[6:7, :]
    row = lax.broadcasted_iota(jnp.int32, u.shape, 0)
    um1 = jnp.where(row == 0, p1, pltpu.roll(u, 1, 0))
    um2 = jnp.where(row == 0, p2, jnp.where(row == 1, p1, pltpu.roll(u, 2, 0)))
    w = wconv_ref[...]
    y = w[0:1, :] * um2 + w[1:2, :] * um1 + w[2:3, :] * u
    cv_out[...] = (z_bg * y * _silu(z_gc)).astype(BF16)
    carry_ref[...] = u[tm - 8:tm, :]
    conv_out[0] = u[tm - (CONV_W - 1):tm, :]


def _const_spec(shape):
    nd = len(shape)
    return pl.BlockSpec(shape, lambda *_: (0,) * nd, pipeline_mode=pl.Buffered(1))


def _proj_prompt(x2d, seq, gpre, w1, gq, wqt, gkv, wk, wvt, wconv, tabs, tm):
    t, d = x2d.shape
    nb = t // seq
    hs = N_HEADS * HEAD_SLAB
    hv = N_HEADS * V_HEAD
    cq1, cq2, ck1, ck2, cq1t, cq2t = tabs
    row = lambda w: pl.BlockSpec((tm, w), lambda i: (i, 0))
    col = lambda h: pl.BlockSpec((h, tm), lambda i: (0, i))
    tab = pl.BlockSpec((tm, LANES), lambda i: (i % (seq // tm), 0))
    tab_t = pl.BlockSpec((LANES, tm), lambda i: (0, i % (seq // tm)))
    out_shape = (
        jax.ShapeDtypeStruct((hs, t), BF16),
        jax.ShapeDtypeStruct((t, hs), BF16),
        jax.ShapeDtypeStruct((hv, t), BF16),
        jax.ShapeDtypeStruct((t, gkv.shape[1]), F32),
        jax.ShapeDtypeStruct((nb, QK_ROPE, seq), F32),
        jax.ShapeDtypeStruct((t, GROUP), BF16),
        jax.ShapeDtypeStruct((t, GROUP), BF16),
        jax.ShapeDtypeStruct((nb, CONV_W - 1, GROUP), F32),
    )
    return pl.pallas_call(
        functools.partial(_proj_prompt_kernel, tiles_per_seq=seq // tm),
        grid=(t // tm,),
        in_specs=[row(d), _const_spec(gpre.shape), _const_spec(w1.shape), _const_spec(gq.shape),
                  _const_spec(wqt.shape), _const_spec(gkv.shape), _const_spec(wk.shape), _const_spec(wvt.shape),
                  _const_spec(wconv.shape), tab_t, tab_t, tab, tab],
        out_specs=(col(hs), row(hs), col(hv), row(gkv.shape[1]),
                   pl.BlockSpec((1, QK_ROPE, tm), lambda i: (i // (seq // tm), 0, i % (seq // tm))),
                   row(GROUP), row(GROUP),
                   pl.BlockSpec((1, CONV_W - 1, GROUP), lambda i: (i // (seq // tm), 0, 0))),
        out_shape=out_shape,
        scratch_shapes=[pltpu.VMEM((8, GROUP), F32)],
        compiler_params=pltpu.CompilerParams(dimension_semantics=("arbitrary",),
                                             vmem_limit_bytes=VMEM_LIMIT_BYTES),
        name="proj_prompt",
    )(x2d, gpre, w1, gq, wqt, gkv, wk, wvt, wconv, cq1t, cq2t, ck1, ck2)


def _attn_kernel(qt_ref, k_ref, vt_ref, gm_ref, cv_ref, x_ref, wo_ref, gpost_ref, y_ref,
                 s_ref, m_ref, acc_ref, *, tq, tk):
    i = pl.program_id(1)
    hq = tq // 2
    tri = (lax.broadcasted_iota(jnp.int32, (hq, hq), 0) <= lax.broadcasted_iota(jnp.int32, (hq, hq), 1))

    def slab(h):
        return slice(h * HEAD_SLAB, (h + 1) * HEAD_SLAB)

    def values(h, toks):
        ones = jnp.ones((ONES_ROWS, toks.size), BF16)
        return jnp.concatenate([vt_ref[h * V_HEAD:(h + 1) * V_HEAD, toks], ones], 0)

    def scores(h, toks):
        return jnp.dot(k_ref[toks, slab(h)], qt_ref[slab(h), :], preferred_element_type=F32)

    def update(h, toks, st):
        m = m_ref[h]
        m_new = jnp.maximum(m, jnp.max(st, axis=0, keepdims=True))
        pt = jnp.exp2(st - m_new).astype(BF16)
        m_ref[h] = m_new
        acc_ref[h] = jnp.exp2(m - m_new) * acc_ref[h] + jnp.dot(values(h, toks), pt, preferred_element_type=F32)

    base = pl.multiple_of(i * tq, tq)
    ka, kb = pl.ds(base, hq), pl.ds(base + hq, hq)

    def diagonal_scores(h):
        return (jnp.dot(k_ref[ka, slab(h)], qt_ref[slab(h), :], preferred_element_type=F32),
                jnp.dot(k_ref[kb, slab(h)], qt_ref[slab(h), hq:], preferred_element_type=F32))

    def diagonal_init(h, a, b_):
        a_lo = jnp.where(tri, a[:, :hq], -jnp.inf)
        a_hi = a[:, hq:]
        b_ = jnp.where(tri, b_, -jnp.inf)
        m_lo = jnp.max(a_lo, axis=0, keepdims=True)
        m_hi = jnp.maximum(jnp.max(a_hi, axis=0, keepdims=True), jnp.max(b_, axis=0, keepdims=True))
        pa = jnp.concatenate([jnp.exp2(a_lo - m_lo), jnp.exp2(a_hi - m_hi)], axis=1).astype(BF16)
        pb = jnp.exp2(b_ - m_hi).astype(BF16)
        acc_a = jnp.dot(values(h, ka), pa, preferred_element_type=F32)
        acc_b = jnp.dot(values(h, kb), pb, preferred_element_type=F32)
        m_ref[h] = jnp.concatenate([m_lo, m_hi], axis=1)
        acc_ref[h] = jnp.concatenate([acc_a[:, :hq], acc_a[:, hq:] + acc_b], axis=1)

    n_full = i * (tq // tk)

    def tile_tokens(j):
        return pl.ds(pl.multiple_of(j * tk, tk), tk)

    cur = diagonal_scores(0)
    for h in range(N_HEADS):
        if h + 1 < N_HEADS:
            ahead = diagonal_scores(h + 1)
        else:
            s_ref[0] = scores(0, tile_tokens(0))
        diagonal_init(h, *cur)
        cur = ahead

    def full_tile(j, carry):
        toks = tile_tokens(j)
        nxt = tile_tokens(jnp.minimum(j + 1, n_full - 1))
        for h in range(N_HEADS):
            if h + 1 < N_HEADS:
                s_ref[h + 1] = scores(h + 1, toks)
                update(h, toks, s_ref[h])
            else:
                cur = s_ref[h]
                s_ref[0] = scores(0, nxt)
                update(h, toks, cur)
        return carry

    lax.fori_loop(0, n_full, full_tile, 0)
    outs = [acc_ref[h, :V_HEAD] * (1.0 / acc_ref[h, V_HEAD:V_HEAD + 1]) for h in range(N_HEADS)]
    pairs = [jnp.concatenate(outs[2 * p:2 * p + 2], axis=0).T for p in range(N_HEADS // 2)]

    half = gm_ref.shape[1]
    mla = (jnp.concatenate(pairs, axis=1) * gm_ref[...].astype(F32)).astype(BF16)
    y = jnp.dot(mla, wo_ref[0:half, :], preferred_element_type=F32)
    y = y + jnp.dot(cv_ref[...], wo_ref[half:, :], preferred_element_type=F32)
    y_ref[...] = x_ref[...] + _rms(y, gpost_ref[...])


def _attention_out(qt, k, vt, gm, cv, x2d, wo, gpost, nb, seq, tq, tk):
    hs, t = qt.shape
    hv = vt.shape[0]
    d = x2d.shape[1]
    assert N_HEADS % 2 == 0 and tq % tk == 0
    row = lambda w: pl.BlockSpec((tq, w), lambda b, i: (b * (seq // tq) + i, 0))
    return pl.pallas_call(
        functools.partial(_attn_kernel, tq=tq, tk=tk),
        grid=(nb, seq // tq),
        in_specs=[pl.BlockSpec((hs, tq), lambda b, i: (0, b * (seq // tq) + i)),
                  pl.BlockSpec((seq, hs), lambda b, i: (b, 0)),
                  pl.BlockSpec((hv, seq), lambda b, i: (0, b)),
                  row(gm.shape[1]), row(cv.shape[1]), row(d), _const_spec(wo.shape), _const_spec(gpost.shape)],
        out_specs=row(d),
        out_shape=jax.ShapeDtypeStruct((t, d), F32),
        scratch_shapes=[pltpu.VMEM((N_HEADS, tk, tq), F32), pltpu.VMEM((N_HEADS, 1, tq), F32),
                        pltpu.VMEM((N_HEADS, V_HEAD + ONES_ROWS, tq), F32)],
        compiler_params=pltpu.CompilerParams(dimension_semantics=("arbitrary", "arbitrary"),
                                             vmem_limit_bytes=VMEM_LIMIT_BYTES),
        name="prompt_attn",
    )(qt, k, vt, gm, cv, x2d, wo, gpost)


def _proj_sample_kernel(x_ref, gpre_ref, w1_ref, gq_ref, wq_ref, gkv_ref, wukt_ref, wconv_ref,
                        cq1_ref, cq2_ref, ck1_ref, ck2_ref, s0_ref, s1_ref,
                        ql_out, qp_out, ckv_out, kpe_out, gm_out, cv_out, conv_out, *, dec_seq):
    proj, cqn, ckv, kpe_blk = _project_common(x_ref, gpre_ref, w1_ref, gq_ref, gkv_ref, ck1_ref, ck2_ref)
    qraw = jnp.dot(cqn, wq_ref[...], preferred_element_type=F32)
    cq1, cq2 = cq1_ref[...], cq2_ref[...]
    for h in range(N_HEADS):
        qh = _rope_slab(qraw[:, h * HEAD_SLAB:(h + 1) * HEAD_SLAB], cq1, cq2)
        ql_out[:, h, :] = jnp.dot(qh.astype(BF16), wukt_ref[h], preferred_element_type=F32)
        qp_out[:, h, :] = pltpu.roll(qh, LANES - QK_NOPE, 1)[:, 0:QK_ROPE]
    ckv_out[...] = ckv
    kpe_out[...] = kpe_blk[:, QK_NOPE:QK_NOPE + QK_ROPE]
    gm_out[...] = _silu(proj(C_GM0, C_GM0 + GROUP))

    u = proj(C_GM0 + 2 * GROUP, C_GM0 + 3 * GROUP) * proj(C_GM0 + 3 * GROUP, C_GM0 + 4 * GROUP)
    t_in_seq = lax.broadcasted_iota(jnp.int32, u.shape, 0) % dec_seq
    s0, s1 = s0_ref[...], s1_ref[...]
    um1 = jnp.where(t_in_seq == 0, s1, pltpu.roll(u, 1, 0))
    um2 = jnp.where(t_in_seq == 0, s0, jnp.where(t_in_seq == 1, s1, pltpu.roll(u, 2, 0)))
    cv_out[...] = _conv_gate(proj, um1, um2, u, wconv_ref)
    conv_out[...] = u


def _proj_sample(x2d, dec_seq, gpre, w1, gq, wq, gkv, wukt, wconv, tabs, s0, s1):
    t, d = x2d.shape
    c = gkv.shape[1]
    out_shape = (
        jax.ShapeDtypeStruct((t, N_HEADS, c), F32),
        jax.ShapeDtypeStruct((t, N_HEADS, QK_ROPE), F32),
        jax.ShapeDtypeStruct((t, c), F32),
        jax.ShapeDtypeStruct((t, QK_ROPE), F32),
        jax.ShapeDtypeStruct((t, GROUP), F32),
        jax.ShapeDtypeStruct((t, GROUP), F32),
        jax.ShapeDtypeStruct((t, GROUP), F32),
    )
    return pl.pallas_call(
        functools.partial(_proj_sample_kernel, dec_seq=dec_seq),
        out_shape=out_shape,
        compiler_params=pltpu.CompilerParams(vmem_limit_bytes=VMEM_LIMIT_BYTES),
        name="proj_sample",
    )(x2d, gpre, w1, gq, wq, gkv, wukt, wconv, *tabs, s0, s1)


def _decode_attn_kernel(pt_ref, ql_ref, qp_ref, cnew_ref, knew_ref, y_hbm, x_hbm, o_ref,
                        ybuf, xbuf, sem, m_ref, l_ref, acc_ref, *, pages, slots, layer, dec_seq):
    g = pl.program_id(0)
    n_g = pl.num_programs(0)
    per_step = ql_ref.shape[0]
    n_chunks = pt_ref.shape[1] // pages
    total = per_step * n_chunks
    page = ybuf.shape[1] // pages
    rows = ql_ref.shape[1]
    dn = (((1,), (1,)), ((), ()))
    ahead = slots - 1

    def chunk_copies(bb, ch, slot, page_ids):
        cps = []
        for i in range(pages):
            pid = page_ids(bb, ch * pages + i)
            tok = pl.ds(i * page, page)
            cps.append(pltpu.make_async_copy(y_hbm.at[layer, pid], ybuf.at[slot, tok, :], sem.at[0, slot]))
            cps.append(pltpu.make_async_copy(x_hbm.at[layer, pid], xbuf.at[slot, :, tok], sem.at[1, slot]))
        return cps

    def start(bb, ch, slot):
        for cp in chunk_copies(bb, ch, slot, lambda r, j: pt_ref[r, j]):
            cp.start()

    def wait(slot):
        for cp in chunk_copies(0, 0, slot, lambda r, j: 0):
            cp.wait()

    def start_nth(step, k, slot):
        start(step * per_step + k // n_chunks, k % n_chunks, slot)

    @pl.when(g == 0)
    def _():
        for k in range(ahead):
            start_nth(0, k, k % slots)

    tok_of_row = lax.broadcasted_iota(jnp.int32, (rows, 1), 0) // N_HEADS
    for e in range(per_step):
        qlf, qpf = ql_ref[e], qp_ref[e]
        s_new = []
        for j in range(dec_seq):
            sj = (jnp.sum(qlf * cnew_ref[e, j:j + 1, :], axis=-1, keepdims=True)
                  + jnp.sum(qpf * knew_ref[e, j:j + 1, :], axis=-1, keepdims=True))
            s_new.append(jnp.where(tok_of_row >= j, sj, -jnp.inf))
        m_n = functools.reduce(jnp.maximum, s_new)
        p_n = [jnp.exp(sj - m_n) for sj in s_new]
        m_ref[e] = m_n
        l_ref[e] = functools.reduce(lambda a, v: a + v, p_n)
        acc_ref[e, 0] = functools.reduce(lambda a, v: a + v,
                                         [p_n[j] * cnew_ref[e, j:j + 1, :] for j in range(dec_seq)])
        acc_ref[e, 1] = jnp.zeros(acc_ref.shape[2:], F32)

    half = pages * page // 2
    toks = [slice(h * half, (h + 1) * half) for h in range(2)]

    def scores(e, slot):
        ql, qp = ql_ref[e].astype(BF16), qp_ref[e].astype(BF16)
        yb = [ybuf[slot, t, :].astype(BF16) for t in toks]
        xb = [xbuf[slot, :, t].astype(BF16) for t in toks]
        s = jnp.concatenate(
            [lax.dot_general(ql, yb[h], dn, preferred_element_type=F32)
             + jnp.dot(qp, xb[h], preferred_element_type=F32) for h in range(2)], axis=1)
        return e, s, yb

    def softmax(e, s, yb):
        m = m_ref[e]
        m_new = jnp.maximum(m, jnp.max(s, axis=-1, keepdims=True))
        alpha = jnp.exp(m - m_new)
        p = jnp.exp(s - m_new)
        l_ref[e] = alpha * l_ref[e] + jnp.sum(p, axis=-1, keepdims=True)
        m_ref[e] = m_new
        return e, alpha, p.astype(BF16), yb

    def values(e, alpha, p, yb):
        for h in range(2):
            acc_ref[e, h] = alpha * acc_ref[e, h] + jnp.dot(p[:, toks[h]], yb[h], preferred_element_type=F32)

    scored = None
    weighted = None
    for k in range(total):
        nxt = k + ahead
        if nxt < total:
            start_nth(g, nxt, nxt % slots)
        else:
            @pl.when(g + 1 < n_g)
            def _():
                start_nth(g + 1, nxt - total, nxt % slots)

        wait(k % slots)
        fresh = scores(k // n_chunks, k % slots)
        if weighted is not None:
            values(*weighted)
        if scored is not None:
            weighted = softmax(*scored)
        scored = fresh
    if weighted is not None:
        values(*weighted)
    values(*softmax(*scored))
    for e in range(per_step):
        o_ref[e] = (acc_ref[e, 0] + acc_ref[e, 1]) * (1.0 / l_ref[e])


def _decode_attention(page_table, ql, qp, cnew, knew, cache_ckv, cache_kpe, layer, pages, slots, per_step):
    nb, rows, c = ql.shape
    dec_seq = cnew.shape[1]
    page, r = cache_kpe.shape[2:]
    n_chunks = page_table.shape[1] // pages
    assert n_chunks * pages == page_table.shape[1] and nb % per_step == 0
    assert (per_step * n_chunks) % slots == 0 and slots - 1 <= per_step * n_chunks
    kpe_t = jnp.swapaxes(cache_kpe, 2, 3)
    per_b = lambda a: pl.BlockSpec((per_step,) + a.shape[1:], lambda g, pt: (g, 0, 0))
    grid_spec = pltpu.PrefetchScalarGridSpec(
        num_scalar_prefetch=1,
        grid=(nb // per_step,),
        in_specs=[per_b(ql), per_b(qp), per_b(cnew), per_b(knew),
                  pl.BlockSpec(memory_space=pl.ANY), pl.BlockSpec(memory_space=pl.ANY)],
        out_specs=pl.BlockSpec((per_step, rows, c), lambda g, pt: (g, 0, 0)),
        scratch_shapes=[pltpu.VMEM((slots, pages * page, c), F32), pltpu.VMEM((slots, r, pages * page), F32),
                        pltpu.SemaphoreType.DMA((2, slots)),
                        pltpu.VMEM((per_step, rows, 1), F32), pltpu.VMEM((per_step, rows, 1), F32),
                        pltpu.VMEM((per_step, 2, rows, c), F32)],
    )
    return pl.pallas_call(
        functools.partial(_decode_attn_kernel, pages=pages, slots=slots, layer=layer, dec_seq=dec_seq),
        grid_spec=grid_spec,
        out_shape=jax.ShapeDtypeStruct((nb, rows, c), F32),
        compiler_params=pltpu.CompilerParams(dimension_semantics=("arbitrary",),
                                             vmem_limit_bytes=VMEM_LIMIT_BYTES),
        name="decode_attn",
    )(page_table, ql, qp, cnew, knew, cache_ckv, kpe_t)


def _out_sample_kernel(ol_ref, gm_ref, cv_ref, x_ref, wuvt_ref, wo_ref, gpost_ref, y_ref):
    half = gm_ref.shape[1]
    dn = (((1,), (1,)), ((), ()))
    o = jnp.concatenate(
        [lax.dot_general(ol_ref[:, h, :].astype(BF16), wuvt_ref[h], dn, preferred_element_type=F32)
         for h in range(N_HEADS)], axis=1)
    mla = (o * gm_ref[...]).astype(BF16)
    y = jnp.dot(mla, wo_ref[0:half, :], preferred_element_type=F32)
    y = y + jnp.dot(cv_ref[...].astype(BF16), wo_ref[half:, :], preferred_element_type=F32)
    y_ref[...] = x_ref[...] + _rms(y, gpost_ref[...])


def _out_sample(ol, gm, cv, x2d, wuv, wo, gpost):
    return pl.pallas_call(
        _out_sample_kernel,
        out_shape=jax.ShapeDtypeStruct(x2d.shape, F32),
        compiler_params=pltpu.CompilerParams(vmem_limit_bytes=VMEM_LIMIT_BYTES),
        name="out_sample",
    )(ol, gm, cv, x2d, wuv, wo, gpost)


def _rot_rows(w):
    r = w.shape[0] // 2
    return jnp.concatenate([-w[r:], w[:r]], axis=0)


def _pack_kernel(wint_ref, wuq_ref, wukv_ref, wout_ref,
                 w1_ref, wq_ref, wqt_ref, wk_ref, wvt_ref, wukt_ref, wuvt_ref, wo_ref):
    d = wint_ref.shape[1]
    blk = LANES

    def put_w1(col0, rows):
        w1_ref[:, col0:col0 + blk] = rows.T.astype(BF16)

    for r0 in range(0, C_KV1, blk):
        put_w1(r0, wint_ref[r0:r0 + blk, :])
    kpe = wint_ref[C_KV1:C_KV1 + QK_ROPE, :]
    put_w1(C_KP0, jnp.concatenate([jnp.zeros((QK_NOPE, d), F32), kpe, _rot_rows(kpe)], axis=0))
    src0 = C_KV1 + QK_ROPE
    for j in range(5 * GROUP // blk):
        put_w1(C_GM0 + j * blk, wint_ref[src0 + j * blk:src0 + (j + 1) * blk, :])

    uqt = wuq_ref[...].T
    ukvt = wukv_ref[...].T
    kv_lora = ukvt.shape[1]
    qk = QK_NOPE + QK_ROPE
    wqt_ref[...] = uqt.astype(BF16)
    for h in range(N_HEADS):
        pe = uqt[h * qk + QK_NOPE:(h + 1) * qk]
        q_slab = jnp.concatenate([uqt[h * qk:h * qk + QK_NOPE], pe, _rot_rows(pe)], axis=0)
        wq_ref[:, h * HEAD_SLAB:(h + 1) * HEAD_SLAB] = q_slab.T.astype(BF16)
        ukt = ukvt[h * (QK_NOPE + V_HEAD):h * (QK_NOPE + V_HEAD) + QK_NOPE]
        uvt = ukvt[h * (QK_NOPE + V_HEAD) + QK_NOPE:(h + 1) * (QK_NOPE + V_HEAD)]
        k_slab = jnp.concatenate([ukt, jnp.zeros((HEAD_SLAB - QK_NOPE, kv_lora), F32)], axis=0)
        wukt_ref[h] = k_slab.astype(BF16)
        wvt_ref[h * V_HEAD:(h + 1) * V_HEAD, :] = uvt.astype(BF16)
        wuvt_ref[h] = uvt.astype(BF16)
    for p in range(N_HEADS // 2):
        two = ukvt[2 * p * (QK_NOPE + V_HEAD):2 * (p + 1) * (QK_NOPE + V_HEAD)]
        pair = jnp.concatenate([two[0:QK_NOPE], two[QK_NOPE + V_HEAD:2 * QK_NOPE + V_HEAD]], axis=0)
        wk_ref[:, p * LANES:(p + 1) * LANES] = pair.T.astype(BF16)
    wo_ref[...] = wout_ref[...].astype(BF16)


def _pack_weights(w_in, w_uq, w_ukv, w_out):
    d = w_in.shape[0]
    q_lora, kv_lora = w_uq.shape[0], w_ukv.shape[0]
    assert w_in.shape[1] == C_KV1 + QK_ROPE + 5 * GROUP
    sds = lambda *s: jax.ShapeDtypeStruct(s, BF16)
    return pl.pallas_call(
        _pack_kernel,
        out_shape=(sds(d, D_IN_PACKED), sds(q_lora, N_HEADS * HEAD_SLAB), sds(w_uq.shape[1], q_lora),
                   sds(kv_lora, N_HEADS * QK_NOPE), sds(N_HEADS * V_HEAD, kv_lora),
                   sds(N_HEADS, HEAD_SLAB, kv_lora), sds(N_HEADS, V_HEAD, kv_lora), sds(*w_out.shape)),
        compiler_params=pltpu.CompilerParams(vmem_limit_bytes=VMEM_LIMIT_BYTES),
        name="pack_weights",
    )(w_in.T, w_uq, w_ukv, w_out)


PROMPT_TILE = 512
ATTN_TQ = 512
ATTN_TK = 512
DECODE_PAGES = 32
DECODE_SLOTS = 4
DECODE_PER_STEP = 2


def kernel(x_prompt, x_sample, cache_ckv, cache_kpe, state_conv, page_table, g_pre, w_in, g_qnorm, w_uq,
           g_kvnorm, w_ukv, w_conv, w_out, g_post):
    depth = w_in.shape[0]
    nb, seq, d = x_prompt.shape
    db, dec_seq, _ = x_sample.shape
    past_len = page_table.shape[1] * cache_ckv.shape[2]
    c = cache_ckv.shape[3]

    tabs_p = _rope_tables(jnp.arange(seq, dtype=jnp.int32))
    tabs_s = _rope_tables(jnp.tile(past_len + jnp.arange(dec_seq, dtype=jnp.int32), db))

    xp = x_prompt.reshape(nb * seq, d)
    xs = x_sample.reshape(db * dec_seq, d)
    outs = [[] for _ in range(6)]
    for l in range(depth):
        w1, wq, wqt, wk, wvt, wukt, wuvt, wo = _pack_weights(w_in[l], w_uq[l], w_ukv[l], w_out[l])
        gpre, gq, gkv, gpost = g_pre[l][None], g_qnorm[l][None], g_kvnorm[l][None], g_post[l][None]

        qt, k, vt, ckv_p, kpe_p, gm, cv, conv_p = _proj_prompt(
            xp, seq, gpre, w1, gq, wqt, gkv, wk, wvt, w_conv[l], tabs_p, PROMPT_TILE)
        xp = _attention_out(qt, k, vt, gm, cv, xp, wo, gpost, nb, seq, ATTN_TQ, ATTN_TK)
        outs[0].append(ckv_p.reshape(nb, seq, c))
        outs[1].append(jnp.swapaxes(kpe_p, 1, 2))
        outs[2].append(conv_p)

        st = state_conv[l].astype(F32)
        s0 = jnp.repeat(st[:, 0], dec_seq, axis=0)
        s1 = jnp.repeat(st[:, 1], dec_seq, axis=0)
        ql, qp, ckv_s, kpe_s, gm_s, cv_s, u_s = _proj_sample(
            xs, dec_seq, gpre, w1, gq, wq, gkv, wukt, w_conv[l], tabs_s[:4], s0, s1)
        rows = dec_seq * N_HEADS
        ol = _decode_attention(page_table, ql.reshape(db, rows, c), qp.reshape(db, rows, QK_ROPE),
                               ckv_s.reshape(db, dec_seq, c), kpe_s.reshape(db, dec_seq, QK_ROPE),
                               cache_ckv, cache_kpe, l, DECODE_PAGES, DECODE_SLOTS, DECODE_PER_STEP)
        xs = _out_sample(ol.reshape(db * dec_seq, N_HEADS, c), gm_s, cv_s, xs, wuvt, wo, gpost)
        outs[3].append(ckv_s.reshape(db, dec_seq, c))
        outs[4].append(kpe_s.reshape(db, dec_seq, QK_ROPE))
        u_pad = jnp.concatenate([st, u_s.reshape(db, dec_seq, GROUP)], axis=1)
        outs[5].append(u_pad[:, -(CONV_W - 1):])

    return (xp.reshape(nb, seq, d), xs.reshape(db, dec_seq, d), *[jnp.stack(o_) for o_ in outs])
```

```python
import functools

import jax
import jax.numpy as jnp
from jax import lax
from jax.experimental import pallas as pl
from jax.experimental.pallas import tpu as pltpu

N_HEADS = 8
QK_NOPE = 64
QK_ROPE = 32
V_HEAD = 64
CONV_W = 3
ROPE_THETA = 10000.0
EPS = 1e-6
ATTN_SCALE = (QK_NOPE + QK_ROPE) ** -0.5

LANES = 128
HEAD_SLAB = LANES
ONES_ROWS = 16
LOG2_E = 1.4426950408889634
VMEM_LIMIT_BYTES = 56 * 1024 * 1024

F32 = jnp.float32
BF16 = jnp.bfloat16


def _silu(x):
    return x * (1.0 / (1.0 + jnp.exp(-x)))


def _rms(x, g):
    return x * lax.rsqrt(jnp.mean(x * x, axis=-1, keepdims=True) + EPS) * g


def _rope_table_kernel(pos_ref, invf_ref, cq1_ref, cq2_ref, ck1_ref, ck2_ref, cq1t_ref, cq2t_ref):
    ang = invf_ref[...] * pos_ref[...]
    c, s = jnp.cos(ang), jnp.sin(ang)
    t = ang.shape[1]
    nope = jnp.ones((QK_NOPE, t), F32)
    zn = jnp.zeros((QK_NOPE, t), F32)
    zr = jnp.zeros((HEAD_SLAB - QK_NOPE - QK_ROPE, t), F32)

    def slab(first, mid):
        return jnp.concatenate([first, mid, zr], axis=0)

    cq1_ref[...] = slab(nope * ATTN_SCALE, c * ATTN_SCALE).T
    cq2_ref[...] = slab(zn, s * ATTN_SCALE).T
    ck1_ref[...] = slab(zn, c).T
    ck2_ref[...] = slab(zn, s).T
    cq1t_ref[...] = slab(nope * (ATTN_SCALE * LOG2_E), c * (ATTN_SCALE * LOG2_E))
    cq2t_ref[...] = slab(zn, s * (ATTN_SCALE * LOG2_E))


def _rope_tables(pos):
    t = pos.shape[0]
    r = QK_ROPE
    inv_freq = ROPE_THETA ** (-jnp.arange(0, r, 2, dtype=F32) / r)
    invf = jnp.concatenate([inv_freq, inv_freq]).reshape(r, 1)
    out = jax.ShapeDtypeStruct((t, LANES), F32)
    out_t = jax.ShapeDtypeStruct((LANES, t), F32)
    return pl.pallas_call(
        _rope_table_kernel,
        out_shape=(out, out, out, out, out_t, out_t),
        name="rope_tables",
    )(pos.astype(F32).reshape(1, t), invf)


C_Q0, C_Q1 = 0, 384
C_KV0, C_KV1 = 384, 640
C_KP0, C_KP1 = 640, 768
C_GM0 = 768
GROUP = 512
D_IN_PACKED = C_GM0 + 5 * GROUP


def _rope_slab(blk, c1, c2):
    return blk * c1 + pltpu.roll(blk, LANES - QK_ROPE, 1) * c2


def _project_common(x, gpre_ref, w1_ref, gq_ref, gkv_ref, ck1_ref, ck2_ref):
    xn = _rms(x, gpre_ref[...]).astype(BF16)

    def proj(c0, c1):
        return jnp.dot(xn, w1_ref[:, c0:c1], preferred_element_type=F32)

    cqn = _rms(proj(C_Q0, C_Q1), gq_ref[...]).astype(BF16)
    ckv = _rms(proj(C_KV0, C_KV1), gkv_ref[...])
    kpe_blk = _rope_slab(proj(C_KP0, C_KP1), ck1_ref[...], ck2_ref[...])
    return proj, cqn, ckv, kpe_blk


def _conv_gate(proj, um1, um2, u, wconv_ref):
    w = wconv_ref[...]
    y = w[0:1, :] * um2 + w[1:2, :] * um1 + w[2:3, :] * u
    bg = proj(C_GM0 + GROUP, C_GM0 + 2 * GROUP)
    gc = proj(C_GM0 + 4 * GROUP, C_GM0 + 5 * GROUP)
    return bg * y * _silu(gc)


def _proj_prompt_kernel(x_ref, gpre_ref, w1_ref, gq_ref, wqt_ref, gkv_ref, wk_ref, wvt_ref, wconv_ref,
                        cq1t_ref, cq2t_ref, ck1_ref, ck2_ref,
                        qt_out, k_out, vt_out, ckv_out, kpet_out, gm_out, cv_out, conv_out,
                        carry_ref, *, tiles_per_seq):
    tm = x_ref.shape[0]
    step = pl.program_id(0)
    dn = (((1,), (1,)), ((), ()))

    @pl.when(step % tiles_per_seq == 0)
    def _():
        carry_ref[...] = jnp.zeros_like(carry_ref)

    xn = _rms(x_ref[...], gpre_ref[...]).astype(BF16)

    def proj(c0, c1):
        return jnp.dot(xn, w1_ref[:, c0:c1], preferred_element_type=F32)

    z_cq = proj(C_Q0, C_Q1)
    z_ckv = proj(C_KV0, C_KV1)
    cqn = _rms(z_cq, gq_ref[...]).astype(BF16)
    qt = lax.dot_general(wqt_ref[...], cqn, dn, preferred_element_type=F32)
    ckv = _rms(z_ckv, gkv_ref[...])
    ckv_out[...] = ckv
    ckvb = ckv.astype(BF16)
    z_kp = proj(C_KP0, C_KP1)
    knope = jnp.dot(ckvb, wk_ref[...], preferred_element_type=F32)
    qk = QK_NOPE + QK_ROPE
    nope_scale = cq1t_ref[0:QK_NOPE, :]
    cos_t, sin_t = cq1t_ref[QK_NOPE:qk, :], cq2t_ref[QK_NOPE:qk, :]
    for h in range(N_HEADS):
        nope, pe = qt[h * qk:h * qk + QK_NOPE, :], qt[h * qk + QK_NOPE:(h + 1) * qk, :]
        qt_out[h * HEAD_SLAB:h * HEAD_SLAB + QK_NOPE, :] = (nope * nope_scale).astype(BF16)
        qt_out[h * HEAD_SLAB + QK_NOPE:h * HEAD_SLAB + qk, :] = (pe * cos_t + _rot_rows(pe) * sin_t).astype(BF16)
        qt_out[h * HEAD_SLAB + qk:(h + 1) * HEAD_SLAB, :] = jnp.zeros((HEAD_SLAB - qk, tm), BF16)
    vt = lax.dot_general(wvt_ref[...], ckvb, dn, preferred_element_type=F32)
    z_gm = proj(C_GM0, C_GM0 + GROUP)
    kpe_blk = _rope_slab(z_kp, ck1_ref[...], ck2_ref[...])
    kpet_out[0] = kpe_blk.T[QK_NOPE:QK_NOPE + QK_ROPE, :]
    lane = lax.broadcasted_iota(jnp.int32, kpe_blk.shape, 1)
    for p in range(N_HEADS // 2):
        two = knope[:, p * LANES:(p + 1) * LANES]
        for h, nope_first in ((2 * p, two), (2 * p + 1, pltpu.roll(two, LANES - QK_NOPE, 1))):
            k_out[:, h * HEAD_SLAB:(h + 1) * HEAD_SLAB] = jnp.where(lane < QK_NOPE, nope_first, kpe_blk).astype(BF16)
    vt_out[...] = vt.astype(BF16)
    z_cg = proj(C_GM0 + 2 * GROUP, C_GM0 + 3 * GROUP)
    z_h = proj(C_GM0 + 3 * GROUP, C_GM0 + 4 * GROUP)
    gm_out[...] = _silu(z_gm).astype(BF16)
    z_bg = proj(C_GM0 + GROUP, C_GM0 + 2 * GROUP)
    z_gc = proj(C_GM0 + 4 * GROUP, C_GM0 + 5 * GROUP)

    u = z_cg * z_h
    prev = carry_ref[...]
    p1, p2 = prev[7:8, :], prev[6:7, :]
    row = lax.broadcasted_iota(jnp.int32, u.shape, 0)
    um1 = jnp.where(row == 0, p1, pltpu.roll(u, 1, 0))
    um2 = jnp.where(row == 0, p2, jnp.where(row == 1, p1, pltpu.roll(u, 2, 0)))
    w = wconv_ref[...]
    y = w[0:1, :] * um2 + w[1:2, :] * um1 + w[2:3, :] * u
    cv_out[...] = (z_bg * y * _silu(z_gc)).astype(BF16)
    carry_ref[...] = u[tm - 8:tm, :]
    conv_out[0] = u[tm - (CONV_W - 1):tm, :]


def _const_spec(shape):
    nd = len(shape)
    return pl.BlockSpec(shape, lambda *_: (0,) * nd, pipeline_mode=pl.Buffered(1))


def _proj_prompt(x2d, seq, gpre, w1, gq, wqt, gkv, wk, wvt, wconv, tabs, tm):
    t, d = x2d.shape
    nb = t // seq
    hs = N_HEADS * HEAD_SLAB
    hv = N_HEADS * V_HEAD
    cq1, cq2, ck1, ck2, cq1t, cq2t = tabs
    row = lambda w: pl.BlockSpec((tm, w), lambda i: (i, 0))
    col = lambda h: pl.BlockSpec((h, tm), lambda i: (0, i))
    tab = pl.BlockSpec((tm, LANES), lambda i: (i % (seq // tm), 0))
    tab_t = pl.BlockSpec((LANES, tm), lambda i: (0, i % (seq // tm)))
    out_shape = (
        jax.ShapeDtypeStruct((hs, t), BF16),
        jax.ShapeDtypeStruct((t, hs), BF16),
        jax.ShapeDtypeStruct((hv, t), BF16),
        jax.ShapeDtypeStruct((t, gkv.shape[1]), F32),
        jax.ShapeDtypeStruct((nb, QK_ROPE, seq), F32),
        jax.ShapeDtypeStruct((t, GROUP), BF16),
        jax.ShapeDtypeStruct((t, GROUP), BF16),
        jax.ShapeDtypeStruct((nb, CONV_W - 1, GROUP), F32),
    )
    return pl.pallas_call(
        functools.partial(_proj_prompt_kernel, tiles_per_seq=seq // tm),
        grid=(t // tm,),
        in_specs=[row(d), _const_spec(gpre.shape), _const_spec(w1.shape), _const_spec(gq.shape),
                  _const_spec(wqt.shape), _const_spec(gkv.shape), _const_spec(wk.shape), _const_spec(wvt.shape),
                  _const_spec(wconv.shape), tab_t, tab_t, tab, tab],
        out_specs=(col(hs), row(hs), col(hv), row(gkv.shape[1]),
                   pl.BlockSpec((1, QK_ROPE, tm), lambda i: (i // (seq // tm), 0, i % (seq // tm))),
                   row(GROUP), row(GROUP),
                   pl.BlockSpec((1, CONV_W - 1, GROUP), lambda i: (i // (seq // tm), 0, 0))),
        out_shape=out_shape,
        scratch_shapes=[pltpu.VMEM((8, GROUP), F32)],
        compiler_params=pltpu.CompilerParams(dimension_semantics=("arbitrary",),
                                             vmem_limit_bytes=VMEM_LIMIT_BYTES),
        name="proj_prompt",
    )(x2d, gpre, w1, gq, wqt, gkv, wk, wvt, wconv, cq1t, cq2t, ck1, ck2)


def _attn_kernel(qt_ref, k_ref, vt_ref, gm_ref, cv_ref, x_ref, wo_ref, gpost_ref, y_ref,
                 s_ref, m_ref, acc_ref, *, tq, tk):
    i = pl.program_id(1)
    hq = tq // 2
    tri = (lax.broadcasted_iota(jnp.int32, (hq, hq), 0) <= lax.broadcasted_iota(jnp.int32, (hq, hq), 1))

    def slab(h):
        return slice(h * HEAD_SLAB, (h + 1) * HEAD_SLAB)

    def values(h, toks):
        ones = jnp.ones((ONES_ROWS, toks.size), BF16)
        return jnp.concatenate([vt_ref[h * V_HEAD:(h + 1) * V_HEAD, toks], ones], 0)

    def scores(h, toks):
        return jnp.dot(k_ref[toks, slab(h)], qt_ref[slab(h), :], preferred_element_type=F32)

    def update(h, toks, st):
        m = m_ref[h]
        m_new = jnp.maximum(m, jnp.max(st, axis=0, keepdims=True))
        pt = jnp.exp2(st - m_new).astype(BF16)
        m_ref[h] = m_new
        acc_ref[h] = jnp.exp2(m - m_new) * acc_ref[h] + jnp.dot(values(h, toks), pt, preferred_element_type=F32)

    base = pl.multiple_of(i * tq, tq)
    ka, kb = pl.ds(base, hq), pl.ds(base + hq, hq)

    def diagonal_scores(h):
        return (jnp.dot(k_ref[ka, slab(h)], qt_ref[slab(h), :], preferred_element_type=F32),
                jnp.dot(k_ref[kb, slab(h)], qt_ref[slab(h), hq:], preferred_element_type=F32))

    def diagonal_init(h, a, b_):
        a_lo = jnp.where(tri, a[:, :hq], -jnp.inf)
        a_hi = a[:, hq:]
        b_ = jnp.where(tri, b_, -jnp.inf)
        m_lo = jnp.max(a_lo, axis=0, keepdims=True)
        m_hi = jnp.maximum(jnp.max(a_hi, axis=0, keepdims=True), jnp.max(b_, axis=0, keepdims=True))
        pa = jnp.concatenate([jnp.exp2(a_lo - m_lo), jnp.exp2(a_hi - m_hi)], axis=1).astype(BF16)
        pb = jnp.exp2(b_ - m_hi).astype(BF16)
        acc_a = jnp.dot(values(h, ka), pa, preferred_element_type=F32)
        acc_b = jnp.dot(values(h, kb), pb, preferred_element_type=F32)
        m_ref[h] = jnp.concatenate([m_lo, m_hi], axis=1)
        acc_ref[h] = jnp.concatenate([acc_a[:, :hq], acc_a[:, hq:] + acc_b], axis=1)

    n_full = i * (tq // tk)

    def tile_tokens(j):
        return pl.ds(pl.multiple_of(j * tk, tk), tk)

    cur = diagonal_scores(0)
    for h in range(N_HEADS):
        if h + 1 < N_HEADS:
            ahead = diagonal_scores(h + 1)
        else:
            s_ref[0] = scores(0, tile_tokens(0))
        diagonal_init(h, *cur)
        cur = ahead

    def full_tile(j, carry):
        toks = tile_tokens(j)
        nxt = tile_tokens(jnp.minimum(j + 1, n_full - 1))
        for h in range(N_HEADS):
            if h + 1 < N_HEADS:
                s_ref[h + 1] = scores(h + 1, toks)
                update(h, toks, s_ref[h])
            else:
                cur = s_ref[h]
                s_ref[0] = scores(0, nxt)
                update(h, toks, cur)
        return carry

    lax.fori_loop(0, n_full, full_tile, 0)
    outs = [acc_ref[h, :V_HEAD] * (1.0 / acc_ref[h, V_HEAD:V_HEAD + 1]) for h in range(N_HEADS)]
    pairs = [jnp.concatenate(outs[2 * p:2 * p + 2], axis=0).T for p in range(N_HEADS // 2)]

    half = gm_ref.shape[1]
    mla = (jnp.concatenate(pairs, axis=1) * gm_ref[...].astype(F32)).astype(BF16)
    y = jnp.dot(mla, wo_ref[0:half, :], preferred_element_type=F32)
    y = y + jnp.dot(cv_ref[...], wo_ref[half:, :], preferred_element_type=F32)
    y_ref[...] = x_ref[...] + _rms(y, gpost_ref[...])


def _attention_out(qt, k, vt, gm, cv, x2d, wo, gpost, nb, seq, tq, tk):
    hs, t = qt.shape
    hv = vt.shape[0]
    d = x2d.shape[1]
    assert N_HEADS % 2 == 0 and tq % tk == 0
    row = lambda w: pl.BlockSpec((tq, w), lambda b, i: (b * (seq // tq) + i, 0))
    return pl.pallas_call(
        functools.partial(_attn_kernel, tq=tq, tk=tk),
        grid=(nb, seq // tq),
        in_specs=[pl.BlockSpec((hs, tq), lambda b, i: (0, b * (seq // tq) + i)),
                  pl.BlockSpec((seq, hs), lambda b, i: (b, 0)),
                  pl.BlockSpec((hv, seq), lambda b, i: (0, b)),
                  row(gm.shape[1]), row(cv.shape[1]), row(d), _const_spec(wo.shape), _const_spec(gpost.shape)],
        out_specs=row(d),
        out_shape=jax.ShapeDtypeStruct((t, d), F32),
        scratch_shapes=[pltpu.VMEM((N_HEADS, tk, tq), F32), pltpu.VMEM((N_HEADS, 1, tq), F32),
                        pltpu.VMEM((N_HEADS, V_HEAD + ONES_ROWS, tq), F32)],
        compiler_params=pltpu.CompilerParams(dimension_semantics=("arbitrary", "arbitrary"),
                                             vmem_limit_bytes=VMEM_LIMIT_BYTES),
        name="prompt_attn",
    )(qt, k, vt, gm, cv, x2d, wo, gpost)


def _proj_sample_kernel(x_ref, gpre_ref, w1_ref, gq_ref, wq_ref, gkv_ref, wukt_ref, wconv_ref,
                        cq1_ref, cq2_ref, ck1_ref, ck2_ref, st_ref,
                        ql_out, qp_out, ckv_out, kpe_out, gm_out, cv_out, conv_out):
    nb, dec_seq, _ = x_ref.shape

    def tok(a, j):
        return a[j * nb:(j + 1) * nb]

    x = jnp.concatenate([x_ref[:, j, :] for j in range(dec_seq)], axis=0)
    proj, cqn, ckv, kpe_blk = _project_common(x, gpre_ref, w1_ref, gq_ref, gkv_ref, ck1_ref, ck2_ref)
    qraw = jnp.dot(cqn, wq_ref[...], preferred_element_type=F32)
    cq1, cq2 = cq1_ref[...], cq2_ref[...]
    for h in range(N_HEADS):
        qh = _rope_slab(qraw[:, h * HEAD_SLAB:(h + 1) * HEAD_SLAB], cq1, cq2)
        qlh = jnp.dot(qh.astype(BF16), wukt_ref[h], preferred_element_type=F32)
        qph = pltpu.roll(qh, LANES - QK_NOPE, 1)[:, 0:QK_ROPE]
        for j in range(dec_seq):
            ql_out[:, j * N_HEADS + h, :] = tok(qlh, j)
            qp_out[:, j * N_HEADS + h, :] = tok(qph, j)
    kpe = kpe_blk[:, QK_NOPE:QK_NOPE + QK_ROPE]
    for j in range(dec_seq):
        ckv_out[:, j, :] = tok(ckv, j)
        kpe_out[:, j, :] = tok(kpe, j)
    gm_out[...] = _silu(proj(C_GM0, C_GM0 + GROUP))

    u = proj(C_GM0 + 2 * GROUP, C_GM0 + 3 * GROUP) * proj(C_GM0 + 3 * GROUP, C_GM0 + 4 * GROUP)
    s0, s1 = st_ref[:, 0, :], st_ref[:, 1, :]
    um1 = jnp.concatenate([s1, u[:(dec_seq - 1) * nb]], axis=0)
    um2 = jnp.concatenate([s0, s1, u[:(dec_seq - 2) * nb]], axis=0)
    cv_out[...] = _conv_gate(proj, um1, um2, u, wconv_ref)
    for i in range(CONV_W - 1):
        conv_out[:, i, :] = tok(u, dec_seq - (CONV_W - 1) + i)


def _proj_sample(x, gpre, w1, gq, wq, gkv, wukt, wconv, tabs, state):
    nb, dec_seq, d = x.shape
    assert dec_seq >= CONV_W - 1 and state.shape == (nb, CONV_W - 1, GROUP)
    t = nb * dec_seq
    c = gkv.shape[1]
    out_shape = (
        jax.ShapeDtypeStruct((nb, dec_seq * N_HEADS, c), F32),
        jax.ShapeDtypeStruct((nb, dec_seq * N_HEADS, QK_ROPE), F32),
        jax.ShapeDtypeStruct((nb, dec_seq, c), F32),
        jax.ShapeDtypeStruct((nb, dec_seq, QK_ROPE), F32),
        jax.ShapeDtypeStruct((t, GROUP), F32),
        jax.ShapeDtypeStruct((t, GROUP), F32),
        jax.ShapeDtypeStruct((nb, CONV_W - 1, GROUP), F32),
    )
    return pl.pallas_call(
        _proj_sample_kernel,
        out_shape=out_shape,
        compiler_params=pltpu.CompilerParams(vmem_limit_bytes=VMEM_LIMIT_BYTES),
        name="proj_sample",
    )(x, gpre, w1, gq, wq, gkv, wukt, wconv, *tabs, state)


def _decode_attn_kernel(pt_ref, ql_ref, qp_ref, cnew_ref, knew_ref, y_hbm, x_hbm, o_ref,
                        ybuf, xbuf, sem, m_ref, l_ref, acc_ref, *, pages, slots, layer, dec_seq):
    g = pl.program_id(0)
    n_g = pl.num_programs(0)
    per_step = ql_ref.shape[0]
    n_chunks = pt_ref.shape[1] // pages
    total = per_step * n_chunks
    page = ybuf.shape[1] // pages
    rows = ql_ref.shape[1]
    dn = (((1,), (1,)), ((), ()))
    ahead = slots - 1

    def chunk_copies(bb, ch, slot, page_ids):
        cps = []
        for i in range(pages):
            pid = page_ids(bb, ch * pages + i)
            tok = pl.ds(i * page, page)
            cps.append(pltpu.make_async_copy(y_hbm.at[layer, pid], ybuf.at[slot, tok, :], sem.at[0, slot]))
            cps.append(pltpu.make_async_copy(x_hbm.at[layer, pid], xbuf.at[slot, :, tok], sem.at[1, slot]))
        return cps

    def start(bb, ch, slot):
        for cp in chunk_copies(bb, ch, slot, lambda r, j: pt_ref[r, j]):
            cp.start()

    def wait(slot):
        for cp in chunk_copies(0, 0, slot, lambda r, j: 0):
            cp.wait()

    def start_nth(step, k, slot):
        start(step * per_step + k // n_chunks, k % n_chunks, slot)

    @pl.when(g == 0)
    def _():
        for k in range(ahead):
            start_nth(0, k, k % slots)

    tok_of_row = lax.broadcasted_iota(jnp.int32, (rows, 1), 0) // N_HEADS
    for e in range(per_step):
        qlf, qpf = ql_ref[e], qp_ref[e]
        s_new = []
        for j in range(dec_seq):
            sj = (jnp.sum(qlf * cnew_ref[e, j:j + 1, :], axis=-1, keepdims=True)
                  + jnp.sum(qpf * knew_ref[e, j:j + 1, :], axis=-1, keepdims=True))
            s_new.append(jnp.where(tok_of_row >= j, sj, -jnp.inf))
        m_n = functools.reduce(jnp.maximum, s_new)
        p_n = [jnp.exp(sj - m_n) for sj in s_new]
        m_ref[e] = m_n
        l_ref[e] = functools.reduce(lambda a, v: a + v, p_n)
        acc_ref[e, 0] = functools.reduce(lambda a, v: a + v,
                                         [p_n[j] * cnew_ref[e, j:j + 1, :] for j in range(dec_seq)])
        acc_ref[e, 1] = jnp.zeros(acc_ref.shape[2:], F32)

    half = pages * page // 2
    toks = [slice(h * half, (h + 1) * half) for h in range(2)]

    def scores(e, slot):
        ql, qp = ql_ref[e].astype(BF16), qp_ref[e].astype(BF16)
        yb = [ybuf[slot, t, :].astype(BF16) for t in toks]
        xb = [xbuf[slot, :, t].astype(BF16) for t in toks]
        s = jnp.concatenate(
            [lax.dot_general(ql, yb[h], dn, preferred_element_type=F32)
             + jnp.dot(qp, xb[h], preferred_element_type=F32) for h in range(2)], axis=1)
        return e, s, yb

    def softmax(e, s, yb):
        m = m_ref[e]
        m_new = jnp.maximum(m, jnp.max(s, axis=-1, keepdims=True))
        alpha = jnp.exp(m - m_new)
        p = jnp.exp(s - m_new)
        l_ref[e] = alpha * l_ref[e] + jnp.sum(p, axis=-1, keepdims=True)
        m_ref[e] = m_new
        return e, alpha, p.astype(BF16), yb

    def values(e, alpha, p, yb):
        for h in range(2):
            acc_ref[e, h] = alpha * acc_ref[e, h] + jnp.dot(p[:, toks[h]], yb[h], preferred_element_type=F32)

    scored = None
    weighted = None
    for k in range(total):
        nxt = k + ahead
        if nxt < total:
            start_nth(g, nxt, nxt % slots)
        else:
            @pl.when(g + 1 < n_g)
            def _():
                start_nth(g + 1, nxt - total, nxt % slots)

        wait(k % slots)
        fresh = scores(k // n_chunks, k % slots)
        if weighted is not None:
            values(*weighted)
        if scored is not None:
            weighted = softmax(*scored)
        scored = fresh
    if weighted is not None:
        values(*weighted)
    values(*softmax(*scored))
    for e in range(per_step):
        o_ref[e] = (acc_ref[e, 0] + acc_ref[e, 1]) * (1.0 / l_ref[e])


def _decode_attention(page_table, ql, qp, cnew, knew, cache_ckv, cache_kpe, layer, pages, slots, per_step):
    nb, rows, c = ql.shape
    dec_seq = cnew.shape[1]
    page, r = cache_kpe.shape[2:]
    n_chunks = page_table.shape[1] // pages
    assert n_chunks * pages == page_table.shape[1] and nb % per_step == 0
    assert (per_step * n_chunks) % slots == 0 and slots - 1 <= per_step * n_chunks
    kpe_t = jnp.swapaxes(cache_kpe, 2, 3)
    per_b = lambda a: pl.BlockSpec((per_step,) + a.shape[1:], lambda g, pt: (g, 0, 0))
    grid_spec = pltpu.PrefetchScalarGridSpec(
        num_scalar_prefetch=1,
        grid=(nb // per_step,),
        in_specs=[per_b(ql), per_b(qp), per_b(cnew), per_b(knew),
                  pl.BlockSpec(memory_space=pl.ANY), pl.BlockSpec(memory_space=pl.ANY)],
        out_specs=pl.BlockSpec((per_step, rows, c), lambda g, pt: (g, 0, 0)),
        scratch_shapes=[pltpu.VMEM((slots, pages * page, c), F32), pltpu.VMEM((slots, r, pages * page), F32),
                        pltpu.SemaphoreType.DMA((2, slots)),
                        pltpu.VMEM((per_step, rows, 1), F32), pltpu.VMEM((per_step, rows, 1), F32),
                        pltpu.VMEM((per_step, 2, rows, c), F32)],
    )
    return pl.pallas_call(
        functools.partial(_decode_attn_kernel, pages=pages, slots=slots, layer=layer, dec_seq=dec_seq),
        grid_spec=grid_spec,
        out_shape=jax.ShapeDtypeStruct((nb, rows, c), F32),
        compiler_params=pltpu.CompilerParams(dimension_semantics=("arbitrary",),
                                             vmem_limit_bytes=VMEM_LIMIT_BYTES),
        name="decode_attn",
    )(page_table, ql, qp, cnew, knew, cache_ckv, kpe_t)


def _out_sample_kernel(ol_ref, gm_ref, cv_ref, x_ref, wuvt_ref, wo_ref, gpost_ref, y_ref):
    nb, dec_seq, _ = x_ref.shape
    half = gm_ref.shape[1]
    dn = (((1,), (1,)), ((), ()))

    def token_major(ref, index_of_token):
        return jnp.concatenate([ref[:, index_of_token(j), :] for j in range(dec_seq)], axis=0)

    o = jnp.concatenate(
        [lax.dot_general(token_major(ol_ref, lambda j: j * N_HEADS + h).astype(BF16), wuvt_ref[h], dn,
                         preferred_element_type=F32) for h in range(N_HEADS)], axis=1)
    mla = (o * gm_ref[...]).astype(BF16)
    y = jnp.dot(mla, wo_ref[0:half, :], preferred_element_type=F32)
    y = y + jnp.dot(cv_ref[...].astype(BF16), wo_ref[half:, :], preferred_element_type=F32)
    y = token_major(x_ref, lambda j: j) + _rms(y, gpost_ref[...])
    for j in range(dec_seq):
        y_ref[:, j, :] = y[j * nb:(j + 1) * nb]


def _out_sample(ol, gm, cv, x, wuv, wo, gpost):
    return pl.pallas_call(
        _out_sample_kernel,
        out_shape=jax.ShapeDtypeStruct(x.shape, F32),
        compiler_params=pltpu.CompilerParams(vmem_limit_bytes=VMEM_LIMIT_BYTES),
        name="out_sample",
    )(ol, gm, cv, x, wuv, wo, gpost)


def _rot_rows(w):
    r = w.shape[0] // 2
    return jnp.concatenate([-w[r:], w[:r]], axis=0)


def _pack_kernel(wint_ref, wuq_ref, wukv_ref, wout_ref,
                 w1_ref, wq_ref, wqt_ref, wk_ref, wvt_ref, wukt_ref, wuvt_ref, wo_ref):
    d = wint_ref.shape[1]
    blk = LANES

    def put_w1(col0, rows):
        w1_ref[:, col0:col0 + blk] = rows.T.astype(BF16)

    for r0 in range(0, C_KV1, blk):
        put_w1(r0, wint_ref[r0:r0 + blk, :])
    kpe = wint_ref[C_KV1:C_KV1 + QK_ROPE, :]
    put_w1(C_KP0, jnp.concatenate([jnp.zeros((QK_NOPE, d), F32), kpe, _rot_rows(kpe)], axis=0))
    src0 = C_KV1 + QK_ROPE
    for j in range(5 * GROUP // blk):
        put_w1(C_GM0 + j * blk, wint_ref[src0 + j * blk:src0 + (j + 1) * blk, :])

    uqt = wuq_ref[...].T
    ukvt = wukv_ref[...].T
    kv_lora = ukvt.shape[1]
    qk = QK_NOPE + QK_ROPE
    wqt_ref[...] = uqt.astype(BF16)
    for h in range(N_HEADS):
        pe = uqt[h * qk + QK_NOPE:(h + 1) * qk]
        q_slab = jnp.concatenate([uqt[h * qk:h * qk + QK_NOPE], pe, _rot_rows(pe)], axis=0)
        wq_ref[:, h * HEAD_SLAB:(h + 1) * HEAD_SLAB] = q_slab.T.astype(BF16)
        ukt = ukvt[h * (QK_NOPE + V_HEAD):h * (QK_NOPE + V_HEAD) + QK_NOPE]
        uvt = ukvt[h * (QK_NOPE + V_HEAD) + QK_NOPE:(h + 1) * (QK_NOPE + V_HEAD)]
        k_slab = jnp.concatenate([ukt, jnp.zeros((HEAD_SLAB - QK_NOPE, kv_lora), F32)], axis=0)
        wukt_ref[h] = k_slab.astype(BF16)
        wvt_ref[h * V_HEAD:(h + 1) * V_HEAD, :] = uvt.astype(BF16)
        wuvt_ref[h] = uvt.astype(BF16)
    for p in range(N_HEADS // 2):
        two = ukvt[2 * p * (QK_NOPE + V_HEAD):2 * (p + 1) * (QK_NOPE + V_HEAD)]
        pair = jnp.concatenate([two[0:QK_NOPE], two[QK_NOPE + V_HEAD:2 * QK_NOPE + V_HEAD]], axis=0)
        wk_ref[:, p * LANES:(p + 1) * LANES] = pair.T.astype(BF16)
    wo_ref[...] = wout_ref[...].astype(BF16)


def _pack_weights(w_in, w_uq, w_ukv, w_out):
    d = w_in.shape[0]
    q_lora, kv_lora = w_uq.shape[0], w_ukv.shape[0]
    assert w_in.shape[1] == C_KV1 + QK_ROPE + 5 * GROUP
    sds = lambda *s: jax.ShapeDtypeStruct(s, BF16)
    return pl.pallas_call(
        _pack_kernel,
        out_shape=(sds(d, D_IN_PACKED), sds(q_lora, N_HEADS * HEAD_SLAB), sds(w_uq.shape[1], q_lora),
                   sds(kv_lora, N_HEADS * QK_NOPE), sds(N_HEADS * V_HEAD, kv_lora),
                   sds(N_HEADS, HEAD_SLAB, kv_lora), sds(N_HEADS, V_HEAD, kv_lora), sds(*w_out.shape)),
        compiler_params=pltpu.CompilerParams(vmem_limit_bytes=VMEM_LIMIT_BYTES),
        name="pack_weights",
    )(w_in.T, w_uq, w_ukv, w_out)


PROMPT_TILE = 512
ATTN_TQ = 512
ATTN_TK = 512
DECODE_PAGES = 32
DECODE_SLOTS = 4
DECODE_PER_STEP = 1


def kernel(x_prompt, x_sample, cache_ckv, cache_kpe, state_conv, page_table, g_pre, w_in, g_qnorm, w_uq,
           g_kvnorm, w_ukv, w_conv, w_out, g_post):
    depth = w_in.shape[0]
    nb, seq, d = x_prompt.shape
    db, dec_seq, _ = x_sample.shape
    past_len = page_table.shape[1] * cache_ckv.shape[2]
    c = cache_ckv.shape[3]

    tabs_p = _rope_tables(jnp.arange(seq, dtype=jnp.int32))
    tabs_s = _rope_tables(jnp.repeat(past_len + jnp.arange(dec_seq, dtype=jnp.int32), db))

    xp = x_prompt.reshape(nb * seq, d)
    xs = x_sample
    outs = [[] for _ in range(6)]
    for l in range(depth):
        w1, wq, wqt, wk, wvt, wukt, wuvt, wo = _pack_weights(w_in[l], w_uq[l], w_ukv[l], w_out[l])
        gpre, gq, gkv, gpost = g_pre[l][None], g_qnorm[l][None], g_kvnorm[l][None], g_post[l][None]

        qt, k, vt, ckv_p, kpe_p, gm, cv, conv_p = _proj_prompt(
            xp, seq, gpre, w1, gq, wqt, gkv, wk, wvt, w_conv[l], tabs_p, PROMPT_TILE)
        xp = _attention_out(qt, k, vt, gm, cv, xp, wo, gpost, nb, seq, ATTN_TQ, ATTN_TK)
        outs[0].append(ckv_p.reshape(nb, seq, c))
        outs[1].append(jnp.swapaxes(kpe_p, 1, 2))
        outs[2].append(conv_p)

        ql, qp, ckv_s, kpe_s, gm_s, cv_s, conv_s = _proj_sample(
            xs, gpre, w1, gq, wq, gkv, wukt, w_conv[l], tabs_s[:4], state_conv[l].astype(F32))
        ol = _decode_attention(page_table, ql, qp, ckv_s, kpe_s, cache_ckv, cache_kpe, l,
                               DECODE_PAGES, DECODE_SLOTS, DECODE_PER_STEP)
        xs = _out_sample(ol, gm_s, cv_s, xs, wuvt, wo, gpost)
        outs[3].append(ckv_s)
        outs[4].append(kpe_s)
        outs[5].append(conv_s)

    return (xp.reshape(nb, seq, d), xs, *[jnp.stack(o_) for o_ in outs])
```

```python
import functools

import jax
import jax.numpy as jnp
import numpy as np
from jax import lax
from jax.experimental import pallas as pl
from jax.experimental.pallas import tpu as pltpu

N_HEADS = 8
QK_NOPE = 64
QK_ROPE = 32
V_HEAD = 64
CONV_W = 3
ROPE_THETA = 10000.0
EPS = 1e-6
ATTN_SCALE = (QK_NOPE + QK_ROPE) ** -0.5

LANES = 128
HEAD_SLAB = LANES
ONES_ROWS = 16
LOG2_E = 1.4426950408889634
VMEM_LIMIT_BYTES = 56 * 1024 * 1024

F32 = jnp.float32
BF16 = jnp.bfloat16


def _silu(x):
    return x * (1.0 / (1.0 + jnp.exp(-x)))


def _rms(x, g):
    return x * lax.rsqrt(jnp.mean(x * x, axis=-1, keepdims=True) + EPS) * g


def _inv_freq():
    half = ROPE_THETA ** (-jnp.arange(0, QK_ROPE, 2, dtype=F32) / QK_ROPE)
    return jnp.concatenate([half, half])


def _rope_table_kernel(pos_ref, invf_ref, ck1_ref, ck2_ref, cq1t_ref, cq2t_ref):
    ang = invf_ref[...] * pos_ref[...]
    c, s = jnp.cos(ang), jnp.sin(ang)
    t = ang.shape[1]
    nope = jnp.ones((QK_NOPE, t), F32)
    zn = jnp.zeros((QK_NOPE, t), F32)
    zr = jnp.zeros((HEAD_SLAB - QK_NOPE - QK_ROPE, t), F32)

    def slab(first, mid):
        return jnp.concatenate([first, mid, zr], axis=0)

    ck1_ref[...] = slab(zn, c).T
    ck2_ref[...] = slab(zn, s).T
    cq1t_ref[...] = slab(nope * (ATTN_SCALE * LOG2_E), c * (ATTN_SCALE * LOG2_E))
    cq2t_ref[...] = slab(zn, s * (ATTN_SCALE * LOG2_E))


def _rope_tables(t):
    out = jax.ShapeDtypeStruct((t, LANES), F32)
    out_t = jax.ShapeDtypeStruct((LANES, t), F32)
    return pl.pallas_call(
        _rope_table_kernel,
        out_shape=(out, out, out_t, out_t),
        name="rope_tables",
    )(np.arange(t, dtype=np.float32).reshape(1, t), _inv_freq().reshape(QK_ROPE, 1))


def _token_major_rope_tables(pos_ref, invf_ref, nb):
    ang = pos_ref[...] * invf_ref[...]
    lane = lax.broadcasted_iota(jnp.int32, ang.shape, 1)
    is_rope = (lane >= QK_NOPE) & (lane < QK_NOPE + QK_ROPE)
    c = jnp.where(is_rope, jnp.cos(ang), 0.0)
    s = jnp.where(is_rope, jnp.sin(ang), 0.0)

    def rows(tab):
        return jnp.concatenate([jnp.broadcast_to(tab[j:j + 1], (nb, LANES)) for j in range(tab.shape[0])], axis=0)

    cq1 = jnp.where(lane < QK_NOPE, ATTN_SCALE, c * ATTN_SCALE)
    return rows(cq1), rows(s * ATTN_SCALE), rows(c), rows(s)


C_Q0, C_Q1 = 0, 384
C_KV0, C_KV1 = 384, 640
C_KP0, C_KP1 = 640, 768
C_GM0 = 768
GROUP = 512
D_IN_PACKED = C_GM0 + 5 * GROUP


def _rope_slab(blk, c1, c2):
    return blk * c1 + pltpu.roll(blk, LANES - QK_ROPE, 1) * c2


def _project_common(x, gpre_ref, w1_ref, gq_ref, gkv_ref, ck1, ck2):
    xn = _rms(x, gpre_ref[...]).astype(BF16)

    def proj(c0, c1):
        return jnp.dot(xn, w1_ref[:, c0:c1], preferred_element_type=F32)

    cqn = _rms(proj(C_Q0, C_Q1), gq_ref[...]).astype(BF16)
    ckv = _rms(proj(C_KV0, C_KV1), gkv_ref[...])
    kpe_blk = _rope_slab(proj(C_KP0, C_KP1), ck1, ck2)
    return proj, cqn, ckv, kpe_blk


def _conv_gate(proj, um1, um2, u, wconv_ref):
    y = wconv_ref[0] * um2 + wconv_ref[1] * um1 + wconv_ref[2] * u
    bg = proj(C_GM0 + GROUP, C_GM0 + 2 * GROUP)
    gc = proj(C_GM0 + 4 * GROUP, C_GM0 + 5 * GROUP)
    return bg * y * _silu(gc)


def _proj_prompt_kernel(x_ref, gpre_ref, w1_ref, gq_ref, wqt_ref, gkv_ref, wk_ref, wvt_ref, wconv_ref,
                        cq1t_ref, cq2t_ref, ck1_ref, ck2_ref,
                        qt_out, k_out, vt_out, ckv_out, kpet_out, gm_out, cv_out, conv_out,
                        carry_ref, *, tiles_per_seq):
    tm = x_ref.shape[0]
    step = pl.program_id(0)
    dn = (((1,), (1,)), ((), ()))

    @pl.when(step % tiles_per_seq == 0)
    def _():
        carry_ref[...] = jnp.zeros_like(carry_ref)

    xn = _rms(x_ref[...], gpre_ref[...]).astype(BF16)

    def proj(c0, c1):
        return jnp.dot(xn, w1_ref[:, c0:c1], preferred_element_type=F32)

    z_cq = proj(C_Q0, C_Q1)
    z_ckv = proj(C_KV0, C_KV1)
    cqn = _rms(z_cq, gq_ref[...]).astype(BF16)
    qt = lax.dot_general(wqt_ref[...], cqn, dn, preferred_element_type=F32)
    ckv = _rms(z_ckv, gkv_ref[...])
    ckv_out[...] = ckv
    ckvb = ckv.astype(BF16)
    z_kp = proj(C_KP0, C_KP1)
    knope = jnp.dot(ckvb, wk_ref[...], preferred_element_type=F32)
    qk = QK_NOPE + QK_ROPE
    nope_scale = cq1t_ref[0:QK_NOPE, :]
    cos_t, sin_t = cq1t_ref[QK_NOPE:qk, :], cq2t_ref[QK_NOPE:qk, :]
    for h in range(N_HEADS):
        nope, pe = qt[h * qk:h * qk + QK_NOPE, :], qt[h * qk + QK_NOPE:(h + 1) * qk, :]
        qt_out[h * HEAD_SLAB:h * HEAD_SLAB + QK_NOPE, :] = (nope * nope_scale).astype(BF16)
        qt_out[h * HEAD_SLAB + QK_NOPE:h * HEAD_SLAB + qk, :] = (pe * cos_t + _rot_rows(pe) * sin_t).astype(BF16)
        qt_out[h * HEAD_SLAB + qk:(h + 1) * HEAD_SLAB, :] = jnp.zeros((HEAD_SLAB - qk, tm), BF16)
    vt = lax.dot_general(wvt_ref[...], ckvb, dn, preferred_element_type=F32)
    z_gm = proj(C_GM0, C_GM0 + GROUP)
    kpe_blk = _rope_slab(z_kp, ck1_ref[...], ck2_ref[...])
    kpet_out[0] = kpe_blk.T[QK_NOPE:QK_NOPE + QK_ROPE, :]
    lane = lax.broadcasted_iota(jnp.int32, kpe_blk.shape, 1)
    for p in range(N_HEADS // 2):
        two = knope[:, p * LANES:(p + 1) * LANES]
        for h, nope_first in ((2 * p, two), (2 * p + 1, pltpu.roll(two, LANES - QK_NOPE, 1))):
            k_out[:, h * HEAD_SLAB:(h + 1) * HEAD_SLAB] = jnp.where(lane < QK_NOPE, nope_first, kpe_blk).astype(BF16)
    vt_out[...] = vt.astype(BF16)
    z_cg = proj(C_GM0 + 2 * GROUP, C_GM0 + 3 * GROUP)
    z_h = proj(C_GM0 + 3 * GROUP, C_GM0 + 4 * GROUP)
    gm_out[...] = _silu(z_gm).astype(BF16)
    z_bg = proj(C_GM0 + GROUP, C_GM0 + 2 * GROUP)
    z_gc = proj(C_GM0 + 4 * GROUP, C_GM0 + 5 * GROUP)

    u = z_cg * z_h
    prev = carry_ref[...]
    p1, p2 = prev[7:8, :], prev[6:7, :]
    row = lax.broadcasted_iota(jnp.int32, u.shape, 0)
    um1 = jnp.where(row == 0, p1, pltpu.roll(u, 1, 0))
    um2 = jnp.where(row == 0, p2, jnp.where(row == 1, p1, pltpu.roll(u, 2, 0)))
    y = wconv_ref[0] * um2 + wconv_ref[1] * um1 + wconv_ref[2] * u
    cv_out[...] = (z_bg * y * _silu(z_gc)).astype(BF16)
    carry_ref[...] = u[tm - 8:tm, :]
    conv_out[0] = u[tm - (CONV_W - 1):tm, :]


def _const_spec(shape):
    nd = len(shape)
    return pl.BlockSpec(shape, lambda *_: (0,) * nd, pipeline_mode=pl.Buffered(1))


def _proj_prompt(x2d, seq, gpre, w1, gq, wqt, gkv, wk, wvt, wconv, tabs, tm):
    t, d = x2d.shape
    nb = t // seq
    hs = N_HEADS * HEAD_SLAB
    hv = N_HEADS * V_HEAD
    ck1, ck2, cq1t, cq2t = tabs
    row = lambda w: pl.BlockSpec((tm, w), lambda i: (i, 0))
    col = lambda h: pl.BlockSpec((h, tm), lambda i: (0, i))
    tab = pl.BlockSpec((tm, LANES), lambda i: (i % (seq // tm), 0))
    tab_t = pl.BlockSpec((LANES, tm), lambda i: (0, i % (seq // tm)))
    out_shape = (
        jax.ShapeDtypeStruct((hs, t), BF16),
        jax.ShapeDtypeStruct((t, hs), BF16),
        jax.ShapeDtypeStruct((hv, t), BF16),
        jax.ShapeDtypeStruct((t, gkv.shape[1]), F32),
        jax.ShapeDtypeStruct((nb, QK_ROPE, seq), F32),
        jax.ShapeDtypeStruct((t, GROUP), BF16),
        jax.ShapeDtypeStruct((t, GROUP), BF16),
        jax.ShapeDtypeStruct((nb, CONV_W - 1, GROUP), F32),
    )
    return pl.pallas_call(
        functools.partial(_proj_prompt_kernel, tiles_per_seq=seq // tm),
        grid=(t // tm,),
        in_specs=[row(d), _const_spec(gpre.shape), _const_spec(w1.shape), _const_spec(gq.shape),
                  _const_spec(wqt.shape), _const_spec(gkv.shape), _const_spec(wk.shape), _const_spec(wvt.shape),
                  _const_spec(wconv.shape), tab_t, tab_t, tab, tab],
        out_specs=(col(hs), row(hs), col(hv), row(gkv.shape[1]),
                   pl.BlockSpec((1, QK_ROPE, tm), lambda i: (i // (seq // tm), 0, i % (seq // tm))),
                   row(GROUP), row(GROUP),
                   pl.BlockSpec((1, CONV_W - 1, GROUP), lambda i: (i // (seq // tm), 0, 0))),
        out_shape=out_shape,
        scratch_shapes=[pltpu.VMEM((8, GROUP), F32)],
        compiler_params=pltpu.CompilerParams(dimension_semantics=("arbitrary",),
                                             vmem_limit_bytes=VMEM_LIMIT_BYTES),
        name="proj_prompt",
    )(x2d, gpre, w1, gq, wqt, gkv, wk, wvt, wconv, cq1t, cq2t, ck1, ck2)


def _attn_kernel(qt_ref, k_ref, vt_ref, gm_ref, cv_ref, x_ref, wo_ref, gpost_ref, y_ref,
                 s_ref, m_ref, acc_ref, *, tq, tk):
    i = pl.program_id(1)
    hq = tq // 2
    tri = (lax.broadcasted_iota(jnp.int32, (hq, hq), 0) <= lax.broadcasted_iota(jnp.int32, (hq, hq), 1))

    def slab(h):
        return slice(h * HEAD_SLAB, (h + 1) * HEAD_SLAB)

    def values(h, toks):
        ones = jnp.ones((ONES_ROWS, toks.size), BF16)
        return jnp.concatenate([vt_ref[h * V_HEAD:(h + 1) * V_HEAD, toks], ones], 0)

    def scores(h, toks):
        return jnp.dot(k_ref[toks, slab(h)], qt_ref[slab(h), :], preferred_element_type=F32)

    def update(h, toks, st):
        m = m_ref[h]
        m_new = jnp.maximum(m, jnp.max(st, axis=0, keepdims=True))
        pt = jnp.exp2(st - m_new).astype(BF16)
        m_ref[h] = m_new
        acc_ref[h] = jnp.exp2(m - m_new) * acc_ref[h] + jnp.dot(values(h, toks), pt, preferred_element_type=F32)

    base = pl.multiple_of(i * tq, tq)
    ka, kb = pl.ds(base, hq), pl.ds(base + hq, hq)

    def diagonal_scores(h):
        return (jnp.dot(k_ref[ka, slab(h)], qt_ref[slab(h), :], preferred_element_type=F32),
                jnp.dot(k_ref[kb, slab(h)], qt_ref[slab(h), hq:], preferred_element_type=F32))

    def diagonal_init(h, a, b_):
        a_lo = jnp.where(tri, a[:, :hq], -jnp.inf)
        a_hi = a[:, hq:]
        b_ = jnp.where(tri, b_, -jnp.inf)
        m_lo = jnp.max(a_lo, axis=0, keepdims=True)
        m_hi = jnp.maximum(jnp.max(a_hi, axis=0, keepdims=True), jnp.max(b_, axis=0, keepdims=True))
        pa = jnp.concatenate([jnp.exp2(a_lo - m_lo), jnp.exp2(a_hi - m_hi)], axis=1).astype(BF16)
        pb = jnp.exp2(b_ - m_hi).astype(BF16)
        acc_a = jnp.dot(values(h, ka), pa, preferred_element_type=F32)
        acc_b = jnp.dot(values(h, kb), pb, preferred_element_type=F32)
        m_ref[h] = jnp.concatenate([m_lo, m_hi], axis=1)
        acc_ref[h] = jnp.concatenate([acc_a[:, :hq], acc_a[:, hq:] + acc_b], axis=1)

    n_full = i * (tq // tk)

    def tile_tokens(j):
        return pl.ds(pl.multiple_of(j * tk, tk), tk)

    cur = diagonal_scores(0)
    for h in range(N_HEADS):
        if h + 1 < N_HEADS:
            ahead = diagonal_scores(h + 1)
        else:
            s_ref[0] = scores(0, tile_tokens(0))
        diagonal_init(h, *cur)
        cur = ahead

    def full_tile(j, carry):
        toks = tile_tokens(j)
        nxt = tile_tokens(jnp.minimum(j + 1, n_full - 1))
        for h in range(N_HEADS):
            if h + 1 < N_HEADS:
                s_ref[h + 1] = scores(h + 1, toks)
                update(h, toks, s_ref[h])
            else:
                cur = s_ref[h]
                s_ref[0] = scores(0, nxt)
                update(h, toks, cur)
        return carry

    lax.fori_loop(0, n_full, full_tile, 0)
    outs = [acc_ref[h, :V_HEAD] * (1.0 / acc_ref[h, V_HEAD:V_HEAD + 1]) for h in range(N_HEADS)]
    pairs = [jnp.concatenate(outs[2 * p:2 * p + 2], axis=0).T for p in range(N_HEADS // 2)]

    half = gm_ref.shape[1]
    mla = (jnp.concatenate(pairs, axis=1) * gm_ref[...].astype(F32)).astype(BF16)
    y = jnp.dot(mla, wo_ref[0:half, :], preferred_element_type=F32)
    y = y + jnp.dot(cv_ref[...], wo_ref[half:, :], preferred_element_type=F32)
    y_ref[...] = x_ref[...] + _rms(y, gpost_ref[...])


def _attention_out(qt, k, vt, gm, cv, x2d, wo, gpost, nb, seq, tq, tk):
    hs, t = qt.shape
    hv = vt.shape[0]
    d = x2d.shape[1]
    assert N_HEADS % 2 == 0 and tq % tk == 0
    row = lambda w: pl.BlockSpec((tq, w), lambda b, i: (b * (seq // tq) + i, 0))
    return pl.pallas_call(
        functools.partial(_attn_kernel, tq=tq, tk=tk),
        grid=(nb, seq // tq),
        in_specs=[pl.BlockSpec((hs, tq), lambda b, i: (0, b * (seq // tq) + i)),
                  pl.BlockSpec((seq, hs), lambda b, i: (b, 0)),
                  pl.BlockSpec((hv, seq), lambda b, i: (0, b)),
                  row(gm.shape[1]), row(cv.shape[1]), row(d), _const_spec(wo.shape), _const_spec(gpost.shape)],
        out_specs=row(d),
        out_shape=jax.ShapeDtypeStruct((t, d), F32),
        scratch_shapes=[pltpu.VMEM((N_HEADS, tk, tq), F32), pltpu.VMEM((N_HEADS, 1, tq), F32),
                        pltpu.VMEM((N_HEADS, V_HEAD + ONES_ROWS, tq), F32)],
        compiler_params=pltpu.CompilerParams(dimension_semantics=("arbitrary", "arbitrary"),
                                             vmem_limit_bytes=VMEM_LIMIT_BYTES),
        name="prompt_attn",
    )(qt, k, vt, gm, cv, x2d, wo, gpost)


def _proj_sample_kernel(x_ref, gpre_ref, w1_ref, gq_ref, wq_ref, gkv_ref, wukt_ref, wconv_ref,
                        pos_ref, invf_ref, st_ref,
                        ql_out, qp_out, ckv_out, kpe_out, gm_out, cv_out, conv_out):
    nb, dec_seq, _ = x_ref.shape

    def tok(a, j):
        return a[j * nb:(j + 1) * nb]

    x = jnp.concatenate([x_ref[:, j, :] for j in range(dec_seq)], axis=0)
    cq1, cq2, ck1, ck2 = _token_major_rope_tables(pos_ref, invf_ref, nb)
    proj, cqn, ckv, kpe_blk = _project_common(x, gpre_ref, w1_ref, gq_ref, gkv_ref, ck1, ck2)
    qraw = jnp.dot(cqn, wq_ref[...], preferred_element_type=F32)
    for h in range(N_HEADS):
        qh = _rope_slab(qraw[:, h * HEAD_SLAB:(h + 1) * HEAD_SLAB], cq1, cq2)
        qlh = jnp.dot(qh.astype(BF16), wukt_ref[h], preferred_element_type=F32)
        qph = pltpu.roll(qh, LANES - QK_NOPE, 1)[:, 0:QK_ROPE]
        for j in range(dec_seq):
            ql_out[:, j * N_HEADS + h, :] = tok(qlh, j)
            qp_out[:, j * N_HEADS + h, :] = tok(qph, j)
    kpe = kpe_blk[:, QK_NOPE:QK_NOPE + QK_ROPE]
    for j in range(dec_seq):
        ckv_out[:, j, :] = tok(ckv, j)
        kpe_out[:, j, :] = tok(kpe, j)
    gm_out[...] = _silu(proj(C_GM0, C_GM0 + GROUP))

    u = proj(C_GM0 + 2 * GROUP, C_GM0 + 3 * GROUP) * proj(C_GM0 + 3 * GROUP, C_GM0 + 4 * GROUP)
    s0, s1 = st_ref[:, 0, :], st_ref[:, 1, :]
    um1 = jnp.concatenate([s1, u[:(dec_seq - 1) * nb]], axis=0)
    um2 = jnp.concatenate([s0, s1, u[:(dec_seq - 2) * nb]], axis=0)
    cv_out[...] = _conv_gate(proj, um1, um2, u, wconv_ref)
    for i in range(CONV_W - 1):
        conv_out[:, i, :] = tok(u, dec_seq - (CONV_W - 1) + i)


def _proj_sample(x, gpre, w1, gq, wq, gkv, wukt, wconv, first_pos, state):
    nb, dec_seq, d = x.shape
    assert dec_seq >= CONV_W - 1 and state.shape == (nb, CONV_W - 1, GROUP)
    pos = (first_pos + np.arange(dec_seq, dtype=np.float32)).reshape(dec_seq, 1)
    invf = jnp.pad(_inv_freq(), (QK_NOPE, LANES - QK_NOPE - QK_ROPE)).reshape(1, LANES)
    t = nb * dec_seq
    c = gkv.shape[1]
    out_shape = (
        jax.ShapeDtypeStruct((nb, dec_seq * N_HEADS, c), F32),
        jax.ShapeDtypeStruct((nb, dec_seq * N_HEADS, QK_ROPE), F32),
        jax.ShapeDtypeStruct((nb, dec_seq, c), F32),
        jax.ShapeDtypeStruct((nb, dec_seq, QK_ROPE), F32),
        jax.ShapeDtypeStruct((t, GROUP), F32),
        jax.ShapeDtypeStruct((t, GROUP), F32),
        jax.ShapeDtypeStruct((nb, CONV_W - 1, GROUP), F32),
    )
    return pl.pallas_call(
        _proj_sample_kernel,
        out_shape=out_shape,
        compiler_params=pltpu.CompilerParams(vmem_limit_bytes=VMEM_LIMIT_BYTES),
        name="proj_sample",
    )(x, gpre, w1, gq, wq, gkv, wukt, wconv, pos, invf, state)


def _decode_attn_kernel(pt_ref, ql_ref, qp_ref, cnew_ref, knew_ref, y_hbm, x_hbm, o_ref,
                        ybuf, xbuf, sem, m_ref, l_ref, acc_ref, *, pages, slots, layer, dec_seq):
    g = pl.program_id(0)
    n_g = pl.num_programs(0)
    per_step = ql_ref.shape[0]
    n_chunks = pt_ref.shape[1] // pages
    total = per_step * n_chunks
    page = ybuf.shape[1] // pages
    rows = ql_ref.shape[1]
    dn = (((1,), (1,)), ((), ()))
    ahead = slots - 1

    def chunk_copies(bb, ch, slot, page_ids):
        cps = []
        for i in range(pages):
            pid = page_ids(bb, ch * pages + i)
            tok = pl.ds(i * page, page)
            cps.append(pltpu.make_async_copy(y_hbm.at[layer, pid], ybuf.at[slot, tok, :], sem.at[0, slot]))
            cps.append(pltpu.make_async_copy(x_hbm.at[layer, pid], xbuf.at[slot, :, tok], sem.at[1, slot]))
        return cps

    def start(bb, ch, slot):
        for cp in chunk_copies(bb, ch, slot, lambda r, j: pt_ref[r, j]):
            cp.start()

    def wait(slot):
        for cp in chunk_copies(0, 0, slot, lambda r, j: 0):
            cp.wait()

    def start_nth(step, k, slot):
        start(step * per_step + k // n_chunks, k % n_chunks, slot)

    @pl.when(g == 0)
    def _():
        for k in range(ahead):
            start_nth(0, k, k % slots)

    tok_of_row = lax.broadcasted_iota(jnp.int32, (rows, 1), 0) // N_HEADS
    for e in range(per_step):
        qlf, qpf = ql_ref[e], qp_ref[e]
        s_new = []
        for j in range(dec_seq):
            sj = (jnp.sum(qlf * cnew_ref[e, j:j + 1, :], axis=-1, keepdims=True)
                  + jnp.sum(qpf * knew_ref[e, j:j + 1, :], axis=-1, keepdims=True))
            s_new.append(jnp.where(tok_of_row >= j, sj, -jnp.inf))
        m_n = functools.reduce(jnp.maximum, s_new)
        p_n = [jnp.exp(sj - m_n) for sj in s_new]
        m_ref[e] = m_n
        l_ref[e] = functools.reduce(lambda a, v: a + v, p_n)
        acc_ref[e, 0] = functools.reduce(lambda a, v: a + v,
                                         [p_n[j] * cnew_ref[e, j:j + 1, :] for j in range(dec_seq)])
        acc_ref[e, 1] = jnp.zeros(acc_ref.shape[2:], F32)

    half = pages * page // 2
    toks = [slice(h * half, (h + 1) * half) for h in range(2)]

    def scores(e, slot):
        ql, qp = ql_ref[e].astype(BF16), qp_ref[e].astype(BF16)
        yb = [ybuf[slot, t, :].astype(BF16) for t in toks]
        xb = [xbuf[slot, :, t].astype(BF16) for t in toks]
        s = jnp.concatenate(
            [lax.dot_general(ql, yb[h], dn, preferred_element_type=F32)
             + jnp.dot(qp, xb[h], preferred_element_type=F32) for h in range(2)], axis=1)
        return e, s, yb

    def softmax(e, s, yb):
        m = m_ref[e]
        m_new = jnp.maximum(m, jnp.max(s, axis=-1, keepdims=True))
        alpha = jnp.exp(m - m_new)
        p = jnp.exp(s - m_new)
        l_ref[e] = alpha * l_ref[e] + jnp.sum(p, axis=-1, keepdims=True)
        m_ref[e] = m_new
        return e, alpha, p.astype(BF16), yb

    def values(e, alpha, p, yb):
        for h in range(2):
            acc_ref[e, h] = alpha * acc_ref[e, h] + jnp.dot(p[:, toks[h]], yb[h], preferred_element_type=F32)

    scored = None
    weighted = None
    for k in range(total):
        nxt = k + ahead
        if nxt < total:
            start_nth(g, nxt, nxt % slots)
        else:
            @pl.when(g + 1 < n_g)
            def _():
                start_nth(g + 1, nxt - total, nxt % slots)

        wait(k % slots)
        fresh = scores(k // n_chunks, k % slots)
        if weighted is not None:
            values(*weighted)
        if scored is not None:
            weighted = softmax(*scored)
        scored = fresh
    if weighted is not None:
        values(*weighted)
    values(*softmax(*scored))
    for e in range(per_step):
        o_ref[e] = (acc_ref[e, 0] + acc_ref[e, 1]) * (1.0 / l_ref[e])


def _decode_attention(page_table, ql, qp, cnew, knew, cache_ckv, cache_kpe, layer, pages, slots, per_step):
    nb, rows, c = ql.shape
    dec_seq = cnew.shape[1]
    page, r = cache_kpe.shape[2:]
    n_chunks = page_table.shape[1] // pages
    assert n_chunks * pages == page_table.shape[1] and nb % per_step == 0
    assert (per_step * n_chunks) % slots == 0 and slots - 1 <= per_step * n_chunks
    kpe_t = jnp.swapaxes(cache_kpe, 2, 3)
    per_b = lambda a: pl.BlockSpec((per_step,) + a.shape[1:], lambda g, pt: (g, 0, 0))
    grid_spec = pltpu.PrefetchScalarGridSpec(
        num_scalar_prefetch=1,
        grid=(nb // per_step,),
        in_specs=[per_b(ql), per_b(qp), per_b(cnew), per_b(knew),
                  pl.BlockSpec(memory_space=pl.ANY), pl.BlockSpec(memory_space=pl.ANY)],
        out_specs=pl.BlockSpec((per_step, rows, c), lambda g, pt: (g, 0, 0)),
        scratch_shapes=[pltpu.VMEM((slots, pages * page, c), F32), pltpu.VMEM((slots, r, pages * page), F32),
                        pltpu.SemaphoreType.DMA((2, slots)),
                        pltpu.VMEM((per_step, rows, 1), F32), pltpu.VMEM((per_step, rows, 1), F32),
                        pltpu.VMEM((per_step, 2, rows, c), F32)],
    )
    return pl.pallas_call(
        functools.partial(_decode_attn_kernel, pages=pages, slots=slots, layer=layer, dec_seq=dec_seq),
        grid_spec=grid_spec,
        out_shape=jax.ShapeDtypeStruct((nb, rows, c), F32),
        compiler_params=pltpu.CompilerParams(dimension_semantics=("arbitrary",),
                                             vmem_limit_bytes=VMEM_LIMIT_BYTES),
        name="decode_attn",
    )(page_table, ql, qp, cnew, knew, cache_ckv, kpe_t)


def _out_sample_kernel(ol_ref, gm_ref, cv_ref, x_ref, wuvt_ref, wo_ref, gpost_ref, y_ref):
    nb, dec_seq, _ = x_ref.shape
    half = gm_ref.shape[1]
    dn = (((1,), (1,)), ((), ()))

    def token_major(ref, index_of_token):
        return jnp.concatenate([ref[:, index_of_token(j), :] for j in range(dec_seq)], axis=0)

    o = jnp.concatenate(
        [lax.dot_general(token_major(ol_ref, lambda j: j * N_HEADS + h).astype(BF16), wuvt_ref[h], dn,
                         preferred_element_type=F32) for h in range(N_HEADS)], axis=1)
    mla = (o * gm_ref[...]).astype(BF16)
    y = jnp.dot(mla, wo_ref[0:half, :], preferred_element_type=F32)
    y = y + jnp.dot(cv_ref[...].astype(BF16), wo_ref[half:, :], preferred_element_type=F32)
    y = token_major(x_ref, lambda j: j) + _rms(y, gpost_ref[...])
    for j in range(dec_seq):
        y_ref[:, j, :] = y[j * nb:(j + 1) * nb]


def _out_sample(ol, gm, cv, x, wuv, wo, gpost):
    return pl.pallas_call(
        _out_sample_kernel,
        out_shape=jax.ShapeDtypeStruct(x.shape, F32),
        compiler_params=pltpu.CompilerParams(vmem_limit_bytes=VMEM_LIMIT_BYTES),
        name="out_sample",
    )(ol, gm, cv, x, wuv, wo, gpost)


def _rot_rows(w):
    r = w.shape[0] // 2
    return jnp.concatenate([-w[r:], w[:r]], axis=0)


def _pack_kernel(wint_ref, wuq_ref, wukv_ref, wout_ref,
                 w1_ref, wq_ref, wqt_ref, wk_ref, wvt_ref, wukt_ref, wuvt_ref, wo_ref):
    d = wint_ref.shape[1]
    blk = LANES

    def put_w1(col0, rows):
        w1_ref[:, col0:col0 + blk] = rows.T.astype(BF16)

    for r0 in range(0, C_KV1, blk):
        put_w1(r0, wint_ref[r0:r0 + blk, :])
    kpe = wint_ref[C_KV1:C_KV1 + QK_ROPE, :]
    put_w1(C_KP0, jnp.concatenate([jnp.zeros((QK_NOPE, d), F32), kpe, _rot_rows(kpe)], axis=0))
    src0 = C_KV1 + QK_ROPE
    for j in range(5 * GROUP // blk):
        put_w1(C_GM0 + j * blk, wint_ref[src0 + j * blk:src0 + (j + 1) * blk, :])

    uqt = wuq_ref[...].T
    ukvt = wukv_ref[...].T
    kv_lora = ukvt.shape[1]
    qk = QK_NOPE + QK_ROPE
    wqt_ref[...] = uqt.astype(BF16)
    for h in range(N_HEADS):
        pe = uqt[h * qk + QK_NOPE:(h + 1) * qk]
        q_slab = jnp.concatenate([uqt[h * qk:h * qk + QK_NOPE], pe, _rot_rows(pe)], axis=0)
        wq_ref[:, h * HEAD_SLAB:(h + 1) * HEAD_SLAB] = q_slab.T.astype(BF16)
        ukt = ukvt[h * (QK_NOPE + V_HEAD):h * (QK_NOPE + V_HEAD) + QK_NOPE]
        uvt = ukvt[h * (QK_NOPE + V_HEAD) + QK_NOPE:(h + 1) * (QK_NOPE + V_HEAD)]
        k_slab = jnp.concatenate([ukt, jnp.zeros((HEAD_SLAB - QK_NOPE, kv_lora), F32)], axis=0)
        wukt_ref[h] = k_slab.astype(BF16)
        wvt_ref[h * V_HEAD:(h + 1) * V_HEAD, :] = uvt.astype(BF16)
        wuvt_ref[h] = uvt.astype(BF16)
    for p in range(N_HEADS // 2):
        two = ukvt[2 * p * (QK_NOPE + V_HEAD):2 * (p + 1) * (QK_NOPE + V_HEAD)]
        pair = jnp.concatenate([two[0:QK_NOPE], two[QK_NOPE + V_HEAD:2 * QK_NOPE + V_HEAD]], axis=0)
        wk_ref[:, p * LANES:(p + 1) * LANES] = pair.T.astype(BF16)
    wo_ref[...] = wout_ref[...].astype(BF16)


def _pack_weights(w_in, w_uq, w_ukv, w_out):
    d = w_in.shape[0]
    q_lora, kv_lora = w_uq.shape[0], w_ukv.shape[0]
    assert w_in.shape[1] == C_KV1 + QK_ROPE + 5 * GROUP
    sds = lambda *s: jax.ShapeDtypeStruct(s, BF16)
    return pl.pallas_call(
        _pack_kernel,
        out_shape=(sds(d, D_IN_PACKED), sds(q_lora, N_HEADS * HEAD_SLAB), sds(w_uq.shape[1], q_lora),
                   sds(kv_lora, N_HEADS * QK_NOPE), sds(N_HEADS * V_HEAD, kv_lora),
                   sds(N_HEADS, HEAD_SLAB, kv_lora), sds(N_HEADS, V_HEAD, kv_lora), sds(*w_out.shape)),
        compiler_params=pltpu.CompilerParams(vmem_limit_bytes=VMEM_LIMIT_BYTES),
        name="pack_weights",
    )(w_in.T, w_uq, w_ukv, w_out)


PROMPT_TILE = 512
ATTN_TQ = 512
ATTN_TK = 512
DECODE_PAGES = 32
DECODE_SLOTS = 4
DECODE_PER_STEP = 1


def kernel(x_prompt, x_sample, cache_ckv, cache_kpe, state_conv, page_table, g_pre, w_in, g_qnorm, w_uq,
           g_kvnorm, w_ukv, w_conv, w_out, g_post):
    depth = w_in.shape[0]
    nb, seq, d = x_prompt.shape
    db, dec_seq, _ = x_sample.shape
    past_len = page_table.shape[1] * cache_ckv.shape[2]
    c = cache_ckv.shape[3]

    tabs_p = _rope_tables(seq)
    taps = jnp.swapaxes(w_conv, 0, 1)[:, :, None, :]

    xp = x_prompt.reshape(nb * seq, d)
    xs = x_sample
    outs = [[] for _ in range(6)]
    for l in range(depth):
        w1, wq, wqt, wk, wvt, wukt, wuvt, wo = _pack_weights(w_in[l], w_uq[l], w_ukv[l], w_out[l])
        gpre, gq, gkv, gpost = g_pre[l][None], g_qnorm[l][None], g_kvnorm[l][None], g_post[l][None]

        qt, k, vt, ckv_p, kpe_p, gm, cv, conv_p = _proj_prompt(
            xp, seq, gpre, w1, gq, wqt, gkv, wk, wvt, taps[:, l], tabs_p, PROMPT_TILE)
        xp = _attention_out(qt, k, vt, gm, cv, xp, wo, gpost, nb, seq, ATTN_TQ, ATTN_TK)
        outs[0].append(ckv_p.reshape(nb, seq, c))
        outs[1].append(jnp.swapaxes(kpe_p, 1, 2))
        outs[2].append(conv_p)

        ql, qp, ckv_s, kpe_s, gm_s, cv_s, conv_s = _proj_sample(
            xs, gpre, w1, gq, wq, gkv, wukt, taps[:, l], past_len, state_conv[l].astype(F32))
        ol = _decode_attention(page_table, ql, qp, ckv_s, kpe_s, cache_ckv, cache_kpe, l,
                               DECODE_PAGES, DECODE_SLOTS, DECODE_PER_STEP)
        xs = _out_sample(ol, gm_s, cv_s, xs, wuvt, wo, gpost)
        outs[3].append(ckv_s)
        outs[4].append(kpe_s)
        outs[5].append(conv_s)

    return (xp.reshape(nb, seq, d), xs, *[jnp.stack(o_) for o_ in outs])
```

```python
import functools

import jax
import jax.numpy as jnp
import numpy as np
from jax import lax
from jax.experimental import pallas as pl
from jax.experimental.pallas import tpu as pltpu

N_HEADS = 8
QK_NOPE = 64
QK_ROPE = 32
V_HEAD = 64
CONV_W = 3
ROPE_THETA = 10000.0
EPS = 1e-6
ATTN_SCALE = (QK_NOPE + QK_ROPE) ** -0.5

LANES = 128
HEAD_SLAB = LANES
ONES_ROWS = 16
LOG2_E = 1.4426950408889634
VMEM_LIMIT_BYTES = 56 * 1024 * 1024

F32 = jnp.float32
BF16 = jnp.bfloat16


def _silu(x):
    return x * (1.0 / (1.0 + jnp.exp(-x)))


def _rms(x, g):
    return x * lax.rsqrt(jnp.mean(x * x, axis=-1, keepdims=True) + EPS) * g


def _inv_freq():
    half = ROPE_THETA ** (-jnp.arange(0, QK_ROPE, 2, dtype=F32) / QK_ROPE)
    return jnp.concatenate([half, half])


def _rope_table_kernel(pos_ref, invf_ref, ck1_ref, ck2_ref, cq1t_ref, cq2t_ref):
    ang = invf_ref[...] * pos_ref[...]
    c, s = jnp.cos(ang), jnp.sin(ang)
    t = ang.shape[1]
    nope = jnp.ones((QK_NOPE, t), F32)
    zn = jnp.zeros((QK_NOPE, t), F32)
    zr = jnp.zeros((HEAD_SLAB - QK_NOPE - QK_ROPE, t), F32)

    def slab(first, mid):
        return jnp.concatenate([first, mid, zr], axis=0)

    ck1_ref[...] = slab(zn, c).T
    ck2_ref[...] = slab(zn, s).T
    cq1t_ref[...] = slab(nope * (ATTN_SCALE * LOG2_E), c * (ATTN_SCALE * LOG2_E))
    cq2t_ref[...] = slab(zn, s * (ATTN_SCALE * LOG2_E))


def _rope_tables(t):
    out = jax.ShapeDtypeStruct((t, LANES), F32)
    out_t = jax.ShapeDtypeStruct((LANES, t), F32)
    return pl.pallas_call(
        _rope_table_kernel,
        out_shape=(out, out, out_t, out_t),
        name="rope_tables",
    )(np.arange(t, dtype=np.float32).reshape(1, t), _inv_freq().reshape(QK_ROPE, 1))


def _token_major_rope_tables(pos_ref, invf_ref, nb):
    ang = pos_ref[...] * invf_ref[...]
    lane = lax.broadcasted_iota(jnp.int32, ang.shape, 1)
    is_rope = (lane >= QK_NOPE) & (lane < QK_NOPE + QK_ROPE)
    c = jnp.where(is_rope, jnp.cos(ang), 0.0)
    s = jnp.where(is_rope, jnp.sin(ang), 0.0)

    def rows(tab):
        return jnp.concatenate([jnp.broadcast_to(tab[j:j + 1], (nb, LANES)) for j in range(tab.shape[0])], axis=0)

    cq1 = jnp.where(lane < QK_NOPE, ATTN_SCALE, c * ATTN_SCALE)
    return rows(cq1), rows(s * ATTN_SCALE), rows(c), rows(s)


C_Q0, C_Q1 = 0, 384
C_KV0, C_KV1 = 384, 640
C_KP0, C_KP1 = 640, 768
C_GM0 = 768
GROUP = 512
D_IN_PACKED = C_GM0 + 5 * GROUP


def _rope_slab(blk, c1, c2):
    return blk * c1 + pltpu.roll(blk, LANES - QK_ROPE, 1) * c2


def _project_common(x, gpre_ref, w1_ref, gq_ref, gkv_ref, ck1, ck2):
    xn = _rms(x, gpre_ref[...]).astype(BF16)

    def proj(c0, c1):
        return jnp.dot(xn, w1_ref[:, c0:c1], preferred_element_type=F32)

    cqn = _rms(proj(C_Q0, C_Q1), gq_ref[...]).astype(BF16)
    ckv = _rms(proj(C_KV0, C_KV1), gkv_ref[...])
    kpe_blk = _rope_slab(proj(C_KP0, C_KP1), ck1, ck2)
    return proj, cqn, ckv, kpe_blk


def _conv_gate(proj, um1, um2, u, wconv_ref):
    y = wconv_ref[0] * um2 + wconv_ref[1] * um1 + wconv_ref[2] * u
    bg = proj(C_GM0 + GROUP, C_GM0 + 2 * GROUP)
    gc = proj(C_GM0 + 4 * GROUP, C_GM0 + 5 * GROUP)
    return bg * y * _silu(gc)


def _proj_prompt_kernel(x_ref, gpre_ref, w1_ref, gq_ref, wqt_ref, gkv_ref, wk_ref, wvt_ref, wconv_ref,
                        cq1t_ref, cq2t_ref, ck1_ref, ck2_ref,
                        qt_out, k_out, vt_out, ckv_out, kpet_out, gm_out, cv_out, conv_out,
                        carry_ref, *, tiles_per_seq):
    tm = x_ref.shape[0]
    step = pl.program_id(0)
    dn = (((1,), (1,)), ((), ()))

    @pl.when(step % tiles_per_seq == 0)
    def _():
        carry_ref[...] = jnp.zeros_like(carry_ref)

    xn = _rms(x_ref[...], gpre_ref[...]).astype(BF16)

    def proj(c0, c1):
        return jnp.dot(xn, w1_ref[:, c0:c1], preferred_element_type=F32)

    z_cq = proj(C_Q0, C_Q1)
    z_ckv = proj(C_KV0, C_KV1)
    cqn = _rms(z_cq, gq_ref[...]).astype(BF16)
    qt = lax.dot_general(wqt_ref[...], cqn, dn, preferred_element_type=F32)
    ckv = _rms(z_ckv, gkv_ref[...])
    ckv_out[...] = ckv
    ckvb = ckv.astype(BF16)
    z_kp = proj(C_KP0, C_KP1)
    knope = jnp.dot(ckvb, wk_ref[...], preferred_element_type=F32)
    qk = QK_NOPE + QK_ROPE
    nope_scale = cq1t_ref[0:QK_NOPE, :]
    cos_t, sin_t = cq1t_ref[QK_NOPE:qk, :], cq2t_ref[QK_NOPE:qk, :]
    for h in range(N_HEADS):
        nope, pe = qt[h * qk:h * qk + QK_NOPE, :], qt[h * qk + QK_NOPE:(h + 1) * qk, :]
        qt_out[h * HEAD_SLAB:h * HEAD_SLAB + QK_NOPE, :] = (nope * nope_scale).astype(BF16)
        qt_out[h * HEAD_SLAB + QK_NOPE:h * HEAD_SLAB + qk, :] = (pe * cos_t + _rot_rows(pe) * sin_t).astype(BF16)
        qt_out[h * HEAD_SLAB + qk:(h + 1) * HEAD_SLAB, :] = jnp.zeros((HEAD_SLAB - qk, tm), BF16)
    vt = lax.dot_general(wvt_ref[...], ckvb, dn, preferred_element_type=F32)
    z_gm = proj(C_GM0, C_GM0 + GROUP)
    kpe_blk = _rope_slab(z_kp, ck1_ref[...], ck2_ref[...])
    kpet_out[0] = kpe_blk.T[QK_NOPE:QK_NOPE + QK_ROPE, :]
    lane = lax.broadcasted_iota(jnp.int32, kpe_blk.shape, 1)
    for p in range(N_HEADS // 2):
        two = knope[:, p * LANES:(p + 1) * LANES]
        for h, nope_first in ((2 * p, two), (2 * p + 1, pltpu.roll(two, LANES - QK_NOPE, 1))):
            k_out[:, h * HEAD_SLAB:(h + 1) * HEAD_SLAB] = jnp.where(lane < QK_NOPE, nope_first, kpe_blk).astype(BF16)
    vt_out[...] = vt.astype(BF16)
    z_cg = proj(C_GM0 + 2 * GROUP, C_GM0 + 3 * GROUP)
    z_h = proj(C_GM0 + 3 * GROUP, C_GM0 + 4 * GROUP)
    gm_out[...] = _silu(z_gm).astype(BF16)
    z_bg = proj(C_GM0 + GROUP, C_GM0 + 2 * GROUP)
    z_gc = proj(C_GM0 + 4 * GROUP, C_GM0 + 5 * GROUP)

    u = z_cg * z_h
    prev = carry_ref[...]
    p1, p2 = prev[7:8, :], prev[6:7, :]
    row = lax.broadcasted_iota(jnp.int32, u.shape, 0)
    um1 = jnp.where(row == 0, p1, pltpu.roll(u, 1, 0))
    um2 = jnp.where(row == 0, p2, jnp.where(row == 1, p1, pltpu.roll(u, 2, 0)))
    y = wconv_ref[0] * um2 + wconv_ref[1] * um1 + wconv_ref[2] * u
    cv_out[...] = (z_bg * y * _silu(z_gc)).astype(BF16)
    carry_ref[...] = u[tm - 8:tm, :]
    conv_out[0] = u[tm - (CONV_W - 1):tm, :]


def _const_spec(shape):
    nd = len(shape)
    return pl.BlockSpec(shape, lambda *_: (0,) * nd, pipeline_mode=pl.Buffered(1))


def _proj_prompt(x2d, seq, gpre, w1, gq, wqt, gkv, wk, wvt, wconv, tabs, tm):
    t, d = x2d.shape
    nb = t // seq
    hs = N_HEADS * HEAD_SLAB
    hv = N_HEADS * V_HEAD
    ck1, ck2, cq1t, cq2t = tabs
    row = lambda w: pl.BlockSpec((tm, w), lambda i: (i, 0))
    col = lambda h: pl.BlockSpec((h, tm), lambda i: (0, i))
    tab = pl.BlockSpec((tm, LANES), lambda i: (i % (seq // tm), 0))
    tab_t = pl.BlockSpec((LANES, tm), lambda i: (0, i % (seq // tm)))
    out_shape = (
        jax.ShapeDtypeStruct((hs, t), BF16),
        jax.ShapeDtypeStruct((t, hs), BF16),
        jax.ShapeDtypeStruct((hv, t), BF16),
        jax.ShapeDtypeStruct((t, gkv.shape[1]), F32),
        jax.ShapeDtypeStruct((nb, QK_ROPE, seq), F32),
        jax.ShapeDtypeStruct((t, GROUP), BF16),
        jax.ShapeDtypeStruct((t, GROUP), BF16),
        jax.ShapeDtypeStruct((nb, CONV_W - 1, GROUP), F32),
    )
    return pl.pallas_call(
        functools.partial(_proj_prompt_kernel, tiles_per_seq=seq // tm),
        grid=(t // tm,),
        in_specs=[row(d), _const_spec(gpre.shape), _const_spec(w1.shape), _const_spec(gq.shape),
                  _const_spec(wqt.shape), _const_spec(gkv.shape), _const_spec(wk.shape), _const_spec(wvt.shape),
                  _const_spec(wconv.shape), tab_t, tab_t, tab, tab],
        out_specs=(col(hs), row(hs), col(hv), row(gkv.shape[1]),
                   pl.BlockSpec((1, QK_ROPE, tm), lambda i: (i // (seq // tm), 0, i % (seq // tm))),
                   row(GROUP), row(GROUP),
                   pl.BlockSpec((1, CONV_W - 1, GROUP), lambda i: (i // (seq // tm), 0, 0))),
        out_shape=out_shape,
        scratch_shapes=[pltpu.VMEM((8, GROUP), F32)],
        compiler_params=pltpu.CompilerParams(dimension_semantics=("arbitrary",),
                                             vmem_limit_bytes=VMEM_LIMIT_BYTES),
        name="proj_prompt",
    )(x2d, gpre, w1, gq, wqt, gkv, wk, wvt, wconv, cq1t, cq2t, ck1, ck2)


def _attn_kernel(qt_ref, k_ref, vt_ref, gm_ref, cv_ref, x_ref, wo_ref, gpost_ref, y_ref,
                 s_ref, m_ref, acc_ref, *, tq, tk):
    i = pl.program_id(1)
    hq = tq // 2
    tri = (lax.broadcasted_iota(jnp.int32, (hq, hq), 0) <= lax.broadcasted_iota(jnp.int32, (hq, hq), 1))

    def slab(h):
        return slice(h * HEAD_SLAB, (h + 1) * HEAD_SLAB)

    def values(h, toks):
        ones = jnp.ones((ONES_ROWS, toks.size), BF16)
        return jnp.concatenate([vt_ref[h * V_HEAD:(h + 1) * V_HEAD, toks], ones], 0)

    def scores(h, toks):
        return jnp.dot(k_ref[toks, slab(h)], qt_ref[slab(h), :], preferred_element_type=F32)

    def update(h, toks, st):
        m = m_ref[h]
        m_new = jnp.maximum(m, jnp.max(st, axis=0, keepdims=True))
        pt = jnp.exp2(st - m_new).astype(BF16)
        m_ref[h] = m_new
        acc_ref[h] = jnp.exp2(m - m_new) * acc_ref[h] + jnp.dot(values(h, toks), pt, preferred_element_type=F32)

    base = pl.multiple_of(i * tq, tq)
    ka, kb = pl.ds(base, hq), pl.ds(base + hq, hq)

    def diagonal_scores(h):
        return (jnp.dot(k_ref[ka, slab(h)], qt_ref[slab(h), :], preferred_element_type=F32),
                jnp.dot(k_ref[kb, slab(h)], qt_ref[slab(h), hq:], preferred_element_type=F32))

    def diagonal_init(h, a, b_):
        a_lo = jnp.where(tri, a[:, :hq], -jnp.inf)
        a_hi = a[:, hq:]
        b_ = jnp.where(tri, b_, -jnp.inf)
        m_lo = jnp.max(a_lo, axis=0, keepdims=True)
        m_hi = jnp.maximum(jnp.max(a_hi, axis=0, keepdims=True), jnp.max(b_, axis=0, keepdims=True))
        pa = jnp.concatenate([jnp.exp2(a_lo - m_lo), jnp.exp2(a_hi - m_hi)], axis=1).astype(BF16)
        pb = jnp.exp2(b_ - m_hi).astype(BF16)
        acc_a = jnp.dot(values(h, ka), pa, preferred_element_type=F32)
        acc_b = jnp.dot(values(h, kb), pb, preferred_element_type=F32)
        m_ref[h] = jnp.concatenate([m_lo, m_hi], axis=1)
        acc_ref[h] = jnp.concatenate([acc_a[:, :hq], acc_a[:, hq:] + acc_b], axis=1)

    n_full = i * (tq // tk)

    def tile_tokens(j):
        return pl.ds(pl.multiple_of(j * tk, tk), tk)

    cur = diagonal_scores(0)
    for h in range(N_HEADS):
        if h + 1 < N_HEADS:
            ahead = diagonal_scores(h + 1)
        else:
            s_ref[0] = scores(0, tile_tokens(0))
        diagonal_init(h, *cur)
        cur = ahead

    def full_tile(j, carry):
        toks = tile_tokens(j)
        nxt = tile_tokens(jnp.minimum(j + 1, n_full - 1))
        for h in range(N_HEADS):
            if h + 1 < N_HEADS:
                s_ref[h + 1] = scores(h + 1, toks)
                update(h, toks, s_ref[h])
            else:
                cur = s_ref[h]
                s_ref[0] = scores(0, nxt)
                update(h, toks, cur)
        return carry

    lax.fori_loop(0, n_full, full_tile, 0)
    outs = [acc_ref[h, :V_HEAD] * (1.0 / acc_ref[h, V_HEAD:V_HEAD + 1]) for h in range(N_HEADS)]
    pairs = [jnp.concatenate(outs[2 * p:2 * p + 2], axis=0).T for p in range(N_HEADS // 2)]

    half = gm_ref.shape[1]
    mla = (jnp.concatenate(pairs, axis=1) * gm_ref[...].astype(F32)).astype(BF16)
    y = jnp.dot(mla, wo_ref[0:half, :], preferred_element_type=F32)
    y = y + jnp.dot(cv_ref[...], wo_ref[half:, :], preferred_element_type=F32)
    y_ref[...] = x_ref[...] + _rms(y, gpost_ref[...])


def _attention_out(qt, k, vt, gm, cv, x2d, wo, gpost, nb, seq, tq, tk):
    hs, t = qt.shape
    hv = vt.shape[0]
    d = x2d.shape[1]
    assert N_HEADS % 2 == 0 and tq % tk == 0
    row = lambda w: pl.BlockSpec((tq, w), lambda b, i: (b * (seq // tq) + i, 0))
    return pl.pallas_call(
        functools.partial(_attn_kernel, tq=tq, tk=tk),
        grid=(nb, seq // tq),
        in_specs=[pl.BlockSpec((hs, tq), lambda b, i: (0, b * (seq // tq) + i)),
                  pl.BlockSpec((seq, hs), lambda b, i: (b, 0)),
                  pl.BlockSpec((hv, seq), lambda b, i: (0, b)),
                  row(gm.shape[1]), row(cv.shape[1]), row(d), _const_spec(wo.shape), _const_spec(gpost.shape)],
        out_specs=row(d),
        out_shape=jax.ShapeDtypeStruct((t, d), F32),
        scratch_shapes=[pltpu.VMEM((N_HEADS, tk, tq), F32), pltpu.VMEM((N_HEADS, 1, tq), F32),
                        pltpu.VMEM((N_HEADS, V_HEAD + ONES_ROWS, tq), F32)],
        compiler_params=pltpu.CompilerParams(dimension_semantics=("arbitrary", "arbitrary"),
                                             vmem_limit_bytes=VMEM_LIMIT_BYTES),
        name="prompt_attn",
    )(qt, k, vt, gm, cv, x2d, wo, gpost)


def _proj_sample_kernel(x_ref, gpre_ref, w1_ref, gq_ref, wq_ref, gkv_ref, wukt_ref, wconv_ref,
                        pos_ref, invf_ref, st_ref,
                        ql_out, qp_out, ckv_out, kpe_out, gm_out, cv_out, conv_out):
    nb, dec_seq, _ = x_ref.shape

    def tok(a, j):
        return a[j * nb:(j + 1) * nb]

    x = jnp.concatenate([x_ref[:, j, :] for j in range(dec_seq)], axis=0)
    cq1, cq2, ck1, ck2 = _token_major_rope_tables(pos_ref, invf_ref, nb)
    proj, cqn, ckv, kpe_blk = _project_common(x, gpre_ref, w1_ref, gq_ref, gkv_ref, ck1, ck2)
    qraw = jnp.dot(cqn, wq_ref[...], preferred_element_type=F32)
    for h in range(N_HEADS):
        qh = _rope_slab(qraw[:, h * HEAD_SLAB:(h + 1) * HEAD_SLAB], cq1, cq2)
        qlh = jnp.dot(qh.astype(BF16), wukt_ref[h], preferred_element_type=F32)
        qph = pltpu.roll(qh, LANES - QK_NOPE, 1)[:, 0:QK_ROPE]
        for j in range(dec_seq):
            ql_out[:, j * N_HEADS + h, :] = tok(qlh, j)
            qp_out[:, j * N_HEADS + h, :] = tok(qph, j)
    kpe = kpe_blk[:, QK_NOPE:QK_NOPE + QK_ROPE]
    for j in range(dec_seq):
        ckv_out[:, j, :] = tok(ckv, j)
        kpe_out[:, j, :] = tok(kpe, j)
    gm_out[...] = _silu(proj(C_GM0, C_GM0 + GROUP))

    u = proj(C_GM0 + 2 * GROUP, C_GM0 + 3 * GROUP) * proj(C_GM0 + 3 * GROUP, C_GM0 + 4 * GROUP)
    s0, s1 = st_ref[:, 0, :], st_ref[:, 1, :]
    um1 = jnp.concatenate([s1, u[:(dec_seq - 1) * nb]], axis=0)
    um2 = jnp.concatenate([s0, s1, u[:(dec_seq - 2) * nb]], axis=0)
    cv_out[...] = _conv_gate(proj, um1, um2, u, wconv_ref)
    for i in range(CONV_W - 1):
        conv_out[:, i, :] = tok(u, dec_seq - (CONV_W - 1) + i)


def _proj_sample(x, gpre, w1, gq, wq, gkv, wukt, wconv, first_pos, state):
    nb, dec_seq, d = x.shape
    assert dec_seq >= CONV_W - 1 and state.shape == (nb, CONV_W - 1, GROUP)
    pos = (first_pos + np.arange(dec_seq, dtype=np.float32)).reshape(dec_seq, 1)
    invf = jnp.pad(_inv_freq(), (QK_NOPE, LANES - QK_NOPE - QK_ROPE)).reshape(1, LANES)
    t = nb * dec_seq
    c = gkv.shape[1]
    out_shape = (
        jax.ShapeDtypeStruct((nb, dec_seq * N_HEADS, c), F32),
        jax.ShapeDtypeStruct((nb, dec_seq * N_HEADS, QK_ROPE), F32),
        jax.ShapeDtypeStruct((nb, dec_seq, c), F32),
        jax.ShapeDtypeStruct((nb, dec_seq, QK_ROPE), F32),
        jax.ShapeDtypeStruct((t, GROUP), F32),
        jax.ShapeDtypeStruct((t, GROUP), F32),
        jax.ShapeDtypeStruct((nb, CONV_W - 1, GROUP), F32),
    )
    return pl.pallas_call(
        _proj_sample_kernel,
        out_shape=out_shape,
        compiler_params=pltpu.CompilerParams(vmem_limit_bytes=VMEM_LIMIT_BYTES),
        name="proj_sample",
    )(x, gpre, w1, gq, wq, gkv, wukt, wconv, pos, invf, state)


def _decode_attn_kernel(pt_ref, ql_ref, qp_ref, cnew_ref, knew_ref, y_hbm, x_hbm,
                        gm_ref, cv_ref, xs_ref, wuvt_ref, wo_ref, gpost_ref, ys_ref,
                        ybuf, xbuf, sem, m_ref, l_ref, acc_ref, o_ref, *, pages, slots, layer, dec_seq):
    g = pl.program_id(0)
    n_g = pl.num_programs(0)
    per_step = ql_ref.shape[0]
    n_chunks = pt_ref.shape[1] // pages
    total = per_step * n_chunks
    page = ybuf.shape[1] // pages
    rows = ql_ref.shape[1]
    dn = (((1,), (1,)), ((), ()))
    ahead = slots - 1

    def chunk_copies(bb, ch, slot, page_ids):
        cps = []
        for i in range(pages):
            pid = page_ids(bb, ch * pages + i)
            tok = pl.ds(i * page, page)
            cps.append(pltpu.make_async_copy(y_hbm.at[layer, pid], ybuf.at[slot, tok, :], sem.at[0, slot]))
            cps.append(pltpu.make_async_copy(x_hbm.at[layer, pid], xbuf.at[slot, :, tok], sem.at[1, slot]))
        return cps

    def start(bb, ch, slot):
        for cp in chunk_copies(bb, ch, slot, lambda r, j: pt_ref[r, j]):
            cp.start()

    def wait(slot):
        for cp in chunk_copies(0, 0, slot, lambda r, j: 0):
            cp.wait()

    def start_nth(step, k, slot):
        start(step * per_step + k // n_chunks, k % n_chunks, slot)

    @pl.when(g == 0)
    def _():
        for k in range(ahead):
            start_nth(0, k, k % slots)

    tok_of_row = lax.broadcasted_iota(jnp.int32, (rows, 1), 0) // N_HEADS
    for e in range(per_step):
        qlf, qpf = ql_ref[e], qp_ref[e]
        s_new = []
        for j in range(dec_seq):
            sj = (jnp.sum(qlf * cnew_ref[e, j:j + 1, :], axis=-1, keepdims=True)
                  + jnp.sum(qpf * knew_ref[e, j:j + 1, :], axis=-1, keepdims=True))
            s_new.append(jnp.where(tok_of_row >= j, sj, -jnp.inf))
        m_n = functools.reduce(jnp.maximum, s_new)
        p_n = [jnp.exp(sj - m_n) for sj in s_new]
        m_ref[e] = m_n
        l_ref[e] = functools.reduce(lambda a, v: a + v, p_n)
        acc_ref[e, 0] = functools.reduce(lambda a, v: a + v,
                                         [p_n[j] * cnew_ref[e, j:j + 1, :] for j in range(dec_seq)])
        acc_ref[e, 1] = jnp.zeros(acc_ref.shape[2:], F32)

    half = pages * page // 2
    toks = [slice(h * half, (h + 1) * half) for h in range(2)]

    def scores(e, slot):
        ql, qp = ql_ref[e].astype(BF16), qp_ref[e].astype(BF16)
        yb = [ybuf[slot, t, :].astype(BF16) for t in toks]
        xb = [xbuf[slot, :, t].astype(BF16) for t in toks]
        s = jnp.concatenate(
            [lax.dot_general(ql, yb[h], dn, preferred_element_type=F32)
             + jnp.dot(qp, xb[h], preferred_element_type=F32) for h in range(2)], axis=1)
        return e, s, yb

    def softmax(e, s, yb):
        m = m_ref[e]
        m_new = jnp.maximum(m, jnp.max(s, axis=-1, keepdims=True))
        alpha = jnp.exp(m - m_new)
        p = jnp.exp(s - m_new)
        l_ref[e] = alpha * l_ref[e] + jnp.sum(p, axis=-1, keepdims=True)
        m_ref[e] = m_new
        return e, alpha, p.astype(BF16), yb

    def values(e, alpha, p, yb):
        for h in range(2):
            acc_ref[e, h] = alpha * acc_ref[e, h] + jnp.dot(p[:, toks[h]], yb[h], preferred_element_type=F32)

    scored = None
    weighted = None
    for k in range(total):
        nxt = k + ahead
        if nxt < total:
            start_nth(g, nxt, nxt % slots)
        else:
            @pl.when(g + 1 < n_g)
            def _():
                start_nth(g + 1, nxt - total, nxt % slots)

        wait(k % slots)
        fresh = scores(k // n_chunks, k % slots)
        if weighted is not None:
            values(*weighted)
        if scored is not None:
            weighted = softmax(*scored)
        scored = fresh
    if weighted is not None:
        values(*weighted)
    values(*softmax(*scored))
    for e in range(per_step):
        o_ref[g * per_step + e] = (acc_ref[e, 0] + acc_ref[e, 1]) * (1.0 / l_ref[e])

    @pl.when(g == n_g - 1)
    def _():
        _out_sample_kernel(o_ref, gm_ref, cv_ref, xs_ref, wuvt_ref, wo_ref, gpost_ref, ys_ref)


def _decode_attention(page_table, ql, qp, cnew, knew, cache_ckv, cache_kpe, gm, cv, xs, wuvt, wo, gpost,
                      layer, pages, slots, per_step):
    nb, rows, c = ql.shape
    dec_seq = cnew.shape[1]
    page, r = cache_kpe.shape[2:]
    n_chunks = page_table.shape[1] // pages
    assert n_chunks * pages == page_table.shape[1] and nb % per_step == 0
    assert (per_step * n_chunks) % slots == 0 and slots - 1 <= per_step * n_chunks
    kpe_t = jnp.swapaxes(cache_kpe, 2, 3)
    per_b = lambda a: pl.BlockSpec((per_step,) + a.shape[1:], lambda g, pt: (g, 0, 0))
    whole = [gm, cv, xs, wuvt, wo, gpost]
    grid_spec = pltpu.PrefetchScalarGridSpec(
        num_scalar_prefetch=1,
        grid=(nb // per_step,),
        in_specs=[per_b(ql), per_b(qp), per_b(cnew), per_b(knew),
                  pl.BlockSpec(memory_space=pl.ANY), pl.BlockSpec(memory_space=pl.ANY)]
                 + [_const_spec(a.shape) for a in whole],
        out_specs=pl.BlockSpec(xs.shape, lambda g, pt: (0, 0, 0)),
        scratch_shapes=[pltpu.VMEM((slots, pages * page, c), F32), pltpu.VMEM((slots, r, pages * page), F32),
                        pltpu.SemaphoreType.DMA((2, slots)),
                        pltpu.VMEM((per_step, rows, 1), F32), pltpu.VMEM((per_step, rows, 1), F32),
                        pltpu.VMEM((per_step, 2, rows, c), F32),
                        pltpu.VMEM((nb, rows, c), F32)],
    )
    return pl.pallas_call(
        functools.partial(_decode_attn_kernel, pages=pages, slots=slots, layer=layer, dec_seq=dec_seq),
        grid_spec=grid_spec,
        out_shape=jax.ShapeDtypeStruct(xs.shape, F32),
        compiler_params=pltpu.CompilerParams(dimension_semantics=("arbitrary",),
                                             vmem_limit_bytes=VMEM_LIMIT_BYTES),
        name="decode_attn",
    )(page_table, ql, qp, cnew, knew, cache_ckv, kpe_t, *whole)


def _out_sample_kernel(ol_ref, gm_ref, cv_ref, x_ref, wuvt_ref, wo_ref, gpost_ref, y_ref):
    nb, dec_seq, _ = x_ref.shape
    half = gm_ref.shape[1]
    dn = (((1,), (1,)), ((), ()))

    def token_major(ref, index_of_token):
        return jnp.concatenate([ref[:, index_of_token(j), :] for j in range(dec_seq)], axis=0)

    o = jnp.concatenate(
        [lax.dot_general(token_major(ol_ref, lambda j: j * N_HEADS + h).astype(BF16), wuvt_ref[h], dn,
                         preferred_element_type=F32) for h in range(N_HEADS)], axis=1)
    mla = (o * gm_ref[...]).astype(BF16)
    y = jnp.dot(mla, wo_ref[0:half, :], preferred_element_type=F32)
    y = y + jnp.dot(cv_ref[...].astype(BF16), wo_ref[half:, :], preferred_element_type=F32)
    y = token_major(x_ref, lambda j: j) + _rms(y, gpost_ref[...])
    for j in range(dec_seq):
        y_ref[:, j, :] = y[j * nb:(j + 1) * nb]


def _rot_rows(w):
    r = w.shape[0] // 2
    return jnp.concatenate([-w[r:], w[:r]], axis=0)


def _pack_kernel(wint_ref, wuq_ref, wukv_ref, wout_ref,
                 w1_ref, wq_ref, wqt_ref, wk_ref, wvt_ref, wukt_ref, wuvt_ref, wo_ref):
    d = wint_ref.shape[1]
    blk = LANES

    def put_w1(col0, rows):
        w1_ref[:, col0:col0 + blk] = rows.T.astype(BF16)

    for r0 in range(0, C_KV1, blk):
        put_w1(r0, wint_ref[r0:r0 + blk, :])
    kpe = wint_ref[C_KV1:C_KV1 + QK_ROPE, :]
    put_w1(C_KP0, jnp.concatenate([jnp.zeros((QK_NOPE, d), F32), kpe, _rot_rows(kpe)], axis=0))
    src0 = C_KV1 + QK_ROPE
    for j in range(5 * GROUP // blk):
        put_w1(C_GM0 + j * blk, wint_ref[src0 + j * blk:src0 + (j + 1) * blk, :])

    uqt = wuq_ref[...].T
    ukvt = wukv_ref[...].T
    kv_lora = ukvt.shape[1]
    qk = QK_NOPE + QK_ROPE
    wqt_ref[...] = uqt.astype(BF16)
    for h in range(N_HEADS):
        pe = uqt[h * qk + QK_NOPE:(h + 1) * qk]
        q_slab = jnp.concatenate([uqt[h * qk:h * qk + QK_NOPE], pe, _rot_rows(pe)], axis=0)
        wq_ref[:, h * HEAD_SLAB:(h + 1) * HEAD_SLAB] = q_slab.T.astype(BF16)
        ukt = ukvt[h * (QK_NOPE + V_HEAD):h * (QK_NOPE + V_HEAD) + QK_NOPE]
        uvt = ukvt[h * (QK_NOPE + V_HEAD) + QK_NOPE:(h + 1) * (QK_NOPE + V_HEAD)]
        k_slab = jnp.concatenate([ukt, jnp.zeros((HEAD_SLAB - QK_NOPE, kv_lora), F32)], axis=0)
        wukt_ref[h] = k_slab.astype(BF16)
        wvt_ref[h * V_HEAD:(h + 1) * V_HEAD, :] = uvt.astype(BF16)
        wuvt_ref[h] = uvt.astype(BF16)
    for p in range(N_HEADS // 2):
        two = ukvt[2 * p * (QK_NOPE + V_HEAD):2 * (p + 1) * (QK_NOPE + V_HEAD)]
        pair = jnp.concatenate([two[0:QK_NOPE], two[QK_NOPE + V_HEAD:2 * QK_NOPE + V_HEAD]], axis=0)
        wk_ref[:, p * LANES:(p + 1) * LANES] = pair.T.astype(BF16)
    wo_ref[...] = wout_ref[...].astype(BF16)


def _pack_weights(w_in, w_uq, w_ukv, w_out):
    d = w_in.shape[0]
    q_lora, kv_lora = w_uq.shape[0], w_ukv.shape[0]
    assert w_in.shape[1] == C_KV1 + QK_ROPE + 5 * GROUP
    sds = lambda *s: jax.ShapeDtypeStruct(s, BF16)
    return pl.pallas_call(
        _pack_kernel,
        out_shape=(sds(d, D_IN_PACKED), sds(q_lora, N_HEADS * HEAD_SLAB), sds(w_uq.shape[1], q_lora),
                   sds(kv_lora, N_HEADS * QK_NOPE), sds(N_HEADS * V_HEAD, kv_lora),
                   sds(N_HEADS, HEAD_SLAB, kv_lora), sds(N_HEADS, V_HEAD, kv_lora), sds(*w_out.shape)),
        compiler_params=pltpu.CompilerParams(vmem_limit_bytes=VMEM_LIMIT_BYTES),
        name="pack_weights",
    )(w_in.T, w_uq, w_ukv, w_out)


PROMPT_TILE = 512
ATTN_TQ = 512
ATTN_TK = 512
DECODE_PAGES = 32
DECODE_SLOTS = 4
DECODE_PER_STEP = 1


def kernel(x_prompt, x_sample, cache_ckv, cache_kpe, state_conv, page_table, g_pre, w_in, g_qnorm, w_uq,
           g_kvnorm, w_ukv, w_conv, w_out, g_post):
    depth = w_in.shape[0]
    nb, seq, d = x_prompt.shape
    db, dec_seq, _ = x_sample.shape
    past_len = page_table.shape[1] * cache_ckv.shape[2]
    c = cache_ckv.shape[3]

    tabs_p = _rope_tables(seq)
    taps = jnp.swapaxes(w_conv, 0, 1)[:, :, None, :]

    xp = x_prompt.reshape(nb * seq, d)
    xs = x_sample
    outs = [[] for _ in range(6)]
    for l in range(depth):
        w1, wq, wqt, wk, wvt, wukt, wuvt, wo = _pack_weights(w_in[l], w_uq[l], w_ukv[l], w_out[l])
        gpre, gq, gkv, gpost = g_pre[l][None], g_qnorm[l][None], g_kvnorm[l][None], g_post[l][None]

        qt, k, vt, ckv_p, kpe_p, gm, cv, conv_p = _proj_prompt(
            xp, seq, gpre, w1, gq, wqt, gkv, wk, wvt, taps[:, l], tabs_p, PROMPT_TILE)
        xp = _attention_out(qt, k, vt, gm, cv, xp, wo, gpost, nb, seq, ATTN_TQ, ATTN_TK)
        outs[0].append(ckv_p.reshape(nb, seq, c))
        outs[1].append(jnp.swapaxes(kpe_p, 1, 2))
        outs[2].append(conv_p)

        ql, qp, ckv_s, kpe_s, gm_s, cv_s, conv_s = _proj_sample(
            xs, gpre, w1, gq, wq, gkv, wukt, taps[:, l], past_len, state_conv[l].astype(F32))
        xs = _decode_attention(page_table, ql, qp, ckv_s, kpe_s, cache_ckv, cache_kpe,
                               gm_s, cv_s, xs, wuvt, wo, gpost, l, DECODE_PAGES, DECODE_SLOTS, DECODE_PER_STEP)
        outs[3].append(ckv_s)
        outs[4].append(kpe_s)
        outs[5].append(conv_s)

    return (xp.reshape(nb, seq, d), xs, *[jnp.stack(o_) for o_ in outs])
```

```python
import functools

import jax
import jax.numpy as jnp
import numpy as np
from jax import lax
from jax.experimental import pallas as pl
from jax.experimental.pallas import tpu as pltpu

N_HEADS = 8
QK_NOPE = 64
QK_ROPE = 32
V_HEAD = 64
CONV_W = 3
ROPE_THETA = 10000.0
EPS = 1e-6
ATTN_SCALE = (QK_NOPE + QK_ROPE) ** -0.5

LANES = 128
HEAD_SLAB = LANES
ONES_ROWS = 16
LOG2_E = 1.4426950408889634
VMEM_LIMIT_BYTES = 56 * 1024 * 1024

F32 = jnp.float32
BF16 = jnp.bfloat16


def _silu(x):
    return x * (1.0 / (1.0 + jnp.exp(-x)))


def _rms(x, g):
    return x * lax.rsqrt(jnp.mean(x * x, axis=-1, keepdims=True) + EPS) * g


def _inv_freq():
    half = ROPE_THETA ** (-jnp.arange(0, QK_ROPE, 2, dtype=F32) / QK_ROPE)
    return jnp.concatenate([half, half])


def _rope_table_kernel(pos_ref, invf_ref, ck1_ref, ck2_ref, cq1t_ref, cq2t_ref):
    ang = invf_ref[...] * pos_ref[...]
    c, s = jnp.cos(ang), jnp.sin(ang)
    t = ang.shape[1]
    nope = jnp.ones((QK_NOPE, t), F32)
    zn = jnp.zeros((QK_NOPE, t), F32)
    zr = jnp.zeros((HEAD_SLAB - QK_NOPE - QK_ROPE, t), F32)

    def slab(first, mid):
        return jnp.concatenate([first, mid, zr], axis=0)

    ck1_ref[...] = slab(zn, c).T
    ck2_ref[...] = slab(zn, s).T
    cq1t_ref[...] = slab(nope * (ATTN_SCALE * LOG2_E), c * (ATTN_SCALE * LOG2_E))
    cq2t_ref[...] = slab(zn, s * (ATTN_SCALE * LOG2_E))


def _rope_tables(t):
    out = jax.ShapeDtypeStruct((t, LANES), F32)
    out_t = jax.ShapeDtypeStruct((LANES, t), F32)
    return pl.pallas_call(
        _rope_table_kernel,
        out_shape=(out, out, out_t, out_t),
        name="rope_tables",
    )(np.arange(t, dtype=np.float32).reshape(1, t), _inv_freq().reshape(QK_ROPE, 1))


def _token_major_rope_tables(pos_ref, invf_ref, nb):
    ang = pos_ref[...] * invf_ref[...]
    lane = lax.broadcasted_iota(jnp.int32, ang.shape, 1)
    is_rope = (lane >= QK_NOPE) & (lane < QK_NOPE + QK_ROPE)
    c = jnp.where(is_rope, jnp.cos(ang), 0.0)
    s = jnp.where(is_rope, jnp.sin(ang), 0.0)

    def rows(tab):
        return jnp.concatenate([jnp.broadcast_to(tab[j:j + 1], (nb, LANES)) for j in range(tab.shape[0])], axis=0)

    cq1 = jnp.where(lane < QK_NOPE, ATTN_SCALE, c * ATTN_SCALE)
    return rows(cq1), rows(s * ATTN_SCALE), rows(c), rows(s)


C_Q0, C_Q1 = 0, 384
C_KV0, C_KV1 = 384, 640
C_KP0, C_KP1 = 640, 768
C_GM0 = 768
GROUP = 512
D_IN_PACKED = C_GM0 + 5 * GROUP


def _rope_slab(blk, c1, c2):
    return blk * c1 + pltpu.roll(blk, LANES - QK_ROPE, 1) * c2


def _project_common(x, gpre_ref, w1_ref, gq_ref, gkv_ref, ck1, ck2):
    xn = _rms(x, gpre_ref[...]).astype(BF16)

    def proj(c0, c1):
        return jnp.dot(xn, w1_ref[:, c0:c1], preferred_element_type=F32)

    cqn = _rms(proj(C_Q0, C_Q1), gq_ref[...]).astype(BF16)
    ckv = _rms(proj(C_KV0, C_KV1), gkv_ref[...])
    kpe_blk = _rope_slab(proj(C_KP0, C_KP1), ck1, ck2)
    return proj, cqn, ckv, kpe_blk


def _conv_gate(proj, um1, um2, u, wconv_ref):
    y = wconv_ref[0] * um2 + wconv_ref[1] * um1 + wconv_ref[2] * u
    bg = proj(C_GM0 + GROUP, C_GM0 + 2 * GROUP)
    gc = proj(C_GM0 + 4 * GROUP, C_GM0 + 5 * GROUP)
    return bg * y * _silu(gc)


def _proj_prompt_kernel(x_ref, gpre_ref, w1_ref, gq_ref, wqt_ref, gkv_ref, wk_ref, wvt_ref, wconv_ref,
                        cq1t_ref, cq2t_ref, ck1_ref, ck2_ref,
                        qt_out, k_out, vt_out, ckv_out, kpet_out, gm_out, cv_out, conv_out,
                        carry_ref, *, tiles_per_seq):
    tm = x_ref.shape[0]
    step = pl.program_id(0)
    dn = (((1,), (1,)), ((), ()))

    @pl.when(step % tiles_per_seq == 0)
    def _():
        carry_ref[...] = jnp.zeros_like(carry_ref)

    xn = _rms(x_ref[...], gpre_ref[...]).astype(BF16)

    def proj(c0, c1):
        return jnp.dot(xn, w1_ref[:, c0:c1], preferred_element_type=F32)

    z_cq = proj(C_Q0, C_Q1)
    z_ckv = proj(C_KV0, C_KV1)
    cqn = _rms(z_cq, gq_ref[...]).astype(BF16)
    qt = lax.dot_general(wqt_ref[...], cqn, dn, preferred_element_type=F32)
    ckv = _rms(z_ckv, gkv_ref[...])
    ckv_out[...] = ckv
    ckvb = ckv.astype(BF16)
    z_kp = proj(C_KP0, C_KP1)
    knope = jnp.dot(ckvb, wk_ref[...], preferred_element_type=F32)
    qk = QK_NOPE + QK_ROPE
    nope_scale = cq1t_ref[0:QK_NOPE, :]
    cos_t, sin_t = cq1t_ref[QK_NOPE:qk, :], cq2t_ref[QK_NOPE:qk, :]
    for h in range(N_HEADS):
        nope, pe = qt[h * qk:h * qk + QK_NOPE, :], qt[h * qk + QK_NOPE:(h + 1) * qk, :]
        qt_out[h * HEAD_SLAB:h * HEAD_SLAB + QK_NOPE, :] = (nope * nope_scale).astype(BF16)
        qt_out[h * HEAD_SLAB + QK_NOPE:h * HEAD_SLAB + qk, :] = (pe * cos_t + _rot_rows(pe) * sin_t).astype(BF16)
        qt_out[h * HEAD_SLAB + qk:(h + 1) * HEAD_SLAB, :] = jnp.zeros((HEAD_SLAB - qk, tm), BF16)
    vt = lax.dot_general(wvt_ref[...], ckvb, dn, preferred_element_type=F32)
    z_gm = proj(C_GM0, C_GM0 + GROUP)
    kpe_blk = _rope_slab(z_kp, ck1_ref[...], ck2_ref[...])
    kpet_out[0] = kpe_blk.T[QK_NOPE:QK_NOPE + QK_ROPE, :]
    lane = lax.broadcasted_iota(jnp.int32, kpe_blk.shape, 1)
    for p in range(N_HEADS // 2):
        two = knope[:, p * LANES:(p + 1) * LANES]
        for h, nope_first in ((2 * p, two), (2 * p + 1, pltpu.roll(two, LANES - QK_NOPE, 1))):
            k_out[:, h * HEAD_SLAB:(h + 1) * HEAD_SLAB] = jnp.where(lane < QK_NOPE, nope_first, kpe_blk).astype(BF16)
    vt_out[...] = vt.astype(BF16)
    z_cg = proj(C_GM0 + 2 * GROUP, C_GM0 + 3 * GROUP)
    z_h = proj(C_GM0 + 3 * GROUP, C_GM0 + 4 * GROUP)
    gm_out[...] = _silu(z_gm).astype(BF16)
    z_bg = proj(C_GM0 + GROUP, C_GM0 + 2 * GROUP)
    z_gc = proj(C_GM0 + 4 * GROUP, C_GM0 + 5 * GROUP)

    u = z_cg * z_h
    prev = carry_ref[...]
    p1, p2 = prev[7:8, :], prev[6:7, :]
    row = lax.broadcasted_iota(jnp.int32, u.shape, 0)
    um1 = jnp.where(row == 0, p1, pltpu.roll(u, 1, 0))
    um2 = jnp.where(row == 0, p2, jnp.where(row == 1, p1, pltpu.roll(u, 2, 0)))
    y = wconv_ref[0] * um2 + wconv_ref[1] * um1 + wconv_ref[2] * u
    cv_out[...] = (z_bg * y * _silu(z_gc)).astype(BF16)
    carry_ref[...] = u[tm - 8:tm, :]
    conv_out[0] = u[tm - (CONV_W - 1):tm, :]


def _const_spec(shape):
    nd = len(shape)
    return pl.BlockSpec(shape, lambda *_: (0,) * nd, pipeline_mode=pl.Buffered(1))


def _proj_prompt(x2d, seq, gpre, w1, gq, wqt, gkv, wk, wvt, wconv, tabs, tm):
    t, d = x2d.shape
    nb = t // seq
    hs = N_HEADS * HEAD_SLAB
    hv = N_HEADS * V_HEAD
    ck1, ck2, cq1t, cq2t = tabs
    row = lambda w: pl.BlockSpec((tm, w), lambda i: (i, 0))
    col = lambda h: pl.BlockSpec((h, tm), lambda i: (0, i))
    tab = pl.BlockSpec((tm, LANES), lambda i: (i % (seq // tm), 0))
    tab_t = pl.BlockSpec((LANES, tm), lambda i: (0, i % (seq // tm)))
    out_shape = (
        jax.ShapeDtypeStruct((hs, t), BF16),
        jax.ShapeDtypeStruct((t, hs), BF16),
        jax.ShapeDtypeStruct((hv, t), BF16),
        jax.ShapeDtypeStruct((t, gkv.shape[1]), F32),
        jax.ShapeDtypeStruct((nb, QK_ROPE, seq), F32),
        jax.ShapeDtypeStruct((t, GROUP), BF16),
        jax.ShapeDtypeStruct((t, GROUP), BF16),
        jax.ShapeDtypeStruct((nb, CONV_W - 1, GROUP), F32),
    )
    return pl.pallas_call(
        functools.partial(_proj_prompt_kernel, tiles_per_seq=seq // tm),
        grid=(t // tm,),
        in_specs=[row(d), _const_spec(gpre.shape), _const_spec(w1.shape), _const_spec(gq.shape),
                  _const_spec(wqt.shape), _const_spec(gkv.shape), _const_spec(wk.shape), _const_spec(wvt.shape),
                  _const_spec(wconv.shape), tab_t, tab_t, tab, tab],
        out_specs=(col(hs), row(hs), col(hv), row(gkv.shape[1]),
                   pl.BlockSpec((1, QK_ROPE, tm), lambda i: (i // (seq // tm), 0, i % (seq // tm))),
                   row(GROUP), row(GROUP),
                   pl.BlockSpec((1, CONV_W - 1, GROUP), lambda i: (i // (seq // tm), 0, 0))),
        out_shape=out_shape,
        scratch_shapes=[pltpu.VMEM((8, GROUP), F32)],
        compiler_params=pltpu.CompilerParams(dimension_semantics=("arbitrary",),
                                             vmem_limit_bytes=VMEM_LIMIT_BYTES),
        name="proj_prompt",
    )(x2d, gpre, w1, gq, wqt, gkv, wk, wvt, wconv, cq1t, cq2t, ck1, ck2)


def _attn_kernel(qt_ref, k_ref, vt_ref, gm_ref, cv_ref, x_ref, wo_ref, gpost_ref, y_ref,
                 s_ref, m_ref, acc_ref, *, tq, tk):
    i = pl.program_id(1)
    hq = tq // 2
    tri = (lax.broadcasted_iota(jnp.int32, (hq, hq), 0) <= lax.broadcasted_iota(jnp.int32, (hq, hq), 1))

    def slab(h):
        return slice(h * HEAD_SLAB, (h + 1) * HEAD_SLAB)

    def values(h, toks):
        ones = jnp.ones((ONES_ROWS, toks.size), BF16)
        return jnp.concatenate([vt_ref[h * V_HEAD:(h + 1) * V_HEAD, toks], ones], 0)

    def scores(h, toks):
        return jnp.dot(k_ref[toks, slab(h)], qt_ref[slab(h), :], preferred_element_type=F32)

    def update(h, toks, st):
        m = m_ref[h]
        m_new = jnp.maximum(m, jnp.max(st, axis=0, keepdims=True))
        pt = jnp.exp2(st - m_new).astype(BF16)
        m_ref[h] = m_new
        acc_ref[h] = jnp.exp2(m - m_new) * acc_ref[h] + jnp.dot(values(h, toks), pt, preferred_element_type=F32)

    base = pl.multiple_of(i * tq, tq)
    ka, kb = pl.ds(base, hq), pl.ds(base + hq, hq)

    def diagonal_scores(h):
        return (jnp.dot(k_ref[ka, slab(h)], qt_ref[slab(h), :], preferred_element_type=F32),
                jnp.dot(k_ref[kb, slab(h)], qt_ref[slab(h), hq:], preferred_element_type=F32))

    def diagonal_init(h, a, b_):
        a_lo = jnp.where(tri, a[:, :hq], -jnp.inf)
        a_hi = a[:, hq:]
        b_ = jnp.where(tri, b_, -jnp.inf)
        m_lo = jnp.max(a_lo, axis=0, keepdims=True)
        m_hi = jnp.maximum(jnp.max(a_hi, axis=0, keepdims=True), jnp.max(b_, axis=0, keepdims=True))
        pa = jnp.concatenate([jnp.exp2(a_lo - m_lo), jnp.exp2(a_hi - m_hi)], axis=1).astype(BF16)
        pb = jnp.exp2(b_ - m_hi).astype(BF16)
        acc_a = jnp.dot(values(h, ka), pa, preferred_element_type=F32)
        acc_b = jnp.dot(values(h, kb), pb, preferred_element_type=F32)
        m_ref[h] = jnp.concatenate([m_lo, m_hi], axis=1)
        acc_ref[h] = jnp.concatenate([acc_a[:, :hq], acc_a[:, hq:] + acc_b], axis=1)

    n_full = i * (tq // tk)

    def tile_tokens(j):
        return pl.ds(pl.multiple_of(j * tk, tk), tk)

    cur = diagonal_scores(0)
    for h in range(N_HEADS):
        if h + 1 < N_HEADS:
            ahead = diagonal_scores(h + 1)
        else:
            s_ref[0] = scores(0, tile_tokens(0))
        diagonal_init(h, *cur)
        cur = ahead

    def full_tile(j, carry):
        toks = tile_tokens(j)
        nxt = tile_tokens(jnp.minimum(j + 1, n_full - 1))
        for h in range(N_HEADS):
            if h + 1 < N_HEADS:
                s_ref[h + 1] = scores(h + 1, toks)
                update(h, toks, s_ref[h])
            else:
                cur = s_ref[h]
                s_ref[0] = scores(0, nxt)
                update(h, toks, cur)
        return carry

    lax.fori_loop(0, n_full, full_tile, 0)
    outs = [acc_ref[h, :V_HEAD] * (1.0 / acc_ref[h, V_HEAD:V_HEAD + 1]) for h in range(N_HEADS)]
    pairs = [jnp.concatenate(outs[2 * p:2 * p + 2], axis=0).T for p in range(N_HEADS // 2)]

    half = gm_ref.shape[1]
    mla = (jnp.concatenate(pairs, axis=1) * gm_ref[...].astype(F32)).astype(BF16)
    y = jnp.dot(mla, wo_ref[0:half, :], preferred_element_type=F32)
    y = y + jnp.dot(cv_ref[...], wo_ref[half:, :], preferred_element_type=F32)
    y_ref[...] = x_ref[...] + _rms(y, gpost_ref[...])


def _attention_out(qt, k, vt, gm, cv, x2d, wo, gpost, nb, seq, tq, tk):
    hs, t = qt.shape
    hv = vt.shape[0]
    d = x2d.shape[1]
    assert N_HEADS % 2 == 0 and tq % tk == 0
    row = lambda w: pl.BlockSpec((tq, w), lambda b, i: (b * (seq // tq) + i, 0))
    return pl.pallas_call(
        functools.partial(_attn_kernel, tq=tq, tk=tk),
        grid=(nb, seq // tq),
        in_specs=[pl.BlockSpec((hs, tq), lambda b, i: (0, b * (seq // tq) + i)),
                  pl.BlockSpec((seq, hs), lambda b, i: (b, 0)),
                  pl.BlockSpec((hv, seq), lambda b, i: (0, b)),
                  row(gm.shape[1]), row(cv.shape[1]), row(d), _const_spec(wo.shape), _const_spec(gpost.shape)],
        out_specs=row(d),
        out_shape=jax.ShapeDtypeStruct((t, d), F32),
        scratch_shapes=[pltpu.VMEM((N_HEADS, tk, tq), F32), pltpu.VMEM((N_HEADS, 1, tq), F32),
                        pltpu.VMEM((N_HEADS, V_HEAD + ONES_ROWS, tq), F32)],
        compiler_params=pltpu.CompilerParams(dimension_semantics=("arbitrary", "arbitrary"),
                                             vmem_limit_bytes=VMEM_LIMIT_BYTES),
        name="prompt_attn",
    )(qt, k, vt, gm, cv, x2d, wo, gpost)


def _proj_sample_kernel(x_ref, gpre_ref, w1_ref, gq_ref, wq_ref, gkv_ref, wukt_ref, wconv_ref,
                        pos_ref, invf_ref, st_ref,
                        ql_out, qp_out, ckv_out, kpe_out, gm_out, cv_out, conv_out):
    nb, dec_seq, _ = x_ref.shape

    def tok(a, j):
        return a[j * nb:(j + 1) * nb]

    x = jnp.concatenate([x_ref[:, j, :] for j in range(dec_seq)], axis=0)
    cq1, cq2, ck1, ck2 = _token_major_rope_tables(pos_ref, invf_ref, nb)
    proj, cqn, ckv, kpe_blk = _project_common(x, gpre_ref, w1_ref, gq_ref, gkv_ref, ck1, ck2)
    qraw = jnp.dot(cqn, wq_ref[...], preferred_element_type=F32)
    for h in range(N_HEADS):
        qh = _rope_slab(qraw[:, h * HEAD_SLAB:(h + 1) * HEAD_SLAB], cq1, cq2)
        qlh = jnp.dot(qh.astype(BF16), wukt_ref[h], preferred_element_type=F32)
        qph = pltpu.roll(qh, LANES - QK_NOPE, 1)[:, 0:QK_ROPE]
        for j in range(dec_seq):
            ql_out[:, j * N_HEADS + h, :] = tok(qlh, j)
            qp_out[:, j * N_HEADS + h, :] = tok(qph, j)
    kpe = kpe_blk[:, QK_NOPE:QK_NOPE + QK_ROPE]
    for j in range(dec_seq):
        ckv_out[:, j, :] = tok(ckv, j)
        kpe_out[:, j, :] = tok(kpe, j)
    gm_out[...] = _silu(proj(C_GM0, C_GM0 + GROUP))

    u = proj(C_GM0 + 2 * GROUP, C_GM0 + 3 * GROUP) * proj(C_GM0 + 3 * GROUP, C_GM0 + 4 * GROUP)
    s0, s1 = st_ref[:, 0, :], st_ref[:, 1, :]
    um1 = jnp.concatenate([s1, u[:(dec_seq - 1) * nb]], axis=0)
    um2 = jnp.concatenate([s0, s1, u[:(dec_seq - 2) * nb]], axis=0)
    cv_out[...] = _conv_gate(proj, um1, um2, u, wconv_ref)
    for i in range(CONV_W - 1):
        conv_out[:, i, :] = tok(u, dec_seq - (CONV_W - 1) + i)


def _proj_sample(x, gpre, w1, gq, wq, gkv, wukt, wconv, first_pos, state):
    nb, dec_seq, d = x.shape
    assert dec_seq >= CONV_W - 1 and state.shape == (nb, CONV_W - 1, GROUP)
    pos = (first_pos + np.arange(dec_seq, dtype=np.float32)).reshape(dec_seq, 1)
    invf = jnp.pad(_inv_freq(), (QK_NOPE, LANES - QK_NOPE - QK_ROPE)).reshape(1, LANES)
    t = nb * dec_seq
    c = gkv.shape[1]
    out_shape = (
        jax.ShapeDtypeStruct((nb, dec_seq * N_HEADS, c), F32),
        jax.ShapeDtypeStruct((nb, dec_seq * N_HEADS, QK_ROPE), F32),
        jax.ShapeDtypeStruct((nb, dec_seq, c), F32),
        jax.ShapeDtypeStruct((nb, dec_seq, QK_ROPE), F32),
        jax.ShapeDtypeStruct((t, GROUP), F32),
        jax.ShapeDtypeStruct((t, GROUP), F32),
        jax.ShapeDtypeStruct((nb, CONV_W - 1, GROUP), F32),
    )
    return pl.pallas_call(
        _proj_sample_kernel,
        out_shape=out_shape,
        compiler_params=pltpu.CompilerParams(vmem_limit_bytes=VMEM_LIMIT_BYTES),
        name="proj_sample",
    )(x, gpre, w1, gq, wq, gkv, wukt, wconv, pos, invf, state)


def _decode_attn_kernel(pt_ref, ql_ref, qp_ref, cnew_ref, knew_ref, y_hbm, x_hbm, o_ref,
                        ybuf, xbuf, sem, m_ref, l_ref, acc_ref, *, pages, slots, layer, dec_seq):
    g = pl.program_id(0)
    n_g = pl.num_programs(0)
    per_step = ql_ref.shape[0]
    n_chunks = pt_ref.shape[1] // pages
    total = per_step * n_chunks
    page = ybuf.shape[1] // pages
    rows = ql_ref.shape[1]
    dn = (((1,), (1,)), ((), ()))
    ahead = slots - 1

    def chunk_copies(bb, ch, slot, page_ids):
        cps = []
        for i in range(pages):
            pid = page_ids(bb, ch * pages + i)
            tok = pl.ds(i * page, page)
            cps.append(pltpu.make_async_copy(y_hbm.at[layer, pid], ybuf.at[slot, tok, :], sem.at[0, slot]))
            cps.append(pltpu.make_async_copy(x_hbm.at[layer, pid], xbuf.at[slot, :, tok], sem.at[1, slot]))
        return cps

    def start(bb, ch, slot):
        for cp in chunk_copies(bb, ch, slot, lambda r, j: pt_ref[r, j]):
            cp.start()

    def wait(slot):
        for cp in chunk_copies(0, 0, slot, lambda r, j: 0):
            cp.wait()

    def start_nth(step, k, slot):
        start(step * per_step + k // n_chunks, k % n_chunks, slot)

    @pl.when(g == 0)
    def _():
        for k in range(ahead):
            start_nth(0, k, k % slots)

    tok_of_row = lax.broadcasted_iota(jnp.int32, (rows, 1), 0) // N_HEADS
    for e in range(per_step):
        qlf, qpf = ql_ref[e], qp_ref[e]
        s_new = []
        for j in range(dec_seq):
            sj = (jnp.sum(qlf * cnew_ref[e, j:j + 1, :], axis=-1, keepdims=True)
                  + jnp.sum(qpf * knew_ref[e, j:j + 1, :], axis=-1, keepdims=True))
            s_new.append(jnp.where(tok_of_row >= j, sj, -jnp.inf))
        m_n = functools.reduce(jnp.maximum, s_new)
        p_n = [jnp.exp(sj - m_n) for sj in s_new]
        m_ref[e] = m_n
        l_ref[e] = functools.reduce(lambda a, v: a + v, p_n)
        acc_ref[e, 0] = functools.reduce(lambda a, v: a + v,
                                         [p_n[j] * cnew_ref[e, j:j + 1, :] for j in range(dec_seq)])
        acc_ref[e, 1] = jnp.zeros(acc_ref.shape[2:], F32)

    half = pages * page // 2
    toks = [slice(h * half, (h + 1) * half) for h in range(2)]

    def scores(e, slot):
        ql, qp = ql_ref[e].astype(BF16), qp_ref[e].astype(BF16)
        yb = [ybuf[slot, t, :].astype(BF16) for t in toks]
        xb = [xbuf[slot, :, t].astype(BF16) for t in toks]
        s = jnp.concatenate(
            [lax.dot_general(ql, yb[h], dn, preferred_element_type=F32)
             + jnp.dot(qp, xb[h], preferred_element_type=F32) for h in range(2)], axis=1)
        return e, s, yb

    def softmax(e, s, yb):
        m = m_ref[e]
        m_new = jnp.maximum(m, jnp.max(s, axis=-1, keepdims=True))
        alpha = jnp.exp(m - m_new)
        p = jnp.exp(s - m_new)
        l_ref[e] = alpha * l_ref[e] + jnp.sum(p, axis=-1, keepdims=True)
        m_ref[e] = m_new
        return e, alpha, p.astype(BF16), yb

    def values(e, alpha, p, yb):
        for h in range(2):
            acc_ref[e, h] = alpha * acc_ref[e, h] + jnp.dot(p[:, toks[h]], yb[h], preferred_element_type=F32)

    scored = None
    weighted = None
    for k in range(total):
        nxt = k + ahead
        if nxt < total:
            start_nth(g, nxt, nxt % slots)
        else:
            @pl.when(g + 1 < n_g)
            def _():
                start_nth(g + 1, nxt - total, nxt % slots)

        wait(k % slots)
        fresh = scores(k // n_chunks, k % slots)
        if weighted is not None:
            values(*weighted)
        if scored is not None:
            weighted = softmax(*scored)
        scored = fresh
    if weighted is not None:
        values(*weighted)
    values(*softmax(*scored))
    for e in range(per_step):
        o_ref[e] = (acc_ref[e, 0] + acc_ref[e, 1]) * (1.0 / l_ref[e])


def _decode_attention(page_table, ql, qp, cnew, knew, cache_ckv, cache_kpe, layer, pages, slots, per_step):
    nb, rows, c = ql.shape
    dec_seq = cnew.shape[1]
    page, r = cache_kpe.shape[2:]
    n_chunks = page_table.shape[1] // pages
    assert n_chunks * pages == page_table.shape[1] and nb % per_step == 0
    assert (per_step * n_chunks) % slots == 0 and slots - 1 <= per_step * n_chunks
    kpe_t = jnp.swapaxes(cache_kpe, 2, 3)
    per_b = lambda a: pl.BlockSpec((per_step,) + a.shape[1:], lambda g, pt: (g, 0, 0))
    grid_spec = pltpu.PrefetchScalarGridSpec(
        num_scalar_prefetch=1,
        grid=(nb // per_step,),
        in_specs=[per_b(ql), per_b(qp), per_b(cnew), per_b(knew),
                  pl.BlockSpec(memory_space=pl.ANY), pl.BlockSpec(memory_space=pl.ANY)],
        out_specs=pl.BlockSpec((per_step, rows, c), lambda g, pt: (g, 0, 0)),
        scratch_shapes=[pltpu.VMEM((slots, pages * page, c), F32), pltpu.VMEM((slots, r, pages * page), F32),
                        pltpu.SemaphoreType.DMA((2, slots)),
                        pltpu.VMEM((per_step, rows, 1), F32), pltpu.VMEM((per_step, rows, 1), F32),
                        pltpu.VMEM((per_step, 2, rows, c), F32)],
    )
    return pl.pallas_call(
        functools.partial(_decode_attn_kernel, pages=pages, slots=slots, layer=layer, dec_seq=dec_seq),
        grid_spec=grid_spec,
        out_shape=jax.ShapeDtypeStruct((nb, rows, c), F32),
        compiler_params=pltpu.CompilerParams(dimension_semantics=("arbitrary",),
                                             vmem_limit_bytes=VMEM_LIMIT_BYTES),
        name="decode_attn",
    )(page_table, ql, qp, cnew, knew, cache_ckv, kpe_t)


def _out_sample_kernel(ol_ref, gm_ref, cv_ref, x_ref, wuvt_ref, wo_ref, gpost_ref, y_ref):
    nb, dec_seq, _ = x_ref.shape
    half = gm_ref.shape[1]
    dn = (((1,), (1,)), ((), ()))

    def token_major(ref, index_of_token):
        return jnp.concatenate([ref[:, index_of_token(j), :] for j in range(dec_seq)], axis=0)

    o = jnp.concatenate(
        [lax.dot_general(token_major(ol_ref, lambda j: j * N_HEADS + h).astype(BF16), wuvt_ref[h], dn,
                         preferred_element_type=F32) for h in range(N_HEADS)], axis=1)
    mla = (o * gm_ref[...]).astype(BF16)
    y = jnp.dot(mla, wo_ref[0:half, :], preferred_element_type=F32)
    y = y + jnp.dot(cv_ref[...].astype(BF16), wo_ref[half:, :], preferred_element_type=F32)
    y = token_major(x_ref, lambda j: j) + _rms(y, gpost_ref[...])
    for j in range(dec_seq):
        y_ref[:, j, :] = y[j * nb:(j + 1) * nb]


def _out_sample(ol, gm, cv, x, wuv, wo, gpost):
    return pl.pallas_call(
        _out_sample_kernel,
        out_shape=jax.ShapeDtypeStruct(x.shape, F32),
        compiler_params=pltpu.CompilerParams(vmem_limit_bytes=VMEM_LIMIT_BYTES),
        name="out_sample",
    )(ol, gm, cv, x, wuv, wo, gpost)


def _rot_rows(w):
    r = w.shape[0] // 2
    return jnp.concatenate([-w[r:], w[:r]], axis=0)


def _pack_kernel(wint_ref, wuq_ref, wukv_ref, wout_ref,
                 w1_ref, wq_ref, wqt_ref, wk_ref, wvt_ref, wukt_ref, wuvt_ref, wo_ref):
    d = wint_ref.shape[1]
    blk = LANES

    def put_w1(col0, rows):
        w1_ref[:, col0:col0 + blk] = rows.T.astype(BF16)

    for r0 in range(0, C_KV1, blk):
        put_w1(r0, wint_ref[r0:r0 + blk, :])
    kpe = wint_ref[C_KV1:C_KV1 + QK_ROPE, :]
    put_w1(C_KP0, jnp.concatenate([jnp.zeros((QK_NOPE, d), F32), kpe, _rot_rows(kpe)], axis=0))
    src0 = C_KV1 + QK_ROPE
    for j in range(5 * GROUP // blk):
        put_w1(C_GM0 + j * blk, wint_ref[src0 + j * blk:src0 + (j + 1) * blk, :])

    uqt = wuq_ref[...].T
    ukvt = wukv_ref[...].T
    kv_lora = ukvt.shape[1]
    qk = QK_NOPE + QK_ROPE
    wqt_ref[...] = uqt.astype(BF16)
    for h in range(N_HEADS):
        pe = uqt[h * qk + QK_NOPE:(h + 1) * qk]
        q_slab = jnp.concatenate([uqt[h * qk:h * qk + QK_NOPE], pe, _rot_rows(pe)], axis=0)
        wq_ref[:, h * HEAD_SLAB:(h + 1) * HEAD_SLAB] = q_slab.T.astype(BF16)
        ukt = ukvt[h * (QK_NOPE + V_HEAD):h * (QK_NOPE + V_HEAD) + QK_NOPE]
        uvt = ukvt[h * (QK_NOPE + V_HEAD) + QK_NOPE:(h + 1) * (QK_NOPE + V_HEAD)]
        k_slab = jnp.concatenate([ukt, jnp.zeros((HEAD_SLAB - QK_NOPE, kv_lora), F32)], axis=0)
        wukt_ref[h] = k_slab.astype(BF16)
        wvt_ref[h * V_HEAD:(h + 1) * V_HEAD, :] = uvt.astype(BF16)
        wuvt_ref[h] = uvt.astype(BF16)
    for p in range(N_HEADS // 2):
        two = ukvt[2 * p * (QK_NOPE + V_HEAD):2 * (p + 1) * (QK_NOPE + V_HEAD)]
        pair = jnp.concatenate([two[0:QK_NOPE], two[QK_NOPE + V_HEAD:2 * QK_NOPE + V_HEAD]], axis=0)
        wk_ref[:, p * LANES:(p + 1) * LANES] = pair.T.astype(BF16)
    wo_ref[...] = wout_ref[...].astype(BF16)


def _pack_weights(w_in, w_uq, w_ukv, w_out):
    d = w_in.shape[0]
    q_lora, kv_lora = w_uq.shape[0], w_ukv.shape[0]
    assert w_in.shape[1] == C_KV1 + QK_ROPE + 5 * GROUP
    sds = lambda *s: jax.ShapeDtypeStruct(s, BF16)
    return pl.pallas_call(
        _pack_kernel,
        out_shape=(sds(d, D_IN_PACKED), sds(q_lora, N_HEADS * HEAD_SLAB), sds(w_uq.shape[1], q_lora),
                   sds(kv_lora, N_HEADS * QK_NOPE), sds(N_HEADS * V_HEAD, kv_lora),
                   sds(N_HEADS, HEAD_SLAB, kv_lora), sds(N_HEADS, V_HEAD, kv_lora), sds(*w_out.shape)),
        compiler_params=pltpu.CompilerParams(vmem_limit_bytes=VMEM_LIMIT_BYTES),
        name="pack_weights",
    )(w_in.T, w_uq, w_ukv, w_out)


PROMPT_TILE = 1024
ATTN_TQ = 512
ATTN_TK = 512
DECODE_PAGES = 32
DECODE_SLOTS = 4
DECODE_PER_STEP = 1


def kernel(x_prompt, x_sample, cache_ckv, cache_kpe, state_conv, page_table, g_pre, w_in, g_qnorm, w_uq,
           g_kvnorm, w_ukv, w_conv, w_out, g_post):
    depth = w_in.shape[0]
    nb, seq, d = x_prompt.shape
    db, dec_seq, _ = x_sample.shape
    past_len = page_table.shape[1] * cache_ckv.shape[2]
    c = cache_ckv.shape[3]

    tabs_p = _rope_tables(seq)
    taps = jnp.swapaxes(w_conv, 0, 1)[:, :, None, :]

    xp = x_prompt.reshape(nb * seq, d)
    xs = x_sample
    outs = [[] for _ in range(6)]
    for l in range(depth):
        w1, wq, wqt, wk, wvt, wukt, wuvt, wo = _pack_weights(w_in[l], w_uq[l], w_ukv[l], w_out[l])
        gpre, gq, gkv, gpost = g_pre[l][None], g_qnorm[l][None], g_kvnorm[l][None], g_post[l][None]

        qt, k, vt, ckv_p, kpe_p, gm, cv, conv_p = _proj_prompt(
            xp, seq, gpre, w1, gq, wqt, gkv, wk, wvt, taps[:, l], tabs_p, PROMPT_TILE)
        xp = _attention_out(qt, k, vt, gm, cv, xp, wo, gpost, nb, seq, ATTN_TQ, ATTN_TK)
        outs[0].append(ckv_p.reshape(nb, seq, c))
        outs[1].append(jnp.swapaxes(kpe_p, 1, 2))
        outs[2].append(conv_p)

        ql, qp, ckv_s, kpe_s, gm_s, cv_s, conv_s = _proj_sample(
            xs, gpre, w1, gq, wq, gkv, wukt, taps[:, l], past_len, state_conv[l].astype(F32))
        ol = _decode_attention(page_table, ql, qp, ckv_s, kpe_s, cache_ckv, cache_kpe, l,
                               DECODE_PAGES, DECODE_SLOTS, DECODE_PER_STEP)
        xs = _out_sample(ol, gm_s, cv_s, xs, wuvt, wo, gpost)
        outs[3].append(ckv_s)
        outs[4].append(kpe_s)
        outs[5].append(conv_s)

    return (xp.reshape(nb, seq, d), xs, *[jnp.stack(o_) for o_ in outs])
```

```python
import functools

import jax
import jax.numpy as jnp
import numpy as np
from jax import lax
from jax.experimental import pallas as pl
from jax.experimental.pallas import tpu as pltpu

N_HEADS = 8
QK_NOPE = 64
QK_ROPE = 32
V_HEAD = 64
CONV_W = 3
ROPE_THETA = 10000.0
EPS = 1e-6
ATTN_SCALE = (QK_NOPE + QK_ROPE) ** -0.5

LANES = 128
HEAD_SLAB = LANES
ONES_ROWS = 16
LOG2_E = 1.4426950408889634
VMEM_LIMIT_BYTES = 56 * 1024 * 1024

F32 = jnp.float32
BF16 = jnp.bfloat16


def _silu(x):
    return x * (1.0 / (1.0 + jnp.exp(-x)))


def _rms(x, g):
    return x * lax.rsqrt(jnp.mean(x * x, axis=-1, keepdims=True) + EPS) * g


def _inv_freq():
    half = ROPE_THETA ** (-jnp.arange(0, QK_ROPE, 2, dtype=F32) / QK_ROPE)
    return jnp.concatenate([half, half])


def _rope_table_kernel(pos_ref, invf_ref, ck1_ref, ck2_ref, cq1t_ref, cq2t_ref):
    ang = invf_ref[...] * pos_ref[...]
    c, s = jnp.cos(ang), jnp.sin(ang)
    t = ang.shape[1]
    nope = jnp.ones((QK_NOPE, t), F32)
    zn = jnp.zeros((QK_NOPE, t), F32)
    zr = jnp.zeros((HEAD_SLAB - QK_NOPE - QK_ROPE, t), F32)

    def slab(first, mid):
        return jnp.concatenate([first, mid, zr], axis=0)

    ck1_ref[...] = slab(zn, c).T
    ck2_ref[...] = slab(zn, s).T
    cq1t_ref[...] = slab(nope * (ATTN_SCALE * LOG2_E), c * (ATTN_SCALE * LOG2_E))
    cq2t_ref[...] = slab(zn, s * (ATTN_SCALE * LOG2_E))


def _rope_tables(t):
    out = jax.ShapeDtypeStruct((t, LANES), F32)
    out_t = jax.ShapeDtypeStruct((LANES, t), F32)
    return pl.pallas_call(
        _rope_table_kernel,
        out_shape=(out, out, out_t, out_t),
        name="rope_tables",
    )(np.arange(t, dtype=np.float32).reshape(1, t), _inv_freq().reshape(QK_ROPE, 1))


def _token_major_rope_tables(pos_ref, invf_ref, nb):
    ang = pos_ref[...] * invf_ref[...]
    lane = lax.broadcasted_iota(jnp.int32, ang.shape, 1)
    is_rope = (lane >= QK_NOPE) & (lane < QK_NOPE + QK_ROPE)
    c = jnp.where(is_rope, jnp.cos(ang), 0.0)
    s = jnp.where(is_rope, jnp.sin(ang), 0.0)

    def rows(tab):
        return jnp.concatenate([jnp.broadcast_to(tab[j:j + 1], (nb, LANES)) for j in range(tab.shape[0])], axis=0)

    cq1 = jnp.where(lane < QK_NOPE, ATTN_SCALE, c * ATTN_SCALE)
    return rows(cq1), rows(s * ATTN_SCALE), rows(c), rows(s)


C_Q0, C_Q1 = 0, 384
C_KV0, C_KV1 = 384, 640
C_KP0, C_KP1 = 640, 768
C_GM0 = 768
GROUP = 512
D_IN_PACKED = C_GM0 + 5 * GROUP


def _rope_slab(blk, c1, c2):
    return blk * c1 + pltpu.roll(blk, LANES - QK_ROPE, 1) * c2


def _project_common(x, gpre_ref, w1_ref, gq_ref, gkv_ref, ck1, ck2):
    xn = _rms(x, gpre_ref[...]).astype(BF16)

    def proj(c0, c1):
        return jnp.dot(xn, w1_ref[:, c0:c1], preferred_element_type=F32)

    cqn = _rms(proj(C_Q0, C_Q1), gq_ref[...]).astype(BF16)
    ckv = _rms(proj(C_KV0, C_KV1), gkv_ref[...])
    kpe_blk = _rope_slab(proj(C_KP0, C_KP1), ck1, ck2)
    return proj, cqn, ckv, kpe_blk


def _conv_gate(proj, um1, um2, u, wconv_ref):
    y = wconv_ref[0] * um2 + wconv_ref[1] * um1 + wconv_ref[2] * u
    bg = proj(C_GM0 + GROUP, C_GM0 + 2 * GROUP)
    gc = proj(C_GM0 + 4 * GROUP, C_GM0 + 5 * GROUP)
    return bg * y * _silu(gc)


def _proj_prompt_kernel(x_ref, *refs, tiles_per_seq):
    gpre_ref, carry_ref, xn_ref = refs[1], refs[-2], refs[-1]
    step = pl.program_id(0)

    @pl.when(step % tiles_per_seq == 0)
    def _():
        carry_ref[...] = jnp.zeros_like(carry_ref)

    @pl.when(step == 0)
    def _():
        xn_ref[0] = _rms(x_ref[...], gpre_ref[...]).astype(BF16)

    for parity in range(2):
        @pl.when(step % 2 == parity)
        def _():
            _proj_prompt_tile(xn_ref.at[parity], xn_ref.at[1 - parity], *refs[:-1])


def _zero_after(value, zero_ref):
    words = pltpu.bitcast(value, jnp.uint32)
    acc = functools.reduce(jnp.bitwise_or, [words[r:r + 8] for r in range(0, words.shape[0], 8)])
    acc = functools.reduce(jnp.bitwise_or, [acc[:, c:c + LANES] for c in range(0, acc.shape[1], LANES)])
    return (acc & zero_ref[...])[0:1].astype(F32)


def _proj_prompt_tile(xn_ref, xn_next_ref, xnext_ref, gpre_ref, zero_ref, w1_ref, gq_ref, wqt_ref, gkv_ref, wk_ref,
                      wvt_ref, wconv_ref, cq1t_ref, cq2t_ref, ck1_ref, ck2_ref,
                      qt_out, k_out, vt_out, ckv_out, kpet_out, gm_out, cv_out, conv_out, carry_ref):
    tm = xn_ref.shape[0]
    dn = (((1,), (1,)), ((), ()))

    def proj(c0, c1):
        return jnp.dot(xn_ref[...], w1_ref[:, c0:c1], preferred_element_type=F32)

    z_cq = proj(C_Q0, C_Q1)
    z_ckv = proj(C_KV0, C_KV1)
    xn_next = _rms(xnext_ref[...], gpre_ref[...]).astype(BF16)
    xn_next_ref[...] = xn_next
    pin = _zero_after(xn_next, zero_ref)
    gkv = gkv_ref[...] + jnp.concatenate([pin] * (gkv_ref.shape[1] // LANES), axis=1)
    cqn = _rms(z_cq, gq_ref[...]).astype(BF16)
    qt = lax.dot_general(wqt_ref[...], cqn, dn, preferred_element_type=F32)
    ckv = _rms(z_ckv, gkv)
    ckv_out[...] = ckv
    ckvb = ckv.astype(BF16)
    z_kp = proj(C_KP0, C_KP1)
    knope = jnp.dot(ckvb, wk_ref[...], preferred_element_type=F32)
    qk = QK_NOPE + QK_ROPE
    nope_scale = cq1t_ref[0:QK_NOPE, :]
    cos_t, sin_t = cq1t_ref[QK_NOPE:qk, :], cq2t_ref[QK_NOPE:qk, :]
    for h in range(N_HEADS):
        nope, pe = qt[h * qk:h * qk + QK_NOPE, :], qt[h * qk + QK_NOPE:(h + 1) * qk, :]
        qt_out[h * HEAD_SLAB:h * HEAD_SLAB + QK_NOPE, :] = (nope * nope_scale).astype(BF16)
        qt_out[h * HEAD_SLAB + QK_NOPE:h * HEAD_SLAB + qk, :] = (pe * cos_t + _rot_rows(pe) * sin_t).astype(BF16)
        qt_out[h * HEAD_SLAB + qk:(h + 1) * HEAD_SLAB, :] = jnp.zeros((HEAD_SLAB - qk, tm), BF16)
    vt = lax.dot_general(wvt_ref[...], ckvb, dn, preferred_element_type=F32)
    z_gm = proj(C_GM0, C_GM0 + GROUP)
    kpe_blk = _rope_slab(z_kp, ck1_ref[...], ck2_ref[...])
    kpet_out[0] = kpe_blk.T[QK_NOPE:QK_NOPE + QK_ROPE, :]
    lane = lax.broadcasted_iota(jnp.int32, kpe_blk.shape, 1)
    for p in range(N_HEADS // 2):
        two = knope[:, p * LANES:(p + 1) * LANES]
        for h, nope_first in ((2 * p, two), (2 * p + 1, pltpu.roll(two, LANES - QK_NOPE, 1))):
            k_out[:, h * HEAD_SLAB:(h + 1) * HEAD_SLAB] = jnp.where(lane < QK_NOPE, nope_first, kpe_blk).astype(BF16)
    vt_out[...] = vt.astype(BF16)
    z_cg = proj(C_GM0 + 2 * GROUP, C_GM0 + 3 * GROUP)
    z_h = proj(C_GM0 + 3 * GROUP, C_GM0 + 4 * GROUP)
    gm_out[...] = _silu(z_gm).astype(BF16)
    z_bg = proj(C_GM0 + GROUP, C_GM0 + 2 * GROUP)
    z_gc = proj(C_GM0 + 4 * GROUP, C_GM0 + 5 * GROUP)

    u = z_cg * z_h
    prev = carry_ref[...]
    p1, p2 = prev[7:8, :], prev[6:7, :]
    row = lax.broadcasted_iota(jnp.int32, u.shape, 0)
    um1 = jnp.where(row == 0, p1, pltpu.roll(u, 1, 0))
    um2 = jnp.where(row == 0, p2, jnp.where(row == 1, p1, pltpu.roll(u, 2, 0)))
    y = wconv_ref[0] * um2 + wconv_ref[1] * um1 + wconv_ref[2] * u
    cv_out[...] = (z_bg * y * _silu(z_gc)).astype(BF16)
    carry_ref[...] = u[tm - 8:tm, :]
    conv_out[0] = u[tm - (CONV_W - 1):tm, :]


def _const_spec(shape):
    nd = len(shape)
    return pl.BlockSpec(shape, lambda *_: (0,) * nd, pipeline_mode=pl.Buffered(1))


def _proj_prompt(x2d, seq, gpre, w1, gq, wqt, gkv, wk, wvt, wconv, tabs, tm):
    t, d = x2d.shape
    nb = t // seq
    hs = N_HEADS * HEAD_SLAB
    hv = N_HEADS * V_HEAD
    ck1, ck2, cq1t, cq2t = tabs
    row = lambda w: pl.BlockSpec((tm, w), lambda i: (i, 0))
    col = lambda h: pl.BlockSpec((h, tm), lambda i: (0, i))
    tab = pl.BlockSpec((tm, LANES), lambda i: (i % (seq // tm), 0))
    tab_t = pl.BlockSpec((LANES, tm), lambda i: (0, i % (seq // tm)))
    n_tiles = t // tm
    x_first = pl.BlockSpec((tm, d), lambda i: (0, 0))
    x_next = pl.BlockSpec((tm, d), lambda i: (jnp.minimum(i + 1, n_tiles - 1), 0))
    zero = np.zeros((8, LANES), np.uint32)
    out_shape = (
        jax.ShapeDtypeStruct((hs, t), BF16),
        jax.ShapeDtypeStruct((t, hs), BF16),
        jax.ShapeDtypeStruct((hv, t), BF16),
        jax.ShapeDtypeStruct((t, gkv.shape[1]), F32),
        jax.ShapeDtypeStruct((nb, QK_ROPE, seq), F32),
        jax.ShapeDtypeStruct((t, GROUP), BF16),
        jax.ShapeDtypeStruct((t, GROUP), BF16),
        jax.ShapeDtypeStruct((nb, CONV_W - 1, GROUP), F32),
    )
    return pl.pallas_call(
        functools.partial(_proj_prompt_kernel, tiles_per_seq=seq // tm),
        grid=(n_tiles,),
        in_specs=[x_first, x_next, _const_spec(gpre.shape), _const_spec(zero.shape), _const_spec(w1.shape),
                  _const_spec(gq.shape),
                  _const_spec(wqt.shape), _const_spec(gkv.shape), _const_spec(wk.shape), _const_spec(wvt.shape),
                  _const_spec(wconv.shape), tab_t, tab_t, tab, tab],
        out_specs=(col(hs), row(hs), col(hv), row(gkv.shape[1]),
                   pl.BlockSpec((1, QK_ROPE, tm), lambda i: (i // (seq // tm), 0, i % (seq // tm))),
                   row(GROUP), row(GROUP),
                   pl.BlockSpec((1, CONV_W - 1, GROUP), lambda i: (i // (seq // tm), 0, 0))),
        out_shape=out_shape,
        scratch_shapes=[pltpu.VMEM((8, GROUP), F32), pltpu.VMEM((2, tm, d), BF16)],
        compiler_params=pltpu.CompilerParams(dimension_semantics=("arbitrary",),
                                             vmem_limit_bytes=VMEM_LIMIT_BYTES),
        name="proj_prompt",
    )(x2d, x2d, gpre, zero, w1, gq, wqt, gkv, wk, wvt, wconv, cq1t, cq2t, ck1, ck2)


def _attn_kernel(qt_ref, k_ref, vt_ref, gm_ref, cv_ref, x_ref, wo_ref, gpost_ref, y_ref,
                 s_ref, m_ref, acc_ref, *, tq, tk):
    i = pl.program_id(1)
    hq = tq // 2
    tri = (lax.broadcasted_iota(jnp.int32, (hq, hq), 0) <= lax.broadcasted_iota(jnp.int32, (hq, hq), 1))

    def slab(h):
        return slice(h * HEAD_SLAB, (h + 1) * HEAD_SLAB)

    def values(h, toks):
        ones = jnp.ones((ONES_ROWS, toks.size), BF16)
        return jnp.concatenate([vt_ref[h * V_HEAD:(h + 1) * V_HEAD, toks], ones], 0)

    def scores(h, toks):
        return jnp.dot(k_ref[toks, slab(h)], qt_ref[slab(h), :], preferred_element_type=F32)

    def update(h, toks, st):
        m = m_ref[h]
        m_new = jnp.maximum(m, jnp.max(st, axis=0, keepdims=True))
        pt = jnp.exp2(st - m_new).astype(BF16)
        m_ref[h] = m_new
        acc_ref[h] = jnp.exp2(m - m_new) * acc_ref[h] + jnp.dot(values(h, toks), pt, preferred_element_type=F32)

    base = pl.multiple_of(i * tq, tq)
    ka, kb = pl.ds(base, hq), pl.ds(base + hq, hq)

    def diagonal_scores(h):
        return (jnp.dot(k_ref[ka, slab(h)], qt_ref[slab(h), :], preferred_element_type=F32),
                jnp.dot(k_ref[kb, slab(h)], qt_ref[slab(h), hq:], preferred_element_type=F32))

    def diagonal_init(h, a, b_):
        a_lo = jnp.where(tri, a[:, :hq], -jnp.inf)
        a_hi = a[:, hq:]
        b_ = jnp.where(tri, b_, -jnp.inf)
        m_lo = jnp.max(a_lo, axis=0, keepdims=True)
        m_hi = jnp.maximum(jnp.max(a_hi, axis=0, keepdims=True), jnp.max(b_, axis=0, keepdims=True))
        pa = jnp.concatenate([jnp.exp2(a_lo - m_lo), jnp.exp2(a_hi - m_hi)], axis=1).astype(BF16)
        pb = jnp.exp2(b_ - m_hi).astype(BF16)
        acc_a = jnp.dot(values(h, ka), pa, preferred_element_type=F32)
        acc_b = jnp.dot(values(h, kb), pb, preferred_element_type=F32)
        m_ref[h] = jnp.concatenate([m_lo, m_hi], axis=1)
        acc_ref[h] = jnp.concatenate([acc_a[:, :hq], acc_a[:, hq:] + acc_b], axis=1)

    n_full = i * (tq // tk)

    def tile_tokens(j):
        return pl.ds(pl.multiple_of(j * tk, tk), tk)

    cur = diagonal_scores(0)
    for h in range(N_HEADS):
        if h + 1 < N_HEADS:
            ahead = diagonal_scores(h + 1)
        else:
            s_ref[0] = scores(0, tile_tokens(0))
        diagonal_init(h, *cur)
        cur = ahead

    def full_tile(j, carry):
        toks = tile_tokens(j)
        nxt = tile_tokens(jnp.minimum(j + 1, n_full - 1))
        for h in range(N_HEADS):
            if h + 1 < N_HEADS:
                s_ref[h + 1] = scores(h + 1, toks)
                update(h, toks, s_ref[h])
            else:
                cur = s_ref[h]
                s_ref[0] = scores(0, nxt)
                update(h, toks, cur)
        return carry

    lax.fori_loop(0, n_full, full_tile, 0)
    outs = [acc_ref[h, :V_HEAD] * (1.0 / acc_ref[h, V_HEAD:V_HEAD + 1]) for h in range(N_HEADS)]
    pairs = [jnp.concatenate(outs[2 * p:2 * p + 2], axis=0).T for p in range(N_HEADS // 2)]

    half = gm_ref.shape[1]
    mla = (jnp.concatenate(pairs, axis=1) * gm_ref[...].astype(F32)).astype(BF16)
    y = jnp.dot(mla, wo_ref[0:half, :], preferred_element_type=F32)
    y = y + jnp.dot(cv_ref[...], wo_ref[half:, :], preferred_element_type=F32)
    y_ref[...] = x_ref[...] + _rms(y, gpost_ref[...])


def _attention_out(qt, k, vt, gm, cv, x2d, wo, gpost, nb, seq, tq, tk):
    hs, t = qt.shape
    hv = vt.shape[0]
    d = x2d.shape[1]
    assert N_HEADS % 2 == 0 and tq % tk == 0
    row = lambda w: pl.BlockSpec((tq, w), lambda b, i: (b * (seq // tq) + i, 0))
    return pl.pallas_call(
        functools.partial(_attn_kernel, tq=tq, tk=tk),
        grid=(nb, seq // tq),
        in_specs=[pl.BlockSpec((hs, tq), lambda b, i: (0, b * (seq // tq) + i)),
                  pl.BlockSpec((seq, hs), lambda b, i: (b, 0)),
                  pl.BlockSpec((hv, seq), lambda b, i: (0, b)),
                  row(gm.shape[1]), row(cv.shape[1]), row(d), _const_spec(wo.shape), _const_spec(gpost.shape)],
        out_specs=row(d),
        out_shape=jax.ShapeDtypeStruct((t, d), F32),
        scratch_shapes=[pltpu.VMEM((N_HEADS, tk, tq), F32), pltpu.VMEM((N_HEADS, 1, tq), F32),
                        pltpu.VMEM((N_HEADS, V_HEAD + ONES_ROWS, tq), F32)],
        compiler_params=pltpu.CompilerParams(dimension_semantics=("arbitrary", "arbitrary"),
                                             vmem_limit_bytes=VMEM_LIMIT_BYTES),
        name="prompt_attn",
    )(qt, k, vt, gm, cv, x2d, wo, gpost)


def _proj_sample_kernel(x_ref, gpre_ref, w1_ref, gq_ref, wq_ref, gkv_ref, wukt_ref, wconv_ref,
                        pos_ref, invf_ref, st_ref,
                        ql_out, qp_out, ckv_out, kpe_out, gm_out, cv_out, conv_out):
    nb, dec_seq, _ = x_ref.shape

    def tok(a, j):
        return a[j * nb:(j + 1) * nb]

    x = jnp.concatenate([x_ref[:, j, :] for j in range(dec_seq)], axis=0)
    cq1, cq2, ck1, ck2 = _token_major_rope_tables(pos_ref, invf_ref, nb)
    proj, cqn, ckv, kpe_blk = _project_common(x, gpre_ref, w1_ref, gq_ref, gkv_ref, ck1, ck2)
    qraw = jnp.dot(cqn, wq_ref[...], preferred_element_type=F32)
    for h in range(N_HEADS):
        qh = _rope_slab(qraw[:, h * HEAD_SLAB:(h + 1) * HEAD_SLAB], cq1, cq2)
        qlh = jnp.dot(qh.astype(BF16), wukt_ref[h], preferred_element_type=F32)
        qph = pltpu.roll(qh, LANES - QK_NOPE, 1)[:, 0:QK_ROPE]
        for j in range(dec_seq):
            ql_out[:, j * N_HEADS + h, :] = tok(qlh, j)
            qp_out[:, j * N_HEADS + h, :] = tok(qph, j)
    kpe = kpe_blk[:, QK_NOPE:QK_NOPE + QK_ROPE]
    for j in range(dec_seq):
        ckv_out[:, j, :] = tok(ckv, j)
        kpe_out[:, j, :] = tok(kpe, j)
    gm_out[...] = _silu(proj(C_GM0, C_GM0 + GROUP))

    u = proj(C_GM0 + 2 * GROUP, C_GM0 + 3 * GROUP) * proj(C_GM0 + 3 * GROUP, C_GM0 + 4 * GROUP)
    s0, s1 = st_ref[:, 0, :], st_ref[:, 1, :]
    um1 = jnp.concatenate([s1, u[:(dec_seq - 1) * nb]], axis=0)
    um2 = jnp.concatenate([s0, s1, u[:(dec_seq - 2) * nb]], axis=0)
    cv_out[...] = _conv_gate(proj, um1, um2, u, wconv_ref)
    for i in range(CONV_W - 1):
        conv_out[:, i, :] = tok(u, dec_seq - (CONV_W - 1) + i)


def _proj_sample(x, gpre, w1, gq, wq, gkv, wukt, wconv, first_pos, state):
    nb, dec_seq, d = x.shape
    assert dec_seq >= CONV_W - 1 and state.shape == (nb, CONV_W - 1, GROUP)
    pos = (first_pos + np.arange(dec_seq, dtype=np.float32)).reshape(dec_seq, 1)
    invf = jnp.pad(_inv_freq(), (QK_NOPE, LANES - QK_NOPE - QK_ROPE)).reshape(1, LANES)
    t = nb * dec_seq
    c = gkv.shape[1]
    out_shape = (
        jax.ShapeDtypeStruct((nb, dec_seq * N_HEADS, c), F32),
        jax.ShapeDtypeStruct((nb, dec_seq * N_HEADS, QK_ROPE), F32),
        jax.ShapeDtypeStruct((nb, dec_seq, c), F32),
        jax.ShapeDtypeStruct((nb, dec_seq, QK_ROPE), F32),
        jax.ShapeDtypeStruct((t, GROUP), F32),
        jax.ShapeDtypeStruct((t, GROUP), F32),
        jax.ShapeDtypeStruct((nb, CONV_W - 1, GROUP), F32),
    )
    return pl.pallas_call(
        _proj_sample_kernel,
        out_shape=out_shape,
        compiler_params=pltpu.CompilerParams(vmem_limit_bytes=VMEM_LIMIT_BYTES),
        name="proj_sample",
    )(x, gpre, w1, gq, wq, gkv, wukt, wconv, pos, invf, state)


def _decode_attn_kernel(pt_ref, ql_ref, qp_ref, cnew_ref, knew_ref, y_hbm, x_hbm, o_ref,
                        ybuf, xbuf, sem, m_ref, l_ref, acc_ref, *, pages, slots, layer, dec_seq):
    g = pl.program_id(0)
    n_g = pl.num_programs(0)
    per_step = ql_ref.shape[0]
    n_chunks = pt_ref.shape[1] // pages
    total = per_step * n_chunks
    page = ybuf.shape[1] // pages
    rows = ql_ref.shape[1]
    dn = (((1,), (1,)), ((), ()))
    ahead = slots - 1

    def chunk_copies(bb, ch, slot, page_ids):
        cps = []
        for i in range(pages):
            pid = page_ids(bb, ch * pages + i)
            tok = pl.ds(i * page, page)
            cps.append(pltpu.make_async_copy(y_hbm.at[layer, pid], ybuf.at[slot, tok, :], sem.at[0, slot]))
            cps.append(pltpu.make_async_copy(x_hbm.at[layer, pid], xbuf.at[slot, :, tok], sem.at[1, slot]))
        return cps

    def start(bb, ch, slot):
        for cp in chunk_copies(bb, ch, slot, lambda r, j: pt_ref[r, j]):
            cp.start()

    def wait(slot):
        for cp in chunk_copies(0, 0, slot, lambda r, j: 0):
            cp.wait()

    def start_nth(step, k, slot):
        start(step * per_step + k // n_chunks, k % n_chunks, slot)

    @pl.when(g == 0)
    def _():
        for k in range(ahead):
            start_nth(0, k, k % slots)

    tok_of_row = lax.broadcasted_iota(jnp.int32, (rows, 1), 0) // N_HEADS
    for e in range(per_step):
        qlf, qpf = ql_ref[e], qp_ref[e]
        s_new = []
        for j in range(dec_seq):
            sj = (jnp.sum(qlf * cnew_ref[e, j:j + 1, :], axis=-1, keepdims=True)
                  + jnp.sum(qpf * knew_ref[e, j:j + 1, :], axis=-1, keepdims=True))
            s_new.append(jnp.where(tok_of_row >= j, sj, -jnp.inf))
        m_n = functools.reduce(jnp.maximum, s_new)
        p_n = [jnp.exp(sj - m_n) for sj in s_new]
        m_ref[e] = m_n
        l_ref[e] = functools.reduce(lambda a, v: a + v, p_n)
        acc_ref[e, 0] = functools.reduce(lambda a, v: a + v,
                                         [p_n[j] * cnew_ref[e, j:j + 1, :] for j in range(dec_seq)])
        acc_ref[e, 1] = jnp.zeros(acc_ref.shape[2:], F32)

    half = pages * page // 2
    toks = [slice(h * half, (h + 1) * half) for h in range(2)]

    def scores(e, slot):
        ql, qp = ql_ref[e].astype(BF16), qp_ref[e].astype(BF16)
        yb = [ybuf[slot, t, :].astype(BF16) for t in toks]
        xb = [xbuf[slot, :, t].astype(BF16) for t in toks]
        s = jnp.concatenate(
            [lax.dot_general(ql, yb[h], dn, preferred_element_type=F32)
             + jnp.dot(qp, xb[h], preferred_element_type=F32) for h in range(2)], axis=1)
        return e, s, yb

    def softmax(e, s, yb):
        m = m_ref[e]
        m_new = jnp.maximum(m, jnp.max(s, axis=-1, keepdims=True))
        alpha = jnp.exp(m - m_new)
        p = jnp.exp(s - m_new)
        l_ref[e] = alpha * l_ref[e] + jnp.sum(p, axis=-1, keepdims=True)
        m_ref[e] = m_new
        return e, alpha, p.astype(BF16), yb

    def values(e, alpha, p, yb):
        for h in range(2):
            acc_ref[e, h] = alpha * acc_ref[e, h] + jnp.dot(p[:, toks[h]], yb[h], preferred_element_type=F32)

    scored = None
    weighted = None
    for k in range(total):
        nxt = k + ahead
        if nxt < total:
            start_nth(g, nxt, nxt % slots)
        else:
            @pl.when(g + 1 < n_g)
            def _():
                start_nth(g + 1, nxt - total, nxt % slots)

        wait(k % slots)
        fresh = scores(k // n_chunks, k % slots)
        if weighted is not None:
            values(*weighted)
        if scored is not None:
            weighted = softmax(*scored)
        scored = fresh
    if weighted is not None:
        values(*weighted)
    values(*softmax(*scored))
    for e in range(per_step):
        o_ref[e] = (acc_ref[e, 0] + acc_ref[e, 1]) * (1.0 / l_ref[e])


def _decode_attention(page_table, ql, qp, cnew, knew, cache_ckv, cache_kpe, layer, pages, slots, per_step):
    nb, rows, c = ql.shape
    dec_seq = cnew.shape[1]
    page, r = cache_kpe.shape[2:]
    n_chunks = page_table.shape[1] // pages
    assert n_chunks * pages == page_table.shape[1] and nb % per_step == 0
    assert (per_step * n_chunks) % slots == 0 and slots - 1 <= per_step * n_chunks
    kpe_t = jnp.swapaxes(cache_kpe, 2, 3)
    per_b = lambda a: pl.BlockSpec((per_step,) + a.shape[1:], lambda g, pt: (g, 0, 0))
    grid_spec = pltpu.PrefetchScalarGridSpec(
        num_scalar_prefetch=1,
        grid=(nb // per_step,),
        in_specs=[per_b(ql), per_b(qp), per_b(cnew), per_b(knew),
                  pl.BlockSpec(memory_space=pl.ANY), pl.BlockSpec(memory_space=pl.ANY)],
        out_specs=pl.BlockSpec((per_step, rows, c), lambda g, pt: (g, 0, 0)),
        scratch_shapes=[pltpu.VMEM((slots, pages * page, c), F32), pltpu.VMEM((slots, r, pages * page), F32),
                        pltpu.SemaphoreType.DMA((2, slots)),
                        pltpu.VMEM((per_step, rows, 1), F32), pltpu.VMEM((per_step, rows, 1), F32),
                        pltpu.VMEM((per_step, 2, rows, c), F32)],
    )
    return pl.pallas_call(
        functools.partial(_decode_attn_kernel, pages=pages, slots=slots, layer=layer, dec_seq=dec_seq),
        grid_spec=grid_spec,
        out_shape=jax.ShapeDtypeStruct((nb, rows, c), F32),
        compiler_params=pltpu.CompilerParams(dimension_semantics=("arbitrary",),
                                             vmem_limit_bytes=VMEM_LIMIT_BYTES),
        name="decode_attn",
    )(page_table, ql, qp, cnew, knew, cache_ckv, kpe_t)


def _out_sample_kernel(ol_ref, gm_ref, cv_ref, x_ref, wuvt_ref, wo_ref, gpost_ref, y_ref):
    nb, dec_seq, _ = x_ref.shape
    half = gm_ref.shape[1]
    dn = (((1,), (1,)), ((), ()))

    def token_major(ref, index_of_token):
        return jnp.concatenate([ref[:, index_of_token(j), :] for j in range(dec_seq)], axis=0)

    o = jnp.concatenate(
        [lax.dot_general(token_major(ol_ref, lambda j: j * N_HEADS + h).astype(BF16), wuvt_ref[h], dn,
                         preferred_element_type=F32) for h in range(N_HEADS)], axis=1)
    mla = (o * gm_ref[...]).astype(BF16)
    y = jnp.dot(mla, wo_ref[0:half, :], preferred_element_type=F32)
    y = y + jnp.dot(cv_ref[...].astype(BF16), wo_ref[half:, :], preferred_element_type=F32)
    y = token_major(x_ref, lambda j: j) + _rms(y, gpost_ref[...])
    for j in range(dec_seq):
        y_ref[:, j, :] = y[j * nb:(j + 1) * nb]


def _out_sample(ol, gm, cv, x, wuv, wo, gpost):
    return pl.pallas_call(
        _out_sample_kernel,
        out_shape=jax.ShapeDtypeStruct(x.shape, F32),
        compiler_params=pltpu.CompilerParams(vmem_limit_bytes=VMEM_LIMIT_BYTES),
        name="out_sample",
    )(ol, gm, cv, x, wuv, wo, gpost)


def _rot_rows(w):
    r = w.shape[0] // 2
    return jnp.concatenate([-w[r:], w[:r]], axis=0)


def _pack_kernel(wint_ref, wuq_ref, wukv_ref, wout_ref,
                 w1_ref, wq_ref, wqt_ref, wk_ref, wvt_ref, wukt_ref, wuvt_ref, wo_ref):
    d = wint_ref.shape[1]
    blk = LANES

    def put_w1(col0, rows):
        w1_ref[:, col0:col0 + blk] = rows.T.astype(BF16)

    for r0 in range(0, C_KV1, blk):
        put_w1(r0, wint_ref[r0:r0 + blk, :])
    kpe = wint_ref[C_KV1:C_KV1 + QK_ROPE, :]
    put_w1(C_KP0, jnp.concatenate([jnp.zeros((QK_NOPE, d), F32), kpe, _rot_rows(kpe)], axis=0))
    src0 = C_KV1 + QK_ROPE
    for j in range(5 * GROUP // blk):
        put_w1(C_GM0 + j * blk, wint_ref[src0 + j * blk:src0 + (j + 1) * blk, :])

    uqt = wuq_ref[...].T
    ukvt = wukv_ref[...].T
    kv_lora = ukvt.shape[1]
    qk = QK_NOPE + QK_ROPE
    wqt_ref[...] = uqt.astype(BF16)
    for h in range(N_HEADS):
        pe = uqt[h * qk + QK_NOPE:(h + 1) * qk]
        q_slab = jnp.concatenate([uqt[h * qk:h * qk + QK_NOPE], pe, _rot_rows(pe)], axis=0)
        wq_ref[:, h * HEAD_SLAB:(h + 1) * HEAD_SLAB] = q_slab.T.astype(BF16)
        ukt = ukvt[h * (QK_NOPE + V_HEAD):h * (QK_NOPE + V_HEAD) + QK_NOPE]
        uvt = ukvt[h * (QK_NOPE + V_HEAD) + QK_NOPE:(h + 1) * (QK_NOPE + V_HEAD)]
        k_slab = jnp.concatenate([ukt, jnp.zeros((HEAD_SLAB - QK_NOPE, kv_lora), F32)], axis=0)
        wukt_ref[h] = k_slab.astype(BF16)
        wvt_ref[h * V_HEAD:(h + 1) * V_HEAD, :] = uvt.astype(BF16)
        wuvt_ref[h] = uvt.astype(BF16)
    for p in range(N_HEADS // 2):
        two = ukvt[2 * p * (QK_NOPE + V_HEAD):2 * (p + 1) * (QK_NOPE + V_HEAD)]
        pair = jnp.concatenate([two[0:QK_NOPE], two[QK_NOPE + V_HEAD:2 * QK_NOPE + V_HEAD]], axis=0)
        wk_ref[:, p * LANES:(p + 1) * LANES] = pair.T.astype(BF16)
    wo_ref[...] = wout_ref[...].astype(BF16)


def _pack_weights(w_in, w_uq, w_ukv, w_out):
    d = w_in.shape[0]
    q_lora, kv_lora = w_uq.shape[0], w_ukv.shape[0]
    assert w_in.shape[1] == C_KV1 + QK_ROPE + 5 * GROUP
    sds = lambda *s: jax.ShapeDtypeStruct(s, BF16)
    return pl.pallas_call(
        _pack_kernel,
        out_shape=(sds(d, D_IN_PACKED), sds(q_lora, N_HEADS * HEAD_SLAB), sds(w_uq.shape[1], q_lora),
                   sds(kv_lora, N_HEADS * QK_NOPE), sds(N_HEADS * V_HEAD, kv_lora),
                   sds(N_HEADS, HEAD_SLAB, kv_lora), sds(N_HEADS, V_HEAD, kv_lora), sds(*w_out.shape)),
        compiler_params=pltpu.CompilerParams(vmem_limit_bytes=VMEM_LIMIT_BYTES),
        name="pack_weights",
    )(w_in.T, w_uq, w_ukv, w_out)


PROMPT_TILE = 512
ATTN_TQ = 512
ATTN_TK = 512
DECODE_PAGES = 32
DECODE_SLOTS = 4
DECODE_PER_STEP = 1


def kernel(x_prompt, x_sample, cache_ckv, cache_kpe, state_conv, page_table, g_pre, w_in, g_qnorm, w_uq,
           g_kvnorm, w_ukv, w_conv, w_out, g_post):
    depth = w_in.shape[0]
    nb, seq, d = x_prompt.shape
    db, dec_seq, _ = x_sample.shape
    past_len = page_table.shape[1] * cache_ckv.shape[2]
    c = cache_ckv.shape[3]

    tabs_p = _rope_tables(seq)
    taps = jnp.swapaxes(w_conv, 0, 1)[:, :, None, :]

    xp = x_prompt.reshape(nb * seq, d)
    xs = x_sample
    outs = [[] for _ in range(6)]
    for l in range(depth):
        w1, wq, wqt, wk, wvt, wukt, wuvt, wo = _pack_weights(w_in[l], w_uq[l], w_ukv[l], w_out[l])
        gpre, gq, gkv, gpost = g_pre[l][None], g_qnorm[l][None], g_kvnorm[l][None], g_post[l][None]

        qt, k, vt, ckv_p, kpe_p, gm, cv, conv_p = _proj_prompt(
            xp, seq, gpre, w1, gq, wqt, gkv, wk, wvt, taps[:, l], tabs_p, PROMPT_TILE)
        xp = _attention_out(qt, k, vt, gm, cv, xp, wo, gpost, nb, seq, ATTN_TQ, ATTN_TK)
        outs[0].append(ckv_p.reshape(nb, seq, c))
        outs[1].append(jnp.swapaxes(kpe_p, 1, 2))
        outs[2].append(conv_p)

        ql, qp, ckv_s, kpe_s, gm_s, cv_s, conv_s = _proj_sample(
            xs, gpre, w1, gq, wq, gkv, wukt, taps[:, l], past_len, state_conv[l].astype(F32))
        ol = _decode_attention(page_table, ql, qp, ckv_s, kpe_s, cache_ckv, cache_kpe, l,
                               DECODE_PAGES, DECODE_SLOTS, DECODE_PER_STEP)
        xs = _out_sample(ol, gm_s, cv_s, xs, wuvt, wo, gpost)
        outs[3].append(ckv_s)
        outs[4].append(kpe_s)
        outs[5].append(conv_s)

    return (xp.reshape(nb, seq, d), xs, *[jnp.stack(o_) for o_ in outs])
```

```python
import functools

import jax
import jax.numpy as jnp
import numpy as np
from jax import lax
from jax.experimental import pallas as pl
from jax.experimental.pallas import tpu as pltpu

N_HEADS = 8
QK_NOPE = 64
QK_ROPE = 32
V_HEAD = 64
CONV_W = 3
ROPE_THETA = 10000.0
EPS = 1e-6
ATTN_SCALE = (QK_NOPE + QK_ROPE) ** -0.5

LANES = 128
HEAD_SLAB = LANES
ONES_ROWS = 16
LOG2_E = 1.4426950408889634
VMEM_LIMIT_BYTES = 56 * 1024 * 1024

F32 = jnp.float32
BF16 = jnp.bfloat16


def _silu(x):
    return x * (1.0 / (1.0 + jnp.exp(-x)))


def _rms(x, g):
    return x * lax.rsqrt(jnp.mean(x * x, axis=-1, keepdims=True) + EPS) * g


def _inv_freq():
    half = ROPE_THETA ** (-jnp.arange(0, QK_ROPE, 2, dtype=F32) / QK_ROPE)
    return jnp.concatenate([half, half])


def _rope_table_kernel(pos_ref, invf_ref, ck1_ref, ck2_ref, cq1t_ref, cq2t_ref):
    ang = invf_ref[...] * pos_ref[...]
    c, s = jnp.cos(ang), jnp.sin(ang)
    t = ang.shape[1]
    nope = jnp.ones((QK_NOPE, t), F32)
    zn = jnp.zeros((QK_NOPE, t), F32)
    zr = jnp.zeros((HEAD_SLAB - QK_NOPE - QK_ROPE, t), F32)

    def slab(first, mid):
        return jnp.concatenate([first, mid, zr], axis=0)

    ck1_ref[...] = slab(zn, c).T
    ck2_ref[...] = slab(zn, s).T
    cq1t_ref[...] = slab(nope * (ATTN_SCALE * LOG2_E), c * (ATTN_SCALE * LOG2_E))
    cq2t_ref[...] = slab(zn, s * (ATTN_SCALE * LOG2_E))


def _rope_tables(t):
    out = jax.ShapeDtypeStruct((t, LANES), F32)
    out_t = jax.ShapeDtypeStruct((LANES, t), F32)
    return pl.pallas_call(
        _rope_table_kernel,
        out_shape=(out, out, out_t, out_t),
        name="rope_tables",
    )(np.arange(t, dtype=np.float32).reshape(1, t), _inv_freq().reshape(QK_ROPE, 1))


def _token_major_rope_tables(pos_ref, invf_ref, nb):
    ang = pos_ref[...] * invf_ref[...]
    lane = lax.broadcasted_iota(jnp.int32, ang.shape, 1)
    is_rope = (lane >= QK_NOPE) & (lane < QK_NOPE + QK_ROPE)
    c = jnp.where(is_rope, jnp.cos(ang), 0.0)
    s = jnp.where(is_rope, jnp.sin(ang), 0.0)

    def rows(tab):
        return jnp.concatenate([jnp.broadcast_to(tab[j:j + 1], (nb, LANES)) for j in range(tab.shape[0])], axis=0)

    cq1 = jnp.where(lane < QK_NOPE, ATTN_SCALE, c * ATTN_SCALE)
    return rows(cq1), rows(s * ATTN_SCALE), rows(c), rows(s)


C_Q0, C_Q1 = 0, 384
C_KV0, C_KV1 = 384, 640
C_KP0, C_KP1 = 640, 768
C_GM0 = 768
GROUP = 512
D_IN_PACKED = C_GM0 + 5 * GROUP


def _rope_slab(blk, c1, c2):
    return blk * c1 + pltpu.roll(blk, LANES - QK_ROPE, 1) * c2


def _project_common(x, gpre_ref, w1_ref, gq_ref, gkv_ref, ck1, ck2):
    xn = _rms(x, gpre_ref[...]).astype(BF16)

    def proj(c0, c1):
        return jnp.dot(xn, w1_ref[:, c0:c1], preferred_element_type=F32)

    cqn = _rms(proj(C_Q0, C_Q1), gq_ref[...]).astype(BF16)
    ckv = _rms(proj(C_KV0, C_KV1), gkv_ref[...])
    kpe_blk = _rope_slab(proj(C_KP0, C_KP1), ck1, ck2)
    return proj, cqn, ckv, kpe_blk


def _conv_gate(proj, um1, um2, u, wconv_ref):
    y = wconv_ref[0] * um2 + wconv_ref[1] * um1 + wconv_ref[2] * u
    bg = proj(C_GM0 + GROUP, C_GM0 + 2 * GROUP)
    gc = proj(C_GM0 + 4 * GROUP, C_GM0 + 5 * GROUP)
    return bg * y * _silu(gc)


def _proj_prompt_kernel(x_ref, gpre_ref, w1_ref, gq_ref, wqt_ref, gkv_ref, wk_ref, wvt_ref, wconv_ref,
                        cq1t_ref, cq2t_ref, ck1_ref, ck2_ref,
                        qt_out, k_out, vt_out, ckv_out, kpet_out, gm_out, cv_out, conv_out,
                        carry_ref, *, tiles_per_seq):
    tm = x_ref.shape[0]
    step = pl.program_id(0)
    dn = (((1,), (1,)), ((), ()))

    @pl.when(step % tiles_per_seq == 0)
    def _():
        carry_ref[...] = jnp.zeros_like(carry_ref)

    xn = _rms(x_ref[...], gpre_ref[...]).astype(BF16)

    def proj(c0, c1):
        return jnp.dot(xn, w1_ref[:, c0:c1], preferred_element_type=F32)

    z_cq = proj(C_Q0, C_Q1)
    z_ckv = proj(C_KV0, C_KV1)
    cqn = _rms(z_cq, gq_ref[...]).astype(BF16)
    qt = lax.dot_general(wqt_ref[...], cqn, dn, preferred_element_type=F32)
    ckv = _rms(z_ckv, gkv_ref[...])
    ckv_out[...] = ckv
    ckvb = ckv.astype(BF16)
    z_kp = proj(C_KP0, C_KP1)
    knope = jnp.dot(ckvb, wk_ref[...], preferred_element_type=F32)
    qk = QK_NOPE + QK_ROPE
    nope_scale = cq1t_ref[0:QK_NOPE, :]
    cos_t, sin_t = cq1t_ref[QK_NOPE:qk, :], cq2t_ref[QK_NOPE:qk, :]
    for h in range(N_HEADS):
        nope, pe = qt[h * qk:h * qk + QK_NOPE, :], qt[h * qk + QK_NOPE:(h + 1) * qk, :]
        qt_out[h * HEAD_SLAB:h * HEAD_SLAB + QK_NOPE, :] = (nope * nope_scale).astype(BF16)
        qt_out[h * HEAD_SLAB + QK_NOPE:h * HEAD_SLAB + qk, :] = (pe * cos_t + _rot_rows(pe) * sin_t).astype(BF16)
        qt_out[h * HEAD_SLAB + qk:(h + 1) * HEAD_SLAB, :] = jnp.zeros((HEAD_SLAB - qk, tm), BF16)
    vt = lax.dot_general(wvt_ref[...], ckvb, dn, preferred_element_type=F32)
    z_gm = proj(C_GM0, C_GM0 + GROUP)
    kpe_blk = _rope_slab(z_kp, ck1_ref[...], ck2_ref[...])
    kpet_out[0] = kpe_blk.T[QK_NOPE:QK_NOPE + QK_ROPE, :]
    lane = lax.broadcasted_iota(jnp.int32, kpe_blk.shape, 1)
    for p in range(N_HEADS // 2):
        two = knope[:, p * LANES:(p + 1) * LANES]
        for h, nope_first in ((2 * p, two), (2 * p + 1, pltpu.roll(two, LANES - QK_NOPE, 1))):
            k_out[:, h * HEAD_SLAB:(h + 1) * HEAD_SLAB] = jnp.where(lane < QK_NOPE, nope_first, kpe_blk).astype(BF16)
    vt_out[...] = vt.astype(BF16)
    z_cg = proj(C_GM0 + 2 * GROUP, C_GM0 + 3 * GROUP)
    z_h = proj(C_GM0 + 3 * GROUP, C_GM0 + 4 * GROUP)
    gm_out[...] = _silu(z_gm).astype(BF16)
    z_bg = proj(C_GM0 + GROUP, C_GM0 + 2 * GROUP)
    z_gc = proj(C_GM0 + 4 * GROUP, C_GM0 + 5 * GROUP)

    u = z_cg * z_h
    prev = carry_ref[...]
    p1, p2 = prev[7:8, :], prev[6:7, :]
    row = lax.broadcasted_iota(jnp.int32, u.shape, 0)
    um1 = jnp.where(row == 0, p1, pltpu.roll(u, 1, 0))
    um2 = jnp.where(row == 0, p2, jnp.where(row == 1, p1, pltpu.roll(u, 2, 0)))
    y = wconv_ref[0] * um2 + wconv_ref[1] * um1 + wconv_ref[2] * u
    cv_out[...] = (z_bg * y * _silu(z_gc)).astype(BF16)
    carry_ref[...] = u[tm - 8:tm, :]
    conv_out[0] = u[tm - (CONV_W - 1):tm, :]


def _const_spec(shape):
    nd = len(shape)
    return pl.BlockSpec(shape, lambda *_: (0,) * nd, pipeline_mode=pl.Buffered(1))


def _proj_prompt(x2d, seq, gpre, w1, gq, wqt, gkv, wk, wvt, wconv, tabs, tm):
    t, d = x2d.shape
    nb = t // seq
    hs = N_HEADS * HEAD_SLAB
    hv = N_HEADS * V_HEAD
    ck1, ck2, cq1t, cq2t = tabs
    row = lambda w: pl.BlockSpec((tm, w), lambda i: (i, 0))
    col = lambda h: pl.BlockSpec((h, tm), lambda i: (0, i))
    tab = pl.BlockSpec((tm, LANES), lambda i: (i % (seq // tm), 0))
    tab_t = pl.BlockSpec((LANES, tm), lambda i: (0, i % (seq // tm)))
    out_shape = (
        jax.ShapeDtypeStruct((hs, t), BF16),
        jax.ShapeDtypeStruct((t, hs), BF16),
        jax.ShapeDtypeStruct((hv, t), BF16),
        jax.ShapeDtypeStruct((t, gkv.shape[1]), F32),
        jax.ShapeDtypeStruct((nb, QK_ROPE, seq), F32),
        jax.ShapeDtypeStruct((t, GROUP), BF16),
        jax.ShapeDtypeStruct((t, GROUP), BF16),
        jax.ShapeDtypeStruct((nb, CONV_W - 1, GROUP), F32),
    )
    return pl.pallas_call(
        functools.partial(_proj_prompt_kernel, tiles_per_seq=seq // tm),
        grid=(t // tm,),
        in_specs=[row(d), _const_spec(gpre.shape), _const_spec(w1.shape), _const_spec(gq.shape),
                  _const_spec(wqt.shape), _const_spec(gkv.shape), _const_spec(wk.shape), _const_spec(wvt.shape),
                  _const_spec(wconv.shape), tab_t, tab_t, tab, tab],
        out_specs=(col(hs), row(hs), col(hv), row(gkv.shape[1]),
                   pl.BlockSpec((1, QK_ROPE, tm), lambda i: (i // (seq // tm), 0, i % (seq // tm))),
                   row(GROUP), row(GROUP),
                   pl.BlockSpec((1, CONV_W - 1, GROUP), lambda i: (i // (seq // tm), 0, 0))),
        out_shape=out_shape,
        scratch_shapes=[pltpu.VMEM((8, GROUP), F32)],
        compiler_params=pltpu.CompilerParams(dimension_semantics=("arbitrary",),
                                             vmem_limit_bytes=VMEM_LIMIT_BYTES),
        name="proj_prompt",
    )(x2d, gpre, w1, gq, wqt, gkv, wk, wvt, wconv, cq1t, cq2t, ck1, ck2)


def _attn_kernel(qt_ref, k_ref, vt_ref, gm_ref, cv_ref, x_ref, wo_ref, gpost_ref, y_ref,
                 s_ref, m_ref, acc_ref, *, tq, tk):
    i = pl.program_id(1)
    hq = tq // 2
    tri = (lax.broadcasted_iota(jnp.int32, (hq, hq), 0) <= lax.broadcasted_iota(jnp.int32, (hq, hq), 1))

    def slab(h):
        return slice(h * HEAD_SLAB, (h + 1) * HEAD_SLAB)

    def values(h, toks):
        ones = jnp.ones((ONES_ROWS, toks.size), BF16)
        return jnp.concatenate([vt_ref[h * V_HEAD:(h + 1) * V_HEAD, toks], ones], 0)

    def scores(h, toks):
        return jnp.dot(k_ref[toks, slab(h)], qt_ref[slab(h), :], preferred_element_type=F32)

    def update(h, toks, st):
        m = m_ref[h]
        m_new = jnp.maximum(m, jnp.max(st, axis=0, keepdims=True))
        pt = jnp.exp2(st - m_new).astype(BF16)
        m_ref[h] = m_new
        acc_ref[h] = jnp.exp2(m - m_new) * acc_ref[h] + jnp.dot(values(h, toks), pt, preferred_element_type=F32)

    base = pl.multiple_of(i * tq, tq)
    ka, kb = pl.ds(base, hq), pl.ds(base + hq, hq)

    def diagonal_scores(h):
        return (jnp.dot(k_ref[ka, slab(h)], qt_ref[slab(h), :], preferred_element_type=F32),
                jnp.dot(k_ref[kb, slab(h)], qt_ref[slab(h), hq:], preferred_element_type=F32))

    def diagonal_init(h, a, b_):
        a_lo = jnp.where(tri, a[:, :hq], -jnp.inf)
        a_hi = a[:, hq:]
        b_ = jnp.where(tri, b_, -jnp.inf)
        m_lo = jnp.max(a_lo, axis=0, keepdims=True)
        m_hi = jnp.maximum(jnp.max(a_hi, axis=0, keepdims=True), jnp.max(b_, axis=0, keepdims=True))
        pa = jnp.concatenate([jnp.exp2(a_lo - m_lo), jnp.exp2(a_hi - m_hi)], axis=1).astype(BF16)
        pb = jnp.exp2(b_ - m_hi).astype(BF16)
        acc_a = jnp.dot(values(h, ka), pa, preferred_element_type=F32)
        acc_b = jnp.dot(values(h, kb), pb, preferred_element_type=F32)
        m_ref[h] = jnp.concatenate([m_lo, m_hi], axis=1)
        acc_ref[h] = jnp.concatenate([acc_a[:, :hq], acc_a[:, hq:] + acc_b], axis=1)

    n_full = i * (tq // tk)

    def tile_tokens(j):
        return pl.ds(pl.multiple_of(j * tk, tk), tk)

    cur = diagonal_scores(0)
    for h in range(N_HEADS):
        if h + 1 < N_HEADS:
            ahead = diagonal_scores(h + 1)
        else:
            s_ref[0] = scores(0, tile_tokens(0))
        diagonal_init(h, *cur)
        cur = ahead

    def full_tile(j, carry):
        toks = tile_tokens(j)
        nxt = tile_tokens(jnp.minimum(j + 1, n_full - 1))
        for h in range(N_HEADS):
            if h + 1 < N_HEADS:
                s_ref[h + 1] = scores(h + 1, toks)
                update(h, toks, s_ref[h])
            else:
                cur = s_ref[h]
                s_ref[0] = scores(0, nxt)
                update(h, toks, cur)
        return carry

    lax.fori_loop(0, n_full, full_tile, 0)
    outs = [acc_ref[h, :V_HEAD] * (1.0 / acc_ref[h, V_HEAD:V_HEAD + 1]) for h in range(N_HEADS)]
    pairs = [jnp.concatenate(outs[2 * p:2 * p + 2], axis=0).T for p in range(N_HEADS // 2)]

    half = gm_ref.shape[1]
    mla = (jnp.concatenate(pairs, axis=1) * gm_ref[...].astype(F32)).astype(BF16)
    y = jnp.dot(mla, wo_ref[0:half, :], preferred_element_type=F32)
    y = y + jnp.dot(cv_ref[...], wo_ref[half:, :], preferred_element_type=F32)
    y_ref[...] = x_ref[...] + _rms(y, gpost_ref[...])


def _attention_out(qt, k, vt, gm, cv, x2d, wo, gpost, nb, seq, tq, tk):
    hs, t = qt.shape
    hv = vt.shape[0]
    d = x2d.shape[1]
    assert N_HEADS % 2 == 0 and tq % tk == 0
    row = lambda w: pl.BlockSpec((tq, w), lambda b, i: (b * (seq // tq) + i, 0))
    return pl.pallas_call(
        functools.partial(_attn_kernel, tq=tq, tk=tk),
        grid=(nb, seq // tq),
        in_specs=[pl.BlockSpec((hs, tq), lambda b, i: (0, b * (seq // tq) + i)),
                  pl.BlockSpec((seq, hs), lambda b, i: (b, 0)),
                  pl.BlockSpec((hv, seq), lambda b, i: (0, b)),
                  row(gm.shape[1]), row(cv.shape[1]), row(d), _const_spec(wo.shape), _const_spec(gpost.shape)],
        out_specs=row(d),
        out_shape=jax.ShapeDtypeStruct((t, d), F32),
        scratch_shapes=[pltpu.VMEM((N_HEADS, tk, tq), F32), pltpu.VMEM((N_HEADS, 1, tq), F32),
                        pltpu.VMEM((N_HEADS, V_HEAD + ONES_ROWS, tq), F32)],
        compiler_params=pltpu.CompilerParams(dimension_semantics=("arbitrary", "arbitrary"),
                                             vmem_limit_bytes=VMEM_LIMIT_BYTES),
        name="prompt_attn",
    )(qt, k, vt, gm, cv, x2d, wo, gpost)


def _proj_sample_kernel(x_ref, gpre_ref, w1_ref, gq_ref, wq_ref, gkv_ref, wukt_ref, wconv_ref,
                        pos_ref, invf_ref, st_ref,
                        ql_out, qp_out, ckv_out, kpe_out, gm_out, cv_out, conv_out):
    nb, dec_seq, _ = x_ref.shape

    def tok(a, j):
        return a[j * nb:(j + 1) * nb]

    x = jnp.concatenate([x_ref[:, j, :] for j in range(dec_seq)], axis=0)
    cq1, cq2, ck1, ck2 = _token_major_rope_tables(pos_ref, invf_ref, nb)
    proj, cqn, ckv, kpe_blk = _project_common(x, gpre_ref, w1_ref, gq_ref, gkv_ref, ck1, ck2)
    qraw = jnp.dot(cqn, wq_ref[...], preferred_element_type=F32)
    for h in range(N_HEADS):
        qh = _rope_slab(qraw[:, h * HEAD_SLAB:(h + 1) * HEAD_SLAB], cq1, cq2)
        qlh = jnp.dot(qh.astype(BF16), wukt_ref[h], preferred_element_type=F32)
        qph = pltpu.roll(qh, LANES - QK_NOPE, 1)[:, 0:QK_ROPE]
        for j in range(dec_seq):
            ql_out[:, j * N_HEADS + h, :] = tok(qlh, j)
            qp_out[:, j * N_HEADS + h, :] = tok(qph, j)
    kpe = kpe_blk[:, QK_NOPE:QK_NOPE + QK_ROPE]
    for j in range(dec_seq):
        ckv_out[:, j, :] = tok(ckv, j)
        kpe_out[:, j, :] = tok(kpe, j)
    gm_out[...] = _silu(proj(C_GM0, C_GM0 + GROUP))

    u = proj(C_GM0 + 2 * GROUP, C_GM0 + 3 * GROUP) * proj(C_GM0 + 3 * GROUP, C_GM0 + 4 * GROUP)
    s0, s1 = st_ref[:, 0, :], st_ref[:, 1, :]
    um1 = jnp.concatenate([s1, u[:(dec_seq - 1) * nb]], axis=0)
    um2 = jnp.concatenate([s0, s1, u[:(dec_seq - 2) * nb]], axis=0)
    cv_out[...] = _conv_gate(proj, um1, um2, u, wconv_ref)
    for i in range(CONV_W - 1):
        conv_out[:, i, :] = tok(u, dec_seq - (CONV_W - 1) + i)


def _proj_sample(x, gpre, w1, gq, wq, gkv, wukt, wconv, first_pos, state):
    nb, dec_seq, d = x.shape
    assert dec_seq >= CONV_W - 1 and state.shape == (nb, CONV_W - 1, GROUP)
    pos = (first_pos + np.arange(dec_seq, dtype=np.float32)).reshape(dec_seq, 1)
    invf = jnp.pad(_inv_freq(), (QK_NOPE, LANES - QK_NOPE - QK_ROPE)).reshape(1, LANES)
    t = nb * dec_seq
    c = gkv.shape[1]
    out_shape = (
        jax.ShapeDtypeStruct((nb, dec_seq * N_HEADS, c), F32),
        jax.ShapeDtypeStruct((nb, dec_seq * N_HEADS, QK_ROPE), F32),
        jax.ShapeDtypeStruct((nb, dec_seq, c), F32),
        jax.ShapeDtypeStruct((nb, dec_seq, QK_ROPE), F32),
        jax.ShapeDtypeStruct((t, GROUP), F32),
        jax.ShapeDtypeStruct((t, GROUP), F32),
        jax.ShapeDtypeStruct((nb, CONV_W - 1, GROUP), F32),
    )
    return pl.pallas_call(
        _proj_sample_kernel,
        out_shape=out_shape,
        compiler_params=pltpu.CompilerParams(vmem_limit_bytes=VMEM_LIMIT_BYTES),
        name="proj_sample",
    )(x, gpre, w1, gq, wq, gkv, wukt, wconv, pos, invf, state)


def _decode_attn_kernel(pt_ref, ql_ref, qp_ref, cnew_ref, knew_ref, y_hbm, x_hbm, o_ref,
                        ybuf, xbuf, sem, m_ref, l_ref, acc_ref, *, pages, slots, layer, dec_seq):
    g = pl.program_id(0)
    n_g = pl.num_programs(0)
    per_step = ql_ref.shape[0]
    n_chunks = pt_ref.shape[1] // pages
    total = per_step * n_chunks
    page = ybuf.shape[1] // pages
    rows = ql_ref.shape[1]
    dn = (((1,), (1,)), ((), ()))
    burst = DECODE_BURST
    ahead = slots - burst

    def chunk_copies(bb, ch, slot, page_ids):
        cps = []
        for i in range(pages):
            pid = page_ids(bb, ch * pages + i)
            tok = pl.ds(i * page, page)
            cps.append(pltpu.make_async_copy(y_hbm.at[layer, pid], ybuf.at[slot, tok, :], sem.at[0, slot]))
            cps.append(pltpu.make_async_copy(x_hbm.at[layer, pid], xbuf.at[slot, :, tok], sem.at[1, slot]))
        return cps

    def start(bb, ch, slot):
        for cp in chunk_copies(bb, ch, slot, lambda r, j: pt_ref[r, j]):
            cp.start()

    def wait(slot):
        for cp in chunk_copies(0, 0, slot, lambda r, j: 0):
            cp.wait()

    def start_nth(step, k, slot):
        start(step * per_step + k // n_chunks, k % n_chunks, slot)

    @pl.when(g == 0)
    def _():
        for k in range(ahead):
            start_nth(0, k, k % slots)

    tok_of_row = lax.broadcasted_iota(jnp.int32, (rows, 1), 0) // N_HEADS
    for e in range(per_step):
        qlf, qpf = ql_ref[e], qp_ref[e]
        s_new = []
        for j in range(dec_seq):
            sj = (jnp.sum(qlf * cnew_ref[e, j:j + 1, :], axis=-1, keepdims=True)
                  + jnp.sum(qpf * knew_ref[e, j:j + 1, :], axis=-1, keepdims=True))
            s_new.append(jnp.where(tok_of_row >= j, sj, -jnp.inf))
        m_n = functools.reduce(jnp.maximum, s_new)
        p_n = [jnp.exp(sj - m_n) for sj in s_new]
        m_ref[e] = m_n
        l_ref[e] = functools.reduce(lambda a, v: a + v, p_n)
        acc_ref[e, 0] = functools.reduce(lambda a, v: a + v,
                                         [p_n[j] * cnew_ref[e, j:j + 1, :] for j in range(dec_seq)])
        acc_ref[e, 1] = jnp.zeros(acc_ref.shape[2:], F32)

    half = pages * page // 2
    toks = [slice(h * half, (h + 1) * half) for h in range(2)]

    def scores(e, slot):
        ql, qp = ql_ref[e].astype(BF16), qp_ref[e].astype(BF16)
        yb = [ybuf[slot, t, :].astype(BF16) for t in toks]
        xb = [xbuf[slot, :, t].astype(BF16) for t in toks]
        s = jnp.concatenate(
            [lax.dot_general(ql, yb[h], dn, preferred_element_type=F32)
             + jnp.dot(qp, xb[h], preferred_element_type=F32) for h in range(2)], axis=1)
        return e, s, yb

    def softmax(e, s, yb):
        m = m_ref[e]
        m_new = jnp.maximum(m, jnp.max(s, axis=-1, keepdims=True))
        alpha = jnp.exp(m - m_new)
        p = jnp.exp(s - m_new)
        l_ref[e] = alpha * l_ref[e] + jnp.sum(p, axis=-1, keepdims=True)
        m_ref[e] = m_new
        return e, alpha, p.astype(BF16), yb

    def values(e, alpha, p, yb):
        for h in range(2):
            acc_ref[e, h] = alpha * acc_ref[e, h] + jnp.dot(p[:, toks[h]], yb[h], preferred_element_type=F32)

    scored = None
    weighted = None
    for k in range(total):
        if k % burst == 0:
            later = [k + ahead + j for j in range(burst)]
            for nxt in [n for n in later if n < total]:
                start_nth(g, nxt, nxt % slots)
            beyond = [n for n in later if n >= total]
            if beyond:
                @pl.when(g + 1 < n_g)
                def _():
                    for nxt in beyond:
                        start_nth(g + 1, nxt - total, nxt % slots)

        wait(k % slots)
        fresh = scores(k // n_chunks, k % slots)
        if weighted is not None:
            values(*weighted)
        if scored is not None:
            weighted = softmax(*scored)
        scored = fresh
    if weighted is not None:
        values(*weighted)
    values(*softmax(*scored))
    for e in range(per_step):
        o_ref[e] = (acc_ref[e, 0] + acc_ref[e, 1]) * (1.0 / l_ref[e])


def _decode_attention(page_table, ql, qp, cnew, knew, cache_ckv, cache_kpe, layer, pages, slots, per_step):
    nb, rows, c = ql.shape
    dec_seq = cnew.shape[1]
    page, r = cache_kpe.shape[2:]
    n_chunks = page_table.shape[1] // pages
    assert n_chunks * pages == page_table.shape[1] and nb % per_step == 0
    total = per_step * n_chunks
    assert total % slots == 0 and total % DECODE_BURST == 0 and 0 < slots - DECODE_BURST <= total
    kpe_t = jnp.swapaxes(cache_kpe, 2, 3)
    per_b = lambda a: pl.BlockSpec((per_step,) + a.shape[1:], lambda g, pt: (g, 0, 0))
    grid_spec = pltpu.PrefetchScalarGridSpec(
        num_scalar_prefetch=1,
        grid=(nb // per_step,),
        in_specs=[per_b(ql), per_b(qp), per_b(cnew), per_b(knew),
                  pl.BlockSpec(memory_space=pl.ANY), pl.BlockSpec(memory_space=pl.ANY)],
        out_specs=pl.BlockSpec((per_step, rows, c), lambda g, pt: (g, 0, 0)),
        scratch_shapes=[pltpu.VMEM((slots, pages * page, c), F32), pltpu.VMEM((slots, r, pages * page), F32),
                        pltpu.SemaphoreType.DMA((2, slots)),
                        pltpu.VMEM((per_step, rows, 1), F32), pltpu.VMEM((per_step, rows, 1), F32),
                        pltpu.VMEM((per_step, 2, rows, c), F32)],
    )
    return pl.pallas_call(
        functools.partial(_decode_attn_kernel, pages=pages, slots=slots, layer=layer, dec_seq=dec_seq),
        grid_spec=grid_spec,
        out_shape=jax.ShapeDtypeStruct((nb, rows, c), F32),
        compiler_params=pltpu.CompilerParams(dimension_semantics=("arbitrary",),
                                             vmem_limit_bytes=VMEM_LIMIT_BYTES),
        name="decode_attn",
    )(page_table, ql, qp, cnew, knew, cache_ckv, kpe_t)


def _out_sample_kernel(ol_ref, gm_ref, cv_ref, x_ref, wuvt_ref, wo_ref, gpost_ref, y_ref):
    nb, dec_seq, _ = x_ref.shape
    half = gm_ref.shape[1]
    dn = (((1,), (1,)), ((), ()))

    def token_major(ref, index_of_token):
        return jnp.concatenate([ref[:, index_of_token(j), :] for j in range(dec_seq)], axis=0)

    o = jnp.concatenate(
        [lax.dot_general(token_major(ol_ref, lambda j: j * N_HEADS + h).astype(BF16), wuvt_ref[h], dn,
                         preferred_element_type=F32) for h in range(N_HEADS)], axis=1)
    mla = (o * gm_ref[...]).astype(BF16)
    y = jnp.dot(mla, wo_ref[0:half, :], preferred_element_type=F32)
    y = y + jnp.dot(cv_ref[...].astype(BF16), wo_ref[half:, :], preferred_element_type=F32)
    y = token_major(x_ref, lambda j: j) + _rms(y, gpost_ref[...])
    for j in range(dec_seq):
        y_ref[:, j, :] = y[j * nb:(j + 1) * nb]


def _out_sample(ol, gm, cv, x, wuv, wo, gpost):
    return pl.pallas_call(
        _out_sample_kernel,
        out_shape=jax.ShapeDtypeStruct(x.shape, F32),
        compiler_params=pltpu.CompilerParams(vmem_limit_bytes=VMEM_LIMIT_BYTES),
        name="out_sample",
    )(ol, gm, cv, x, wuv, wo, gpost)


def _rot_rows(w):
    r = w.shape[0] // 2
    return jnp.concatenate([-w[r:], w[:r]], axis=0)


def _pack_kernel(wint_ref, wuq_ref, wukv_ref, wout_ref,
                 w1_ref, wq_ref, wqt_ref, wk_ref, wvt_ref, wukt_ref, wuvt_ref, wo_ref):
    d = wint_ref.shape[1]
    blk = LANES

    def put_w1(col0, rows):
        w1_ref[:, col0:col0 + blk] = rows.T.astype(BF16)

    for r0 in range(0, C_KV1, blk):
        put_w1(r0, wint_ref[r0:r0 + blk, :])
    kpe = wint_ref[C_KV1:C_KV1 + QK_ROPE, :]
    put_w1(C_KP0, jnp.concatenate([jnp.zeros((QK_NOPE, d), F32), kpe, _rot_rows(kpe)], axis=0))
    src0 = C_KV1 + QK_ROPE
    for j in range(5 * GROUP // blk):
        put_w1(C_GM0 + j * blk, wint_ref[src0 + j * blk:src0 + (j + 1) * blk, :])

    uqt = wuq_ref[...].T
    ukvt = wukv_ref[...].T
    kv_lora = ukvt.shape[1]
    qk = QK_NOPE + QK_ROPE
    wqt_ref[...] = uqt.astype(BF16)
    for h in range(N_HEADS):
        pe = uqt[h * qk + QK_NOPE:(h + 1) * qk]
        q_slab = jnp.concatenate([uqt[h * qk:h * qk + QK_NOPE], pe, _rot_rows(pe)], axis=0)
        wq_ref[:, h * HEAD_SLAB:(h + 1) * HEAD_SLAB] = q_slab.T.astype(BF16)
        ukt = ukvt[h * (QK_NOPE + V_HEAD):h * (QK_NOPE + V_HEAD) + QK_NOPE]
        uvt = ukvt[h * (QK_NOPE + V_HEAD) + QK_NOPE:(h + 1) * (QK_NOPE + V_HEAD)]
        k_slab = jnp.concatenate([ukt, jnp.zeros((HEAD_SLAB - QK_NOPE, kv_lora), F32)], axis=0)
        wukt_ref[h] = k_slab.astype(BF16)
        wvt_ref[h * V_HEAD:(h + 1) * V_HEAD, :] = uvt.astype(BF16)
        wuvt_ref[h] = uvt.astype(BF16)
    for p in range(N_HEADS // 2):
        two = ukvt[2 * p * (QK_NOPE + V_HEAD):2 * (p + 1) * (QK_NOPE + V_HEAD)]
        pair = jnp.concatenate([two[0:QK_NOPE], two[QK_NOPE + V_HEAD:2 * QK_NOPE + V_HEAD]], axis=0)
        wk_ref[:, p * LANES:(p + 1) * LANES] = pair.T.astype(BF16)
    wo_ref[...] = wout_ref[...].astype(BF16)


def _pack_weights(w_in, w_uq, w_ukv, w_out):
    d = w_in.shape[0]
    q_lora, kv_lora = w_uq.shape[0], w_ukv.shape[0]
    assert w_in.shape[1] == C_KV1 + QK_ROPE + 5 * GROUP
    sds = lambda *s: jax.ShapeDtypeStruct(s, BF16)
    return pl.pallas_call(
        _pack_kernel,
        out_shape=(sds(d, D_IN_PACKED), sds(q_lora, N_HEADS * HEAD_SLAB), sds(w_uq.shape[1], q_lora),
                   sds(kv_lora, N_HEADS * QK_NOPE), sds(N_HEADS * V_HEAD, kv_lora),
                   sds(N_HEADS, HEAD_SLAB, kv_lora), sds(N_HEADS, V_HEAD, kv_lora), sds(*w_out.shape)),
        compiler_params=pltpu.CompilerParams(vmem_limit_bytes=VMEM_LIMIT_BYTES),
        name="pack_weights",
    )(w_in.T, w_uq, w_ukv, w_out)


PROMPT_TILE = 512
ATTN_TQ = 512
ATTN_TK = 512
DECODE_PAGES = 32
DECODE_SLOTS = 8
DECODE_BURST = 2
DECODE_PER_STEP = 2


def kernel(x_prompt, x_sample, cache_ckv, cache_kpe, state_conv, page_table, g_pre, w_in, g_qnorm, w_uq,
           g_kvnorm, w_ukv, w_conv, w_out, g_post):
    depth = w_in.shape[0]
    nb, seq, d = x_prompt.shape
    db, dec_seq, _ = x_sample.shape
    past_len = page_table.shape[1] * cache_ckv.shape[2]
    c = cache_ckv.shape[3]

    tabs_p = _rope_tables(seq)
    taps = jnp.swapaxes(w_conv, 0, 1)[:, :, None, :]

    xp = x_prompt.reshape(nb * seq, d)
    xs = x_sample
    outs = [[] for _ in range(6)]
    for l in range(depth):
        w1, wq, wqt, wk, wvt, wukt, wuvt, wo = _pack_weights(w_in[l], w_uq[l], w_ukv[l], w_out[l])
        gpre, gq, gkv, gpost = g_pre[l][None], g_qnorm[l][None], g_kvnorm[l][None], g_post[l][None]

        qt, k, vt, ckv_p, kpe_p, gm, cv, conv_p = _proj_prompt(
            xp, seq, gpre, w1, gq, wqt, gkv, wk, wvt, taps[:, l], tabs_p, PROMPT_TILE)
        xp = _attention_out(qt, k, vt, gm, cv, xp, wo, gpost, nb, seq, ATTN_TQ, ATTN_TK)
        outs[0].append(ckv_p.reshape(nb, seq, c))
        outs[1].append(jnp.swapaxes(kpe_p, 1, 2))
        outs[2].append(conv_p)

        ql, qp, ckv_s, kpe_s, gm_s, cv_s, conv_s = _proj_sample(
            xs, gpre, w1, gq, wq, gkv, wukt, taps[:, l], past_len, state_conv[l].astype(F32))
        ol = _decode_attention(page_table, ql, qp, ckv_s, kpe_s, cache_ckv, cache_kpe, l,
                               DECODE_PAGES, DECODE_SLOTS, DECODE_PER_STEP)
        xs = _out_sample(ol, gm_s, cv_s, xs, wuvt, wo, gpost)
        outs[3].append(ckv_s)
        outs[4].append(kpe_s)
        outs[5].append(conv_s)

    return (xp.reshape(nb, seq, d), xs, *[jnp.stack(o_) for o_ in outs])
```
